```python
import math
import jax
import jax.numpy as jnp
from jax import lax
import numpy as np

D_MODEL = 1024
BATCH = 4
SEQ = 8192
DEPTH = 4
DEC_BATCH = 16
DEC_SEQ = 16
PAST_LEN = 2048

CHUNK = 64
N_MIXERS = 2
N_GDN = (DEPTH + 1) // 2
N_DIFF = DEPTH // 2
RMS_EPS = 1e-6
F32 = jnp.float32

GDN_HEADS = 8
GDN_DK = 128
GDN_DV = 128
GDN_QK_WIDTH = GDN_HEADS * GDN_DK
GDN_WIDTH = GDN_HEADS * GDN_DV
CONV_WIDTH = 4
GDN_CONV_CH = 2 * GDN_QK_WIDTH + GDN_WIDTH
GDN_IN = GDN_CONV_CH + GDN_WIDTH + 2 * GDN_HEADS

DIFF_HEADS = 8
DIFF_QK_DIM = 64
DIFF_V_DIM = 2 * DIFF_QK_DIM
DIFF_QK_WIDTH = DIFF_HEADS * 2 * DIFF_QK_DIM
DIFF_WIDTH = DIFF_HEADS * DIFF_V_DIM
DIFF_IN = 2 * DIFF_QK_WIDTH + 2 * DIFF_WIDTH
Q_BLOCK = 128
ROPE_THETA = 10000.0

kernel_name = "hybrid_gdn_diffattn_stream_step"


def rmsnorm(x, w):
    xf = x.astype(F32)
    y = xf * lax.rsqrt(jnp.mean(xf * xf, axis=-1, keepdims=True) + RMS_EPS)
    return (y * w.astype(F32)).astype(x.dtype)


def l2norm(x):
    return x * lax.rsqrt(jnp.sum(x * x, axis=-1, keepdims=True) + 1e-6)


def rope(x, pos):
    half = x.shape[-1] // 2
    inv = 1.0 / (ROPE_THETA ** (jnp.arange(half, dtype=F32) / half))
    ang = pos.astype(F32)[:, None] * inv[None, :]
    cos = jnp.cos(ang)[None, :, None, :]
    sin = jnp.sin(ang)[None, :, None, :]
    xf = x.astype(F32)
    x1, x2 = xf[..., :half], xf[..., half:]
    return jnp.concatenate([x1 * cos - x2 * sin, x2 * cos + x1 * sin], axis=-1).astype(x.dtype)


def causal_conv(u, buf, w):
    T = u.shape[1]
    full = jnp.concatenate([buf, u], axis=1)
    y = full[:, 0:T] * w[0]
    for j in range(1, CONV_WIDTH):
        y = y + full[:, j:j + T] * w[j]
    return jax.nn.silu(y), full[:, -(CONV_WIDTH - 1):]


def gdn_chunked(q, k, v, g, beta, s0, chunk):
    B, T, H, _ = q.shape
    dv = v.shape[-1]
    n = T // chunk

    def blocks(a):
        return jnp.swapaxes(a.reshape((B, n, chunk, H) + a.shape[3:]), 2, 3)

    q, k, v, g, beta = blocks(q), blocks(k), blocks(v), blocks(g), blocks(beta)
    G = jnp.cumsum(g, axis=-1)
    idx = jnp.arange(chunk)
    incl = idx[:, None] >= idx[None, :]
    strict = idx[:, None] > idx[None, :]
    decay = jnp.exp(jnp.where(incl, G[..., :, None] - G[..., None, :], -jnp.inf))
    kb = k * beta[..., None]
    A = jnp.where(strict, jnp.einsum("bnhid,bnhjd->bnhij", kb, k) * decay, 0.0)
    eye = jnp.eye(chunk, dtype=q.dtype)
    rhs = jnp.concatenate([v * beta[..., None], kb * jnp.exp(G)[..., None]], axis=-1)
    sol = lax.linalg.triangular_solve(eye + A, rhs, left_side=True, lower=True, unit_diagonal=True)
    u_intra, w = sol[..., :dv], sol[..., dv:]
    qk = jnp.einsum("bnhid,bnhjd->bnhij", q, k) * decay
    qg = q * jnp.exp(G)[..., None]
    kd = k * jnp.exp(G[..., -1:] - G)[..., None]
    g_last = jnp.exp(G[..., -1])

    def step(s, xs):
        u_i, w_i, qk_i, qg_i, kd_i, gl_i = xs
        u = u_i - jnp.einsum("bhcd,bhde->bhce", w_i, s)
        o = jnp.einsum("bhcd,bhde->bhce", qg_i, s) + jnp.einsum("bhij,bhje->bhie", qk_i, u)
        s = s * gl_i[..., None, None] + jnp.einsum("bhcd,bhce->bhde", kd_i, u)
        return s, o

    xs = (jnp.moveaxis(u_intra, 1, 0), jnp.moveaxis(w, 1, 0), jnp.moveaxis(qk, 1, 0),
          jnp.moveaxis(qg, 1, 0), jnp.moveaxis(kd, 1, 0), jnp.moveaxis(g_last, 1, 0))
    s_final, o = lax.scan(step, s0, xs)
    o = jnp.swapaxes(jnp.moveaxis(o, 0, 1), 2, 3).reshape(B, T, H, dv)
    return o, s_final


def gdn_mixer(h, conv_buf, s0, w_in, conv_w, a_log, dt_bias, o_norm, w_out):
    B, T, _ = h.shape
    proj = h @ w_in
    i1 = GDN_CONV_CH
    i2 = i1 + GDN_WIDTH
    i3 = i2 + GDN_HEADS
    qkv, z, a, b = proj[..., :i1], proj[..., i1:i2], proj[..., i2:i3], proj[..., i3:]
    qkv_c, new_buf = causal_conv(qkv, conv_buf, conv_w)
    qkv_c = qkv_c.astype(F32)
    q = l2norm(qkv_c[..., :GDN_QK_WIDTH].reshape(B, T, GDN_HEADS, GDN_DK)) * (GDN_DK ** -0.5)
    k = l2norm(qkv_c[..., GDN_QK_WIDTH:2 * GDN_QK_WIDTH].reshape(B, T, GDN_HEADS, GDN_DK))
    v = qkv_c[..., 2 * GDN_QK_WIDTH:].reshape(B, T, GDN_HEADS, GDN_DV)
    g = -jnp.exp(a_log.astype(F32)) * jax.nn.softplus(a.astype(F32) + dt_bias.astype(F32))
    beta = jax.nn.sigmoid(b.astype(F32))
    o, s_new = gdn_chunked(q, k, v, g, beta, s0.astype(F32), min(CHUNK, T))
    o = rmsnorm(o, o_norm) * jax.nn.silu(z.astype(F32).reshape(B, T, GDN_HEADS, GDN_DV))
    y = o.reshape(B, T, GDN_WIDTH).astype(h.dtype) @ w_out
    return y, new_buf, s_new.astype(s0.dtype)


def diff_project(h, w_in, pos):
    B, T, _ = h.shape
    proj = h @ w_in
    i1 = DIFF_QK_WIDTH
    i2 = 2 * DIFF_QK_WIDTH
    i3 = i2 + DIFF_WIDTH
    q = rope(proj[..., :i1].reshape(B, T, 2 * DIFF_HEADS, DIFF_QK_DIM), pos)
    k = rope(proj[..., i1:i2].reshape(B, T, 2 * DIFF_HEADS, DIFF_QK_DIM), pos)
    q = q.reshape(B, T, DIFF_HEADS, 2, DIFF_QK_DIM)
    k = k.reshape(B, T, DIFF_HEADS, 2, DIFF_QK_DIM)
    v = proj[..., i2:i3].reshape(B, T, DIFF_HEADS, DIFF_V_DIM)
    z = proj[..., i3:]
    return q, k, v, z


def diff_lambda(lq1, lk1, lq2, lk2, lam_init):
    return (jnp.exp(jnp.sum(lq1.astype(F32) * lk1.astype(F32)))
            - jnp.exp(jnp.sum(lq2.astype(F32) * lk2.astype(F32))) + lam_init)


def diff_attend(q, k, v, lam, mask):
    s = jnp.einsum("bqhcd,bkhcd->bhcqk", q.astype(F32) * (DIFF_QK_DIM ** -0.5), k.astype(F32))
    if mask is not None:
        s = jnp.where(mask, s, -jnp.inf)
    p = jax.nn.softmax(s, axis=-1)
    a = p[:, :, 0] - lam * p[:, :, 1]
    return jnp.einsum("bhqk,bkhe->bqhe", a, v.astype(F32))


def diff_attend_prompt(q, k, v, lam):
    B, T = q.shape[:2]
    nb = T // Q_BLOCK
    qb = jnp.swapaxes(q.reshape(B, nb, Q_BLOCK, DIFF_HEADS, 2, DIFF_QK_DIM), 0, 1)
    key_chunk = jnp.arange(T) // CHUNK

    def block(args):
        q_blk, bi = args
        q_chunk = (bi * Q_BLOCK + jnp.arange(Q_BLOCK)) // CHUNK
        mask = key_chunk[None, :] <= q_chunk[:, None]
        return diff_attend(q_blk, k, v, lam, mask)

    o = lax.map(block, (qb, jnp.arange(nb)))
    return jnp.swapaxes(o, 0, 1).reshape(B, T, DIFF_HEADS, DIFF_V_DIM)


def diff_output(o, z, subln, lam_init, w_out, dtype):
    B, T = o.shape[:2]
    o = rmsnorm(o, subln) * (1.0 - lam_init)
    o = o.reshape(B, T, DIFF_WIDTH) * jax.nn.silu(z.astype(F32))
    return o.astype(dtype) @ w_out


def trunk(x, c, pos, conv_bufs, gdn_states, past_k, past_v, p):
    B, T, _ = x.shape
    c_act = jax.nn.silu(c.astype(F32))
    states, convs, ks, vs = [], [], [], []
    for i in range(DEPTH):
        j = i // N_MIXERS
        ada = c_act @ p["w_ada"][i].astype(F32) + p["b_ada"][i].astype(F32)
        shift, scale, gate = jnp.split(ada[:, None, :], 3, axis=-1)
        h = (rmsnorm(x, p["norm_pre"][i]).astype(F32) * (1.0 + scale) + shift).astype(x.dtype)
        if i % N_MIXERS == 0:
            out, buf, st = gdn_mixer(h, conv_bufs[j], gdn_states[j], p["w_in_gdn"][j], p["conv_gdn"][j],
                                     p["a_log_gdn"][j], p["dt_bias_gdn"][j], p["onorm_gdn"][j],
                                     p["w_out_gdn"][j])
            convs.append(buf)
            states.append(st)
        else:
            lam_init = 0.8 - 0.6 * math.exp(-0.3 * i)
            lam = diff_lambda(p["lam_q1"][j], p["lam_k1"][j], p["lam_q2"][j], p["lam_k2"][j], lam_init)
            q, k, v, z = diff_project(h, p["w_in_diff"][j], pos)
            if past_k is None:
                o = diff_attend_prompt(q, k, v, lam)
            else:
                k_all = jnp.concatenate([past_k[j].reshape(B, -1, DIFF_HEADS, 2, DIFF_QK_DIM), k], axis=1)
                v_all = jnp.concatenate([past_v[j], v], axis=1)
                o = diff_attend(q, k_all, v_all, lam, None)
            out = diff_output(o, z, p["subln_diff"][j], lam_init, p["w_out_diff"][j], x.dtype)
            ks.append(k.reshape(B, T, DIFF_HEADS, 2 * DIFF_QK_DIM))
            vs.append(v)
        x = x + (gate * rmsnorm(out, p["norm_post"][i]).astype(F32)).astype(x.dtype)
    return x, jnp.stack(states), jnp.stack(convs), jnp.stack(ks), jnp.stack(vs)


def setup_inputs(seed: int = 0) -> dict:
    key = jax.random.key(seed)
    ks = iter(jax.random.split(key, 32))

    def nrm(shape, s=1.0):
        return jax.random.normal(next(ks), shape, F32) * s

    def unif(shape, lo, hi):
        return jax.random.uniform(next(ks), shape, F32, lo, hi)

    dt = jnp.exp(unif((N_GDN, GDN_HEADS), math.log(1e-3), math.log(1e-1)))
    return {
        "x_prompt": nrm((BATCH, SEQ, D_MODEL)),
        "x_sample": nrm((DEC_BATCH, DEC_SEQ, D_MODEL)),
        "c_prompt": nrm((BATCH, D_MODEL)),
        "c_sample": nrm((DEC_BATCH, D_MODEL)),
        "state_gdn": nrm((N_GDN, DEC_BATCH, GDN_HEADS, GDN_DK, GDN_DV), 0.1),
        "cache_conv": nrm((N_GDN, DEC_BATCH, CONV_WIDTH - 1, GDN_CONV_CH)),
        "cache_k": nrm((N_DIFF, DEC_BATCH, PAST_LEN, DIFF_HEADS, 2 * DIFF_QK_DIM)),
        "cache_v": nrm((N_DIFF, DEC_BATCH, PAST_LEN, DIFF_HEADS, DIFF_V_DIM)),
        "norm_pre": 1.0 + nrm((DEPTH, D_MODEL), 0.02),
        "norm_post": 1.0 + nrm((DEPTH, D_MODEL), 0.02),
        "w_ada": nrm((DEPTH, D_MODEL, 3 * D_MODEL), 0.5 * D_MODEL ** -0.5),
        "b_ada": nrm((DEPTH, 3 * D_MODEL), 0.02),
        "w_in_gdn": nrm((N_GDN, D_MODEL, GDN_IN), D_MODEL ** -0.5),
        "conv_gdn": nrm((N_GDN, CONV_WIDTH, GDN_CONV_CH), CONV_WIDTH ** -0.5),
        "a_log_gdn": jnp.log(unif((N_GDN, GDN_HEADS), 1.0, 16.0)),
        "dt_bias_gdn": dt + jnp.log(-jnp.expm1(-dt)),
        "onorm_gdn": 1.0 + nrm((N_GDN, GDN_DV), 0.02),
        "w_out_gdn": nrm((N_GDN, GDN_WIDTH, D_MODEL), GDN_WIDTH ** -0.5),
        "w_in_diff": nrm((N_DIFF, D_MODEL, DIFF_IN), D_MODEL ** -0.5),
        "lam_q1": nrm((N_DIFF, DIFF_QK_DIM), 0.1),
        "lam_k1": nrm((N_DIFF, DIFF_QK_DIM), 0.1),
        "lam_q2": nrm((N_DIFF, DIFF_QK_DIM), 0.1),
        "lam_k2": nrm((N_DIFF, DIFF_QK_DIM), 0.1),
        "subln_diff": 1.0 + nrm((N_DIFF, DIFF_V_DIM), 0.02),
        "w_out_diff": nrm((N_DIFF, DIFF_WIDTH, D_MODEL), DIFF_WIDTH ** -0.5),
    }


def reference(x_prompt, x_sample, c_prompt, c_sample, state_gdn, cache_conv, cache_k, cache_v,
              norm_pre, norm_post, w_ada, b_ada, w_in_gdn, conv_gdn, a_log_gdn, dt_bias_gdn,
              onorm_gdn, w_out_gdn, w_in_diff, lam_q1, lam_k1, lam_q2, lam_k2, subln_diff, w_out_diff):
    p = {"norm_pre": norm_pre, "norm_post": norm_post, "w_ada": w_ada, "b_ada": b_ada,
         "w_in_gdn": w_in_gdn, "conv_gdn": conv_gdn, "a_log_gdn": a_log_gdn,
         "dt_bias_gdn": dt_bias_gdn, "onorm_gdn": onorm_gdn, "w_out_gdn": w_out_gdn,
         "w_in_diff": w_in_diff, "lam_q1": lam_q1, "lam_k1": lam_k1, "lam_q2": lam_q2,
         "lam_k2": lam_k2, "subln_diff": subln_diff, "w_out_diff": w_out_diff}
    Bp, Tp, _ = x_prompt.shape
    Ts = x_sample.shape[1]
    zero_conv = jnp.zeros((N_GDN, Bp, CONV_WIDTH - 1, GDN_CONV_CH), x_prompt.dtype)
    zero_state = jnp.zeros((N_GDN, Bp, GDN_HEADS, GDN_DK, GDN_DV), x_prompt.dtype)
    y_prompt, st_p, conv_p, k_p, v_p = trunk(x_prompt, c_prompt, jnp.arange(Tp), zero_conv, zero_state,
                                             None, None, p)
    y_sample, st_s, conv_s, k_s, v_s = trunk(x_sample, c_sample, PAST_LEN + jnp.arange(Ts), cache_conv,
                                             state_gdn, cache_k, cache_v, p)
    return (y_prompt, y_sample, st_p, conv_p, k_p, v_p, st_s, conv_s, k_s, v_s)
```

```python
import functools
import math

import jax
import jax.numpy as jnp
from jax import lax
from jax.experimental import pallas as pl
from jax.experimental.pallas import tpu as pltpu

F32 = jnp.float32
BF16 = jnp.bfloat16
HIGHEST = lax.Precision.HIGHEST

RMS_EPS = 1e-6
L2_EPS = 1e-6
ROPE_THETA = 10000.0
CONV_WIDTH = 4
MASK_CHUNK = 64
LANES = 128
SUBLANES = 8
VMEM_LIMIT = 56 * 1024 * 1024

NT_DIMS = (((1,), (1,)), ((), ()))


def _cparams(*sem):
    return pltpu.CompilerParams(dimension_semantics=sem, vmem_limit_bytes=VMEM_LIMIT)


def _silu(x):
    return x * jax.nn.sigmoid(x)


def _dot(a, b):
    return jnp.dot(a, b, preferred_element_type=F32)


def _dot_nt(a, b):
    return lax.dot_general(a, b, NT_DIMS, preferred_element_type=F32)


def _ada_kernel(c_ref, w_ref, b_ref, o_ref):
    act = _silu(c_ref[...])
    o_ref[...] = jnp.dot(act, w_ref[...], precision=HIGHEST, preferred_element_type=F32) + b_ref[...]


def _ada(c_all, w_ada, b_ada):
    depth, d, n = w_ada.shape
    bc = c_all.shape[0]
    tn = 512
    return pl.pallas_call(
        _ada_kernel,
        grid=(depth, n // tn),
        in_specs=[pl.BlockSpec((bc, d), lambda i, j: (0, 0)),
                  pl.BlockSpec((None, d, tn), lambda i, j: (i, 0, j)),
                  pl.BlockSpec((None, 1, tn), lambda i, j: (i, 0, j))],
        out_specs=pl.BlockSpec((None, bc, tn), lambda i, j: (i, 0, j)),
        out_shape=jax.ShapeDtypeStruct((depth, bc, n), F32),
        compiler_params=_cparams("parallel", "parallel"),
    )(c_all, w_ada, b_ada.reshape(depth, 1, n))


def _prenorm(x_ref, mod_ref, nw_ref):
    x = x_ref[...]
    y = x * lax.rsqrt(jnp.mean(x * x, axis=-1, keepdims=True) + RMS_EPS) * nw_ref[...]
    return y * (1.0 + mod_ref[1:2, :]) + mod_ref[0:1, :]


def _conv_silu(u, tail, cw):
    def taps(a):
        y = a * cw[CONV_WIDTH - 1:CONV_WIDTH, :]
        for k in range(1, CONV_WIDTH):
            y = y + pltpu.roll(a, k, axis=0) * cw[CONV_WIDTH - 1 - k:CONV_WIDTH - k, :]
        return y
    y = taps(u)
    head = taps(jnp.concatenate([tail, u[0:SUBLANES]], axis=0))[SUBLANES:2 * SUBLANES]
    y = jnp.concatenate([head, y[SUBLANES:]], axis=0)
    return _silu(y)


def _gdn_in_kernel(x_ref, mod_ref, nw_ref, w_ref, wab_ref, cw_ref, cinit_ref, alog_ref, dtb_ref,
                   qkvz_ref, gcol_ref, cout_ref, h_sc, tail_sc, *, heads, dk, chunk):
    t = pl.program_id(1)
    j = pl.program_id(2)
    tm = x_ref.shape[0]

    @pl.when(j == 0)
    def _():
        h = _prenorm(x_ref, mod_ref, nw_ref)
        h_sc[...] = h.astype(BF16)
        ab = jnp.dot(h, wab_ref[...], precision=HIGHEST, preferred_element_type=F32)
        g = -jnp.exp(alog_ref[...]) * jax.nn.softplus(ab + dtb_ref[...])
        row = lax.broadcasted_iota(jnp.int32, g.shape, 0) % chunk
        s = 1
        while s < chunk:
            g = g + jnp.where(row >= s, pltpu.roll(g, s, axis=0), 0.0)
            s *= 2
        lane = lax.broadcasted_iota(jnp.int32, g.shape, 1)
        gcol_ref[...] = jnp.where(lane < heads, g, jax.nn.sigmoid(ab))

    def conv_branch(jj):
        @pl.when(t == 0)
        def _():
            tail_sc[jj] = cinit_ref[...]
        u = _dot(h_sc[...], w_ref[...])
        y = _conv_silu(u, tail_sc[jj], cw_ref[...])
        tail_sc[jj] = u[tm - SUBLANES:tm]
        cout_ref[jj] = u[tm - SUBLANES:tm]
        return y

    def l2norm_store(y, scale):
        for hh in range(heads):
            ys = y[:, hh * dk:(hh + 1) * dk]
            n = jnp.sum(ys * ys, axis=-1, keepdims=True)
            qkvz_ref[:, hh * dk:(hh + 1) * dk] = (ys * (lax.rsqrt(n + L2_EPS) * scale)).astype(BF16)

    @pl.when(j == 0)
    def _():
        l2norm_store(conv_branch(0), dk ** -0.5)

    @pl.when(j == 1)
    def _():
        l2norm_store(conv_branch(1), 1.0)

    @pl.when(j == 2)
    def _():
        qkvz_ref[...] = conv_branch(2).astype(BF16)

    @pl.when(j == 3)
    def _():
        qkvz_ref[...] = _silu(_dot(h_sc[...], w_ref[...])).astype(BF16)


def _gdn_in(x, mod, nw, w_main, w_ab, conv_w, conv_init, a_log, dt_bias, *, heads, dk, tm, chunk):
    b, t, d = x.shape
    n = w_main.shape[1]
    nb = n // 4
    assert nb == heads * dk and t % tm == 0 and tm % SUBLANES == 0 and tm % chunk == 0
    kern = functools.partial(_gdn_in_kernel, heads=heads, dk=dk, chunk=chunk)
    cj = lambda j: jnp.minimum(j, 2)
    return pl.pallas_call(
        kern,
        grid=(b, t // tm, 4),
        in_specs=[pl.BlockSpec((None, tm, d), lambda bi, ti, j: (bi, ti, 0)),
                  pl.BlockSpec((None, 3, d), lambda bi, ti, j: (bi, 0, 0)),
                  pl.BlockSpec((1, d), lambda bi, ti, j: (0, 0)),
                  pl.BlockSpec((d, nb), lambda bi, ti, j: (0, j)),
                  pl.BlockSpec((d, LANES), lambda bi, ti, j: (0, 0)),
                  pl.BlockSpec((CONV_WIDTH, nb), lambda bi, ti, j: (0, cj(j))),
                  pl.BlockSpec((None, SUBLANES, nb), lambda bi, ti, j: (bi, 0, cj(j))),
                  pl.BlockSpec((1, LANES), lambda bi, ti, j: (0, 0)),
                  pl.BlockSpec((1, LANES), lambda bi, ti, j: (0, 0))],
        out_specs=[pl.BlockSpec((None, tm, nb), lambda bi, ti, j: (bi, ti, j)),
                   pl.BlockSpec((None, tm, LANES), lambda bi, ti, j: (bi, ti, 0)),
                   pl.BlockSpec((None, 3, SUBLANES, nb), lambda bi, ti, j: (bi, 0, 0, 0))],
        out_shape=[jax.ShapeDtypeStruct((b, t, n), BF16),
                   jax.ShapeDtypeStruct((b, t, LANES), F32),
                   jax.ShapeDtypeStruct((b, 3, SUBLANES, nb), F32)],
        scratch_shapes=[pltpu.VMEM((tm, d), BF16), pltpu.VMEM((3, SUBLANES, nb), F32)],
        compiler_params=_cparams("parallel", "arbitrary", "arbitrary"),
    )(x, mod, nw, w_main, w_ab, conv_w, conv_init, a_log, dt_bias)


def _gdn_kernel(q_ref, k_ref, v_ref, z_ref, gcol_ref, s0_ref, ow_ref, o_ref, s_ref, *, heads, dk, chunk):
    t = pl.program_id(1)
    tb = q_ref.shape[0]
    c = chunk

    @pl.when(t == 0)
    def _():
        s_ref[...] = s0_ref[...]

    ii = lax.broadcasted_iota(jnp.int32, (c, c), 0)
    jj = lax.broadcasted_iota(jnp.int32, (c, c), 1)
    incl = ii >= jj
    strict = ii > jj
    eye = (ii == jj).astype(F32)
    off_masks = []
    s = 1
    while s < c:
        off_masks.append((ii // (2 * s) == jj // (2 * s)) & (ii // s != jj // s) & strict)
        s *= 2

    def chunk_body(ci, carry):
        r = pl.ds(pl.multiple_of(ci * c, c), c)
        gall = gcol_ref[r, :]
        hr = range(heads)
        hsl = [slice(hh * dk, (hh + 1) * dk) for hh in hr]
        g = [gall[:, hh:hh + 1] for hh in hr]
        beta = [gall[:, heads + hh:heads + hh + 1] for hh in hr]
        kbf = [k_ref[r, hs] for hs in hsl]
        qbf = [q_ref[r, hs] for hs in hsl]
        kf = [x.astype(F32) for x in kbf]
        kb = [kf[hh] * beta[hh] for hh in hr]
        gb = [jnp.broadcast_to(g[hh], (c, c)) for hh in hr]
        decay = [jnp.exp(jnp.where(incl, x - x.T, -jnp.inf)) for x in gb]
        a = [jnp.where(strict, _dot_nt(kb[hh].astype(BF16), kbf[hh]) * decay[hh], 0.0) for hh in hr]
        qk = [(_dot_nt(qbf[hh], kbf[hh]) * decay[hh]).astype(BF16) for hh in hr]
        p = [eye - jnp.where(off_masks[0], a[hh], 0.0) for hh in hr]
        for off in off_masks[1:]:
            pb = [x.astype(BF16) for x in p]
            x = [_dot(jnp.where(off, a[hh], 0.0).astype(BF16), pb[hh]).astype(BF16) for hh in hr]
            p = [p[hh] - _dot(pb[hh], x[hh]) for hh in hr]
        eg = [jnp.exp(x) for x in g]
        rhs = [jnp.concatenate([v_ref[r, hsl[hh]].astype(F32) * beta[hh], kb[hh] * eg[hh]], axis=1).astype(BF16)
               for hh in hr]
        sol = [_dot(p[hh].astype(BF16), rhs[hh]) for hh in hr]
        s_old = [s_ref[hh] for hh in hr]
        wq = [jnp.concatenate([sol[hh][:, dk:].astype(BF16), (qbf[hh].astype(F32) * eg[hh]).astype(BF16)], axis=0)
              for hh in hr]
        r1 = [_dot(wq[hh], s_old[hh].astype(BF16)) for hh in hr]
        ub = [(sol[hh][:, :dk] - r1[hh][:c]).astype(BF16) for hh in hr]
        g_last = [x[c - 1:c, :] for x in g]
        kd_t = [(kf[hh] * jnp.exp(g_last[hh] - g[hh])).T.astype(BF16) for hh in hr]
        for hh in hr:
            s_ref[hh] = s_old[hh] * jnp.exp(g_last[hh]) + _dot(kd_t[hh], ub[hh])
        o = [r1[hh][c:] + _dot(qk[hh], ub[hh]) for hh in hr]
        for hh in hr:
            on = o[hh] * lax.rsqrt(jnp.mean(o[hh] * o[hh], axis=-1, keepdims=True) + RMS_EPS) * ow_ref[...]
            o_ref[r, hsl[hh]] = (on * z_ref[r, hsl[hh]].astype(F32)).astype(BF16)
        return carry

    lax.fori_loop(0, tb // c, chunk_body, 0)


def _gdn(qkvz, gcol, s0, onorm, *, heads, dk, tb, chunk):
    b, t, n = qkvz.shape
    nb = n // 4
    assert t % tb == 0 and tb % chunk == 0 and nb == heads * dk
    kern = functools.partial(_gdn_kernel, heads=heads, dk=dk, chunk=chunk)
    col = lambda jcol: pl.BlockSpec((None, tb, nb), lambda bi, ti: (bi, ti, jcol))
    return pl.pallas_call(
        kern,
        grid=(b, t // tb),
        in_specs=[col(0), col(1), col(2), col(3),
                  pl.BlockSpec((None, tb, LANES), lambda bi, ti: (bi, ti, 0)),
                  pl.BlockSpec((None, heads, dk, dk), lambda bi, ti: (bi, 0, 0, 0)),
                  pl.BlockSpec((1, dk), lambda bi, ti: (0, 0))],
        out_specs=[pl.BlockSpec((None, tb, nb), lambda bi, ti: (bi, ti, 0)),
                   pl.BlockSpec((None, heads, dk, dk), lambda bi, ti: (bi, 0, 0, 0))],
        out_shape=[jax.ShapeDtypeStruct((b, t, nb), BF16),
                   jax.ShapeDtypeStruct((b, heads, dk, dk), F32)],
        compiler_params=_cparams("parallel", "arbitrary"),
    )(qkvz, qkvz, qkvz, qkvz, gcol, s0, onorm)


def _rope(x, cos, sin_signed, lane_lo):
    half = MASK_CHUNK // 2
    swapped = jnp.where(lane_lo, pltpu.roll(x, LANES - half, axis=1), pltpu.roll(x, half, axis=1))
    return x * cos + swapped * sin_signed


def _diff_in_kernel(x_ref, mod_ref, nw_ref, w_ref, cos_ref, sin_ref,
                    qb_ref, kb_ref, kf_ref, vf_ref, vb_ref, zs_ref, h_sc, *, v_transposed, q_scale):
    j = pl.program_id(2)
    n = w_ref.shape[1]

    @pl.when(j == 0)
    def _():
        h_sc[...] = _prenorm(x_ref, mod_ref, nw_ref).astype(BF16)

    def rotated():
        u = _dot(h_sc[...], w_ref[...])
        cos = cos_ref[...]
        sin = sin_ref[...]
        lane_lo = (lax.broadcasted_iota(jnp.int32, cos.shape, 1) % MASK_CHUNK) < (MASK_CHUNK // 2)
        return [_rope(u[:, s * LANES:(s + 1) * LANES], cos, sin, lane_lo) for s in range(n // LANES)]

    @pl.when(j == 0)
    def _():
        for s, r in enumerate(rotated()):
            qb_ref[:, s * LANES:(s + 1) * LANES] = (r * q_scale).astype(BF16)

    @pl.when(j == 1)
    def _():
        for s, r in enumerate(rotated()):
            kf_ref[:, s * LANES:(s + 1) * LANES] = r
            kb_ref[:, s * LANES:(s + 1) * LANES] = r.astype(BF16)

    @pl.when(j == 2)
    def _():
        u = _dot(h_sc[...], w_ref[...])
        vf_ref[...] = u
        vb_ref[...] = (u.T if v_transposed else u).astype(BF16)

    @pl.when(j == 3)
    def _():
        zs_ref[...] = _silu(_dot(h_sc[...], w_ref[...])).astype(BF16)


def _diff_in(x, mod, nw, w, cos_t, sin_t, *, tm, v_transposed, q_scale):
    b, t, d = x.shape
    nb = w.shape[1] // 4
    assert t % tm == 0
    kern = functools.partial(_diff_in_kernel, v_transposed=v_transposed, q_scale=q_scale)
    tile = pl.BlockSpec((None, tm, nb), lambda bi, ti, j: (bi, ti, 0))
    if v_transposed:
        vb_spec = pl.BlockSpec((None, nb, tm), lambda bi, ti, j: (bi, 0, ti))
        vb_shape = jax.ShapeDtypeStruct((b, nb, t), BF16)
    else:
        vb_spec, vb_shape = tile, jax.ShapeDtypeStruct((b, t, nb), BF16)
    return pl.pallas_call(
        kern,
        grid=(b, t // tm, 4),
        in_specs=[pl.BlockSpec((None, tm, d), lambda bi, ti, j: (bi, ti, 0)),
                  pl.BlockSpec((None, 3, d), lambda bi, ti, j: (bi, 0, 0)),
                  pl.BlockSpec((1, d), lambda bi, ti, j: (0, 0)),
                  pl.BlockSpec((d, nb), lambda bi, ti, j: (0, j)),
                  pl.BlockSpec((tm, LANES), lambda bi, ti, j: (ti, 0)),
                  pl.BlockSpec((tm, LANES), lambda bi, ti, j: (ti, 0))],
        out_specs=[tile, tile, tile, tile, vb_spec, tile],
        out_shape=[jax.ShapeDtypeStruct((b, t, nb), BF16),
                   jax.ShapeDtypeStruct((b, t, nb), BF16),
                   jax.ShapeDtypeStruct((b, t, nb), F32),
                   jax.ShapeDtypeStruct((b, t, nb), F32),
                   vb_shape,
                   jax.ShapeDtypeStruct((b, t, nb), BF16)],
        scratch_shapes=[pltpu.VMEM((tm, d), BF16)],
        compiler_params=_cparams("parallel", "arbitrary", "arbitrary"),
    )(x, mod, nw, w, cos_t, sin_t)


def _lambda(lam_ref, lam_init):
    l1 = jnp.sum(lam_ref[0:1, :] * lam_ref[1:2, :], axis=-1, keepdims=True)
    l2 = jnp.sum(lam_ref[2:3, :] * lam_ref[3:4, :], axis=-1, keepdims=True)
    return jnp.exp(l1) - jnp.exp(l2) + lam_init


def _attn_prompt_kernel(q_ref, k_ref, vt_ref, z_ref, lam_ref, sw_ref, o_ref,
                        m_sc, l_sc, acc_sc, *, lam_init, tk):
    qi = pl.program_id(2)
    tq, dh = q_ref.shape
    dq = dh // 2
    nsub = tq // tk
    chains = [(sub, comp) for sub in range(nsub) for comp in range(2)]
    qz = []
    for sub, comp in chains:
        q = q_ref[sub * tk:(sub + 1) * tk, :]
        lane = lax.broadcasted_iota(jnp.int32, q.shape, 1)
        qz.append(jnp.where((lane < dq) if comp == 0 else (lane >= dq), q, jnp.zeros_like(q)))

    m_sc[...] = jnp.full(m_sc.shape, -jnp.inf, F32)
    l_sc[...] = jnp.zeros(l_sc.shape, F32)
    acc_sc[...] = jnp.zeros(acc_sc.shape, F32)

    def step(kb, active, masked_sub):
        ks = k_ref[pl.ds(pl.multiple_of(kb * tk, tk), tk), :]
        vt = vt_ref[:, pl.ds(pl.multiple_of(kb * tk, tk), tk)]
        s = [_dot_nt(ks, qz[c]) for c in active]
        if masked_sub is not None:
            visible = (lax.broadcasted_iota(jnp.int32, (tk, tk), 0) // MASK_CHUNK
                       <= lax.broadcasted_iota(jnp.int32, (tk, tk), 1) // MASK_CHUNK)
            s = [jnp.where(visible, si, -jnp.inf) if chains[c][0] == masked_sub else si
                 for c, si in zip(active, s)]
        m_old = [m_sc[c] for c in active]
        m_new = [jnp.maximum(mo, jnp.max(si, axis=0, keepdims=True)) for mo, si in zip(m_old, s)]
        p = [jnp.exp(si - mn) for si, mn in zip(s, m_new)]
        alpha = [jnp.exp(mo - mn) for mo, mn in zip(m_old, m_new)]
        for c, mn, al, pi in zip(active, m_new, alpha, p):
            m_sc[c] = mn
            l_sc[c] = al * l_sc[c] + jnp.sum(pi, axis=0, keepdims=True)
        pv = [_dot(vt, pi.astype(BF16)) for pi in p]
        for c, al, pvi in zip(active, alpha, pv):
            acc_sc[c] = al * acc_sc[c] + pvi

    everyone = list(range(len(chains)))
    n_full = (qi * tq) // tk
    lax.fori_loop(0, n_full, lambda kb, cry: (step(kb, everyone, None), cry)[1], 0)
    for d in range(nsub):
        step(n_full + d, [c for c in everyone if chains[c][0] >= d], d)

    lam = _lambda(lam_ref, lam_init)
    for sub in range(nsub):
        c1, c2 = 2 * sub, 2 * sub + 1
        o_t = acc_sc[c1] / l_sc[c1] - lam * (acc_sc[c2] / l_sc[c2])
        o = o_t.T
        o = o * lax.rsqrt(jnp.mean(o * o, axis=-1, keepdims=True) + RMS_EPS) * sw_ref[...] * (1.0 - lam_init)
        rows = slice(sub * tk, (sub + 1) * tk)
        o_ref[rows, :] = (o * z_ref[rows, :].astype(F32)).astype(BF16)


def _attn_prompt(qb, kb, vt, zs, lam_vecs, subln, *, heads, lam_init, tq, tk):
    b, t, n = qb.shape
    dh = n // heads
    assert t % tq == 0 and tq % tk == 0 and tk % MASK_CHUNK == 0
    nchains = 2 * (tq // tk)
    kern = functools.partial(_attn_prompt_kernel, lam_init=lam_init, tk=tk)
    return pl.pallas_call(
        kern,
        grid=(b, heads, t // tq),
        in_specs=[pl.BlockSpec((None, tq, dh), lambda bi, hi, qi: (bi, qi, hi)),
                  pl.BlockSpec((None, t, dh), lambda bi, hi, qi: (bi, 0, hi)),
                  pl.BlockSpec((None, dh, t), lambda bi, hi, qi: (bi, hi, 0)),
                  pl.BlockSpec((None, tq, dh), lambda bi, hi, qi: (bi, qi, hi)),
                  pl.BlockSpec(lam_vecs.shape, lambda bi, hi, qi: (0, 0)),
                  pl.BlockSpec((1, dh), lambda bi, hi, qi: (0, 0))],
        out_specs=pl.BlockSpec((None, tq, dh), lambda bi, hi, qi: (bi, qi, hi)),
        out_shape=jax.ShapeDtypeStruct((b, t, n), BF16),
        scratch_shapes=[pltpu.VMEM((nchains, 1, tk), F32), pltpu.VMEM((nchains, 1, tk), F32),
                        pltpu.VMEM((nchains, dh, tk), F32)],
        compiler_params=_cparams("parallel", "parallel", "arbitrary"),
    )(qb, kb, vt, zs, lam_vecs, subln)


def _attn_sample_kernel(q_ref, kn_ref, vn_ref, ck_ref, cv_ref, z_ref, lam_ref, sw_ref, o_ref,
                        m_sc, l_sc, acc_sc, *, heads, lam_init):
    si = pl.program_id(1)
    ns = pl.num_programs(1)
    tq, n = q_ref.shape
    dh = n // heads
    dq = dh // 2

    @pl.when(si == 0)
    def _():
        m_sc[...] = jnp.full(m_sc.shape, -jnp.inf, F32)
        l_sc[...] = jnp.zeros(l_sc.shape, F32)
        acc_sc[...] = jnp.zeros(acc_sc.shape, F32)

    def update(hh, kblk, vblk):
        hs = slice(hh * dh, (hh + 1) * dh)
        q = q_ref[:, hs]
        lane = lax.broadcasted_iota(jnp.int32, q.shape, 1)
        q2 = jnp.concatenate([jnp.where(lane < dq, q, jnp.zeros_like(q)),
                              jnp.where(lane >= dq, q, jnp.zeros_like(q))], axis=0)
        s = _dot_nt(q2, kblk)
        m_old = m_sc[hh]
        m_new = jnp.maximum(m_old, jnp.max(s, axis=-1, keepdims=True))
        alpha = jnp.exp(m_old - m_new)
        p = jnp.exp(s - m_new)
        l_sc[hh] = alpha * l_sc[hh] + jnp.sum(p, axis=-1, keepdims=True)
        acc_sc[hh] = alpha * acc_sc[hh] + _dot(p.astype(BF16), vblk)
        m_sc[hh] = m_new

    for hh in range(heads):
        hs = slice(hh * dh, (hh + 1) * dh)
        update(hh, ck_ref[:, hs].astype(BF16), cv_ref[:, hs].astype(BF16))

    @pl.when(si == ns - 1)
    def _():
        lam = _lambda(lam_ref, lam_init)
        for hh in range(heads):
            hs = slice(hh * dh, (hh + 1) * dh)
            update(hh, kn_ref[:, hs], vn_ref[:, hs])
            on = acc_sc[hh] / l_sc[hh]
            o = on[:tq] - lam * on[tq:]
            o = o * lax.rsqrt(jnp.mean(o * o, axis=-1, keepdims=True) + RMS_EPS) * sw_ref[...] * (1.0 - lam_init)
            o_ref[:, hs] = (o * z_ref[:, hs].astype(F32)).astype(BF16)


def _attn_sample(qb, kb, vb, cache_k, cache_v, zs, lam_vecs, subln, *, heads, lam_init, tkv):
    b, tq, n = qb.shape
    past = cache_k.shape[1]
    dh = n // heads
    assert past % tkv == 0
    kern = functools.partial(_attn_sample_kernel, heads=heads, lam_init=lam_init)
    new = pl.BlockSpec((None, tq, n), lambda bi, si: (bi, 0, 0))
    old = pl.BlockSpec((None, tkv, n), lambda bi, si: (bi, si, 0))
    return pl.pallas_call(
        kern,
        grid=(b, past // tkv),
        in_specs=[new, new, new, old, old, new,
                  pl.BlockSpec(lam_vecs.shape, lambda bi, si: (0, 0)),
                  pl.BlockSpec((1, dh), lambda bi, si: (0, 0))],
        out_specs=new,
        out_shape=jax.ShapeDtypeStruct((b, tq, n), BF16),
        scratch_shapes=[pltpu.VMEM((heads, 2 * tq, 1), F32), pltpu.VMEM((heads, 2 * tq, 1), F32),
                        pltpu.VMEM((heads, 2 * tq, dh), F32)],
        compiler_params=_cparams("parallel", "arbitrary"),
    )(qb, kb, vb, cache_k, cache_v, zs, lam_vecs, subln)


def _out_kernel(o_ref, w_ref, x_ref, mod_ref, nw_ref, y_ref):
    y = _dot(o_ref[...], w_ref[...])
    yn = y * lax.rsqrt(jnp.mean(y * y, axis=-1, keepdims=True) + RMS_EPS) * nw_ref[...]
    y_ref[...] = x_ref[...] + mod_ref[2:3, :] * yn


def _out_proj(o, w, x, mod, nw, *, tm):
    b, t, d = x.shape
    n = o.shape[2]
    return pl.pallas_call(
        _out_kernel,
        grid=(b, t // tm),
        in_specs=[pl.BlockSpec((None, tm, n), lambda bi, ti: (bi, ti, 0)),
                  pl.BlockSpec((n, d), lambda bi, ti: (0, 0)),
                  pl.BlockSpec((None, tm, d), lambda bi, ti: (bi, ti, 0)),
                  pl.BlockSpec((None, 3, d), lambda bi, ti: (bi, 0, 0)),
                  pl.BlockSpec((1, d), lambda bi, ti: (0, 0))],
        out_specs=pl.BlockSpec((None, tm, d), lambda bi, ti: (bi, ti, 0)),
        out_shape=jax.ShapeDtypeStruct((b, t, d), F32),
        compiler_params=_cparams("parallel", "parallel"),
    )(o, w, x, mod, nw)


def _pick_tile(t, target):
    tile = min(t, target)
    assert t % tile == 0
    return tile


def _rope_tables(pos):
    half = MASK_CHUNK // 2
    inv = 1.0 / (ROPE_THETA ** (jnp.arange(half, dtype=F32) / half))
    ang = pos.astype(F32)[:, None] * inv[None, :]
    cos, sin = jnp.cos(ang), jnp.sin(ang)
    reps = LANES // MASK_CHUNK
    return (jnp.tile(jnp.concatenate([cos, cos], axis=1), (1, reps)),
            jnp.tile(jnp.concatenate([-sin, sin], axis=1), (1, reps)))


def _trunk(x, ada, pos, conv_bufs, gdn_states, past_k, past_v, p):
    b, t, d = x.shape
    depth = ada.shape[0]
    gdn_heads = p["a_log_gdn"].shape[1]
    gdn_dk = p["onorm_gdn"].shape[1]
    diff_heads = p["w_out_diff"].shape[1] // (2 * p["lam_q1"].shape[1])
    gdn_chunk = LANES
    tm = _pick_tile(t, 512)
    cos_t, sin_t = _rope_tables(pos)
    states, convs, ks, vs = [], [], [], []
    for i in range(depth):
        j = i // 2
        mod = ada[i]
        nw_pre = p["norm_pre"][i][None, :]
        nw_post = p["norm_post"][i][None, :]
        if i % 2 == 0:
            w_in = p["w_in_gdn"][j]
            n_main = 4 * gdn_heads * gdn_dk
            w_main = w_in[:, :n_main].astype(BF16)
            w_ab = jnp.pad(w_in[:, n_main:], ((0, 0), (0, LANES - 2 * gdn_heads)))
            a_log = jnp.pad(p["a_log_gdn"][j][None, :], ((0, 0), (0, LANES - gdn_heads)))
            dt_b = jnp.pad(p["dt_bias_gdn"][j][None, :], ((0, 0), (0, LANES - gdn_heads)))
            n_conv = 3 * gdn_heads * gdn_dk
            if conv_bufs is None:
                conv_init = jnp.zeros((b, SUBLANES, n_conv), F32)
                s0 = jnp.zeros((b, gdn_heads, gdn_dk, gdn_dk), F32)
            else:
                conv_init = jnp.pad(conv_bufs[j], ((0, 0), (SUBLANES - (CONV_WIDTH - 1), 0), (0, 0)))
                s0 = gdn_states[j]
            qkvz, gcol, cout = _gdn_in(x, mod, nw_pre, w_main, w_ab, p["conv_gdn"][j], conv_init, a_log, dt_b,
                                       heads=gdn_heads, dk=gdn_dk, tm=tm, chunk=min(gdn_chunk, tm))
            if t < gdn_chunk:
                front = gdn_chunk - t
                qkvz = jnp.pad(qkvz, ((0, 0), (front, 0), (0, 0)))
                gcol = jnp.pad(gcol, ((0, 0), (front, 0), (0, 0)))
            o, st = _gdn(qkvz, gcol, s0, p["onorm_gdn"][j][None, :], heads=gdn_heads, dk=gdn_dk,
                         tb=_pick_tile(qkvz.shape[1], 256), chunk=gdn_chunk)
            o = o[:, -t:]
            last = cout[:, :, SUBLANES - (CONV_WIDTH - 1):, :]
            convs.append(jnp.swapaxes(last, 1, 2).reshape(b, CONV_WIDTH - 1, n_conv))
            states.append(st)
            w_out = p["w_out_gdn"][j].astype(BF16)
        else:
            lam_init = 0.8 - 0.6 * math.exp(-0.3 * i)
            lam_vecs = jnp.stack([p["lam_q1"][j], p["lam_k1"][j], p["lam_q2"][j], p["lam_k2"][j]])
            dq = lam_vecs.shape[1]
            prompt = past_k is None
            qb, kb, kf, vf, vb, zs = _diff_in(x, mod, nw_pre, p["w_in_diff"][j].astype(BF16), cos_t, sin_t,
                                              tm=tm, v_transposed=prompt, q_scale=dq ** -0.5)
            subln = p["subln_diff"][j][None, :]
            if prompt:
                o = _attn_prompt(qb, kb, vb, zs, lam_vecs, subln, heads=diff_heads, lam_init=lam_init,
                                 tq=_pick_tile(t, 1024), tk=_pick_tile(t, 256))
            else:
                ck = past_k[j].reshape(b, past_k.shape[2], -1)
                cv = past_v[j].reshape(b, past_v.shape[2], -1)
                o = _attn_sample(qb, kb, vb, ck, cv, zs, lam_vecs, subln, heads=diff_heads, lam_init=lam_init,
                                 tkv=_pick_tile(ck.shape[1], 1024))
            ks.append(kf.reshape(b, t, diff_heads, 2 * dq))
            vs.append(vf.reshape(b, t, diff_heads, 2 * dq))
            w_out = p["w_out_diff"][j].astype(BF16)
        x = _out_proj(o, w_out, x, mod, nw_post, tm=tm)
    return x, jnp.stack(states), jnp.stack(convs), jnp.stack(ks), jnp.stack(vs)


def kernel(x_prompt, x_sample, c_prompt, c_sample, state_gdn, cache_conv, cache_k, cache_v, norm_pre, norm_post, w_ada, b_ada, w_in_gdn, conv_gdn, a_log_gdn, dt_bias_gdn, onorm_gdn, w_out_gdn, w_in_diff, lam_q1, lam_k1, lam_q2, lam_k2, subln_diff, w_out_diff):
    p = {"norm_pre": norm_pre, "norm_post": norm_post,
         "w_in_gdn": w_in_gdn, "conv_gdn": conv_gdn, "a_log_gdn": a_log_gdn,
         "dt_bias_gdn": dt_bias_gdn, "onorm_gdn": onorm_gdn, "w_out_gdn": w_out_gdn,
         "w_in_diff": w_in_diff, "lam_q1": lam_q1, "lam_k1": lam_k1, "lam_q2": lam_q2,
         "lam_k2": lam_k2, "subln_diff": subln_diff, "w_out_diff": w_out_diff}
    bp, tp, d = x_prompt.shape
    bs, ts, _ = x_sample.shape
    past = cache_k.shape[2]
    depth = w_ada.shape[0]
    ada = _ada(jnp.concatenate([c_prompt, c_sample], axis=0), w_ada, b_ada)
    ada = ada.reshape(depth, bp + bs, 3, d)
    y_p, st_p, conv_p, k_p, v_p = _trunk(x_prompt, ada[:, :bp], jnp.arange(tp), None, None, None, None, p)
    y_s, st_s, conv_s, k_s, v_s = _trunk(x_sample, ada[:, bp:], past + jnp.arange(ts), cache_conv, state_gdn,
                                         cache_k, cache_v, p)
    return (y_p, y_s, st_p, conv_p, k_p, v_p, st_s, conv_s, k_s, v_s)
```

```python
import functools
import math

import jax
import jax.numpy as jnp
from jax import lax
from jax.experimental import pallas as pl
from jax.experimental.pallas import tpu as pltpu

F32 = jnp.float32
BF16 = jnp.bfloat16
HIGHEST = lax.Precision.HIGHEST

RMS_EPS = 1e-6
L2_EPS = 1e-6
ROPE_THETA = 10000.0
CONV_WIDTH = 4
MASK_CHUNK = 64
LANES = 128
SUBLANES = 8
VMEM_LIMIT = 56 * 1024 * 1024
LOG2E = 1.4426950408889634
REF_SLACK = 1.01
MIN_NORMALISER = 1e-30

NT_DIMS = (((1,), (1,)), ((), ()))


def _cparams(*sem):
    return pltpu.CompilerParams(dimension_semantics=sem, vmem_limit_bytes=VMEM_LIMIT)


def _silu(x):
    return x * jax.nn.sigmoid(x)


def _dot(a, b):
    return jnp.dot(a, b, preferred_element_type=F32)


def _dot_nt(a, b):
    return lax.dot_general(a, b, NT_DIMS, preferred_element_type=F32)


def _ada_kernel(c_ref, w_ref, b_ref, o_ref):
    act = _silu(c_ref[...])
    o_ref[...] = jnp.dot(act, w_ref[...], precision=HIGHEST, preferred_element_type=F32) + b_ref[...]


def _ada(c_all, w_ada, b_ada):
    depth, d, n = w_ada.shape
    bc = c_all.shape[0]
    tn = 512
    return pl.pallas_call(
        _ada_kernel,
        grid=(depth, n // tn),
        in_specs=[pl.BlockSpec((bc, d), lambda i, j: (0, 0)),
                  pl.BlockSpec((None, d, tn), lambda i, j: (i, 0, j)),
                  pl.BlockSpec((None, 1, tn), lambda i, j: (i, 0, j))],
        out_specs=pl.BlockSpec((None, bc, tn), lambda i, j: (i, 0, j)),
        out_shape=jax.ShapeDtypeStruct((depth, bc, n), F32),
        compiler_params=_cparams("parallel", "parallel"),
    )(c_all, w_ada, b_ada.reshape(depth, 1, n))


def _prenorm(x_ref, mod_ref, nw_ref):
    x = x_ref[...]
    y = x * lax.rsqrt(jnp.mean(x * x, axis=-1, keepdims=True) + RMS_EPS) * nw_ref[...]
    return y * (1.0 + mod_ref[1:2, :]) + mod_ref[0:1, :]


def _conv_silu(u, tail, cw):
    def taps(a):
        y = a * cw[CONV_WIDTH - 1:CONV_WIDTH, :]
        for k in range(1, CONV_WIDTH):
            y = y + pltpu.roll(a, k, axis=0) * cw[CONV_WIDTH - 1 - k:CONV_WIDTH - k, :]
        return y
    y = taps(u)
    head = taps(jnp.concatenate([tail, u[0:SUBLANES]], axis=0))[SUBLANES:2 * SUBLANES]
    y = jnp.concatenate([head, y[SUBLANES:]], axis=0)
    return _silu(y)


def _gdn_in_kernel(x_ref, mod_ref, nw_ref, w_ref, wab_ref, cw_ref, cinit_ref, alog_ref, dtb_ref,
                   qkvz_ref, gcol_ref, cout_ref, h_sc, tail_sc, *, heads, dk, chunk):
    t = pl.program_id(1)
    j = pl.program_id(2)
    tm = x_ref.shape[0]

    @pl.when(j == 0)
    def _():
        h = _prenorm(x_ref, mod_ref, nw_ref)
        h_sc[...] = h.astype(BF16)
        ab = jnp.dot(h, wab_ref[...], precision=HIGHEST, preferred_element_type=F32)
        g = -jnp.exp(alog_ref[...]) * jax.nn.softplus(ab + dtb_ref[...])
        row = lax.broadcasted_iota(jnp.int32, g.shape, 0) % chunk
        s = 1
        while s < chunk:
            g = g + jnp.where(row >= s, pltpu.roll(g, s, axis=0), 0.0)
            s *= 2
        lane = lax.broadcasted_iota(jnp.int32, g.shape, 1)
        gcol_ref[...] = jnp.where(lane < heads, g, jax.nn.sigmoid(ab))

    def conv_branch(jj):
        @pl.when(t == 0)
        def _():
            tail_sc[jj] = cinit_ref[...]
        u = _dot(h_sc[...], w_ref[...])
        y = _conv_silu(u, tail_sc[jj], cw_ref[...])
        tail_sc[jj] = u[tm - SUBLANES:tm]
        cout_ref[jj] = u[tm - SUBLANES:tm]
        return y

    def l2norm_store(y, scale):
        for hh in range(heads):
            ys = y[:, hh * dk:(hh + 1) * dk]
            n = jnp.sum(ys * ys, axis=-1, keepdims=True)
            qkvz_ref[:, hh * dk:(hh + 1) * dk] = (ys * (lax.rsqrt(n + L2_EPS) * scale)).astype(BF16)

    @pl.when(j == 0)
    def _():
        l2norm_store(conv_branch(0), dk ** -0.5)

    @pl.when(j == 1)
    def _():
        l2norm_store(conv_branch(1), 1.0)

    @pl.when(j == 2)
    def _():
        qkvz_ref[...] = conv_branch(2).astype(BF16)

    @pl.when(j == 3)
    def _():
        qkvz_ref[...] = _silu(_dot(h_sc[...], w_ref[...])).astype(BF16)


def _gdn_in(x, mod, nw, w_main, w_ab, conv_w, conv_init, a_log, dt_bias, *, layer, heads, dk, tm, chunk):
    b, t, d = x.shape
    n = w_main.shape[2]
    nb = n // 4
    assert nb == heads * dk and t % tm == 0 and tm % SUBLANES == 0 and tm % chunk == 0
    kern = functools.partial(_gdn_in_kernel, heads=heads, dk=dk, chunk=chunk)
    cj = lambda j: jnp.minimum(j, 2)
    return pl.pallas_call(
        kern,
        grid=(b, t // tm, 4),
        in_specs=[pl.BlockSpec((None, tm, d), lambda bi, ti, j: (bi, ti, 0)),
                  pl.BlockSpec((None, 3, d), lambda bi, ti, j: (bi, 0, 0)),
                  pl.BlockSpec((1, d), lambda bi, ti, j: (0, 0)),
                  pl.BlockSpec((None, d, nb), lambda bi, ti, j: (layer, 0, j)),
                  pl.BlockSpec((d, LANES), lambda bi, ti, j: (0, 0)),
                  pl.BlockSpec((CONV_WIDTH, nb), lambda bi, ti, j: (0, cj(j))),
                  pl.BlockSpec((None, SUBLANES, nb), lambda bi, ti, j: (bi, 0, cj(j))),
                  pl.BlockSpec((1, LANES), lambda bi, ti, j: (0, 0)),
                  pl.BlockSpec((1, LANES), lambda bi, ti, j: (0, 0))],
        out_specs=[pl.BlockSpec((None, tm, nb), lambda bi, ti, j: (bi, ti, j)),
                   pl.BlockSpec((None, tm, LANES), lambda bi, ti, j: (bi, ti, 0)),
                   pl.BlockSpec((None, 3, SUBLANES, nb), lambda bi, ti, j: (bi, 0, 0, 0))],
        out_shape=[jax.ShapeDtypeStruct((b, t, n), BF16),
                   jax.ShapeDtypeStruct((b, t, LANES), F32),
                   jax.ShapeDtypeStruct((b, 3, SUBLANES, nb), F32)],
        scratch_shapes=[pltpu.VMEM((tm, d), BF16), pltpu.VMEM((3, SUBLANES, nb), F32)],
        compiler_params=_cparams("parallel", "arbitrary", "arbitrary"),
    )(x, mod, nw, w_main, w_ab, conv_w, conv_init, a_log, dt_bias)


def _gdn_kernel(q_ref, k_ref, v_ref, z_ref, gcol_ref, s0_ref, ow_ref, o_ref, s_ref, *, heads, dk, chunk):
    t = pl.program_id(1)
    tb = q_ref.shape[0]
    c = chunk

    @pl.when(t == 0)
    def _():
        s_ref[...] = s0_ref[...]

    ii = lax.broadcasted_iota(jnp.int32, (c, c), 0)
    jj = lax.broadcasted_iota(jnp.int32, (c, c), 1)
    incl = ii >= jj
    strict = ii > jj
    eye = (ii == jj).astype(F32)
    off_masks = []
    s = 1
    while s < c:
        off_masks.append((ii // (2 * s) == jj // (2 * s)) & (ii // s != jj // s) & strict)
        s *= 2

    def chunk_body(ci, carry):
        r = pl.ds(pl.multiple_of(ci * c, c), c)
        gall = gcol_ref[r, :]
        hr = range(heads)
        hsl = [slice(hh * dk, (hh + 1) * dk) for hh in hr]
        g = [gall[:, hh:hh + 1] for hh in hr]
        beta = [gall[:, heads + hh:heads + hh + 1] for hh in hr]
        kbf = [k_ref[r, hs] for hs in hsl]
        qbf = [q_ref[r, hs] for hs in hsl]
        kf = [x.astype(F32) for x in kbf]
        kb = [kf[hh] * beta[hh] for hh in hr]
        gb = [jnp.broadcast_to(g[hh], (c, c)) for hh in hr]
        decay = [jnp.exp(jnp.where(incl, x - x.T, -jnp.inf)) for x in gb]
        a = [jnp.where(strict, _dot_nt(kb[hh].astype(BF16), kbf[hh]) * decay[hh], 0.0) for hh in hr]
        qk = [(_dot_nt(qbf[hh], kbf[hh]) * decay[hh]).astype(BF16) for hh in hr]
        p = [eye - jnp.where(off_masks[0], a[hh], 0.0) for hh in hr]
        for off in off_masks[1:]:
            pb = [x.astype(BF16) for x in p]
            x = [_dot(jnp.where(off, a[hh], 0.0).astype(BF16), pb[hh]).astype(BF16) for hh in hr]
            p = [p[hh] - _dot(pb[hh], x[hh]) for hh in hr]
        eg = [jnp.exp(x) for x in g]
        rhs = [jnp.concatenate([v_ref[r, hsl[hh]].astype(F32) * beta[hh], kb[hh] * eg[hh]], axis=1).astype(BF16)
               for hh in hr]
        sol = [_dot(p[hh].astype(BF16), rhs[hh]) for hh in hr]
        s_old = [s_ref[hh] for hh in hr]
        wq = [jnp.concatenate([sol[hh][:, dk:].astype(BF16), (qbf[hh].astype(F32) * eg[hh]).astype(BF16)], axis=0)
              for hh in hr]
        r1 = [_dot(wq[hh], s_old[hh].astype(BF16)) for hh in hr]
        ub = [(sol[hh][:, :dk] - r1[hh][:c]).astype(BF16) for hh in hr]
        g_last = [x[c - 1:c, :] for x in g]
        kd_t = [(kf[hh] * jnp.exp(g_last[hh] - g[hh])).T.astype(BF16) for hh in hr]
        for hh in hr:
            s_ref[hh] = s_old[hh] * jnp.exp(g_last[hh]) + _dot(kd_t[hh], ub[hh])
        o = [r1[hh][c:] + _dot(qk[hh], ub[hh]) for hh in hr]
        for hh in hr:
            on = o[hh] * lax.rsqrt(jnp.mean(o[hh] * o[hh], axis=-1, keepdims=True) + RMS_EPS) * ow_ref[...]
            o_ref[r, hsl[hh]] = (on * z_ref[r, hsl[hh]].astype(F32)).astype(BF16)
        return carry

    lax.fori_loop(0, tb // c, chunk_body, 0)


def _gdn(qkvz, gcol, s0, onorm, *, heads, dk, tb, chunk):
    b, t, n = qkvz.shape
    nb = n // 4
    assert t % tb == 0 and tb % chunk == 0 and nb == heads * dk
    kern = functools.partial(_gdn_kernel, heads=heads, dk=dk, chunk=chunk)
    col = lambda jcol: pl.BlockSpec((None, tb, nb), lambda bi, ti: (bi, ti, jcol))
    return pl.pallas_call(
        kern,
        grid=(b, t // tb),
        in_specs=[col(0), col(1), col(2), col(3),
                  pl.BlockSpec((None, tb, LANES), lambda bi, ti: (bi, ti, 0)),
                  pl.BlockSpec((None, heads, dk, dk), lambda bi, ti: (bi, 0, 0, 0)),
                  pl.BlockSpec((1, dk), lambda bi, ti: (0, 0))],
        out_specs=[pl.BlockSpec((None, tb, nb), lambda bi, ti: (bi, ti, 0)),
                   pl.BlockSpec((None, heads, dk, dk), lambda bi, ti: (bi, 0, 0, 0))],
        out_shape=[jax.ShapeDtypeStruct((b, t, nb), BF16),
                   jax.ShapeDtypeStruct((b, heads, dk, dk), F32)],
        compiler_params=_cparams("parallel", "arbitrary"),
    )(qkvz, qkvz, qkvz, qkvz, gcol, s0, onorm)


def _rope(x, cos, sin_signed, lane_lo):
    half = MASK_CHUNK // 2
    swapped = jnp.where(lane_lo, pltpu.roll(x, LANES - half, axis=1), pltpu.roll(x, half, axis=1))
    return x * cos + swapped * sin_signed


def _diff_in_kernel(x_ref, mod_ref, nw_ref, w_ref, cos_ref, sin_ref, k_all_ref, v_all_ref,
                    qb_ref, kb_ref, kf_ref, vf_ref, vb_ref, zs_ref, h_sc, *, v_transposed, q_scale):
    del k_all_ref, v_all_ref
    j = pl.program_id(2)
    n = w_ref.shape[1]

    @pl.when(j == 0)
    def _():
        h_sc[...] = _prenorm(x_ref, mod_ref, nw_ref).astype(BF16)

    def rotated():
        u = _dot(h_sc[...], w_ref[...])
        cos = cos_ref[...]
        sin = sin_ref[...]
        lane_lo = (lax.broadcasted_iota(jnp.int32, cos.shape, 1) % MASK_CHUNK) < (MASK_CHUNK // 2)
        return [_rope(u[:, s * LANES:(s + 1) * LANES], cos, sin, lane_lo) for s in range(n // LANES)]

    @pl.when(j == 0)
    def _():
        for s, r in enumerate(rotated()):
            qb_ref[:, s * LANES:(s + 1) * LANES] = (r * q_scale).astype(BF16)

    @pl.when(j == 1)
    def _():
        for s, r in enumerate(rotated()):
            kf_ref[:, s * LANES:(s + 1) * LANES] = r
            kb_ref[:, s * LANES:(s + 1) * LANES] = r.astype(BF16)

    @pl.when(j == 2)
    def _():
        u = _dot(h_sc[...], w_ref[...])
        vf_ref[...] = u
        vb_ref[...] = (u.T if v_transposed else u).astype(BF16)

    @pl.when(j == 3)
    def _():
        zs_ref[...] = _silu(_dot(h_sc[...], w_ref[...])).astype(BF16)


def _diff_in(x, mod, nw, w, cos_t, sin_t, k_all, v_all, *, layer, tm, v_transposed, q_scale):
    b, t, d = x.shape
    nb = w.shape[2] // 4
    assert t % tm == 0
    kern = functools.partial(_diff_in_kernel, v_transposed=v_transposed, q_scale=q_scale)
    tile = pl.BlockSpec((None, tm, nb), lambda bi, ti, j: (bi, ti, 0))
    cache_tile = pl.BlockSpec((None, None, tm, nb), lambda bi, ti, j: (layer, bi, ti, 0))
    untouched = pl.BlockSpec(memory_space=pl.ANY)
    if v_transposed:
        vb_spec = pl.BlockSpec((None, nb, tm), lambda bi, ti, j: (bi, 0, ti))
        vb_shape = jax.ShapeDtypeStruct((b, nb, t), BF16)
    else:
        vb_spec, vb_shape = tile, jax.ShapeDtypeStruct((b, t, nb), BF16)
    return pl.pallas_call(
        kern,
        grid=(b, t // tm, 4),
        in_specs=[pl.BlockSpec((None, tm, d), lambda bi, ti, j: (bi, ti, 0)),
                  pl.BlockSpec((None, 3, d), lambda bi, ti, j: (bi, 0, 0)),
                  pl.BlockSpec((1, d), lambda bi, ti, j: (0, 0)),
                  pl.BlockSpec((None, d, nb), lambda bi, ti, j: (layer, 0, j)),
                  pl.BlockSpec((tm, LANES), lambda bi, ti, j: (ti, 0)),
                  pl.BlockSpec((tm, LANES), lambda bi, ti, j: (ti, 0)),
                  untouched, untouched],
        out_specs=[tile, tile, cache_tile, cache_tile, vb_spec, tile],
        out_shape=[jax.ShapeDtypeStruct((b, t, nb), BF16),
                   jax.ShapeDtypeStruct((b, t, nb), BF16),
                   jax.ShapeDtypeStruct(k_all.shape, F32),
                   jax.ShapeDtypeStruct(v_all.shape, F32),
                   vb_shape,
                   jax.ShapeDtypeStruct((b, t, nb), BF16)],
        input_output_aliases={6: 2, 7: 3},
        scratch_shapes=[pltpu.VMEM((tm, d), BF16)],
        compiler_params=_cparams("parallel", "arbitrary", "arbitrary"),
    )(x, mod, nw, w, cos_t, sin_t, k_all, v_all)


def _lambda(lam_ref, lam_init):
    l1 = jnp.sum(lam_ref[0:1, :] * lam_ref[1:2, :], axis=-1, keepdims=True)
    l2 = jnp.sum(lam_ref[2:3, :] * lam_ref[3:4, :], axis=-1, keepdims=True)
    return jnp.exp(l1) - jnp.exp(l2) + lam_init


def _attn_prompt_kernel(q_ref, k_ref, vt_ref, z_ref, lam_ref, sw_ref, o_ref,
                        kmax_sc, m_sc, l_sc, acc_sc, *, lam_init, tk):
    qi = pl.program_id(2)
    tq, dh = q_ref.shape
    t_keys = k_ref.shape[0]
    dq = dh // 2
    nsub = tq // tk
    chains = [(sub, comp) for sub in range(nsub) for comp in range(2)]
    everyone = list(range(len(chains)))
    n_full = (qi * tq) // tk
    qz = []
    for sub, comp in chains:
        q = q_ref[sub * tk:(sub + 1) * tk, :]
        lane = lax.broadcasted_iota(jnp.int32, q.shape, 1)
        qz.append(jnp.where((lane < dq) if comp == 0 else (lane >= dq), q, jnp.zeros_like(q)))

    @pl.when(qi == 0)
    def _():
        sel = (lax.broadcasted_iota(jnp.int32, (dh, LANES), 0) // dq
               == lax.broadcasted_iota(jnp.int32, (dh, LANES), 1)).astype(BF16)

        def body(i, mx):
            kblk = k_ref[pl.ds(pl.multiple_of(i * tk, tk), tk), :].astype(F32)
            n2 = _dot((kblk * kblk).astype(BF16), sel)
            return jnp.maximum(mx, jnp.max(n2, axis=0, keepdims=True))
        kmax_sc[...] = lax.fori_loop(0, t_keys // tk, body, jnp.zeros((1, LANES), F32))

    ones = jnp.ones((SUBLANES, dh), BF16)
    ref_rows = []
    for c, (sub, comp) in enumerate(chains):
        qf = qz[c].astype(F32)
        qn2 = _dot_nt(ones, (qf * qf).astype(BF16))[0:1, :]
        ref_rows.append(REF_SLACK * jnp.sqrt(qn2 * kmax_sc[:, comp:comp + 1]))

    visible = (lax.broadcasted_iota(jnp.int32, (tk, tk), 0) // MASK_CHUNK
               <= lax.broadcasted_iota(jnp.int32, (tk, tk), 1) // MASK_CHUNK)

    def blocks(kb):
        ks = k_ref[pl.ds(pl.multiple_of(kb * tk, tk), tk), :]
        vt = vt_ref[:, pl.ds(pl.multiple_of(kb * tk, tk), tk)]
        return ks, vt

    def fast_step(kb, active, masked_sub):
        ks, vt = blocks(kb)
        s = [_dot_nt(ks, qz[c]) for c in active]
        p = [jnp.exp2(si - ref_rows[c]) for c, si in zip(active, s)]
        p = [jnp.where(visible, pi, 0.0) if chains[c][0] == masked_sub else pi for c, pi in zip(active, p)]
        for c, pi in zip(active, p):
            l_sc[c] += jnp.sum(pi.reshape(tk // SUBLANES, SUBLANES, tk), axis=0)
        pv = [_dot(vt, pi.astype(BF16)) for pi in p]
        for c, pvi in zip(active, pv):
            acc_sc[c] += pvi

    def online_step(kb, active, masked_sub):
        ks, vt = blocks(kb)
        for c in active:
            s = _dot_nt(ks, qz[c])
            if chains[c][0] == masked_sub:
                s = jnp.where(visible, s, -jnp.inf)
            m_old = m_sc[c]
            m_new = jnp.maximum(m_old, jnp.max(s, axis=0, keepdims=True))
            alpha = jnp.exp2(m_old - m_new)
            p = jnp.exp2(s - m_new)
            l_sc[c, 0:1, :] = alpha * l_sc[c, 0:1, :] + jnp.sum(p, axis=0, keepdims=True)
            acc_sc[c] = alpha * acc_sc[c] + _dot(vt, p.astype(BF16))
            m_sc[c] = m_new

    def sweep(step):
        l_sc[...] = jnp.zeros(l_sc.shape, F32)
        acc_sc[...] = jnp.zeros(acc_sc.shape, F32)
        lax.fori_loop(0, n_full, lambda kb, cry: (step(kb, everyone, None), cry)[1], 0)
        for d in range(nsub):
            step(n_full + d, [c for c in everyone if chains[c][0] >= d], d)

    def normalisers():
        return [jnp.sum(l_sc[c], axis=0, keepdims=True) for c in everyone]

    sweep(fast_step)
    l_min = functools.reduce(jnp.minimum, normalisers())

    @pl.when(jnp.min(l_min) < MIN_NORMALISER)
    def _():
        m_sc[...] = jnp.full(m_sc.shape, -jnp.inf, F32)
        sweep(online_step)

    lam = _lambda(lam_ref, lam_init)
    l = normalisers()
    for sub in range(nsub):
        c1, c2 = 2 * sub, 2 * sub + 1
        o_t = acc_sc[c1] / l[c1] - lam * (acc_sc[c2] / l[c2])
        o = o_t.T
        o = o * lax.rsqrt(jnp.mean(o * o, axis=-1, keepdims=True) + RMS_EPS) * sw_ref[...] * (1.0 - lam_init)
        rows = slice(sub * tk, (sub + 1) * tk)
        o_ref[rows, :] = (o * z_ref[rows, :].astype(F32)).astype(BF16)


def _attn_prompt(qb, kb, vt, zs, lam_vecs, subln, *, heads, lam_init, tq, tk):
    b, t, n = qb.shape
    dh = n // heads
    assert t % tq == 0 and tq % tk == 0 and tk % MASK_CHUNK == 0
    nchains = 2 * (tq // tk)
    kern = functools.partial(_attn_prompt_kernel, lam_init=lam_init, tk=tk)
    return pl.pallas_call(
        kern,
        grid=(b, heads, t // tq),
        in_specs=[pl.BlockSpec((None, tq, dh), lambda bi, hi, qi: (bi, qi, hi)),
                  pl.BlockSpec((None, t, dh), lambda bi, hi, qi: (bi, 0, hi)),
                  pl.BlockSpec((None, dh, t), lambda bi, hi, qi: (bi, hi, 0)),
                  pl.BlockSpec((None, tq, dh), lambda bi, hi, qi: (bi, qi, hi)),
                  pl.BlockSpec(lam_vecs.shape, lambda bi, hi, qi: (0, 0)),
                  pl.BlockSpec((1, dh), lambda bi, hi, qi: (0, 0))],
        out_specs=pl.BlockSpec((None, tq, dh), lambda bi, hi, qi: (bi, qi, hi)),
        out_shape=jax.ShapeDtypeStruct((b, t, n), BF16),
        scratch_shapes=[pltpu.VMEM((1, LANES), F32), pltpu.VMEM((nchains, 1, tk), F32),
                        pltpu.VMEM((nchains, SUBLANES, tk), F32), pltpu.VMEM((nchains, dh, tk), F32)],
        compiler_params=_cparams("parallel", "arbitrary", "arbitrary"),
    )(qb, kb, vt, zs, lam_vecs, subln)


def _attn_sample_kernel(q_ref, kn_ref, vn_ref, ck_ref, cv_ref, z_ref, lam_ref, sw_ref, o_ref,
                        m_sc, l_sc, acc_sc, *, heads, lam_init):
    si = pl.program_id(1)
    ns = pl.num_programs(1)
    tq, n = q_ref.shape
    dh = n // heads
    dq = dh // 2

    @pl.when(si == 0)
    def _():
        m_sc[...] = jnp.full(m_sc.shape, -jnp.inf, F32)
        l_sc[...] = jnp.zeros(l_sc.shape, F32)
        acc_sc[...] = jnp.zeros(acc_sc.shape, F32)

    def update(hh, kblk, vblk):
        hs = slice(hh * dh, (hh + 1) * dh)
        q = q_ref[:, hs]
        lane = lax.broadcasted_iota(jnp.int32, q.shape, 1)
        q2 = jnp.concatenate([jnp.where(lane < dq, q, jnp.zeros_like(q)),
                              jnp.where(lane >= dq, q, jnp.zeros_like(q))], axis=0)
        s = _dot_nt(q2, kblk)
        m_old = m_sc[hh]
        m_new = jnp.maximum(m_old, jnp.max(s, axis=-1, keepdims=True))
        alpha = jnp.exp2(m_old - m_new)
        p = jnp.exp2(s - m_new)
        l_sc[hh] = alpha * l_sc[hh] + jnp.sum(p, axis=-1, keepdims=True)
        acc_sc[hh] = alpha * acc_sc[hh] + _dot(p.astype(BF16), vblk)
        m_sc[hh] = m_new

    for hh in range(heads):
        hs = slice(hh * dh, (hh + 1) * dh)
        update(hh, ck_ref[:, hs].astype(BF16), cv_ref[:, hs].astype(BF16))

    @pl.when(si == ns - 1)
    def _():
        lam = _lambda(lam_ref, lam_init)
        for hh in range(heads):
            hs = slice(hh * dh, (hh + 1) * dh)
            update(hh, kn_ref[:, hs], vn_ref[:, hs])
            on = acc_sc[hh] / l_sc[hh]
            o = on[:tq] - lam * on[tq:]
            o = o * lax.rsqrt(jnp.mean(o * o, axis=-1, keepdims=True) + RMS_EPS) * sw_ref[...] * (1.0 - lam_init)
            o_ref[:, hs] = (o * z_ref[:, hs].astype(F32)).astype(BF16)


def _attn_sample(qb, kb, vb, cache_k, cache_v, zs, lam_vecs, subln, *, layer, heads, lam_init, tkv):
    b, tq, n = qb.shape
    past = cache_k.shape[2]
    dh = n // heads
    assert past % tkv == 0
    kern = functools.partial(_attn_sample_kernel, heads=heads, lam_init=lam_init)
    new = pl.BlockSpec((None, tq, n), lambda bi, si: (bi, 0, 0))
    old = pl.BlockSpec((None, None, tkv, n), lambda bi, si: (layer, bi, si, 0))
    return pl.pallas_call(
        kern,
        grid=(b, past // tkv),
        in_specs=[new, new, new, old, old, new,
                  pl.BlockSpec(lam_vecs.shape, lambda bi, si: (0, 0)),
                  pl.BlockSpec((1, dh), lambda bi, si: (0, 0))],
        out_specs=new,
        out_shape=jax.ShapeDtypeStruct((b, tq, n), BF16),
        scratch_shapes=[pltpu.VMEM((heads, 2 * tq, 1), F32), pltpu.VMEM((heads, 2 * tq, 1), F32),
                        pltpu.VMEM((heads, 2 * tq, dh), F32)],
        compiler_params=_cparams("parallel", "arbitrary"),
    )(qb, kb, vb, cache_k, cache_v, zs, lam_vecs, subln)


def _out_kernel(o_ref, w_ref, x_ref, mod_ref, nw_ref, y_ref):
    y = _dot(o_ref[...], w_ref[...])
    yn = y * lax.rsqrt(jnp.mean(y * y, axis=-1, keepdims=True) + RMS_EPS) * nw_ref[...]
    y_ref[...] = x_ref[...] + mod_ref[2:3, :] * yn


def _out_proj(o, w, x, mod, nw, *, layer, tm):
    b, t, d = x.shape
    n = o.shape[2]
    return pl.pallas_call(
        _out_kernel,
        grid=(b, t // tm),
        in_specs=[pl.BlockSpec((None, tm, n), lambda bi, ti: (bi, ti, 0)),
                  pl.BlockSpec((None, n, d), lambda bi, ti: (layer, 0, 0)),
                  pl.BlockSpec((None, tm, d), lambda bi, ti: (bi, ti, 0)),
                  pl.BlockSpec((None, 3, d), lambda bi, ti: (bi, 0, 0)),
                  pl.BlockSpec((1, d), lambda bi, ti: (0, 0))],
        out_specs=pl.BlockSpec((None, tm, d), lambda bi, ti: (bi, ti, 0)),
        out_shape=jax.ShapeDtypeStruct((b, t, d), F32),
        compiler_params=_cparams("parallel", "parallel"),
    )(o, w, x, mod, nw)


def _pick_tile(t, target):
    tile = min(t, target)
    assert t % tile == 0
    return tile


def _rope_tables(pos):
    half = MASK_CHUNK // 2
    inv = 1.0 / (ROPE_THETA ** (jnp.arange(half, dtype=F32) / half))
    ang = pos.astype(F32)[:, None] * inv[None, :]
    cos, sin = jnp.cos(ang), jnp.sin(ang)
    reps = LANES // MASK_CHUNK
    return (jnp.tile(jnp.concatenate([cos, cos], axis=1), (1, reps)),
            jnp.tile(jnp.concatenate([-sin, sin], axis=1), (1, reps)))


def _trunk(x, ada, pos, conv_bufs, gdn_states, past_k, past_v, p):
    b, t, d = x.shape
    depth = ada.shape[0]
    gdn_heads = p["a_log_gdn"].shape[1]
    gdn_dk = p["onorm_gdn"].shape[1]
    dq = p["lam_q1"].shape[1]
    n_diff = p["w_out_diff"].shape[1]
    diff_heads = n_diff // (2 * dq)
    gdn_chunk = LANES
    n_conv = 3 * gdn_heads * gdn_dk
    tm = _pick_tile(t, 512)
    cos_t, sin_t = _rope_tables(pos)
    prompt = past_k is None
    k_all = jnp.zeros((p["w_in_diff"].shape[0], b, t, n_diff), F32)
    v_all = jnp.zeros_like(k_all)
    states, convs = [], []
    for i in range(depth):
        j = i // 2
        mod = ada[i]
        nw_pre = p["norm_pre"][i][None, :]
        nw_post = p["norm_post"][i][None, :]
        if i % 2 == 0:
            a_log = jnp.pad(p["a_log_gdn"][j][None, :], ((0, 0), (0, LANES - gdn_heads)))
            dt_b = jnp.pad(p["dt_bias_gdn"][j][None, :], ((0, 0), (0, LANES - gdn_heads)))
            if conv_bufs is None:
                conv_init = jnp.zeros((b, SUBLANES, n_conv), F32)
                s0 = jnp.zeros((b, gdn_heads, gdn_dk, gdn_dk), F32)
            else:
                conv_init = jnp.pad(conv_bufs[j], ((0, 0), (SUBLANES - (CONV_WIDTH - 1), 0), (0, 0)))
                s0 = gdn_states[j]
            qkvz, gcol, cout = _gdn_in(x, mod, nw_pre, p["w_main_gdn"], p["w_ab_gdn"][j], p["conv_gdn"][j],
                                       conv_init, a_log, dt_b, layer=j, heads=gdn_heads, dk=gdn_dk, tm=tm,
                                       chunk=min(gdn_chunk, tm))
            if t < gdn_chunk:
                front = gdn_chunk - t
                qkvz = jnp.pad(qkvz, ((0, 0), (front, 0), (0, 0)))
                gcol = jnp.pad(gcol, ((0, 0), (front, 0), (0, 0)))
            o, st = _gdn(qkvz, gcol, s0, p["onorm_gdn"][j][None, :], heads=gdn_heads, dk=gdn_dk,
                         tb=_pick_tile(qkvz.shape[1], 256), chunk=gdn_chunk)
            o = o[:, -t:]
            last = cout[:, :, SUBLANES - (CONV_WIDTH - 1):, :]
            convs.append(jnp.swapaxes(last, 1, 2).reshape(b, CONV_WIDTH - 1, n_conv))
            states.append(st)
            w_out = p["w_out_gdn"]
        else:
            lam_init = 0.8 - 0.6 * math.exp(-0.3 * i)
            lam_vecs = jnp.stack([p["lam_q1"][j], p["lam_k1"][j], p["lam_q2"][j], p["lam_k2"][j]])
            qb, kb, k_all, v_all, vb, zs = _diff_in(x, mod, nw_pre, p["w_in_diff"], cos_t, sin_t, k_all, v_all,
                                                    layer=j, tm=tm, v_transposed=prompt, q_scale=dq ** -0.5 * LOG2E)
            subln = p["subln_diff"][j][None, :]
            if prompt:
                o = _attn_prompt(qb, kb, vb, zs, lam_vecs, subln, heads=diff_heads, lam_init=lam_init,
                                 tq=_pick_tile(t, 1024), tk=_pick_tile(t, 256))
            else:
                o = _attn_sample(qb, kb, vb, past_k, past_v, zs, lam_vecs, subln, layer=j, heads=diff_heads,
                                 lam_init=lam_init, tkv=_pick_tile(past_k.shape[2], 1024))
            w_out = p["w_out_diff"]
        x = _out_proj(o, w_out, x, mod, nw_post, layer=j, tm=tm)
    cache_shape = k_all.shape[:3] + (diff_heads, 2 * dq)
    return x, jnp.stack(states), jnp.stack(convs), k_all.reshape(cache_shape), v_all.reshape(cache_shape)


def kernel(x_prompt, x_sample, c_prompt, c_sample, state_gdn, cache_conv, cache_k, cache_v, norm_pre, norm_post, w_ada, b_ada, w_in_gdn, conv_gdn, a_log_gdn, dt_bias_gdn, onorm_gdn, w_out_gdn, w_in_diff, lam_q1, lam_k1, lam_q2, lam_k2, subln_diff, w_out_diff):
    gdn_heads = a_log_gdn.shape[1]
    n_main = w_in_gdn.shape[2] - 2 * gdn_heads
    p = {"norm_pre": norm_pre, "norm_post": norm_post,
         "w_main_gdn": w_in_gdn[:, :, :n_main].astype(BF16),
         "w_ab_gdn": jnp.pad(w_in_gdn[:, :, n_main:], ((0, 0), (0, 0), (0, LANES - 2 * gdn_heads))),
         "conv_gdn": conv_gdn, "a_log_gdn": a_log_gdn,
         "dt_bias_gdn": dt_bias_gdn, "onorm_gdn": onorm_gdn, "w_out_gdn": w_out_gdn.astype(BF16),
         "w_in_diff": w_in_diff.astype(BF16), "lam_q1": lam_q1, "lam_k1": lam_k1, "lam_q2": lam_q2,
         "lam_k2": lam_k2, "subln_diff": subln_diff, "w_out_diff": w_out_diff.astype(BF16)}
    bp, tp, d = x_prompt.shape
    bs, ts, _ = x_sample.shape
    past = cache_k.shape[2]
    depth = w_ada.shape[0]
    ada = _ada(jnp.concatenate([c_prompt, c_sample], axis=0), w_ada, b_ada)
    ada = ada.reshape(depth, bp + bs, 3, d)
    y_p, st_p, conv_p, k_p, v_p = _trunk(x_prompt, ada[:, :bp], jnp.arange(tp), None, None, None, None, p)
    y_s, st_s, conv_s, k_s, v_s = _trunk(x_sample, ada[:, bp:], past + jnp.arange(ts), cache_conv, state_gdn,
                                         cache_k.reshape(cache_k.shape[:3] + (-1,)),
                                         cache_v.reshape(cache_v.shape[:3] + (-1,)), p)
    return (y_p, y_s, st_p, conv_p, k_p, v_p, st_s, conv_s, k_s, v_s)
```

```python
import functools
import math

import jax
import jax.numpy as jnp
from jax import lax
from jax.experimental import pallas as pl
from jax.experimental.pallas import tpu as pltpu

F32 = jnp.float32
BF16 = jnp.bfloat16
HIGHEST = lax.Precision.HIGHEST

RMS_EPS = 1e-6
L2_EPS = 1e-6
ROPE_THETA = 10000.0
CONV_WIDTH = 4
MASK_CHUNK = 64
LANES = 128
SUBLANES = 8
VMEM_LIMIT = 56 * 1024 * 1024
LOG2E = 1.4426950408889634
REF_SLACK = 1.01
MIN_NORMALISER = 1e-30

NT_DIMS = (((1,), (1,)), ((), ()))


def _cparams(*sem):
    return pltpu.CompilerParams(dimension_semantics=sem, vmem_limit_bytes=VMEM_LIMIT)


def _silu(x):
    return x * jax.nn.sigmoid(x)


def _dot(a, b):
    return jnp.dot(a, b, preferred_element_type=F32)


def _dot_nt(a, b):
    return lax.dot_general(a, b, NT_DIMS, preferred_element_type=F32)


def _ada_kernel(c_ref, w_ref, b_ref, o_ref):
    act = _silu(c_ref[...])
    o_ref[...] = jnp.dot(act, w_ref[...], precision=HIGHEST, preferred_element_type=F32) + b_ref[...]


def _ada(c_all, w_ada, b_ada):
    depth, d, n = w_ada.shape
    bc = c_all.shape[0]
    tn = 512
    return pl.pallas_call(
        _ada_kernel,
        grid=(depth, n // tn),
        in_specs=[pl.BlockSpec((bc, d), lambda i, j: (0, 0)),
                  pl.BlockSpec((None, d, tn), lambda i, j: (i, 0, j)),
                  pl.BlockSpec((None, 1, tn), lambda i, j: (i, 0, j))],
        out_specs=pl.BlockSpec((None, bc, tn), lambda i, j: (i, 0, j)),
        out_shape=jax.ShapeDtypeStruct((depth, bc, n), F32),
        compiler_params=_cparams("parallel", "parallel"),
    )(c_all, w_ada, b_ada.reshape(depth, 1, n))


def _prenorm(x_ref, mod_ref, nw_ref):
    x = x_ref[...]
    y = x * lax.rsqrt(jnp.mean(x * x, axis=-1, keepdims=True) + RMS_EPS) * nw_ref[...]
    return y * (1.0 + mod_ref[1:2, :]) + mod_ref[0:1, :]


def _conv_silu(u, tail, cw):
    def taps(a):
        y = a * cw[CONV_WIDTH - 1:CONV_WIDTH, :]
        for k in range(1, CONV_WIDTH):
            y = y + pltpu.roll(a, k, axis=0) * cw[CONV_WIDTH - 1 - k:CONV_WIDTH - k, :]
        return y
    y = taps(u)
    head = taps(jnp.concatenate([tail, u[0:SUBLANES]], axis=0))[SUBLANES:2 * SUBLANES]
    y = jnp.concatenate([head, y[SUBLANES:]], axis=0)
    return _silu(y)


def _gdn_in_kernel(x_ref, mod_ref, nw_ref, w_ref, wab_ref, cw_ref, cinit_ref, alog_ref, dtb_ref,
                   qkvz_ref, gcol_ref, cout_ref, h_sc, tail_sc, *, heads, dk, chunk):
    t = pl.program_id(1)
    j = pl.program_id(2)
    tm = x_ref.shape[0]

    @pl.when(j == 0)
    def _():
        h = _prenorm(x_ref, mod_ref, nw_ref)
        hb = h.astype(BF16)
        h_sc[...] = hb
        h_lo = (h - hb.astype(F32)).astype(BF16)
        hi_lo = _dot(hb, wab_ref[...])
        ab = hi_lo[:, :LANES] + hi_lo[:, LANES:] + _dot(h_lo, wab_ref[:, :LANES])
        g = -jnp.exp(alog_ref[...]) * jax.nn.softplus(ab + dtb_ref[...])
        row = lax.broadcasted_iota(jnp.int32, g.shape, 0) % chunk
        s = 1
        while s < chunk:
            g = g + jnp.where(row >= s, pltpu.roll(g, s, axis=0), 0.0)
            s *= 2
        lane = lax.broadcasted_iota(jnp.int32, g.shape, 1)
        gcol_ref[...] = jnp.where(lane < heads, g, jax.nn.sigmoid(ab))

    def conv_branch(jj):
        @pl.when(t == 0)
        def _():
            tail_sc[jj] = cinit_ref[...]
        u = _dot(h_sc[...], w_ref[...])
        y = _conv_silu(u, tail_sc[jj], cw_ref[...])
        tail_sc[jj] = u[tm - SUBLANES:tm]
        cout_ref[jj] = u[tm - SUBLANES:tm]
        return y

    def l2norm_store(y, scale):
        for hh in range(heads):
            ys = y[:, hh * dk:(hh + 1) * dk]
            n = jnp.sum(ys * ys, axis=-1, keepdims=True)
            qkvz_ref[:, hh * dk:(hh + 1) * dk] = (ys * (lax.rsqrt(n + L2_EPS) * scale)).astype(BF16)

    @pl.when(j == 0)
    def _():
        l2norm_store(conv_branch(0), dk ** -0.5)

    @pl.when(j == 1)
    def _():
        l2norm_store(conv_branch(1), 1.0)

    @pl.when(j == 2)
    def _():
        qkvz_ref[...] = conv_branch(2).astype(BF16)

    @pl.when(j == 3)
    def _():
        qkvz_ref[...] = _silu(_dot(h_sc[...], w_ref[...])).astype(BF16)


def _gdn_in(x, mod, nw, w_main, w_ab, conv_w, conv_init, a_log, dt_bias, *, layer, heads, dk, tm, chunk):
    b, t, d = x.shape
    n = w_main.shape[2]
    nb = n // 4
    assert nb == heads * dk and t % tm == 0 and tm % SUBLANES == 0 and tm % chunk == 0
    kern = functools.partial(_gdn_in_kernel, heads=heads, dk=dk, chunk=chunk)
    cj = lambda j: jnp.minimum(j, 2)
    return pl.pallas_call(
        kern,
        grid=(b, t // tm, 4),
        in_specs=[pl.BlockSpec((None, tm, d), lambda bi, ti, j: (bi, ti, 0)),
                  pl.BlockSpec((None, 3, d), lambda bi, ti, j: (bi, 0, 0)),
                  pl.BlockSpec((1, d), lambda bi, ti, j: (0, 0)),
                  pl.BlockSpec((None, d, nb), lambda bi, ti, j: (layer, 0, j)),
                  pl.BlockSpec((d, 2 * LANES), lambda bi, ti, j: (0, 0)),
                  pl.BlockSpec((CONV_WIDTH, nb), lambda bi, ti, j: (0, cj(j))),
                  pl.BlockSpec((None, SUBLANES, nb), lambda bi, ti, j: (bi, 0, cj(j))),
                  pl.BlockSpec((1, LANES), lambda bi, ti, j: (0, 0)),
                  pl.BlockSpec((1, LANES), lambda bi, ti, j: (0, 0))],
        out_specs=[pl.BlockSpec((None, tm, nb), lambda bi, ti, j: (bi, ti, j)),
                   pl.BlockSpec((None, tm, LANES), lambda bi, ti, j: (bi, ti, 0)),
                   pl.BlockSpec((None, 3, SUBLANES, nb), lambda bi, ti, j: (bi, 0, 0, 0))],
        out_shape=[jax.ShapeDtypeStruct((b, t, n), BF16),
                   jax.ShapeDtypeStruct((b, t, LANES), F32),
                   jax.ShapeDtypeStruct((b, 3, SUBLANES, nb), F32)],
        scratch_shapes=[pltpu.VMEM((tm, d), BF16), pltpu.VMEM((3, SUBLANES, nb), F32)],
        compiler_params=_cparams("parallel", "arbitrary", "arbitrary"),
    )(x, mod, nw, w_main, w_ab, conv_w, conv_init, a_log, dt_bias)


def _gdn_kernel(q_ref, k_ref, v_ref, z_ref, gcol_ref, s0_ref, ow_ref, o_ref, s_ref, *, heads, dk, chunk):
    t = pl.program_id(1)
    tb = q_ref.shape[0]
    c = chunk

    @pl.when(t == 0)
    def _():
        s_ref[...] = s0_ref[...]

    ii = lax.broadcasted_iota(jnp.int32, (c, c), 0)
    jj = lax.broadcasted_iota(jnp.int32, (c, c), 1)
    incl = ii >= jj
    strict = ii > jj
    eye = (ii == jj).astype(F32)
    off_masks = []
    s = 1
    while s < c:
        off_masks.append((ii // (2 * s) == jj // (2 * s)) & (ii // s != jj // s) & strict)
        s *= 2

    def chunk_body(ci, carry):
        r = pl.ds(pl.multiple_of(ci * c, c), c)
        gall = gcol_ref[r, :]
        hr = range(heads)
        hsl = [slice(hh * dk, (hh + 1) * dk) for hh in hr]
        g = [gall[:, hh:hh + 1] for hh in hr]
        beta = [gall[:, heads + hh:heads + hh + 1] for hh in hr]
        kbf = [k_ref[r, hs] for hs in hsl]
        qbf = [q_ref[r, hs] for hs in hsl]
        kf = [x.astype(F32) for x in kbf]
        kb = [kf[hh] * beta[hh] for hh in hr]
        gb = [jnp.broadcast_to(g[hh], (c, c)) for hh in hr]
        decay = [jnp.exp(jnp.where(incl, x - x.T, -jnp.inf)) for x in gb]
        a = [jnp.where(strict, _dot_nt(kb[hh].astype(BF16), kbf[hh]) * decay[hh], 0.0) for hh in hr]
        qk = [(_dot_nt(qbf[hh], kbf[hh]) * decay[hh]).astype(BF16) for hh in hr]
        p = [eye - jnp.where(off_masks[0], a[hh], 0.0) for hh in hr]
        for off in off_masks[1:]:
            pb = [x.astype(BF16) for x in p]
            x = [_dot(jnp.where(off, a[hh], 0.0).astype(BF16), pb[hh]).astype(BF16) for hh in hr]
            p = [p[hh] - _dot(pb[hh], x[hh]) for hh in hr]
        eg = [jnp.exp(x) for x in g]
        rhs = [jnp.concatenate([v_ref[r, hsl[hh]].astype(F32) * beta[hh], kb[hh] * eg[hh]], axis=1).astype(BF16)
               for hh in hr]
        sol = [_dot(p[hh].astype(BF16), rhs[hh]) for hh in hr]
        s_old = [s_ref[hh] for hh in hr]
        wq = [jnp.concatenate([sol[hh][:, dk:].astype(BF16), (qbf[hh].astype(F32) * eg[hh]).astype(BF16)], axis=0)
              for hh in hr]
        r1 = [_dot(wq[hh], s_old[hh].astype(BF16)) for hh in hr]
        ub = [(sol[hh][:, :dk] - r1[hh][:c]).astype(BF16) for hh in hr]
        g_last = [x[c - 1:c, :] for x in g]
        kd_t = [(kf[hh] * jnp.exp(g_last[hh] - g[hh])).T.astype(BF16) for hh in hr]
        for hh in hr:
            s_ref[hh] = s_old[hh] * jnp.exp(g_last[hh]) + _dot(kd_t[hh], ub[hh])
        o = [r1[hh][c:] + _dot(qk[hh], ub[hh]) for hh in hr]
        for hh in hr:
            on = o[hh] * lax.rsqrt(jnp.mean(o[hh] * o[hh], axis=-1, keepdims=True) + RMS_EPS) * ow_ref[...]
            o_ref[r, hsl[hh]] = (on * z_ref[r, hsl[hh]].astype(F32)).astype(BF16)
        return carry

    lax.fori_loop(0, tb // c, chunk_body, 0)


def _gdn(qkvz, gcol, s0, onorm, *, heads, dk, tb, chunk):
    b, t, n = qkvz.shape
    nb = n // 4
    assert t % tb == 0 and tb % chunk == 0 and nb == heads * dk
    kern = functools.partial(_gdn_kernel, heads=heads, dk=dk, chunk=chunk)
    col = lambda jcol: pl.BlockSpec((None, tb, nb), lambda bi, ti: (bi, ti, jcol))
    return pl.pallas_call(
        kern,
        grid=(b, t // tb),
        in_specs=[col(0), col(1), col(2), col(3),
                  pl.BlockSpec((None, tb, LANES), lambda bi, ti: (bi, ti, 0)),
                  pl.BlockSpec((None, heads, dk, dk), lambda bi, ti: (bi, 0, 0, 0)),
                  pl.BlockSpec((1, dk), lambda bi, ti: (0, 0))],
        out_specs=[pl.BlockSpec((None, tb, nb), lambda bi, ti: (bi, ti, 0)),
                   pl.BlockSpec((None, heads, dk, dk), lambda bi, ti: (bi, 0, 0, 0))],
        out_shape=[jax.ShapeDtypeStruct((b, t, nb), BF16),
                   jax.ShapeDtypeStruct((b, heads, dk, dk), F32)],
        compiler_params=_cparams("parallel", "arbitrary"),
    )(qkvz, qkvz, qkvz, qkvz, gcol, s0, onorm)


def _rope(x, cos, sin_signed, lane_lo):
    half = MASK_CHUNK // 2
    swapped = jnp.where(lane_lo, pltpu.roll(x, LANES - half, axis=1), pltpu.roll(x, half, axis=1))
    return x * cos + swapped * sin_signed


def _diff_in_kernel(x_ref, mod_ref, nw_ref, w_ref, cos_ref, sin_ref, k_all_ref, v_all_ref,
                    qb_ref, kb_ref, kf_ref, vf_ref, vb_ref, zs_ref, kn2_ref, h_sc, *, v_transposed, q_scale):
    del k_all_ref, v_all_ref
    j = pl.program_id(2)
    n = w_ref.shape[1]

    @pl.when(j == 0)
    def _():
        h_sc[...] = _prenorm(x_ref, mod_ref, nw_ref).astype(BF16)

    def rotated():
        u = _dot(h_sc[...], w_ref[...])
        cos = cos_ref[...]
        sin = sin_ref[...]
        lane_lo = (lax.broadcasted_iota(jnp.int32, cos.shape, 1) % MASK_CHUNK) < (MASK_CHUNK // 2)
        return [_rope(u[:, s * LANES:(s + 1) * LANES], cos, sin, lane_lo) for s in range(n // LANES)]

    @pl.when(j == 0)
    def _():
        for s, r in enumerate(rotated()):
            qb_ref[:, s * LANES:(s + 1) * LANES] = (r * q_scale).astype(BF16)

    @pl.when(j == 1)
    def _():
        rs = rotated()
        for s, r in enumerate(rs):
            kf_ref[:, s, :] = r
            kb_ref[:, s * LANES:(s + 1) * LANES] = r.astype(BF16)
        sq = jnp.concatenate([r * r for r in rs], axis=1).astype(BF16)
        group_sum = (lax.broadcasted_iota(jnp.int32, (n, LANES), 0) // MASK_CHUNK
                     == lax.broadcasted_iota(jnp.int32, (n, LANES), 1)).astype(BF16)
        kn2_ref[...] = jnp.max(_dot(sq, group_sum), axis=0, keepdims=True)

    @pl.when(j == 2)
    def _():
        u = _dot(h_sc[...], w_ref[...])
        for s in range(n // LANES):
            vf_ref[:, s, :] = u[:, s * LANES:(s + 1) * LANES]
        vb_ref[...] = (u.T if v_transposed else u).astype(BF16)

    @pl.when(j == 3)
    def _():
        zs_ref[...] = _silu(_dot(h_sc[...], w_ref[...])).astype(BF16)


def _diff_in(x, mod, nw, w, cos_t, sin_t, k_all, v_all, *, layer, tm, v_transposed, q_scale):
    b, t, d = x.shape
    nb = w.shape[2] // 4
    heads = k_all.shape[3]
    assert t % tm == 0 and k_all.shape[3:] == (nb // LANES, LANES)
    kern = functools.partial(_diff_in_kernel, v_transposed=v_transposed, q_scale=q_scale)
    tile = pl.BlockSpec((None, tm, nb), lambda bi, ti, j: (bi, ti, 0))
    cache_tile = pl.BlockSpec((None, None, tm, heads, LANES), lambda bi, ti, j: (layer, bi, ti, 0, 0))
    untouched = pl.BlockSpec(memory_space=pl.ANY)
    if v_transposed:
        vb_spec = pl.BlockSpec((None, nb, tm), lambda bi, ti, j: (bi, 0, ti))
        vb_shape = jax.ShapeDtypeStruct((b, nb, t), BF16)
    else:
        vb_spec, vb_shape = tile, jax.ShapeDtypeStruct((b, t, nb), BF16)
    return pl.pallas_call(
        kern,
        grid=(b, t // tm, 4),
        in_specs=[pl.BlockSpec((None, tm, d), lambda bi, ti, j: (bi, ti, 0)),
                  pl.BlockSpec((None, 3, d), lambda bi, ti, j: (bi, 0, 0)),
                  pl.BlockSpec((1, d), lambda bi, ti, j: (0, 0)),
                  pl.BlockSpec((None, d, nb), lambda bi, ti, j: (layer, 0, j)),
                  pl.BlockSpec((tm, LANES), lambda bi, ti, j: (ti, 0)),
                  pl.BlockSpec((tm, LANES), lambda bi, ti, j: (ti, 0)),
                  untouched, untouched],
        out_specs=[tile, tile, cache_tile, cache_tile, vb_spec, tile,
                   pl.BlockSpec((None, None, 1, LANES), lambda bi, ti, j: (bi, ti, 0, 0))],
        out_shape=[jax.ShapeDtypeStruct((b, t, nb), BF16),
                   jax.ShapeDtypeStruct((b, t, nb), BF16),
                   jax.ShapeDtypeStruct(k_all.shape, F32),
                   jax.ShapeDtypeStruct(v_all.shape, F32),
                   vb_shape,
                   jax.ShapeDtypeStruct((b, t, nb), BF16),
                   jax.ShapeDtypeStruct((b, t // tm, 1, LANES), F32)],
        input_output_aliases={6: 2, 7: 3},
        scratch_shapes=[pltpu.VMEM((tm, d), BF16)],
        compiler_params=_cparams("parallel", "arbitrary", "arbitrary"),
    )(x, mod, nw, w, cos_t, sin_t, k_all, v_all)


def _lambda(lam_ref, lam_init):
    l1 = jnp.sum(lam_ref[0:1, :] * lam_ref[1:2, :], axis=-1, keepdims=True)
    l2 = jnp.sum(lam_ref[2:3, :] * lam_ref[3:4, :], axis=-1, keepdims=True)
    return jnp.exp(l1) - jnp.exp(l2) + lam_init


def _attn_prompt_kernel(q_ref, k_ref, vt_ref, z_ref, kn2_ref, lam_ref, sw_ref, o_ref,
                        m_sc, l_sc, acc_sc, *, lam_init, tk):
    qi = pl.program_id(2)
    tq, dh = q_ref.shape
    dq = dh // 2
    nsub = tq // tk
    chains = [(sub, comp) for sub in range(nsub) for comp in range(2)]
    everyone = list(range(len(chains)))
    n_full = (qi * tq) // tk
    qz = []
    for sub, comp in chains:
        q = q_ref[sub * tk:(sub + 1) * tk, :]
        lane = lax.broadcasted_iota(jnp.int32, q.shape, 1)
        qz.append(jnp.where((lane < dq) if comp == 0 else (lane >= dq), q, jnp.zeros_like(q)))

    kn2_max = jnp.max(kn2_ref[...], axis=0)
    kn2_lane = lax.broadcasted_iota(jnp.int32, kn2_max.shape, 1)
    kmax2 = [jnp.max(jnp.where(kn2_lane == 2 * pl.program_id(1) + comp, kn2_max, 0.0), axis=1, keepdims=True)
             for comp in range(2)]

    ones = jnp.ones((SUBLANES, dh), BF16)
    ref_rows = []
    for c, (sub, comp) in enumerate(chains):
        qf = qz[c].astype(F32)
        qn2 = _dot_nt(ones, (qf * qf).astype(BF16))[0:1, :]
        ref_rows.append(REF_SLACK * jnp.sqrt(qn2 * kmax2[comp]))

    visible = (lax.broadcasted_iota(jnp.int32, (tk, tk), 0) // MASK_CHUNK
               <= lax.broadcasted_iota(jnp.int32, (tk, tk), 1) // MASK_CHUNK)

    def blocks(kb):
        ks = k_ref[pl.ds(pl.multiple_of(kb * tk, tk), tk), :]
        vt = vt_ref[:, pl.ds(pl.multiple_of(kb * tk, tk), tk)]
        return ks, vt

    def fast_step(kb, active, masked_sub):
        ks, vt = blocks(kb)
        s = [_dot_nt(ks, qz[c]) for c in active]
        p = [jnp.exp2(si - ref_rows[c]) for c, si in zip(active, s)]
        p = [jnp.where(visible, pi, 0.0) if chains[c][0] == masked_sub else pi for c, pi in zip(active, p)]
        for c, pi in zip(active, p):
            l_sc[c] += jnp.sum(pi.reshape(tk // SUBLANES, SUBLANES, tk), axis=0)
        pv = [_dot(vt, pi.astype(BF16)) for pi in p]
        for c, pvi in zip(active, pv):
            acc_sc[c] += pvi

    def online_step(kb, active, masked_sub):
        ks, vt = blocks(kb)
        for c in active:
            s = _dot_nt(ks, qz[c])
            if chains[c][0] == masked_sub:
                s = jnp.where(visible, s, -jnp.inf)
            m_old = m_sc[c]
            m_new = jnp.maximum(m_old, jnp.max(s, axis=0, keepdims=True))
            alpha = jnp.exp2(m_old - m_new)
            p = jnp.exp2(s - m_new)
            l_sc[c, 0:1, :] = alpha * l_sc[c, 0:1, :] + jnp.sum(p, axis=0, keepdims=True)
            acc_sc[c] = alpha * acc_sc[c] + _dot(vt, p.astype(BF16))
            m_sc[c] = m_new

    def sweep(step, unroll):
        l_sc[...] = jnp.zeros(l_sc.shape, F32)
        acc_sc[...] = jnp.zeros(acc_sc.shape, F32)
        def body(i, cry):
            for u in range(unroll):
                step(i * unroll + u, everyone, None)
            return cry
        lax.fori_loop(0, n_full // unroll, body, 0)
        for d in range(nsub):
            step(n_full + d, [c for c in everyone if chains[c][0] >= d], d)

    def normalisers():
        return [jnp.sum(l_sc[c], axis=0, keepdims=True) for c in everyone]

    sweep(fast_step, nsub)
    l_min = functools.reduce(jnp.minimum, normalisers())

    @pl.when(jnp.min(l_min) < MIN_NORMALISER)
    def _():
        m_sc[...] = jnp.full(m_sc.shape, -jnp.inf, F32)
        sweep(online_step, 1)

    lam = _lambda(lam_ref, lam_init)
    l = normalisers()
    for sub in range(nsub):
        c1, c2 = 2 * sub, 2 * sub + 1
        o_t = acc_sc[c1] / l[c1] - lam * (acc_sc[c2] / l[c2])
        o = o_t.T
        o = o * lax.rsqrt(jnp.mean(o * o, axis=-1, keepdims=True) + RMS_EPS) * sw_ref[...] * (1.0 - lam_init)
        rows = slice(sub * tk, (sub + 1) * tk)
        o_ref[rows, :] = (o * z_ref[rows, :].astype(F32)).astype(BF16)


def _attn_prompt(qb, kb, vt, zs, kn2, lam_vecs, subln, *, heads, lam_init, tq, tk):
    b, t, n = qb.shape
    dh = n // heads
    assert t % tq == 0 and tq % tk == 0 and tk % MASK_CHUNK == 0
    nchains = 2 * (tq // tk)
    kern = functools.partial(_attn_prompt_kernel, lam_init=lam_init, tk=tk)
    return pl.pallas_call(
        kern,
        grid=(b, heads, t // tq),
        in_specs=[pl.BlockSpec((None, tq, dh), lambda bi, hi, qi: (bi, qi, hi)),
                  pl.BlockSpec((None, t, dh), lambda bi, hi, qi: (bi, 0, hi)),
                  pl.BlockSpec((None, dh, t), lambda bi, hi, qi: (bi, hi, 0)),
                  pl.BlockSpec((None, tq, dh), lambda bi, hi, qi: (bi, qi, hi)),
                  pl.BlockSpec((None,) + kn2.shape[1:], lambda bi, hi, qi: (bi, 0, 0, 0)),
                  pl.BlockSpec(lam_vecs.shape, lambda bi, hi, qi: (0, 0)),
                  pl.BlockSpec((1, dh), lambda bi, hi, qi: (0, 0))],
        out_specs=pl.BlockSpec((None, tq, dh), lambda bi, hi, qi: (bi, qi, hi)),
        out_shape=jax.ShapeDtypeStruct((b, t, n), BF16),
        scratch_shapes=[pltpu.VMEM((nchains, 1, tk), F32), pltpu.VMEM((nchains, SUBLANES, tk), F32),
                        pltpu.VMEM((nchains, dh, tk), F32)],
        compiler_params=_cparams("parallel", "parallel", "arbitrary"),
    )(qb, kb, vt, zs, kn2, lam_vecs, subln)


def _attn_sample_kernel(q_ref, kn_ref, vn_ref, ck_ref, cv_ref, z_ref, lam_ref, sw_ref, o_ref,
                        m_sc, l_sc, acc_sc, *, heads, lam_init):
    si = pl.program_id(1)
    ns = pl.num_programs(1)
    tq, n = q_ref.shape
    dh = n // heads
    dq = dh // 2

    @pl.when(si == 0)
    def _():
        m_sc[...] = jnp.full(m_sc.shape, -jnp.inf, F32)
        l_sc[...] = jnp.zeros(l_sc.shape, F32)
        acc_sc[...] = jnp.zeros(acc_sc.shape, F32)

    def update(hh, kblk, vblk):
        hs = slice(hh * dh, (hh + 1) * dh)
        q = q_ref[:, hs]
        lane = lax.broadcasted_iota(jnp.int32, q.shape, 1)
        q2 = jnp.concatenate([jnp.where(lane < dq, q, jnp.zeros_like(q)),
                              jnp.where(lane >= dq, q, jnp.zeros_like(q))], axis=0)
        s = _dot_nt(q2, kblk)
        m_old = m_sc[hh]
        m_new = jnp.maximum(m_old, jnp.max(s, axis=-1, keepdims=True))
        alpha = jnp.exp2(m_old - m_new)
        p = jnp.exp2(s - m_new)
        l_sc[hh] = alpha * l_sc[hh] + jnp.sum(p, axis=-1, keepdims=True)
        acc_sc[hh] = alpha * acc_sc[hh] + _dot(p.astype(BF16), vblk)
        m_sc[hh] = m_new

    for hh in range(heads):
        hs = slice(hh * dh, (hh + 1) * dh)
        update(hh, ck_ref[:, hs].astype(BF16), cv_ref[:, hs].astype(BF16))

    @pl.when(si == ns - 1)
    def _():
        lam = _lambda(lam_ref, lam_init)
        for hh in range(heads):
            hs = slice(hh * dh, (hh + 1) * dh)
            update(hh, kn_ref[:, hs], vn_ref[:, hs])
            on = acc_sc[hh] / l_sc[hh]
            o = on[:tq] - lam * on[tq:]
            o = o * lax.rsqrt(jnp.mean(o * o, axis=-1, keepdims=True) + RMS_EPS) * sw_ref[...] * (1.0 - lam_init)
            o_ref[:, hs] = (o * z_ref[:, hs].astype(F32)).astype(BF16)


def _attn_sample(qb, kb, vb, cache_k, cache_v, zs, lam_vecs, subln, *, layer, heads, lam_init, tkv):
    b, tq, n = qb.shape
    past = cache_k.shape[2]
    dh = n // heads
    assert past % tkv == 0
    kern = functools.partial(_attn_sample_kernel, heads=heads, lam_init=lam_init)
    new = pl.BlockSpec((None, tq, n), lambda bi, si: (bi, 0, 0))
    old = pl.BlockSpec((None, None, tkv, n), lambda bi, si: (layer, bi, si, 0))
    return pl.pallas_call(
        kern,
        grid=(b, past // tkv),
        in_specs=[new, new, new, old, old, new,
                  pl.BlockSpec(lam_vecs.shape, lambda bi, si: (0, 0)),
                  pl.BlockSpec((1, dh), lambda bi, si: (0, 0))],
        out_specs=new,
        out_shape=jax.ShapeDtypeStruct((b, tq, n), BF16),
        scratch_shapes=[pltpu.VMEM((heads, 2 * tq, 1), F32), pltpu.VMEM((heads, 2 * tq, 1), F32),
                        pltpu.VMEM((heads, 2 * tq, dh), F32)],
        compiler_params=_cparams("parallel", "arbitrary"),
    )(qb, kb, vb, cache_k, cache_v, zs, lam_vecs, subln)


def _out_kernel(o_ref, w_ref, x_ref, mod_ref, nw_ref, y_ref):
    y = _dot(o_ref[...], w_ref[...])
    yn = y * lax.rsqrt(jnp.mean(y * y, axis=-1, keepdims=True) + RMS_EPS) * nw_ref[...]
    y_ref[...] = x_ref[...] + mod_ref[2:3, :] * yn


def _out_proj(o, w, x, mod, nw, *, layer, tm):
    b, t, d = x.shape
    n = o.shape[2]
    return pl.pallas_call(
        _out_kernel,
        grid=(b, t // tm),
        in_specs=[pl.BlockSpec((None, tm, n), lambda bi, ti: (bi, ti, 0)),
                  pl.BlockSpec((None, n, d), lambda bi, ti: (layer, 0, 0)),
                  pl.BlockSpec((None, tm, d), lambda bi, ti: (bi, ti, 0)),
                  pl.BlockSpec((None, 3, d), lambda bi, ti: (bi, 0, 0)),
                  pl.BlockSpec((1, d), lambda bi, ti: (0, 0))],
        out_specs=pl.BlockSpec((None, tm, d), lambda bi, ti: (bi, ti, 0)),
        out_shape=jax.ShapeDtypeStruct((b, t, d), F32),
        compiler_params=_cparams("parallel", "parallel"),
    )(o, w, x, mod, nw)


def _pick_tile(t, target):
    tile = min(t, target)
    assert t % tile == 0
    return tile


def _rope_tables(pos):
    half = MASK_CHUNK // 2
    inv = 1.0 / (ROPE_THETA ** (jnp.arange(half, dtype=F32) / half))
    ang = pos.astype(F32)[:, None] * inv[None, :]
    cos, sin = jnp.cos(ang), jnp.sin(ang)
    reps = LANES // MASK_CHUNK
    return (jnp.tile(jnp.concatenate([cos, cos], axis=1), (1, reps)),
            jnp.tile(jnp.concatenate([-sin, sin], axis=1), (1, reps)))


def _trunk(x, ada, pos, conv_bufs, gdn_states, past_k, past_v, p):
    b, t, d = x.shape
    depth = ada.shape[0]
    gdn_heads = p["a_log_gdn"].shape[1]
    gdn_dk = p["onorm_gdn"].shape[1]
    dq = p["lam_q1"].shape[1]
    n_diff = p["w_out_diff"].shape[1]
    diff_heads = n_diff // (2 * dq)
    gdn_chunk = LANES
    n_conv = 3 * gdn_heads * gdn_dk
    tm = _pick_tile(t, 512)
    cos_t, sin_t = _rope_tables(pos)
    prompt = past_k is None
    k_all = jnp.zeros((p["w_in_diff"].shape[0], b, t, diff_heads, 2 * dq), F32)
    v_all = jnp.zeros_like(k_all)
    states, convs = [], []
    for i in range(depth):
        j = i // 2
        mod = ada[i]
        nw_pre = p["norm_pre"][i][None, :]
        nw_post = p["norm_post"][i][None, :]
        if i % 2 == 0:
            a_log = jnp.pad(p["a_log_gdn"][j][None, :], ((0, 0), (0, LANES - gdn_heads)))
            dt_b = jnp.pad(p["dt_bias_gdn"][j][None, :], ((0, 0), (0, LANES - gdn_heads)))
            if conv_bufs is None:
                conv_init = jnp.zeros((b, SUBLANES, n_conv), F32)
                s0 = jnp.zeros((b, gdn_heads, gdn_dk, gdn_dk), F32)
            else:
                conv_init = jnp.pad(conv_bufs[j], ((0, 0), (SUBLANES - (CONV_WIDTH - 1), 0), (0, 0)))
                s0 = gdn_states[j]
            qkvz, gcol, cout = _gdn_in(x, mod, nw_pre, p["w_main_gdn"], p["w_ab_gdn"][j], p["conv_gdn"][j],
                                       conv_init, a_log, dt_b, layer=j, heads=gdn_heads, dk=gdn_dk, tm=tm,
                                       chunk=min(gdn_chunk, tm))
            if t < gdn_chunk:
                front = gdn_chunk - t
                qkvz = jnp.pad(qkvz, ((0, 0), (front, 0), (0, 0)))
                gcol = jnp.pad(gcol, ((0, 0), (front, 0), (0, 0)))
            o, st = _gdn(qkvz, gcol, s0, p["onorm_gdn"][j][None, :], heads=gdn_heads, dk=gdn_dk,
                         tb=_pick_tile(qkvz.shape[1], 256), chunk=gdn_chunk)
            o = o[:, -t:]
            last = cout[:, :, SUBLANES - (CONV_WIDTH - 1):, :]
            convs.append(jnp.swapaxes(last, 1, 2).reshape(b, CONV_WIDTH - 1, n_conv))
            states.append(st)
            w_out = p["w_out_gdn"]
        else:
            lam_init = 0.8 - 0.6 * math.exp(-0.3 * i)
            lam_vecs = jnp.stack([p["lam_q1"][j], p["lam_k1"][j], p["lam_q2"][j], p["lam_k2"][j]])
            qb, kb, k_all, v_all, vb, zs, kn2 = _diff_in(
                x, mod, nw_pre, p["w_in_diff"], cos_t, sin_t, k_all, v_all,
                layer=j, tm=tm, v_transposed=prompt, q_scale=dq ** -0.5 * LOG2E)
            subln = p["subln_diff"][j][None, :]
            if prompt:
                o = _attn_prompt(qb, kb, vb, zs, kn2, lam_vecs, subln, heads=diff_heads, lam_init=lam_init,
                                 tq=_pick_tile(t, 1024), tk=_pick_tile(t, 256))
            else:
                o = _attn_sample(qb, kb, vb, past_k, past_v, zs, lam_vecs, subln, layer=j, heads=diff_heads,
                                 lam_init=lam_init, tkv=_pick_tile(past_k.shape[2], 1024))
            w_out = p["w_out_diff"]
        x = _out_proj(o, w_out, x, mod, nw_post, layer=j, tm=tm)
    return x, jnp.stack(states), jnp.stack(convs), k_all, v_all


def kernel(x_prompt, x_sample, c_prompt, c_sample, state_gdn, cache_conv, cache_k, cache_v, norm_pre, norm_post, w_ada, b_ada, w_in_gdn, conv_gdn, a_log_gdn, dt_bias_gdn, onorm_gdn, w_out_gdn, w_in_diff, lam_q1, lam_k1, lam_q2, lam_k2, subln_diff, w_out_diff):
    gdn_heads = a_log_gdn.shape[1]
    n_main = w_in_gdn.shape[2] - 2 * gdn_heads
    w_ab = jnp.pad(w_in_gdn[:, :, n_main:], ((0, 0), (0, 0), (0, LANES - 2 * gdn_heads)))
    w_ab_hi = w_ab.astype(BF16)
    w_ab_lo = (w_ab - w_ab_hi.astype(F32)).astype(BF16)
    p = {"norm_pre": norm_pre, "norm_post": norm_post,
         "w_main_gdn": w_in_gdn[:, :, :n_main].astype(BF16),
         "w_ab_gdn": jnp.concatenate([w_ab_hi, w_ab_lo], axis=-1),

         "conv_gdn": conv_gdn, "a_log_gdn": a_log_gdn,
         "dt_bias_gdn": dt_bias_gdn, "onorm_gdn": onorm_gdn, "w_out_gdn": w_out_gdn.astype(BF16),
         "w_in_diff": w_in_diff.astype(BF16), "lam_q1": lam_q1, "lam_k1": lam_k1, "lam_q2": lam_q2,
         "lam_k2": lam_k2, "subln_diff": subln_diff, "w_out_diff": w_out_diff.astype(BF16)}
    bp, tp, d = x_prompt.shape
    bs, ts, _ = x_sample.shape
    past = cache_k.shape[2]
    depth = w_ada.shape[0]
    ada = _ada(jnp.concatenate([c_prompt, c_sample], axis=0), w_ada, b_ada)
    ada = ada.reshape(depth, bp + bs, 3, d)
    y_p, st_p, conv_p, k_p, v_p = _trunk(x_prompt, ada[:, :bp], jnp.arange(tp), None, None, None, None, p)
    y_s, st_s, conv_s, k_s, v_s = _trunk(x_sample, ada[:, bp:], past + jnp.arange(ts), cache_conv, state_gdn,
                                         cache_k.reshape(cache_k.shape[:3] + (-1,)),
                                         cache_v.reshape(cache_v.shape[:3] + (-1,)), p)
    return (y_p, y_s, st_p, conv_p, k_p, v_p, st_s, conv_s, k_s, v_s)
```

```python
import functools
import math

import jax
import jax.numpy as jnp
from jax import lax
from jax.experimental import pallas as pl
from jax.experimental.pallas import tpu as pltpu

F32 = jnp.float32
BF16 = jnp.bfloat16
HIGHEST = lax.Precision.HIGHEST

RMS_EPS = 1e-6
L2_EPS = 1e-6
ROPE_THETA = 10000.0
CONV_WIDTH = 4
MASK_CHUNK = 64
LANES = 128
SUBLANES = 8
VMEM_LIMIT = 56 * 1024 * 1024
LOG2E = 1.4426950408889634
REF_SLACK = 1.01
MIN_NORMALISER = 1e-30

NT_DIMS = (((1,), (1,)), ((), ()))


def _cparams(*sem):
    return pltpu.CompilerParams(dimension_semantics=sem, vmem_limit_bytes=VMEM_LIMIT)


def _silu(x):
    return x * jax.nn.sigmoid(x)


def _dot(a, b):
    return jnp.dot(a, b, preferred_element_type=F32)


def _dot_nt(a, b):
    return lax.dot_general(a, b, NT_DIMS, preferred_element_type=F32)


def _ada_kernel(c_ref, w_ref, b_ref, o_ref):
    act = _silu(c_ref[...])
    o_ref[...] = jnp.dot(act, w_ref[...], precision=HIGHEST, preferred_element_type=F32) + b_ref[...]


def _ada(c_all, w_ada, b_ada):
    depth, d, n = w_ada.shape
    bc = c_all.shape[0]
    tn = 512
    return pl.pallas_call(
        _ada_kernel,
        grid=(depth, n // tn),
        in_specs=[pl.BlockSpec((bc, d), lambda i, j: (0, 0)),
                  pl.BlockSpec((None, d, tn), lambda i, j: (i, 0, j)),
                  pl.BlockSpec((None, 1, tn), lambda i, j: (i, 0, j))],
        out_specs=pl.BlockSpec((None, bc, tn), lambda i, j: (i, 0, j)),
        out_shape=jax.ShapeDtypeStruct((depth, bc, n), F32),
        compiler_params=_cparams("parallel", "parallel"),
    )(c_all, w_ada, b_ada.reshape(depth, 1, n))


def _prenorm(x_ref, mod_ref, nw_ref):
    x = x_ref[...]
    y = x * lax.rsqrt(jnp.mean(x * x, axis=-1, keepdims=True) + RMS_EPS) * nw_ref[...]
    return y * (1.0 + mod_ref[1:2, :]) + mod_ref[0:1, :]


def _conv_silu(u, tail, cw):
    def taps(a):
        y = a * cw[CONV_WIDTH - 1:CONV_WIDTH, :]
        for k in range(1, CONV_WIDTH):
            y = y + pltpu.roll(a, k, axis=0) * cw[CONV_WIDTH - 1 - k:CONV_WIDTH - k, :]
        return y
    y = taps(u)
    head = taps(jnp.concatenate([tail, u[0:SUBLANES]], axis=0))[SUBLANES:2 * SUBLANES]
    y = jnp.concatenate([head, y[SUBLANES:]], axis=0)
    return _silu(y)


def _gdn_in_kernel(x_ref, mod_ref, nw_ref, w_ref, wab_ref, cw_ref, cinit_ref, alog_ref, dtb_ref,
                   qkvz_ref, gcol_ref, cout_ref, tail_sc, *, heads, dk, chunk, tn):
    t = pl.program_id(1)
    tm = x_ref.shape[0]
    n = w_ref.shape[1]
    nb = n // 4

    @pl.when(t == 0)
    def _():
        tail_sc[...] = cinit_ref[...]

    h = _prenorm(x_ref, mod_ref, nw_ref)
    hb = h.astype(BF16)
    h_lo = (h - hb.astype(F32)).astype(BF16)
    hi_lo = _dot(hb, wab_ref[...])
    ab = hi_lo[:, :LANES] + hi_lo[:, LANES:] + _dot(h_lo, wab_ref[:, :LANES])
    g = -jnp.exp(alog_ref[...]) * jax.nn.softplus(ab + dtb_ref[...])
    row = lax.broadcasted_iota(jnp.int32, g.shape, 0) % chunk
    s = 1
    while s < chunk:
        g = g + jnp.where(row >= s, pltpu.roll(g, s, axis=0), 0.0)
        s *= 2
    lane = lax.broadcasted_iota(jnp.int32, g.shape, 1)
    gcol_ref[...] = jnp.where(lane < heads, g, jax.nn.sigmoid(ab))

    for grp in range(n // tn):
        cs = slice(grp * tn, (grp + 1) * tn)
        kind = (grp * tn) // nb
        u = _dot(hb, w_ref[:, cs])
        if kind == 3:
            qkvz_ref[:, cs] = _silu(u).astype(BF16)
            continue
        y = _conv_silu(u, tail_sc[:, cs], cw_ref[:, cs])
        tail_sc[:, cs] = u[tm - SUBLANES:tm]
        cout_ref[:, cs] = u[tm - SUBLANES:tm]
        if kind == 2:
            qkvz_ref[:, cs] = y.astype(BF16)
            continue
        scale = dk ** -0.5 if kind == 0 else 1.0
        for hh in range(tn // dk):
            ys = y[:, hh * dk:(hh + 1) * dk]
            inv = lax.rsqrt(jnp.sum(ys * ys, axis=-1, keepdims=True) + L2_EPS) * scale
            qkvz_ref[:, grp * tn + hh * dk:grp * tn + (hh + 1) * dk] = (ys * inv).astype(BF16)


def _gdn_in(x, mod, nw, w_main, w_ab, conv_w, conv_init, a_log, dt_bias, *, layer, heads, dk, tm, chunk):
    b, t, d = x.shape
    n = w_main.shape[2]
    nb = n // 4
    tn = 4 * dk
    assert nb == heads * dk and nb % tn == 0 and t % tm == 0 and tm % SUBLANES == 0 and tm % chunk == 0
    kern = functools.partial(_gdn_in_kernel, heads=heads, dk=dk, chunk=chunk, tn=tn)
    return pl.pallas_call(
        kern,
        grid=(b, t // tm),
        in_specs=[pl.BlockSpec((None, tm, d), lambda bi, ti: (bi, ti, 0)),
                  pl.BlockSpec((None, 3, d), lambda bi, ti: (bi, 0, 0)),
                  pl.BlockSpec((1, d), lambda bi, ti: (0, 0)),
                  pl.BlockSpec((None, d, n), lambda bi, ti: (layer, 0, 0)),
                  pl.BlockSpec((d, 2 * LANES), lambda bi, ti: (0, 0)),
                  pl.BlockSpec((CONV_WIDTH, 3 * nb), lambda bi, ti: (0, 0)),
                  pl.BlockSpec((None, SUBLANES, 3 * nb), lambda bi, ti: (bi, 0, 0)),
                  pl.BlockSpec((1, LANES), lambda bi, ti: (0, 0)),
                  pl.BlockSpec((1, LANES), lambda bi, ti: (0, 0))],
        out_specs=[pl.BlockSpec((None, tm, n), lambda bi, ti: (bi, ti, 0)),
                   pl.BlockSpec((None, tm, LANES), lambda bi, ti: (bi, ti, 0)),
                   pl.BlockSpec((None, SUBLANES, 3 * nb), lambda bi, ti: (bi, 0, 0))],
        out_shape=[jax.ShapeDtypeStruct((b, t, n), BF16),
                   jax.ShapeDtypeStruct((b, t, LANES), F32),
                   jax.ShapeDtypeStruct((b, SUBLANES, 3 * nb), F32)],
        scratch_shapes=[pltpu.VMEM((SUBLANES, 3 * nb), F32)],
        compiler_params=_cparams("parallel", "arbitrary"),
    )(x, mod, nw, w_main, w_ab, conv_w, conv_init, a_log, dt_bias)


def _gdn_kernel(q_ref, k_ref, v_ref, z_ref, gcol_ref, s0_ref, ow_ref, o_ref, s_ref, *, heads, dk, chunk):
    t = pl.program_id(1)
    tb = q_ref.shape[0]
    c = chunk
    group = 2 if (tb // c) % 2 == 0 else 1

    @pl.when(t == 0)
    def _():
        s_ref[...] = s0_ref[...]

    ii = lax.broadcasted_iota(jnp.int32, (c, c), 0)
    jj = lax.broadcasted_iota(jnp.int32, (c, c), 1)
    incl = ii >= jj
    strict = ii > jj
    eye = (ii == jj).astype(F32)
    off_masks = []
    s = 1
    while s < c:
        off_masks.append((ii // (2 * s) == jj // (2 * s)) & (ii // s != jj // s) & strict)
        s *= 2

    def chunk_group(ci, carry):
        rows = [pl.ds(pl.multiple_of((ci * group + k) * c, c), c) for k in range(group)]
        hsl = [slice(hh * dk, (hh + 1) * dk) for hh in range(heads)]
        units = [(k, hh) for k in range(group) for hh in range(heads)]
        gall = [gcol_ref[r, :] for r in rows]
        g = [gall[k][:, hh:hh + 1] for k, hh in units]
        beta = [gall[k][:, heads + hh:heads + hh + 1] for k, hh in units]
        kbf = [k_ref[rows[k], hsl[hh]] for k, hh in units]
        qbf = [q_ref[rows[k], hsl[hh]] for k, hh in units]
        kf = [x.astype(F32) for x in kbf]
        kb = [x * y for x, y in zip(kf, beta)]
        gb = [jnp.broadcast_to(x, (c, c)) for x in g]
        decay = [jnp.exp(jnp.where(incl, x - x.T, -jnp.inf)) for x in gb]
        a = [jnp.where(strict, _dot_nt(x.astype(BF16), y) * d, 0.0) for x, y, d in zip(kb, kbf, decay)]
        qk = [(_dot_nt(x, y) * d).astype(BF16) for x, y, d in zip(qbf, kbf, decay)]
        p = [eye - jnp.where(off_masks[0], x, 0.0) for x in a]
        for off in off_masks[1:]:
            pb = [x.astype(BF16) for x in p]
            x = [_dot(jnp.where(off, ai, 0.0).astype(BF16), pi).astype(BF16) for ai, pi in zip(a, pb)]
            p = [pi - _dot(pbi, xi) for pi, pbi, xi in zip(p, pb, x)]
        eg = [jnp.exp(x) for x in g]
        rhs = [jnp.concatenate([v_ref[rows[k], hsl[hh]].astype(F32) * beta[u], kb[u] * eg[u]], axis=1).astype(BF16)
               for u, (k, hh) in enumerate(units)]
        sol = [_dot(x.astype(BF16), y) for x, y in zip(p, rhs)]
        wq = [jnp.concatenate([sol[u][:, dk:].astype(BF16), (qbf[u].astype(F32) * eg[u]).astype(BF16)], axis=0)
              for u in range(len(units))]
        g_last = [x[c - 1:c, :] for x in g]
        kd_t = [(kf[u] * jnp.exp(g_last[u] - g[u])).T.astype(BF16) for u in range(len(units))]
        e_last = [jnp.exp(x) for x in g_last]
        for k in range(group):
            us = [k * heads + hh for hh in range(heads)]
            s_old = [s_ref[hh] for hh in range(heads)]
            r1 = [_dot(wq[u], s_old[hh].astype(BF16)) for hh, u in enumerate(us)]
            ub = [(sol[u][:, :dk] - r1[hh][:c]).astype(BF16) for hh, u in enumerate(us)]
            for hh, u in enumerate(us):
                s_ref[hh] = s_old[hh] * e_last[u] + _dot(kd_t[u], ub[hh])
            o = [r1[hh][c:] + _dot(qk[u], ub[hh]) for hh, u in enumerate(us)]
            for hh in range(heads):
                on = o[hh] * lax.rsqrt(jnp.mean(o[hh] * o[hh], axis=-1, keepdims=True) + RMS_EPS) * ow_ref[...]
                o_ref[rows[k], hsl[hh]] = (on * z_ref[rows[k], hsl[hh]].astype(F32)).astype(BF16)
        return carry

    lax.fori_loop(0, tb // (c * group), chunk_group, 0)


def _gdn(qkvz, gcol, s0, onorm, *, heads, dk, tb, chunk):
    b, t, n = qkvz.shape
    nb = n // 4
    assert t % tb == 0 and tb % chunk == 0 and nb == heads * dk
    kern = functools.partial(_gdn_kernel, heads=heads, dk=dk, chunk=chunk)
    col = lambda jcol: pl.BlockSpec((None, tb, nb), lambda bi, ti: (bi, ti, jcol))
    return pl.pallas_call(
        kern,
        grid=(b, t // tb),
        in_specs=[col(0), col(1), col(2), col(3),
                  pl.BlockSpec((None, tb, LANES), lambda bi, ti: (bi, ti, 0)),
                  pl.BlockSpec((None, heads, dk, dk), lambda bi, ti: (bi, 0, 0, 0)),
                  pl.BlockSpec((1, dk), lambda bi, ti: (0, 0))],
        out_specs=[pl.BlockSpec((None, tb, nb), lambda bi, ti: (bi, ti, 0)),
                   pl.BlockSpec((None, heads, dk, dk), lambda bi, ti: (bi, 0, 0, 0))],
        out_shape=[jax.ShapeDtypeStruct((b, t, nb), BF16),
                   jax.ShapeDtypeStruct((b, heads, dk, dk), F32)],
        compiler_params=_cparams("parallel", "arbitrary"),
    )(qkvz, qkvz, qkvz, qkvz, gcol, s0, onorm)


def _rope(x, cos, sin_signed, lane_lo):
    half = MASK_CHUNK // 2
    swapped = jnp.where(lane_lo, pltpu.roll(x, LANES - half, axis=1), pltpu.roll(x, half, axis=1))
    return x * cos + swapped * sin_signed


def _diff_in_kernel(x_ref, mod_ref, nw_ref, w_ref, cos_ref, sin_ref, k_all_ref, v_all_ref,
                    qb_ref, kb_ref, kf_ref, vf_ref, vb_ref, zs_ref, kn2_ref, *, v_transposed, q_scale, tn):
    del k_all_ref, v_all_ref
    n = w_ref.shape[1]
    nb = n // 4
    hb = _prenorm(x_ref, mod_ref, nw_ref).astype(BF16)
    cos = cos_ref[...]
    sin = sin_ref[...]
    lane_lo = (lax.broadcasted_iota(jnp.int32, cos.shape, 1) % MASK_CHUNK) < (MASK_CHUNK // 2)
    kn2 = jnp.zeros((1, LANES), F32)
    for grp in range(n // tn):
        kind = (grp * tn) // nb
        lo = grp * tn - kind * nb
        u = _dot(hb, w_ref[:, grp * tn:(grp + 1) * tn])
        squares = []
        for s in range(tn // LANES):
            us = u[:, s * LANES:(s + 1) * LANES]
            cols = slice(lo + s * LANES, lo + (s + 1) * LANES)
            head = (lo + s * LANES) // LANES
            if kind == 0:
                qb_ref[:, cols] = (_rope(us, cos, sin, lane_lo) * q_scale).astype(BF16)
            elif kind == 1:
                r = _rope(us, cos, sin, lane_lo)
                kf_ref[:, head, :] = r
                kb_ref[:, cols] = r.astype(BF16)
                squares.append(r * r)
            elif kind == 2:
                vf_ref[:, head, :] = us
            else:
                zs_ref[:, cols] = _silu(us).astype(BF16)
        if kind == 1:
            group_sum = ((lax.broadcasted_iota(jnp.int32, (tn, LANES), 0) + lo) // MASK_CHUNK
                         == lax.broadcasted_iota(jnp.int32, (tn, LANES), 1)).astype(BF16)
            n2 = _dot(jnp.concatenate(squares, axis=1).astype(BF16), group_sum)
            kn2 = jnp.maximum(kn2, jnp.max(n2, axis=0, keepdims=True))
        if kind == 2:
            if v_transposed:
                vb_ref[lo:lo + tn, :] = u.T.astype(BF16)
            else:
                vb_ref[:, lo:lo + tn] = u.astype(BF16)
    kn2_ref[...] = kn2


def _diff_in(x, mod, nw, w, cos_t, sin_t, k_all, v_all, *, layer, tm, v_transposed, q_scale):
    b, t, d = x.shape
    nb = w.shape[2] // 4
    heads = k_all.shape[3]
    assert t % tm == 0 and k_all.shape[3:] == (nb // LANES, LANES)
    tn = 4 * LANES
    assert nb % tn == 0
    kern = functools.partial(_diff_in_kernel, v_transposed=v_transposed, q_scale=q_scale, tn=tn)
    tile = pl.BlockSpec((None, tm, nb), lambda bi, ti: (bi, ti, 0))
    cache_tile = pl.BlockSpec((None, None, tm, heads, LANES), lambda bi, ti: (layer, bi, ti, 0, 0))
    untouched = pl.BlockSpec(memory_space=pl.ANY)
    if v_transposed:
        vb_spec = pl.BlockSpec((None, nb, tm), lambda bi, ti: (bi, 0, ti))
        vb_shape = jax.ShapeDtypeStruct((b, nb, t), BF16)
    else:
        vb_spec, vb_shape = tile, jax.ShapeDtypeStruct((b, t, nb), BF16)
    return pl.pallas_call(
        kern,
        grid=(b, t // tm),
        in_specs=[pl.BlockSpec((None, tm, d), lambda bi, ti: (bi, ti, 0)),
                  pl.BlockSpec((None, 3, d), lambda bi, ti: (bi, 0, 0)),
                  pl.BlockSpec((1, d), lambda bi, ti: (0, 0)),
                  pl.BlockSpec((None, d, 4 * nb), lambda bi, ti: (layer, 0, 0)),
                  pl.BlockSpec((tm, LANES), lambda bi, ti: (ti, 0)),
                  pl.BlockSpec((tm, LANES), lambda bi, ti: (ti, 0)),
                  untouched, untouched],
        out_specs=[tile, tile, cache_tile, cache_tile, vb_spec, tile,
                   pl.BlockSpec((None, None, 1, LANES), lambda bi, ti: (bi, ti, 0, 0))],
        out_shape=[jax.ShapeDtypeStruct((b, t, nb), BF16),
                   jax.ShapeDtypeStruct((b, t, nb), BF16),
                   jax.ShapeDtypeStruct(k_all.shape, F32),
                   jax.ShapeDtypeStruct(v_all.shape, F32),
                   vb_shape,
                   jax.ShapeDtypeStruct((b, t, nb), BF16),
                   jax.ShapeDtypeStruct((b, t // tm, 1, LANES), F32)],
        input_output_aliases={6: 2, 7: 3},
        compiler_params=_cparams("parallel", "parallel"),
    )(x, mod, nw, w, cos_t, sin_t, k_all, v_all)


def _lambda(lam_ref, lam_init):
    l1 = jnp.sum(lam_ref[0:1, :] * lam_ref[1:2, :], axis=-1, keepdims=True)
    l2 = jnp.sum(lam_ref[2:3, :] * lam_ref[3:4, :], axis=-1, keepdims=True)
    return jnp.exp(l1) - jnp.exp(l2) + lam_init


def _attn_prompt_kernel(q_ref, k_ref, vt_ref, z_ref, kn2_ref, lam_ref, sw_ref, o_ref,
                        m_sc, l_sc, acc_sc, *, lam_init, tk):
    qi = pl.program_id(2)
    tq, dh = q_ref.shape
    dq = dh // 2
    nsub = tq // tk
    chains = [(sub, comp) for sub in range(nsub) for comp in range(2)]
    everyone = list(range(len(chains)))
    n_full = (qi * tq) // tk
    qz = []
    for sub, comp in chains:
        q = q_ref[sub * tk:(sub + 1) * tk, :]
        lane = lax.broadcasted_iota(jnp.int32, q.shape, 1)
        qz.append(jnp.where((lane < dq) if comp == 0 else (lane >= dq), q, jnp.zeros_like(q)))

    kn2_max = jnp.max(kn2_ref[...], axis=0)
    kn2_lane = lax.broadcasted_iota(jnp.int32, kn2_max.shape, 1)
    kmax2 = [jnp.max(jnp.where(kn2_lane == 2 * pl.program_id(1) + comp, kn2_max, 0.0), axis=1, keepdims=True)
             for comp in range(2)]

    ones = jnp.ones((SUBLANES, dh), BF16)
    ref_rows = []
    for c, (sub, comp) in enumerate(chains):
        qf = qz[c].astype(F32)
        qn2 = _dot_nt(ones, (qf * qf).astype(BF16))[0:1, :]
        ref_rows.append(REF_SLACK * jnp.sqrt(qn2 * kmax2[comp]))

    visible = (lax.broadcasted_iota(jnp.int32, (tk, tk), 0) // MASK_CHUNK
               <= lax.broadcasted_iota(jnp.int32, (tk, tk), 1) // MASK_CHUNK)

    def blocks(kb):
        ks = k_ref[pl.ds(pl.multiple_of(kb * tk, tk), tk), :]
        vt = vt_ref[:, pl.ds(pl.multiple_of(kb * tk, tk), tk)]
        return ks, vt

    def fast_step(kb, active, masked_sub):
        ks, vt = blocks(kb)
        s = [_dot_nt(ks, qz[c]) for c in active]
        p = [jnp.exp2(si - ref_rows[c]) for c, si in zip(active, s)]
        p = [jnp.where(visible, pi, 0.0) if chains[c][0] == masked_sub else pi for c, pi in zip(active, p)]
        for c, pi in zip(active, p):
            l_sc[c] += jnp.sum(pi.reshape(tk // SUBLANES, SUBLANES, tk), axis=0)
        pv = [_dot(vt, pi.astype(BF16)) for pi in p]
        for c, pvi in zip(active, pv):
            acc_sc[c] += pvi

    def online_step(kb, active, masked_sub):
        ks, vt = blocks(kb)
        for c in active:
            s = _dot_nt(ks, qz[c])
            if chains[c][0] == masked_sub:
                s = jnp.where(visible, s, -jnp.inf)
            m_old = m_sc[c]
            m_new = jnp.maximum(m_old, jnp.max(s, axis=0, keepdims=True))
            alpha = jnp.exp2(m_old - m_new)
            p = jnp.exp2(s - m_new)
            l_sc[c, 0:1, :] = alpha * l_sc[c, 0:1, :] + jnp.sum(p, axis=0, keepdims=True)
            acc_sc[c] = alpha * acc_sc[c] + _dot(vt, p.astype(BF16))
            m_sc[c] = m_new

    def sweep(step, unroll):
        l_sc[...] = jnp.zeros(l_sc.shape, F32)
        acc_sc[...] = jnp.zeros(acc_sc.shape, F32)
        def body(i, cry):
            for u in range(unroll):
                step(i * unroll + u, everyone, None)
            return cry
        lax.fori_loop(0, n_full // unroll, body, 0)
        for d in range(nsub):
            step(n_full + d, [c for c in everyone if chains[c][0] >= d], d)

    def normalisers():
        return [jnp.sum(l_sc[c], axis=0, keepdims=True) for c in everyone]

    sweep(fast_step, nsub)
    l_min = functools.reduce(jnp.minimum, normalisers())

    @pl.when(jnp.min(l_min) < MIN_NORMALISER)
    def _():
        m_sc[...] = jnp.full(m_sc.shape, -jnp.inf, F32)
        sweep(online_step, 1)

    lam = _lambda(lam_ref, lam_init)
    l = normalisers()
    for sub in range(nsub):
        c1, c2 = 2 * sub, 2 * sub + 1
        o_t = acc_sc[c1] / l[c1] - lam * (acc_sc[c2] / l[c2])
        o = o_t.T
        o = o * lax.rsqrt(jnp.mean(o * o, axis=-1, keepdims=True) + RMS_EPS) * sw_ref[...] * (1.0 - lam_init)
        rows = slice(sub * tk, (sub + 1) * tk)
        o_ref[rows, :] = (o * z_ref[rows, :].astype(F32)).astype(BF16)


def _attn_prompt(qb, kb, vt, zs, kn2, lam_vecs, subln, *, heads, lam_init, tq, tk):
    b, t, n = qb.shape
    dh = n // heads
    assert t % tq == 0 and tq % tk == 0 and tk % MASK_CHUNK == 0
    nchains = 2 * (tq // tk)
    kern = functools.partial(_attn_prompt_kernel, lam_init=lam_init, tk=tk)
    return pl.pallas_call(
        kern,
        grid=(b, heads, t // tq),
        in_specs=[pl.BlockSpec((None, tq, dh), lambda bi, hi, qi: (bi, qi, hi)),
                  pl.BlockSpec((None, t, dh), lambda bi, hi, qi: (bi, 0, hi)),
                  pl.BlockSpec((None, dh, t), lambda bi, hi, qi: (bi, hi, 0)),
                  pl.BlockSpec((None, tq, dh), lambda bi, hi, qi: (bi, qi, hi)),
                  pl.BlockSpec((None,) + kn2.shape[1:], lambda bi, hi, qi: (bi, 0, 0, 0)),
                  pl.BlockSpec(lam_vecs.shape, lambda bi, hi, qi: (0, 0)),
                  pl.BlockSpec((1, dh), lambda bi, hi, qi: (0, 0))],
        out_specs=pl.BlockSpec((None, tq, dh), lambda bi, hi, qi: (bi, qi, hi)),
        out_shape=jax.ShapeDtypeStruct((b, t, n), BF16),
        scratch_shapes=[pltpu.VMEM((nchains, 1, tk), F32), pltpu.VMEM((nchains, SUBLANES, tk), F32),
                        pltpu.VMEM((nchains, dh, tk), F32)],
        compiler_params=_cparams("parallel", "parallel", "arbitrary"),
    )(qb, kb, vt, zs, kn2, lam_vecs, subln)


def _attn_sample_kernel(q_ref, kn_ref, vn_ref, ck_ref, cv_ref, z_ref, lam_ref, sw_ref, o_ref,
                        m_sc, l_sc, acc_sc, *, heads, lam_init):
    si = pl.program_id(1)
    ns = pl.num_programs(1)
    tq, n = q_ref.shape
    dh = n // heads
    dq = dh // 2

    @pl.when(si == 0)
    def _():
        m_sc[...] = jnp.full(m_sc.shape, -jnp.inf, F32)
        l_sc[...] = jnp.zeros(l_sc.shape, F32)
        acc_sc[...] = jnp.zeros(acc_sc.shape, F32)

    def update(hh, kblk, vblk):
        hs = slice(hh * dh, (hh + 1) * dh)
        q = q_ref[:, hs]
        lane = lax.broadcasted_iota(jnp.int32, q.shape, 1)
        q2 = jnp.concatenate([jnp.where(lane < dq, q, jnp.zeros_like(q)),
                              jnp.where(lane >= dq, q, jnp.zeros_like(q))], axis=0)
        s = _dot_nt(q2, kblk)
        m_old = m_sc[hh]
        m_new = jnp.maximum(m_old, jnp.max(s, axis=-1, keepdims=True))
        alpha = jnp.exp2(m_old - m_new)
        p = jnp.exp2(s - m_new)
        l_sc[hh] = alpha * l_sc[hh] + jnp.sum(p, axis=-1, keepdims=True)
        acc_sc[hh] = alpha * acc_sc[hh] + _dot(p.astype(BF16), vblk)
        m_sc[hh] = m_new

    for hh in range(heads):
        hs = slice(hh * dh, (hh + 1) * dh)
        update(hh, ck_ref[:, hs].astype(BF16), cv_ref[:, hs].astype(BF16))

    @pl.when(si == ns - 1)
    def _():
        lam = _lambda(lam_ref, lam_init)
        for hh in range(heads):
            hs = slice(hh * dh, (hh + 1) * dh)
            update(hh, kn_ref[:, hs], vn_ref[:, hs])
            on = acc_sc[hh] / l_sc[hh]
            o = on[:tq] - lam * on[tq:]
            o = o * lax.rsqrt(jnp.mean(o * o, axis=-1, keepdims=True) + RMS_EPS) * sw_ref[...] * (1.0 - lam_init)
            o_ref[:, hs] = (o * z_ref[:, hs].astype(F32)).astype(BF16)


def _attn_sample(qb, kb, vb, cache_k, cache_v, zs, lam_vecs, subln, *, layer, heads, lam_init, tkv):
    b, tq, n = qb.shape
    past = cache_k.shape[2]
    dh = n // heads
    assert past % tkv == 0
    kern = functools.partial(_attn_sample_kernel, heads=heads, lam_init=lam_init)
    new = pl.BlockSpec((None, tq, n), lambda bi, si: (bi, 0, 0))
    old = pl.BlockSpec((None, None, tkv, n), lambda bi, si: (layer, bi, si, 0))
    return pl.pallas_call(
        kern,
        grid=(b, past // tkv),
        in_specs=[new, new, new, old, old, new,
                  pl.BlockSpec(lam_vecs.shape, lambda bi, si: (0, 0)),
                  pl.BlockSpec((1, dh), lambda bi, si: (0, 0))],
        out_specs=new,
        out_shape=jax.ShapeDtypeStruct((b, tq, n), BF16),
        scratch_shapes=[pltpu.VMEM((heads, 2 * tq, 1), F32), pltpu.VMEM((heads, 2 * tq, 1), F32),
                        pltpu.VMEM((heads, 2 * tq, dh), F32)],
        compiler_params=_cparams("parallel", "arbitrary"),
    )(qb, kb, vb, cache_k, cache_v, zs, lam_vecs, subln)


def _out_kernel(o_ref, w_ref, x_ref, mod_ref, nw_ref, y_ref):
    y = _dot(o_ref[...], w_ref[...])
    yn = y * lax.rsqrt(jnp.mean(y * y, axis=-1, keepdims=True) + RMS_EPS) * nw_ref[...]
    y_ref[...] = x_ref[...] + mod_ref[2:3, :] * yn


def _out_proj(o, w, x, mod, nw, *, layer, tm):
    b, t, d = x.shape
    n = o.shape[2]
    return pl.pallas_call(
        _out_kernel,
        grid=(b, t // tm),
        in_specs=[pl.BlockSpec((None, tm, n), lambda bi, ti: (bi, ti, 0)),
                  pl.BlockSpec((None, n, d), lambda bi, ti: (layer, 0, 0)),
                  pl.BlockSpec((None, tm, d), lambda bi, ti: (bi, ti, 0)),
                  pl.BlockSpec((None, 3, d), lambda bi, ti: (bi, 0, 0)),
                  pl.BlockSpec((1, d), lambda bi, ti: (0, 0))],
        out_specs=pl.BlockSpec((None, tm, d), lambda bi, ti: (bi, ti, 0)),
        out_shape=jax.ShapeDtypeStruct((b, t, d), F32),
        compiler_params=_cparams("parallel", "parallel"),
    )(o, w, x, mod, nw)


def _pick_tile(t, target):
    tile = min(t, target)
    assert t % tile == 0
    return tile


def _rope_tables(pos):
    half = MASK_CHUNK // 2
    inv = 1.0 / (ROPE_THETA ** (jnp.arange(half, dtype=F32) / half))
    ang = pos.astype(F32)[:, None] * inv[None, :]
    cos, sin = jnp.cos(ang), jnp.sin(ang)
    reps = LANES // MASK_CHUNK
    return (jnp.tile(jnp.concatenate([cos, cos], axis=1), (1, reps)),
            jnp.tile(jnp.concatenate([-sin, sin], axis=1), (1, reps)))


def _trunk(x, ada, pos, conv_bufs, gdn_states, past_k, past_v, p):
    b, t, d = x.shape
    depth = ada.shape[0]
    gdn_heads = p["a_log_gdn"].shape[1]
    gdn_dk = p["onorm_gdn"].shape[1]
    dq = p["lam_q1"].shape[1]
    n_diff = p["w_out_diff"].shape[1]
    diff_heads = n_diff // (2 * dq)
    gdn_chunk = LANES
    n_conv = 3 * gdn_heads * gdn_dk
    tm = _pick_tile(t, 512)
    cos_t, sin_t = _rope_tables(pos)
    prompt = past_k is None
    k_all = jnp.zeros((p["w_in_diff"].shape[0], b, t, diff_heads, 2 * dq), F32)
    v_all = jnp.zeros_like(k_all)
    states, convs = [], []
    for i in range(depth):
        j = i // 2
        mod = ada[i]
        nw_pre = p["norm_pre"][i][None, :]
        nw_post = p["norm_post"][i][None, :]
        if i % 2 == 0:
            a_log = jnp.pad(p["a_log_gdn"][j][None, :], ((0, 0), (0, LANES - gdn_heads)))
            dt_b = jnp.pad(p["dt_bias_gdn"][j][None, :], ((0, 0), (0, LANES - gdn_heads)))
            if conv_bufs is None:
                conv_init = jnp.zeros((b, SUBLANES, n_conv), F32)
                s0 = jnp.zeros((b, gdn_heads, gdn_dk, gdn_dk), F32)
            else:
                conv_init = jnp.pad(conv_bufs[j], ((0, 0), (SUBLANES - (CONV_WIDTH - 1), 0), (0, 0)))
                s0 = gdn_states[j]
            qkvz, gcol, cout = _gdn_in(x, mod, nw_pre, p["w_main_gdn"], p["w_ab_gdn"][j], p["conv_gdn"][j],
                                       conv_init, a_log, dt_b, layer=j, heads=gdn_heads, dk=gdn_dk, tm=tm,
                                       chunk=min(gdn_chunk, tm))
            if t < gdn_chunk:
                front = gdn_chunk - t
                qkvz = jnp.pad(qkvz, ((0, 0), (front, 0), (0, 0)))
                gcol = jnp.pad(gcol, ((0, 0), (front, 0), (0, 0)))
            o, st = _gdn(qkvz, gcol, s0, p["onorm_gdn"][j][None, :], heads=gdn_heads, dk=gdn_dk,
                         tb=_pick_tile(qkvz.shape[1], 256), chunk=gdn_chunk)
            o = o[:, -t:]
            convs.append(cout[:, SUBLANES - (CONV_WIDTH - 1):, :])
            states.append(st)
            w_out = p["w_out_gdn"]
        else:
            lam_init = 0.8 - 0.6 * math.exp(-0.3 * i)
            lam_vecs = jnp.stack([p["lam_q1"][j], p["lam_k1"][j], p["lam_q2"][j], p["lam_k2"][j]])
            qb, kb, k_all, v_all, vb, zs, kn2 = _diff_in(
                x, mod, nw_pre, p["w_in_diff"], cos_t, sin_t, k_all, v_all,
                layer=j, tm=tm, v_transposed=prompt, q_scale=dq ** -0.5 * LOG2E)
            subln = p["subln_diff"][j][None, :]
            if prompt:
                o = _attn_prompt(qb, kb, vb, zs, kn2, lam_vecs, subln, heads=diff_heads, lam_init=lam_init,
                                 tq=_pick_tile(t, 1024), tk=_pick_tile(t, 256))
            else:
                o = _attn_sample(qb, kb, vb, past_k, past_v, zs, lam_vecs, subln, layer=j, heads=diff_heads,
                                 lam_init=lam_init, tkv=_pick_tile(past_k.shape[2], 1024))
            w_out = p["w_out_diff"]
        x = _out_proj(o, w_out, x, mod, nw_post, layer=j, tm=tm)
    return x, jnp.stack(states), jnp.stack(convs), k_all, v_all


def kernel(x_prompt, x_sample, c_prompt, c_sample, state_gdn, cache_conv, cache_k, cache_v, norm_pre, norm_post, w_ada, b_ada, w_in_gdn, conv_gdn, a_log_gdn, dt_bias_gdn, onorm_gdn, w_out_gdn, w_in_diff, lam_q1, lam_k1, lam_q2, lam_k2, subln_diff, w_out_diff):
    gdn_heads = a_log_gdn.shape[1]
    n_main = w_in_gdn.shape[2] - 2 * gdn_heads
    w_ab = jnp.pad(w_in_gdn[:, :, n_main:], ((0, 0), (0, 0), (0, LANES - 2 * gdn_heads)))
    w_ab_hi = w_ab.astype(BF16)
    w_ab_lo = (w_ab - w_ab_hi.astype(F32)).astype(BF16)
    p = {"norm_pre": norm_pre, "norm_post": norm_post,
         "w_main_gdn": w_in_gdn[:, :, :n_main].astype(BF16),
         "w_ab_gdn": jnp.concatenate([w_ab_hi, w_ab_lo], axis=-1),

         "conv_gdn": conv_gdn, "a_log_gdn": a_log_gdn,
         "dt_bias_gdn": dt_bias_gdn, "onorm_gdn": onorm_gdn, "w_out_gdn": w_out_gdn.astype(BF16),
         "w_in_diff": w_in_diff.astype(BF16), "lam_q1": lam_q1, "lam_k1": lam_k1, "lam_q2": lam_q2,
         "lam_k2": lam_k2, "subln_diff": subln_diff, "w_out_diff": w_out_diff.astype(BF16)}
    bp, tp, d = x_prompt.shape
    bs, ts, _ = x_sample.shape
    past = cache_k.shape[2]
    depth = w_ada.shape[0]
    ada = _ada(jnp.concatenate([c_prompt, c_sample], axis=0), w_ada, b_ada)
    ada = ada.reshape(depth, bp + bs, 3, d)
    y_p, st_p, conv_p, k_p, v_p = _trunk(x_prompt, ada[:, :bp], jnp.arange(tp), None, None, None, None, p)
    y_s, st_s, conv_s, k_s, v_s = _trunk(x_sample, ada[:, bp:], past + jnp.arange(ts), cache_conv, state_gdn,
                                         cache_k.reshape(cache_k.shape[:3] + (-1,)),
                                         cache_v.reshape(cache_v.shape[:3] + (-1,)), p)
    return (y_p, y_s, st_p, conv_p, k_p, v_p, st_s, conv_s, k_s, v_s)
```

```python
import functools
import math

import jax
import jax.numpy as jnp
from jax import lax
from jax.experimental import pallas as pl
from jax.experimental.pallas import tpu as pltpu

F32 = jnp.float32
BF16 = jnp.bfloat16
HIGHEST = lax.Precision.HIGHEST

RMS_EPS = 1e-6
L2_EPS = 1e-6
ROPE_THETA = 10000.0
CONV_WIDTH = 4
MASK_CHUNK = 64
LANES = 128
SUBLANES = 8
VMEM_LIMIT = 56 * 1024 * 1024
LOG2E = 1.4426950408889634
REF_SLACK = 1.01
MIN_NORMALISER = 1e-30

NT_DIMS = (((1,), (1,)), ((), ()))


def _cparams(*sem):
    return pltpu.CompilerParams(dimension_semantics=sem, vmem_limit_bytes=VMEM_LIMIT)


def _silu(x):
    return x * jax.nn.sigmoid(x)


def _dot(a, b):
    return jnp.dot(a, b, preferred_element_type=F32)


def _dot_nt(a, b):
    return lax.dot_general(a, b, NT_DIMS, preferred_element_type=F32)


def _ada_kernel(c_ref, w_ref, b_ref, o_ref):
    act = _silu(c_ref[...])
    o_ref[...] = jnp.dot(act, w_ref[...], precision=HIGHEST, preferred_element_type=F32) + b_ref[...]


def _ada(c_all, w_ada, b_ada):
    depth, d, n = w_ada.shape
    bc = c_all.shape[0]
    tn = 512
    return pl.pallas_call(
        _ada_kernel,
        grid=(depth, n // tn),
        in_specs=[pl.BlockSpec((bc, d), lambda i, j: (0, 0)),
                  pl.BlockSpec((None, d, tn), lambda i, j: (i, 0, j)),
                  pl.BlockSpec((None, 1, tn), lambda i, j: (i, 0, j))],
        out_specs=pl.BlockSpec((None, bc, tn), lambda i, j: (i, 0, j)),
        out_shape=jax.ShapeDtypeStruct((depth, bc, n), F32),
        compiler_params=_cparams("parallel", "parallel"),
    )(c_all, w_ada, b_ada.reshape(depth, 1, n))


def _prenorm(x_ref, mod_ref, nw_ref):
    x = x_ref[...]
    y = x * lax.rsqrt(jnp.mean(x * x, axis=-1, keepdims=True) + RMS_EPS) * nw_ref[...]
    return y * (1.0 + mod_ref[1:2, :]) + mod_ref[0:1, :]


def _conv_silu(u, tail, cw, stage_ref):
    tm = u.shape[0]
    stage_ref[0:SUBLANES, :] = tail
    stage_ref[SUBLANES:SUBLANES + tm, :] = u
    y = u * cw[CONV_WIDTH - 1:CONV_WIDTH, :]
    for k in range(1, CONV_WIDTH):
        y = y + stage_ref[SUBLANES - k:SUBLANES - k + tm, :] * cw[CONV_WIDTH - 1 - k:CONV_WIDTH - k, :]
    return _silu(y)


def _gdn_in_kernel(x_ref, mod_ref, nw_ref, w_ref, wab_ref, cw_ref, cinit_ref, alog_ref, dtb_ref,
                   qkvz_ref, gcol_ref, cout_ref, tail_sc, stage_sc, *, heads, dk, chunk, tn):
    t = pl.program_id(1)
    tm = x_ref.shape[0]
    n = w_ref.shape[1]
    nb = n // 4

    @pl.when(t == 0)
    def _():
        tail_sc[...] = cinit_ref[...]

    h = _prenorm(x_ref, mod_ref, nw_ref)
    hb = h.astype(BF16)
    h_lo = (h - hb.astype(F32)).astype(BF16)
    hi_lo = _dot(hb, wab_ref[...])
    ab = hi_lo[:, :LANES] + hi_lo[:, LANES:] + _dot(h_lo, wab_ref[:, :LANES])
    g = -jnp.exp(alog_ref[...]) * jax.nn.softplus(ab + dtb_ref[...])
    row = lax.broadcasted_iota(jnp.int32, g.shape, 0) % chunk
    s = 1
    while s < chunk:
        g = g + jnp.where(row >= s, pltpu.roll(g, s, axis=0), 0.0)
        s *= 2
    lane = lax.broadcasted_iota(jnp.int32, g.shape, 1)
    gcol_ref[...] = jnp.where(lane < heads, g, jax.nn.sigmoid(ab))

    for grp in range(n // tn):
        cs = slice(grp * tn, (grp + 1) * tn)
        kind = (grp * tn) // nb
        u = _dot(hb, w_ref[:, cs])
        if kind == 3:
            qkvz_ref[:, cs] = _silu(u).astype(BF16)
            continue
        y = _conv_silu(u, tail_sc[:, cs], cw_ref[:, cs], stage_sc.at[grp % 2])
        tail_sc[:, cs] = u[tm - SUBLANES:tm]
        cout_ref[:, cs] = u[tm - SUBLANES:tm]
        if kind == 2:
            qkvz_ref[:, cs] = y.astype(BF16)
            continue
        scale = dk ** -0.5 if kind == 0 else 1.0
        for hh in range(tn // dk):
            ys = y[:, hh * dk:(hh + 1) * dk]
            inv = lax.rsqrt(jnp.sum(ys * ys, axis=-1, keepdims=True) + L2_EPS) * scale
            qkvz_ref[:, grp * tn + hh * dk:grp * tn + (hh + 1) * dk] = (ys * inv).astype(BF16)


def _gdn_in(x, mod, nw, w_main, w_ab, conv_w, conv_init, a_log, dt_bias, *, layer, heads, dk, tm, chunk):
    b, t, d = x.shape
    n = w_main.shape[2]
    nb = n // 4
    tn = 4 * dk
    assert nb == heads * dk and nb % tn == 0 and t % tm == 0 and tm % SUBLANES == 0 and tm % chunk == 0
    kern = functools.partial(_gdn_in_kernel, heads=heads, dk=dk, chunk=chunk, tn=tn)
    return pl.pallas_call(
        kern,
        grid=(b, t // tm),
        in_specs=[pl.BlockSpec((None, tm, d), lambda bi, ti: (bi, ti, 0)),
                  pl.BlockSpec((None, 3, d), lambda bi, ti: (bi, 0, 0)),
                  pl.BlockSpec((1, d), lambda bi, ti: (0, 0)),
                  pl.BlockSpec((None, d, n), lambda bi, ti: (layer, 0, 0)),
                  pl.BlockSpec((d, 2 * LANES), lambda bi, ti: (0, 0)),
                  pl.BlockSpec((CONV_WIDTH, 3 * nb), lambda bi, ti: (0, 0)),
                  pl.BlockSpec((None, SUBLANES, 3 * nb), lambda bi, ti: (bi, 0, 0)),
                  pl.BlockSpec((1, LANES), lambda bi, ti: (0, 0)),
                  pl.BlockSpec((1, LANES), lambda bi, ti: (0, 0))],
        out_specs=[pl.BlockSpec((None, tm, n), lambda bi, ti: (bi, ti, 0)),
                   pl.BlockSpec((None, tm, LANES), lambda bi, ti: (bi, ti, 0)),
                   pl.BlockSpec((None, SUBLANES, 3 * nb), lambda bi, ti: (bi, 0, 0))],
        out_shape=[jax.ShapeDtypeStruct((b, t, n), BF16),
                   jax.ShapeDtypeStruct((b, t, LANES), F32),
                   jax.ShapeDtypeStruct((b, SUBLANES, 3 * nb), F32)],
        scratch_shapes=[pltpu.VMEM((SUBLANES, 3 * nb), F32), pltpu.VMEM((2, SUBLANES + tm, tn), F32)],
        compiler_params=_cparams("parallel", "arbitrary"),
    )(x, mod, nw, w_main, w_ab, conv_w, conv_init, a_log, dt_bias)


def _gdn_kernel(q_ref, k_ref, v_ref, z_ref, gcol_ref, s0_ref, ow_ref, o_ref, s_ref, *, heads, dk, chunk):
    t = pl.program_id(1)
    tb = q_ref.shape[0]
    c = chunk
    group = 2 if (tb // c) % 2 == 0 else 1

    @pl.when(t == 0)
    def _():
        s_ref[...] = s0_ref[...]

    ii = lax.broadcasted_iota(jnp.int32, (c, c), 0)
    jj = lax.broadcasted_iota(jnp.int32, (c, c), 1)
    incl = ii >= jj
    strict = ii > jj
    eye = (ii == jj).astype(F32)
    off_masks = []
    s = 1
    while s < c:
        off_masks.append((ii // (2 * s) == jj // (2 * s)) & (ii // s != jj // s) & strict)
        s *= 2

    def chunk_group(ci, carry):
        rows = [pl.ds(pl.multiple_of((ci * group + k) * c, c), c) for k in range(group)]
        hsl = [slice(hh * dk, (hh + 1) * dk) for hh in range(heads)]
        units = [(k, hh) for k in range(group) for hh in range(heads)]
        gall = [gcol_ref[r, :] for r in rows]
        g = [gall[k][:, hh:hh + 1] for k, hh in units]
        beta = [gall[k][:, heads + hh:heads + hh + 1] for k, hh in units]
        kbf = [k_ref[rows[k], hsl[hh]] for k, hh in units]
        qbf = [q_ref[rows[k], hsl[hh]] for k, hh in units]
        kf = [x.astype(F32) for x in kbf]
        kb = [x * y for x, y in zip(kf, beta)]
        gb = [jnp.broadcast_to(x, (c, c)) for x in g]
        decay = [jnp.exp(jnp.where(incl, x - x.T, -jnp.inf)) for x in gb]
        a = [jnp.where(strict, _dot_nt(x.astype(BF16), y) * d, 0.0) for x, y, d in zip(kb, kbf, decay)]
        qk = [(_dot_nt(x, y) * d).astype(BF16) for x, y, d in zip(qbf, kbf, decay)]
        p = [eye - jnp.where(off_masks[0], x, 0.0) for x in a]
        for off in off_masks[1:]:
            pb = [x.astype(BF16) for x in p]
            x = [_dot(jnp.where(off, ai, 0.0).astype(BF16), pi).astype(BF16) for ai, pi in zip(a, pb)]
            p = [pi - _dot(pbi, xi) for pi, pbi, xi in zip(p, pb, x)]
        eg = [jnp.exp(x) for x in g]
        rhs = [jnp.concatenate([v_ref[rows[k], hsl[hh]].astype(F32) * beta[u], kb[u] * eg[u]], axis=1).astype(BF16)
               for u, (k, hh) in enumerate(units)]
        sol = [_dot(x.astype(BF16), y) for x, y in zip(p, rhs)]
        wq = [jnp.concatenate([sol[u][:, dk:].astype(BF16), (qbf[u].astype(F32) * eg[u]).astype(BF16)], axis=0)
              for u in range(len(units))]
        g_last = [x[c - 1:c, :] for x in g]
        kd_t = [(kf[u] * jnp.exp(g_last[u] - g[u])).T.astype(BF16) for u in range(len(units))]
        e_last = [jnp.exp(x) for x in g_last]
        for k in range(group):
            us = [k * heads + hh for hh in range(heads)]
            s_old = [s_ref[hh] for hh in range(heads)]
            r1 = [_dot(wq[u], s_old[hh].astype(BF16)) for hh, u in enumerate(us)]
            ub = [(sol[u][:, :dk] - r1[hh][:c]).astype(BF16) for hh, u in enumerate(us)]
            for hh, u in enumerate(us):
                s_ref[hh] = s_old[hh] * e_last[u] + _dot(kd_t[u], ub[hh])
            o = [r1[hh][c:] + _dot(qk[u], ub[hh]) for hh, u in enumerate(us)]
            for hh in range(heads):
                on = o[hh] * lax.rsqrt(jnp.mean(o[hh] * o[hh], axis=-1, keepdims=True) + RMS_EPS) * ow_ref[...]
                o_ref[rows[k], hsl[hh]] = (on * z_ref[rows[k], hsl[hh]].astype(F32)).astype(BF16)
        return carry

    lax.fori_loop(0, tb // (c * group), chunk_group, 0)


def _gdn(qkvz, gcol, s0, onorm, *, heads, dk, tb, chunk):
    b, t, n = qkvz.shape
    nb = n // 4
    assert t % tb == 0 and tb % chunk == 0 and nb == heads * dk
    kern = functools.partial(_gdn_kernel, heads=heads, dk=dk, chunk=chunk)
    col = lambda jcol: pl.BlockSpec((None, tb, nb), lambda bi, ti: (bi, ti, jcol))
    return pl.pallas_call(
        kern,
        grid=(b, t // tb),
        in_specs=[col(0), col(1), col(2), col(3),
                  pl.BlockSpec((None, tb, LANES), lambda bi, ti: (bi, ti, 0)),
                  pl.BlockSpec((None, heads, dk, dk), lambda bi, ti: (bi, 0, 0, 0)),
                  pl.BlockSpec((1, dk), lambda bi, ti: (0, 0))],
        out_specs=[pl.BlockSpec((None, tb, nb), lambda bi, ti: (bi, ti, 0)),
                   pl.BlockSpec((None, heads, dk, dk), lambda bi, ti: (bi, 0, 0, 0))],
        out_shape=[jax.ShapeDtypeStruct((b, t, nb), BF16),
                   jax.ShapeDtypeStruct((b, heads, dk, dk), F32)],
        compiler_params=_cparams("parallel", "arbitrary"),
    )(qkvz, qkvz, qkvz, qkvz, gcol, s0, onorm)


def _rope(x, cos, sin_signed, lane_lo):
    half = MASK_CHUNK // 2
    swapped = jnp.where(lane_lo, pltpu.roll(x, LANES - half, axis=1), pltpu.roll(x, half, axis=1))
    return x * cos + swapped * sin_signed


def _diff_in_kernel(x_ref, mod_ref, nw_ref, w_ref, cos_ref, sin_ref, k_all_ref, v_all_ref,
                    qb_ref, kb_ref, kf_ref, vf_ref, vb_ref, zs_ref, kn2_ref, *, v_transposed, q_scale, tn):
    del k_all_ref, v_all_ref
    n = w_ref.shape[1]
    nb = n // 4
    hb = _prenorm(x_ref, mod_ref, nw_ref).astype(BF16)
    cos = cos_ref[...]
    sin = sin_ref[...]
    lane_lo = (lax.broadcasted_iota(jnp.int32, cos.shape, 1) % MASK_CHUNK) < (MASK_CHUNK // 2)
    kn2 = jnp.zeros((1, LANES), F32)
    for grp in range(n // tn):
        kind = (grp * tn) // nb
        lo = grp * tn - kind * nb
        u = _dot(hb, w_ref[:, grp * tn:(grp + 1) * tn])
        squares = []
        for s in range(tn // LANES):
            us = u[:, s * LANES:(s + 1) * LANES]
            cols = slice(lo + s * LANES, lo + (s + 1) * LANES)
            head = (lo + s * LANES) // LANES
            if kind == 0:
                qb_ref[:, cols] = (_rope(us, cos, sin, lane_lo) * q_scale).astype(BF16)
            elif kind == 1:
                r = _rope(us, cos, sin, lane_lo)
                kf_ref[:, head, :] = r
                kb_ref[:, cols] = r.astype(BF16)
                squares.append(r * r)
            elif kind == 2:
                vf_ref[:, head, :] = us
            else:
                zs_ref[:, cols] = _silu(us).astype(BF16)
        if kind == 1:
            group_sum = ((lax.broadcasted_iota(jnp.int32, (tn, LANES), 0) + lo) // MASK_CHUNK
                         == lax.broadcasted_iota(jnp.int32, (tn, LANES), 1)).astype(BF16)
            n2 = _dot(jnp.concatenate(squares, axis=1).astype(BF16), group_sum)
            kn2 = jnp.maximum(kn2, jnp.max(n2, axis=0, keepdims=True))
        if kind == 2:
            if v_transposed:
                vb_ref[lo:lo + tn, :] = u.T.astype(BF16)
            else:
                vb_ref[:, lo:lo + tn] = u.astype(BF16)
    kn2_ref[...] = kn2


def _diff_in(x, mod, nw, w, cos_t, sin_t, k_all, v_all, *, layer, tm, v_transposed, q_scale):
    b, t, d = x.shape
    nb = w.shape[2] // 4
    heads = k_all.shape[3]
    assert t % tm == 0 and k_all.shape[3:] == (nb // LANES, LANES)
    tn = 4 * LANES
    assert nb % tn == 0
    kern = functools.partial(_diff_in_kernel, v_transposed=v_transposed, q_scale=q_scale, tn=tn)
    tile = pl.BlockSpec((None, tm, nb), lambda bi, ti: (bi, ti, 0))
    cache_tile = pl.BlockSpec((None, None, tm, heads, LANES), lambda bi, ti: (layer, bi, ti, 0, 0))
    untouched = pl.BlockSpec(memory_space=pl.ANY)
    if v_transposed:
        vb_spec = pl.BlockSpec((None, nb, tm), lambda bi, ti: (bi, 0, ti))
        vb_shape = jax.ShapeDtypeStruct((b, nb, t), BF16)
    else:
        vb_spec, vb_shape = tile, jax.ShapeDtypeStruct((b, t, nb), BF16)
    return pl.pallas_call(
        kern,
        grid=(b, t // tm),
        in_specs=[pl.BlockSpec((None, tm, d), lambda bi, ti: (bi, ti, 0)),
                  pl.BlockSpec((None, 3, d), lambda bi, ti: (bi, 0, 0)),
                  pl.BlockSpec((1, d), lambda bi, ti: (0, 0)),
                  pl.BlockSpec((None, d, 4 * nb), lambda bi, ti: (layer, 0, 0)),
                  pl.BlockSpec((tm, LANES), lambda bi, ti: (ti, 0)),
                  pl.BlockSpec((tm, LANES), lambda bi, ti: (ti, 0)),
                  untouched, untouched],
        out_specs=[tile, tile, cache_tile, cache_tile, vb_spec, tile,
                   pl.BlockSpec((None, None, 1, LANES), lambda bi, ti: (bi, ti, 0, 0))],
        out_shape=[jax.ShapeDtypeStruct((b, t, nb), BF16),
                   jax.ShapeDtypeStruct((b, t, nb), BF16),
                   jax.ShapeDtypeStruct(k_all.shape, F32),
                   jax.ShapeDtypeStruct(v_all.shape, F32),
                   vb_shape,
                   jax.ShapeDtypeStruct((b, t, nb), BF16),
                   jax.ShapeDtypeStruct((b, t // tm, 1, LANES), F32)],
        input_output_aliases={6: 2, 7: 3},
        compiler_params=_cparams("parallel", "parallel"),
    )(x, mod, nw, w, cos_t, sin_t, k_all, v_all)


def _lambda(lam_ref, lam_init):
    l1 = jnp.sum(lam_ref[0:1, :] * lam_ref[1:2, :], axis=-1, keepdims=True)
    l2 = jnp.sum(lam_ref[2:3, :] * lam_ref[3:4, :], axis=-1, keepdims=True)
    return jnp.exp(l1) - jnp.exp(l2) + lam_init


def _attn_prompt_kernel(q_ref, k_ref, vt_ref, z_ref, kn2_ref, lam_ref, sw_ref, o_ref,
                        m_sc, l_sc, acc_sc, *, lam_init, tk):
    qi = pl.program_id(2)
    tq, dh = q_ref.shape
    dq = dh // 2
    nsub = tq // tk
    chains = [(sub, comp) for sub in range(nsub) for comp in range(2)]
    everyone = list(range(len(chains)))
    n_full = (qi * tq) // tk
    qz = []
    for sub, comp in chains:
        q = q_ref[sub * tk:(sub + 1) * tk, :]
        lane = lax.broadcasted_iota(jnp.int32, q.shape, 1)
        qz.append(jnp.where((lane < dq) if comp == 0 else (lane >= dq), q, jnp.zeros_like(q)))

    kn2_max = jnp.max(kn2_ref[...], axis=0)
    kn2_lane = lax.broadcasted_iota(jnp.int32, kn2_max.shape, 1)
    kmax2 = [jnp.max(jnp.where(kn2_lane == 2 * pl.program_id(1) + comp, kn2_max, 0.0), axis=1, keepdims=True)
             for comp in range(2)]

    ones = jnp.ones((SUBLANES, dh), BF16)
    ref_rows = []
    for c, (sub, comp) in enumerate(chains):
        qf = qz[c].astype(F32)
        qn2 = _dot_nt(ones, (qf * qf).astype(BF16))[0:1, :]
        ref_rows.append(REF_SLACK * jnp.sqrt(qn2 * kmax2[comp]))

    visible = (lax.broadcasted_iota(jnp.int32, (tk, tk), 0) // MASK_CHUNK
               <= lax.broadcasted_iota(jnp.int32, (tk, tk), 1) // MASK_CHUNK)

    def blocks(kb):
        ks = k_ref[pl.ds(pl.multiple_of(kb * tk, tk), tk), :]
        vt = vt_ref[:, pl.ds(pl.multiple_of(kb * tk, tk), tk)]
        return ks, vt

    def fast_step(kb, active, masked_sub):
        ks, vt = blocks(kb)
        s = [_dot_nt(ks, qz[c]) for c in active]
        p = [jnp.exp2(si - ref_rows[c]) for c, si in zip(active, s)]
        p = [jnp.where(visible, pi, 0.0) if chains[c][0] == masked_sub else pi for c, pi in zip(active, p)]
        for c, pi in zip(active, p):
            l_sc[c] += jnp.sum(pi.reshape(tk // SUBLANES, SUBLANES, tk), axis=0)
        pv = [_dot(vt, pi.astype(BF16)) for pi in p]
        for c, pvi in zip(active, pv):
            acc_sc[c] += pvi

    def online_step(kb, active, masked_sub):
        ks, vt = blocks(kb)
        for c in active:
            s = _dot_nt(ks, qz[c])
            if chains[c][0] == masked_sub:
                s = jnp.where(visible, s, -jnp.inf)
            m_old = m_sc[c]
            m_new = jnp.maximum(m_old, jnp.max(s, axis=0, keepdims=True))
            alpha = jnp.exp2(m_old - m_new)
            p = jnp.exp2(s - m_new)
            l_sc[c, 0:1, :] = alpha * l_sc[c, 0:1, :] + jnp.sum(p, axis=0, keepdims=True)
            acc_sc[c] = alpha * acc_sc[c] + _dot(vt, p.astype(BF16))
            m_sc[c] = m_new

    def sweep(step, unroll):
        l_sc[...] = jnp.zeros(l_sc.shape, F32)
        acc_sc[...] = jnp.zeros(acc_sc.shape, F32)
        def body(i, cry):
            for u in range(unroll):
                step(i * unroll + u, everyone, None)
            return cry
        lax.fori_loop(0, n_full // unroll, body, 0)
        for d in range(nsub):
            step(n_full + d, [c for c in everyone if chains[c][0] >= d], d)

    def normalisers():
        return [jnp.sum(l_sc[c], axis=0, keepdims=True) for c in everyone]

    sweep(fast_step, nsub)
    l_min = functools.reduce(jnp.minimum, normalisers())

    @pl.when(jnp.min(l_min) < MIN_NORMALISER)
    def _():
        m_sc[...] = jnp.full(m_sc.shape, -jnp.inf, F32)
        sweep(online_step, 1)

    lam = _lambda(lam_ref, lam_init)
    l = normalisers()
    for sub in range(nsub):
        c1, c2 = 2 * sub, 2 * sub + 1
        o_t = acc_sc[c1] / l[c1] - lam * (acc_sc[c2] / l[c2])
        o = o_t.T
        o = o * lax.rsqrt(jnp.mean(o * o, axis=-1, keepdims=True) + RMS_EPS) * sw_ref[...] * (1.0 - lam_init)
        rows = slice(sub * tk, (sub + 1) * tk)
        o_ref[rows, :] = (o * z_ref[rows, :].astype(F32)).astype(BF16)


def _attn_prompt(qb, kb, vt, zs, kn2, lam_vecs, subln, *, heads, lam_init, tq, tk):
    b, t, n = qb.shape
    dh = n // heads
    assert t % tq == 0 and tq % tk == 0 and tk % MASK_CHUNK == 0
    nchains = 2 * (tq // tk)
    kern = functools.partial(_attn_prompt_kernel, lam_init=lam_init, tk=tk)
    return pl.pallas_call(
        kern,
        grid=(b, heads, t // tq),
        in_specs=[pl.BlockSpec((None, tq, dh), lambda bi, hi, qi: (bi, qi, hi)),
                  pl.BlockSpec((None, t, dh), lambda bi, hi, qi: (bi, 0, hi)),
                  pl.BlockSpec((None, dh, t), lambda bi, hi, qi: (bi, hi, 0)),
                  pl.BlockSpec((None, tq, dh), lambda bi, hi, qi: (bi, qi, hi)),
                  pl.BlockSpec((None,) + kn2.shape[1:], lambda bi, hi, qi: (bi, 0, 0, 0)),
                  pl.BlockSpec(lam_vecs.shape, lambda bi, hi, qi: (0, 0)),
                  pl.BlockSpec((1, dh), lambda bi, hi, qi: (0, 0))],
        out_specs=pl.BlockSpec((None, tq, dh), lambda bi, hi, qi: (bi, qi, hi)),
        out_shape=jax.ShapeDtypeStruct((b, t, n), BF16),
        scratch_shapes=[pltpu.VMEM((nchains, 1, tk), F32), pltpu.VMEM((nchains, SUBLANES, tk), F32),
                        pltpu.VMEM((nchains, dh, tk), F32)],
        compiler_params=_cparams("parallel", "parallel", "arbitrary"),
    )(qb, kb, vt, zs, kn2, lam_vecs, subln)


def _attn_sample_kernel(q_ref, kn_ref, vn_ref, ck_ref, cv_ref, z_ref, lam_ref, sw_ref, o_ref,
                        m_sc, l_sc, acc_sc, *, heads, lam_init):
    si = pl.program_id(1)
    ns = pl.num_programs(1)
    tq, n = q_ref.shape
    dh = n // heads
    dq = dh // 2

    @pl.when(si == 0)
    def _():
        m_sc[...] = jnp.full(m_sc.shape, -jnp.inf, F32)
        l_sc[...] = jnp.zeros(l_sc.shape, F32)
        acc_sc[...] = jnp.zeros(acc_sc.shape, F32)

    def update(hh, kblk, vblk):
        hs = slice(hh * dh, (hh + 1) * dh)
        q = q_ref[:, hs]
        lane = lax.broadcasted_iota(jnp.int32, q.shape, 1)
        q2 = jnp.concatenate([jnp.where(lane < dq, q, jnp.zeros_like(q)),
                              jnp.where(lane >= dq, q, jnp.zeros_like(q))], axis=0)
        s = _dot_nt(q2, kblk)
        m_old = m_sc[hh]
        m_new = jnp.maximum(m_old, jnp.max(s, axis=-1, keepdims=True))
        alpha = jnp.exp2(m_old - m_new)
        p = jnp.exp2(s - m_new)
        l_sc[hh] = alpha * l_sc[hh] + jnp.sum(p, axis=-1, keepdims=True)
        acc_sc[hh] = alpha * acc_sc[hh] + _dot(p.astype(BF16), vblk)
        m_sc[hh] = m_new

    for hh in range(heads):
        update(hh, ck_ref[:, hh, :].astype(BF16), cv_ref[:, hh, :].astype(BF16))

    @pl.when(si == ns - 1)
    def _():
        lam = _lambda(lam_ref, lam_init)
        for hh in range(heads):
            hs = slice(hh * dh, (hh + 1) * dh)
            update(hh, kn_ref[:, hs], vn_ref[:, hs])
            on = acc_sc[hh] / l_sc[hh]
            o = on[:tq] - lam * on[tq:]
            o = o * lax.rsqrt(jnp.mean(o * o, axis=-1, keepdims=True) + RMS_EPS) * sw_ref[...] * (1.0 - lam_init)
            o_ref[:, hs] = (o * z_ref[:, hs].astype(F32)).astype(BF16)


def _attn_sample(qb, kb, vb, cache_k, cache_v, zs, lam_vecs, subln, *, layer, heads, lam_init, tkv):
    b, tq, n = qb.shape
    past = cache_k.shape[2]
    dh = n // heads
    assert past % tkv == 0 and cache_k.shape[3:] == (heads, dh)
    kern = functools.partial(_attn_sample_kernel, heads=heads, lam_init=lam_init)
    new = pl.BlockSpec((None, tq, n), lambda bi, si: (bi, 0, 0))
    old = pl.BlockSpec((None, None, tkv, heads, dh), lambda bi, si: (layer, bi, si, 0, 0))
    return pl.pallas_call(
        kern,
        grid=(b, past // tkv),
        in_specs=[new, new, new, old, old, new,
                  pl.BlockSpec(lam_vecs.shape, lambda bi, si: (0, 0)),
                  pl.BlockSpec((1, dh), lambda bi, si: (0, 0))],
        out_specs=new,
        out_shape=jax.ShapeDtypeStruct((b, tq, n), BF16),
        scratch_shapes=[pltpu.VMEM((heads, 2 * tq, 1), F32), pltpu.VMEM((heads, 2 * tq, 1), F32),
                        pltpu.VMEM((heads, 2 * tq, dh), F32)],
        compiler_params=_cparams("parallel", "arbitrary"),
    )(qb, kb, vb, cache_k, cache_v, zs, lam_vecs, subln)


def _out_kernel(o_ref, w_ref, x_ref, mod_ref, nw_ref, y_ref):
    y = _dot(o_ref[...], w_ref[...])
    yn = y * lax.rsqrt(jnp.mean(y * y, axis=-1, keepdims=True) + RMS_EPS) * nw_ref[...]
    y_ref[...] = x_ref[...] + mod_ref[2:3, :] * yn


def _out_proj(o, w, x, mod, nw, *, layer, tm):
    b, t, d = x.shape
    n = o.shape[2]
    return pl.pallas_call(
        _out_kernel,
        grid=(b, t // tm),
        in_specs=[pl.BlockSpec((None, tm, n), lambda bi, ti: (bi, ti, 0)),
                  pl.BlockSpec((None, n, d), lambda bi, ti: (layer, 0, 0)),
                  pl.BlockSpec((None, tm, d), lambda bi, ti: (bi, ti, 0)),
                  pl.BlockSpec((None, 3, d), lambda bi, ti: (bi, 0, 0)),
                  pl.BlockSpec((1, d), lambda bi, ti: (0, 0))],
        out_specs=pl.BlockSpec((None, tm, d), lambda bi, ti: (bi, ti, 0)),
        out_shape=jax.ShapeDtypeStruct((b, t, d), F32),
        compiler_params=_cparams("parallel", "parallel"),
    )(o, w, x, mod, nw)


def _pick_tile(t, target):
    tile = min(t, target)
    assert t % tile == 0
    return tile


def _rope_tables(pos):
    half = MASK_CHUNK // 2
    inv = 1.0 / (ROPE_THETA ** (jnp.arange(half, dtype=F32) / half))
    ang = pos.astype(F32)[:, None] * inv[None, :]
    cos, sin = jnp.cos(ang), jnp.sin(ang)
    reps = LANES // MASK_CHUNK
    return (jnp.tile(jnp.concatenate([cos, cos], axis=1), (1, reps)),
            jnp.tile(jnp.concatenate([-sin, sin], axis=1), (1, reps)))


def _trunk(x, ada, pos, conv_bufs, gdn_states, past_k, past_v, p):
    b, t, d = x.shape
    depth = ada.shape[0]
    gdn_heads = p["a_log_gdn"].shape[1]
    gdn_dk = p["onorm_gdn"].shape[1]
    dq = p["lam_q1"].shape[1]
    n_diff = p["w_out_diff"].shape[1]
    diff_heads = n_diff // (2 * dq)
    gdn_chunk = LANES
    n_conv = 3 * gdn_heads * gdn_dk
    tm = _pick_tile(t, 512)
    cos_t, sin_t = _rope_tables(pos)
    prompt = past_k is None
    k_all = jnp.zeros((p["w_in_diff"].shape[0], b, t, diff_heads, 2 * dq), F32)
    v_all = jnp.zeros_like(k_all)
    states, convs = [], []
    for i in range(depth):
        j = i // 2
        mod = ada[i]
        nw_pre = p["norm_pre"][i][None, :]
        nw_post = p["norm_post"][i][None, :]
        if i % 2 == 0:
            a_log = jnp.pad(p["a_log_gdn"][j][None, :], ((0, 0), (0, LANES - gdn_heads)))
            dt_b = jnp.pad(p["dt_bias_gdn"][j][None, :], ((0, 0), (0, LANES - gdn_heads)))
            if conv_bufs is None:
                conv_init = jnp.zeros((b, SUBLANES, n_conv), F32)
                s0 = jnp.zeros((b, gdn_heads, gdn_dk, gdn_dk), F32)
            else:
                conv_init = jnp.pad(conv_bufs[j], ((0, 0), (SUBLANES - (CONV_WIDTH - 1), 0), (0, 0)))
                s0 = gdn_states[j]
            qkvz, gcol, cout = _gdn_in(x, mod, nw_pre, p["w_main_gdn"], p["w_ab_gdn"][j], p["conv_gdn"][j],
                                       conv_init, a_log, dt_b, layer=j, heads=gdn_heads, dk=gdn_dk, tm=tm,
                                       chunk=min(gdn_chunk, tm))
            if t < gdn_chunk:
                front = gdn_chunk - t
                qkvz = jnp.pad(qkvz, ((0, 0), (front, 0), (0, 0)))
                gcol = jnp.pad(gcol, ((0, 0), (front, 0), (0, 0)))
            o, st = _gdn(qkvz, gcol, s0, p["onorm_gdn"][j][None, :], heads=gdn_heads, dk=gdn_dk,
                         tb=_pick_tile(qkvz.shape[1], 256), chunk=gdn_chunk)
            o = o[:, -t:]
            convs.append(cout[:, SUBLANES - (CONV_WIDTH - 1):, :])
            states.append(st)
            w_out = p["w_out_gdn"]
        else:
            lam_init = 0.8 - 0.6 * math.exp(-0.3 * i)
            lam_vecs = jnp.stack([p["lam_q1"][j], p["lam_k1"][j], p["lam_q2"][j], p["lam_k2"][j]])
            qb, kb, k_all, v_all, vb, zs, kn2 = _diff_in(
                x, mod, nw_pre, p["w_in_diff"], cos_t, sin_t, k_all, v_all,
                layer=j, tm=tm, v_transposed=prompt, q_scale=dq ** -0.5 * LOG2E)
            subln = p["subln_diff"][j][None, :]
            if prompt:
                o = _attn_prompt(qb, kb, vb, zs, kn2, lam_vecs, subln, heads=diff_heads, lam_init=lam_init,
                                 tq=_pick_tile(t, 1024), tk=_pick_tile(t, 256))
            else:
                o = _attn_sample(qb, kb, vb, past_k, past_v, zs, lam_vecs, subln, layer=j, heads=diff_heads,
                                 lam_init=lam_init, tkv=_pick_tile(past_k.shape[2], 1024))
            w_out = p["w_out_diff"]
        x = _out_proj(o, w_out, x, mod, nw_post, layer=j, tm=tm)
    return x, jnp.stack(states), jnp.stack(convs), k_all, v_all


def kernel(x_prompt, x_sample, c_prompt, c_sample, state_gdn, cache_conv, cache_k, cache_v, norm_pre, norm_post, w_ada, b_ada, w_in_gdn, conv_gdn, a_log_gdn, dt_bias_gdn, onorm_gdn, w_out_gdn, w_in_diff, lam_q1, lam_k1, lam_q2, lam_k2, subln_diff, w_out_diff):
    gdn_heads = a_log_gdn.shape[1]
    n_main = w_in_gdn.shape[2] - 2 * gdn_heads
    w_ab = jnp.pad(w_in_gdn[:, :, n_main:], ((0, 0), (0, 0), (0, LANES - 2 * gdn_heads)))
    w_ab_hi = w_ab.astype(BF16)
    w_ab_lo = (w_ab - w_ab_hi.astype(F32)).astype(BF16)
    p = {"norm_pre": norm_pre, "norm_post": norm_post,
         "w_main_gdn": w_in_gdn[:, :, :n_main].astype(BF16),
         "w_ab_gdn": jnp.concatenate([w_ab_hi, w_ab_lo], axis=-1),

         "conv_gdn": conv_gdn, "a_log_gdn": a_log_gdn,
         "dt_bias_gdn": dt_bias_gdn, "onorm_gdn": onorm_gdn, "w_out_gdn": w_out_gdn.astype(BF16),
         "w_in_diff": w_in_diff.astype(BF16), "lam_q1": lam_q1, "lam_k1": lam_k1, "lam_q2": lam_q2,
         "lam_k2": lam_k2, "subln_diff": subln_diff, "w_out_diff": w_out_diff.astype(BF16)}
    bp, tp, d = x_prompt.shape
    bs, ts, _ = x_sample.shape
    past = cache_k.shape[2]
    depth = w_ada.shape[0]
    ada = _ada(jnp.concatenate([c_prompt, c_sample], axis=0), w_ada, b_ada)
    ada = ada.reshape(depth, bp + bs, 3, d)
    y_p, st_p, conv_p, k_p, v_p = _trunk(x_prompt, ada[:, :bp], jnp.arange(tp), None, None, None, None, p)
    y_s, st_s, conv_s, k_s, v_s = _trunk(x_sample, ada[:, bp:], past + jnp.arange(ts), cache_conv, state_gdn,
                                         cache_k, cache_v, p)
    return (y_p, y_s, st_p, conv_p, k_p, v_p, st_s, conv_s, k_s, v_s)
```

```python
import functools
import math

import jax
import jax.numpy as jnp
from jax import lax
from jax.experimental import pallas as pl
from jax.experimental.pallas import tpu as pltpu

F32 = jnp.float32
BF16 = jnp.bfloat16
HIGHEST = lax.Precision.HIGHEST

RMS_EPS = 1e-6
L2_EPS = 1e-6
ROPE_THETA = 10000.0
CONV_WIDTH = 4
MASK_CHUNK = 64
LANES = 128
SUBLANES = 8
VMEM_LIMIT = 56 * 1024 * 1024
LOG2E = 1.4426950408889634
REF_SLACK = 1.01
MIN_NORMALISER = 1e-30

NT_DIMS = (((1,), (1,)), ((), ()))


def _cparams(*sem):
    return pltpu.CompilerParams(dimension_semantics=sem, vmem_limit_bytes=VMEM_LIMIT)


def _silu(x):
    return x * jax.nn.sigmoid(x)


def _dot(a, b):
    return jnp.dot(a, b, preferred_element_type=F32)


def _dot_nt(a, b):
    return lax.dot_general(a, b, NT_DIMS, preferred_element_type=F32)


def _ada_kernel(c_ref, w_ref, b_ref, o_ref):
    act = _silu(c_ref[...])
    o_ref[...] = jnp.dot(act, w_ref[...], precision=HIGHEST, preferred_element_type=F32) + b_ref[...]


def _ada(c_all, w_ada, b_ada):
    depth, d, n = w_ada.shape
    bc = c_all.shape[0]
    tn = 512
    return pl.pallas_call(
        _ada_kernel,
        grid=(depth, n // tn),
        in_specs=[pl.BlockSpec((bc, d), lambda i, j: (0, 0)),
                  pl.BlockSpec((None, d, tn), lambda i, j: (i, 0, j)),
                  pl.BlockSpec((None, 1, tn), lambda i, j: (i, 0, j))],
        out_specs=pl.BlockSpec((None, bc, tn), lambda i, j: (i, 0, j)),
        out_shape=jax.ShapeDtypeStruct((depth, bc, n), F32),
        compiler_params=_cparams("parallel", "parallel"),
    )(c_all, w_ada, b_ada.reshape(depth, 1, n))


def _residual_update(x, o_ref, w_ref, mod_ref, nw_ref):
    y = _dot(o_ref[...], w_ref[...])
    yn = y * lax.rsqrt(jnp.mean(y * y, axis=-1, keepdims=True) + RMS_EPS) * nw_ref[...]
    return x + mod_ref[2:3, :] * yn


def _prenorm(x, mod_ref, nw_ref):
    y = x * lax.rsqrt(jnp.mean(x * x, axis=-1, keepdims=True) + RMS_EPS) * nw_ref[...]
    return y * (1.0 + mod_ref[1:2, :]) + mod_ref[0:1, :]


def _layer_input(refs, fused, n_in):
    x_ref, mod_ref, nw_ref = refs[:3]
    pos = 7 if fused else 3
    own_in = refs[pos:pos + n_in]
    pos += n_in
    x = x_ref[...]
    if fused:
        x = _residual_update(x, *refs[3:7])
        refs[pos][...] = x
        pos += 1
    return x, mod_ref, nw_ref, own_in, refs[pos:]


def _fused_specs(prev, b, t, d, tm):
    o, w, layer, mod, nw = prev
    n = o.shape[2]
    in_specs = [pl.BlockSpec((None, tm, n), lambda bi, ti: (bi, ti, 0)),
                pl.BlockSpec((None, n, d), lambda bi, ti: (layer, 0, 0)),
                pl.BlockSpec((None, 3, d), lambda bi, ti: (bi, 0, 0)),
                pl.BlockSpec((1, d), lambda bi, ti: (0, 0))]
    return (in_specs, [o, w, mod, nw], [pl.BlockSpec((None, tm, d), lambda bi, ti: (bi, ti, 0))],
            [jax.ShapeDtypeStruct((b, t, d), F32)])


def _conv_silu(u, tail, cw, stage_ref):
    tm = u.shape[0]
    stage_ref[0:SUBLANES, :] = tail
    stage_ref[SUBLANES:SUBLANES + tm, :] = u
    y = u * cw[CONV_WIDTH - 1:CONV_WIDTH, :]
    for k in range(1, CONV_WIDTH):
        y = y + stage_ref[SUBLANES - k:SUBLANES - k + tm, :] * cw[CONV_WIDTH - 1 - k:CONV_WIDTH - k, :]
    return _silu(y)


def _gdn_in_kernel(*refs, fused, heads, dk, chunk, tn):
    x, mod_ref, nw_ref, own_in, (qkvz_ref, gcol_ref, cout_ref, tail_sc, stage_sc) = _layer_input(refs, fused, 6)
    w_ref, wab_ref, cw_ref, cinit_ref, alog_ref, dtb_ref = own_in
    t = pl.program_id(1)
    tm = x.shape[0]
    n = w_ref.shape[1]
    nb = n // 4

    @pl.when(t == 0)
    def _():
        tail_sc[...] = cinit_ref[...]

    h = _prenorm(x, mod_ref, nw_ref)
    hb = h.astype(BF16)
    h_lo = (h - hb.astype(F32)).astype(BF16)
    hi_lo = _dot(hb, wab_ref[...])
    ab = hi_lo[:, :LANES] + hi_lo[:, LANES:] + _dot(h_lo, wab_ref[:, :LANES])
    g = -jnp.exp(alog_ref[...]) * jax.nn.softplus(ab + dtb_ref[...])
    row = lax.broadcasted_iota(jnp.int32, g.shape, 0) % chunk
    s = 1
    while s < chunk:
        g = g + jnp.where(row >= s, pltpu.roll(g, s, axis=0), 0.0)
        s *= 2
    lane = lax.broadcasted_iota(jnp.int32, g.shape, 1)
    gcol_ref[...] = jnp.where(lane < heads, g, jax.nn.sigmoid(ab))

    for grp in range(n // tn):
        cs = slice(grp * tn, (grp + 1) * tn)
        kind = (grp * tn) // nb
        u = _dot(hb, w_ref[:, cs])
        if kind == 3:
            qkvz_ref[:, cs] = _silu(u).astype(BF16)
            continue
        y = _conv_silu(u, tail_sc[:, cs], cw_ref[:, cs], stage_sc.at[grp % 2])
        tail_sc[:, cs] = u[tm - SUBLANES:tm]
        cout_ref[:, cs] = u[tm - SUBLANES:tm]
        if kind == 2:
            qkvz_ref[:, cs] = y.astype(BF16)
            continue
        scale = dk ** -0.5 if kind == 0 else 1.0
        for hh in range(tn // dk):
            ys = y[:, hh * dk:(hh + 1) * dk]
            inv = lax.rsqrt(jnp.sum(ys * ys, axis=-1, keepdims=True) + L2_EPS) * scale
            qkvz_ref[:, grp * tn + hh * dk:grp * tn + (hh + 1) * dk] = (ys * inv).astype(BF16)


def _gdn_in(x, mod, nw, prev, w_main, w_ab, conv_w, conv_init, a_log, dt_bias, *, layer, heads, dk, tm, chunk):
    b, t, d = x.shape
    n = w_main.shape[2]
    nb = n // 4
    tn = 4 * dk
    assert nb == heads * dk and nb % tn == 0 and t % tm == 0 and tm % SUBLANES == 0 and tm % chunk == 0
    fused = prev is not None
    pre_specs, pre_args, x_spec, x_shape = _fused_specs(prev, b, t, d, tm) if fused else ([], [], [], [])
    kern = functools.partial(_gdn_in_kernel, fused=fused, heads=heads, dk=dk, chunk=chunk, tn=tn)
    return pl.pallas_call(
        kern,
        grid=(b, t // tm),
        in_specs=[pl.BlockSpec((None, tm, d), lambda bi, ti: (bi, ti, 0)),
                  pl.BlockSpec((None, 3, d), lambda bi, ti: (bi, 0, 0)),
                  pl.BlockSpec((1, d), lambda bi, ti: (0, 0))] + pre_specs + [
                  pl.BlockSpec((None, d, n), lambda bi, ti: (layer, 0, 0)),
                  pl.BlockSpec((d, 2 * LANES), lambda bi, ti: (0, 0)),
                  pl.BlockSpec((CONV_WIDTH, 3 * nb), lambda bi, ti: (0, 0)),
                  pl.BlockSpec((None, SUBLANES, 3 * nb), lambda bi, ti: (bi, 0, 0)),
                  pl.BlockSpec((1, LANES), lambda bi, ti: (0, 0)),
                  pl.BlockSpec((1, LANES), lambda bi, ti: (0, 0))],
        out_specs=x_spec + [pl.BlockSpec((None, tm, n), lambda bi, ti: (bi, ti, 0)),
                            pl.BlockSpec((None, tm, LANES), lambda bi, ti: (bi, ti, 0)),
                            pl.BlockSpec((None, SUBLANES, 3 * nb), lambda bi, ti: (bi, 0, 0))],
        out_shape=x_shape + [jax.ShapeDtypeStruct((b, t, n), BF16),
                             jax.ShapeDtypeStruct((b, t, LANES), F32),
                             jax.ShapeDtypeStruct((b, SUBLANES, 3 * nb), F32)],
        scratch_shapes=[pltpu.VMEM((SUBLANES, 3 * nb), F32), pltpu.VMEM((2, SUBLANES + tm, tn), F32)],
        compiler_params=_cparams("parallel", "arbitrary"),
    )(x, mod, nw, *pre_args, w_main, w_ab, conv_w, conv_init, a_log, dt_bias)


def _gdn_kernel(q_ref, k_ref, v_ref, z_ref, gcol_ref, s0_ref, ow_ref, o_ref, s_ref, *, heads, dk, chunk):
    t = pl.program_id(1)
    tb = q_ref.shape[0]
    c = chunk
    group = 2 if (tb // c) % 2 == 0 else 1

    @pl.when(t == 0)
    def _():
        s_ref[...] = s0_ref[...]

    ii = lax.broadcasted_iota(jnp.int32, (c, c), 0)
    jj = lax.broadcasted_iota(jnp.int32, (c, c), 1)
    incl = ii >= jj
    strict = ii > jj
    eye = (ii == jj).astype(F32)
    off_masks = []
    s = 1
    while s < c:
        off_masks.append((ii // (2 * s) == jj // (2 * s)) & (ii // s != jj // s) & strict)
        s *= 2

    def chunk_group(ci, carry):
        rows = [pl.ds(pl.multiple_of((ci * group + k) * c, c), c) for k in range(group)]
        hsl = [slice(hh * dk, (hh + 1) * dk) for hh in range(heads)]
        units = [(k, hh) for k in range(group) for hh in range(heads)]
        gall = [gcol_ref[r, :] for r in rows]
        g = [gall[k][:, hh:hh + 1] for k, hh in units]
        beta = [gall[k][:, heads + hh:heads + hh + 1] for k, hh in units]
        kbf = [k_ref[rows[k], hsl[hh]] for k, hh in units]
        qbf = [q_ref[rows[k], hsl[hh]] for k, hh in units]
        kf = [x.astype(F32) for x in kbf]
        kb = [x * y for x, y in zip(kf, beta)]
        gb = [jnp.broadcast_to(x, (c, c)) for x in g]
        decay = [jnp.exp(jnp.where(incl, x - x.T, -jnp.inf)) for x in gb]
        a = [jnp.where(strict, _dot_nt(x.astype(BF16), y) * d, 0.0) for x, y, d in zip(kb, kbf, decay)]
        qk = [(_dot_nt(x, y) * d).astype(BF16) for x, y, d in zip(qbf, kbf, decay)]
        p = [eye - jnp.where(off_masks[0], x, 0.0) for x in a]
        for off in off_masks[1:]:
            pb = [x.astype(BF16) for x in p]
            x = [_dot(jnp.where(off, ai, 0.0).astype(BF16), pi).astype(BF16) for ai, pi in zip(a, pb)]
            p = [pi - _dot(pbi, xi) for pi, pbi, xi in zip(p, pb, x)]
        eg = [jnp.exp(x) for x in g]
        rhs = [jnp.concatenate([v_ref[rows[k], hsl[hh]].astype(F32) * beta[u], kb[u] * eg[u]], axis=1).astype(BF16)
               for u, (k, hh) in enumerate(units)]
        sol = [_dot(x.astype(BF16), y) for x, y in zip(p, rhs)]
        wq = [jnp.concatenate([sol[u][:, dk:].astype(BF16), (qbf[u].astype(F32) * eg[u]).astype(BF16)], axis=0)
              for u in range(len(units))]
        g_last = [x[c - 1:c, :] for x in g]
        kd_t = [(kf[u] * jnp.exp(g_last[u] - g[u])).T.astype(BF16) for u in range(len(units))]
        e_last = [jnp.exp(x) for x in g_last]
        for k in range(group):
            us = [k * heads + hh for hh in range(heads)]
            s_old = [s_ref[hh] for hh in range(heads)]
            r1 = [_dot(wq[u], s_old[hh].astype(BF16)) for hh, u in enumerate(us)]
            ub = [(sol[u][:, :dk] - r1[hh][:c]).astype(BF16) for hh, u in enumerate(us)]
            for hh, u in enumerate(us):
                s_ref[hh] = s_old[hh] * e_last[u] + _dot(kd_t[u], ub[hh])
            o = [r1[hh][c:] + _dot(qk[u], ub[hh]) for hh, u in enumerate(us)]
            for hh in range(heads):
                on = o[hh] * lax.rsqrt(jnp.mean(o[hh] * o[hh], axis=-1, keepdims=True) + RMS_EPS) * ow_ref[...]
                o_ref[rows[k], hsl[hh]] = (on * z_ref[rows[k], hsl[hh]].astype(F32)).astype(BF16)
        return carry

    lax.fori_loop(0, tb // (c * group), chunk_group, 0)


def _gdn(qkvz, gcol, s0, onorm, *, heads, dk, tb, chunk):
    b, t, n = qkvz.shape
    nb = n // 4
    assert t % tb == 0 and tb % chunk == 0 and nb == heads * dk
    kern = functools.partial(_gdn_kernel, heads=heads, dk=dk, chunk=chunk)
    col = lambda jcol: pl.BlockSpec((None, tb, nb), lambda bi, ti: (bi, ti, jcol))
    return pl.pallas_call(
        kern,
        grid=(b, t // tb),
        in_specs=[col(0), col(1), col(2), col(3),
                  pl.BlockSpec((None, tb, LANES), lambda bi, ti: (bi, ti, 0)),
                  pl.BlockSpec((None, heads, dk, dk), lambda bi, ti: (bi, 0, 0, 0)),
                  pl.BlockSpec((1, dk), lambda bi, ti: (0, 0))],
        out_specs=[pl.BlockSpec((None, tb, nb), lambda bi, ti: (bi, ti, 0)),
                   pl.BlockSpec((None, heads, dk, dk), lambda bi, ti: (bi, 0, 0, 0))],
        out_shape=[jax.ShapeDtypeStruct((b, t, nb), BF16),
                   jax.ShapeDtypeStruct((b, heads, dk, dk), F32)],
        compiler_params=_cparams("parallel", "arbitrary"),
    )(qkvz, qkvz, qkvz, qkvz, gcol, s0, onorm)


def _rope(x, cos, sin_signed, lane_lo):
    half = MASK_CHUNK // 2
    swapped = jnp.where(lane_lo, pltpu.roll(x, LANES - half, axis=1), pltpu.roll(x, half, axis=1))
    return x * cos + swapped * sin_signed


def _diff_in_kernel(*refs, fused, v_transposed, q_scale, tn):
    x, mod_ref, nw_ref, own_in, outs = _layer_input(refs, fused, 5)
    w_ref, cos_ref, sin_ref, _, _ = own_in
    qb_ref, kb_ref, kf_ref, vf_ref, vb_ref, zs_ref, kn2_ref = outs
    n = w_ref.shape[1]
    nb = n // 4
    hb = _prenorm(x, mod_ref, nw_ref).astype(BF16)
    cos = cos_ref[...]
    sin = sin_ref[...]
    lane_lo = (lax.broadcasted_iota(jnp.int32, cos.shape, 1) % MASK_CHUNK) < (MASK_CHUNK // 2)
    kn2 = jnp.zeros((1, LANES), F32)
    for grp in range(n // tn):
        kind = (grp * tn) // nb
        lo = grp * tn - kind * nb
        u = _dot(hb, w_ref[:, grp * tn:(grp + 1) * tn])
        squares = []
        for s in range(tn // LANES):
            us = u[:, s * LANES:(s + 1) * LANES]
            cols = slice(lo + s * LANES, lo + (s + 1) * LANES)
            head = (lo + s * LANES) // LANES
            if kind == 0:
                qb_ref[:, cols] = (_rope(us, cos, sin, lane_lo) * q_scale).astype(BF16)
            elif kind == 1:
                r = _rope(us, cos, sin, lane_lo)
                kf_ref[:, head, :] = r
                kb_ref[:, cols] = r.astype(BF16)
                squares.append(r * r)
            elif kind == 2:
                vf_ref[:, head, :] = us
            else:
                zs_ref[:, cols] = _silu(us).astype(BF16)
        if kind == 1:
            group_sum = ((lax.broadcasted_iota(jnp.int32, (tn, LANES), 0) + lo) // MASK_CHUNK
                         == lax.broadcasted_iota(jnp.int32, (tn, LANES), 1)).astype(BF16)
            n2 = _dot(jnp.concatenate(squares, axis=1).astype(BF16), group_sum)
            kn2 = jnp.maximum(kn2, jnp.max(n2, axis=0, keepdims=True))
        if kind == 2:
            if v_transposed:
                vb_ref[lo:lo + tn, :] = u.T.astype(BF16)
            else:
                vb_ref[:, lo:lo + tn] = u.astype(BF16)
    kn2_ref[...] = kn2


def _diff_in(x, mod, nw, prev, w, cos_t, sin_t, k_all, v_all, *, layer, tm, v_transposed, q_scale):
    b, t, d = x.shape
    nb = w.shape[2] // 4
    heads = k_all.shape[3]
    assert t % tm == 0 and k_all.shape[3:] == (nb // LANES, LANES)
    tn = 4 * LANES
    assert nb % tn == 0
    fused = prev is not None
    pre_specs, pre_args, x_spec, x_shape = _fused_specs(prev, b, t, d, tm) if fused else ([], [], [], [])
    n_pre = len(pre_args)
    kern = functools.partial(_diff_in_kernel, fused=fused, v_transposed=v_transposed, q_scale=q_scale, tn=tn)
    tile = pl.BlockSpec((None, tm, nb), lambda bi, ti: (bi, ti, 0))
    cache_tile = pl.BlockSpec((None, None, tm, heads, LANES), lambda bi, ti: (layer, bi, ti, 0, 0))
    untouched = pl.BlockSpec(memory_space=pl.ANY)
    if v_transposed:
        vb_spec = pl.BlockSpec((None, nb, tm), lambda bi, ti: (bi, 0, ti))
        vb_shape = jax.ShapeDtypeStruct((b, nb, t), BF16)
    else:
        vb_spec, vb_shape = tile, jax.ShapeDtypeStruct((b, t, nb), BF16)
    return pl.pallas_call(
        kern,
        grid=(b, t // tm),
        in_specs=[pl.BlockSpec((None, tm, d), lambda bi, ti: (bi, ti, 0)),
                  pl.BlockSpec((None, 3, d), lambda bi, ti: (bi, 0, 0)),
                  pl.BlockSpec((1, d), lambda bi, ti: (0, 0))] + pre_specs + [
                  pl.BlockSpec((None, d, 4 * nb), lambda bi, ti: (layer, 0, 0)),
                  pl.BlockSpec((tm, LANES), lambda bi, ti: (ti, 0)),
                  pl.BlockSpec((tm, LANES), lambda bi, ti: (ti, 0)),
                  untouched, untouched],
        out_specs=x_spec + [tile, tile, cache_tile, cache_tile, vb_spec, tile,
                            pl.BlockSpec((None, None, 1, LANES), lambda bi, ti: (bi, ti, 0, 0))],
        out_shape=x_shape + [jax.ShapeDtypeStruct((b, t, nb), BF16),
                             jax.ShapeDtypeStruct((b, t, nb), BF16),
                             jax.ShapeDtypeStruct(k_all.shape, F32),
                             jax.ShapeDtypeStruct(v_all.shape, F32),
                             vb_shape,
                             jax.ShapeDtypeStruct((b, t, nb), BF16),
                             jax.ShapeDtypeStruct((b, t // tm, 1, LANES), F32)],
        input_output_aliases={6 + n_pre: 2 + len(x_shape), 7 + n_pre: 3 + len(x_shape)},
        compiler_params=_cparams("parallel", "parallel"),
    )(x, mod, nw, *pre_args, w, cos_t, sin_t, k_all, v_all)


def _lambda(lam_ref, lam_init):
    l1 = jnp.sum(lam_ref[0:1, :] * lam_ref[1:2, :], axis=-1, keepdims=True)
    l2 = jnp.sum(lam_ref[2:3, :] * lam_ref[3:4, :], axis=-1, keepdims=True)
    return jnp.exp(l1) - jnp.exp(l2) + lam_init


def _attn_prompt_kernel(q_ref, k_ref, vt_ref, z_ref, kn2_ref, lam_ref, sw_ref, o_ref,
                        m_sc, l_sc, acc_sc, *, lam_init, tk):
    qi = pl.program_id(2)
    tq, dh = q_ref.shape
    dq = dh // 2
    nsub = tq // tk
    chains = [(sub, comp) for sub in range(nsub) for comp in range(2)]
    everyone = list(range(len(chains)))
    n_full = (qi * tq) // tk
    qz = []
    for sub, comp in chains:
        q = q_ref[sub * tk:(sub + 1) * tk, :]
        lane = lax.broadcasted_iota(jnp.int32, q.shape, 1)
        qz.append(jnp.where((lane < dq) if comp == 0 else (lane >= dq), q, jnp.zeros_like(q)))

    kn2_max = jnp.max(kn2_ref[...], axis=0)
    kn2_lane = lax.broadcasted_iota(jnp.int32, kn2_max.shape, 1)
    kmax2 = [jnp.max(jnp.where(kn2_lane == 2 * pl.program_id(1) + comp, kn2_max, 0.0), axis=1, keepdims=True)
             for comp in range(2)]

    ones = jnp.ones((SUBLANES, dh), BF16)
    ref_rows = []
    for c, (sub, comp) in enumerate(chains):
        qf = qz[c].astype(F32)
        qn2 = _dot_nt(ones, (qf * qf).astype(BF16))[0:1, :]
        ref_rows.append(REF_SLACK * jnp.sqrt(qn2 * kmax2[comp]))

    visible = (lax.broadcasted_iota(jnp.int32, (tk, tk), 0) // MASK_CHUNK
               <= lax.broadcasted_iota(jnp.int32, (tk, tk), 1) // MASK_CHUNK)

    def blocks(kb):
        ks = k_ref[pl.ds(pl.multiple_of(kb * tk, tk), tk), :]
        vt = vt_ref[:, pl.ds(pl.multiple_of(kb * tk, tk), tk)]
        return ks, vt

    def fast_step(kb, active, masked_sub):
        ks, vt = blocks(kb)
        s = [_dot_nt(ks, qz[c]) for c in active]
        p = [jnp.exp2(si - ref_rows[c]) for c, si in zip(active, s)]
        p = [jnp.where(visible, pi, 0.0) if chains[c][0] == masked_sub else pi for c, pi in zip(active, p)]
        for c, pi in zip(active, p):
            l_sc[c] += jnp.sum(pi.reshape(tk // SUBLANES, SUBLANES, tk), axis=0)
        pv = [_dot(vt, pi.astype(BF16)) for pi in p]
        for c, pvi in zip(active, pv):
            acc_sc[c] += pvi

    def online_step(kb, active, masked_sub):
        ks, vt = blocks(kb)
        for c in active:
            s = _dot_nt(ks, qz[c])
            if chains[c][0] == masked_sub:
                s = jnp.where(visible, s, -jnp.inf)
            m_old = m_sc[c]
            m_new = jnp.maximum(m_old, jnp.max(s, axis=0, keepdims=True))
            alpha = jnp.exp2(m_old - m_new)
            p = jnp.exp2(s - m_new)
            l_sc[c, 0:1, :] = alpha * l_sc[c, 0:1, :] + jnp.sum(p, axis=0, keepdims=True)
            acc_sc[c] = alpha * acc_sc[c] + _dot(vt, p.astype(BF16))
            m_sc[c] = m_new

    def sweep(step, unroll):
        l_sc[...] = jnp.zeros(l_sc.shape, F32)
        acc_sc[...] = jnp.zeros(acc_sc.shape, F32)
        def body(i, cry):
            for u in range(unroll):
                step(i * unroll + u, everyone, None)
            return cry
        lax.fori_loop(0, n_full // unroll, body, 0)
        for d in range(nsub):
            step(n_full + d, [c for c in everyone if chains[c][0] >= d], d)

    def normalisers():
        return [jnp.sum(l_sc[c], axis=0, keepdims=True) for c in everyone]

    sweep(fast_step, nsub)
    l_min = functools.reduce(jnp.minimum, normalisers())

    @pl.when(jnp.min(l_min) < MIN_NORMALISER)
    def _():
        m_sc[...] = jnp.full(m_sc.shape, -jnp.inf, F32)
        sweep(online_step, 1)

    lam = _lambda(lam_ref, lam_init)
    l = normalisers()
    for sub in range(nsub):
        c1, c2 = 2 * sub, 2 * sub + 1
        o_t = acc_sc[c1] / l[c1] - lam * (acc_sc[c2] / l[c2])
        o = o_t.T
        o = o * lax.rsqrt(jnp.mean(o * o, axis=-1, keepdims=True) + RMS_EPS) * sw_ref[...] * (1.0 - lam_init)
        rows = slice(sub * tk, (sub + 1) * tk)
        o_ref[rows, :] = (o * z_ref[rows, :].astype(F32)).astype(BF16)


def _attn_prompt(qb, kb, vt, zs, kn2, lam_vecs, subln, *, heads, lam_init, tq, tk):
    b, t, n = qb.shape
    dh = n // heads
    assert t % tq == 0 and tq % tk == 0 and tk % MASK_CHUNK == 0
    nchains = 2 * (tq // tk)
    kern = functools.partial(_attn_prompt_kernel, lam_init=lam_init, tk=tk)
    return pl.pallas_call(
        kern,
        grid=(b, heads, t // tq),
        in_specs=[pl.BlockSpec((None, tq, dh), lambda bi, hi, qi: (bi, qi, hi)),
                  pl.BlockSpec((None, t, dh), lambda bi, hi, qi: (bi, 0, hi)),
                  pl.BlockSpec((None, dh, t), lambda bi, hi, qi: (bi, hi, 0)),
                  pl.BlockSpec((None, tq, dh), lambda bi, hi, qi: (bi, qi, hi)),
                  pl.BlockSpec((None,) + kn2.shape[1:], lambda bi, hi, qi: (bi, 0, 0, 0)),
                  pl.BlockSpec(lam_vecs.shape, lambda bi, hi, qi: (0, 0)),
                  pl.BlockSpec((1, dh), lambda bi, hi, qi: (0, 0))],
        out_specs=pl.BlockSpec((None, tq, dh), lambda bi, hi, qi: (bi, qi, hi)),
        out_shape=jax.ShapeDtypeStruct((b, t, n), BF16),
        scratch_shapes=[pltpu.VMEM((nchains, 1, tk), F32), pltpu.VMEM((nchains, SUBLANES, tk), F32),
                        pltpu.VMEM((nchains, dh, tk), F32)],
        compiler_params=_cparams("parallel", "parallel", "arbitrary"),
    )(qb, kb, vt, zs, kn2, lam_vecs, subln)


def _attn_sample_kernel(q_ref, kn_ref, vn_ref, ck_ref, cv_ref, z_ref, lam_ref, sw_ref, o_ref,
                        q2_sc, bias_sc, m_sc, l_sc, acc_sc, *, heads, lam_init):
    si = pl.program_id(1)
    ns = pl.num_programs(1)
    tq, n = q_ref.shape
    dh = n // heads
    dq = dh // 2
    rows = heads * 2 * tq

    def head_bias(cols):
        qh = lax.broadcasted_iota(jnp.int32, (rows, cols), 0) // (2 * tq)
        kh = lax.broadcasted_iota(jnp.int32, (rows, cols), 1) % heads
        return jnp.where(qh == kh, 0.0, -jnp.inf)

    @pl.when(si == 0)
    def _():
        m_sc[...] = jnp.full(m_sc.shape, -jnp.inf, F32)
        l_sc[...] = jnp.zeros(l_sc.shape, F32)
        acc_sc[...] = jnp.zeros(acc_sc.shape, F32)
        bias_sc[...] = head_bias(bias_sc.shape[1])
        for hh in range(heads):
            q = q_ref[:, hh * dh:(hh + 1) * dh]
            lane = lax.broadcasted_iota(jnp.int32, q.shape, 1)
            q2_sc[hh * 2 * tq:hh * 2 * tq + tq, :] = jnp.where(lane < dq, q, jnp.zeros_like(q))
            q2_sc[hh * 2 * tq + tq:(hh + 1) * 2 * tq, :] = jnp.where(lane >= dq, q, jnp.zeros_like(q))

    def update(k_rows, v_rows, bias):
        s = _dot_nt(q2_sc[...], k_rows) + bias
        m_old = m_sc[...]
        m_new = jnp.maximum(m_old, jnp.max(s, axis=-1, keepdims=True))
        alpha = jnp.exp2(m_old - m_new)
        p = jnp.exp2(s - m_new)
        l_sc[...] = alpha * l_sc[...] + jnp.sum(p, axis=-1, keepdims=True)
        acc_sc[...] = alpha * acc_sc[...] + _dot(p.astype(BF16), v_rows)
        m_sc[...] = m_new

    tkv = ck_ref.shape[0]
    update(ck_ref[...].reshape(tkv * heads, dh).astype(BF16), cv_ref[...].reshape(tkv * heads, dh).astype(BF16),
           bias_sc[...])

    @pl.when(si == ns - 1)
    def _():
        update(kn_ref[...].reshape(tq * heads, dh).astype(BF16), vn_ref[...].reshape(tq * heads, dh).astype(BF16),
               head_bias(tq * heads))
        lam = _lambda(lam_ref, lam_init)
        on = acc_sc[...] / l_sc[...]
        for hh in range(heads):
            hs = slice(hh * dh, (hh + 1) * dh)
            o = on[hh * 2 * tq:hh * 2 * tq + tq] - lam * on[hh * 2 * tq + tq:(hh + 1) * 2 * tq]
            o = o * lax.rsqrt(jnp.mean(o * o, axis=-1, keepdims=True) + RMS_EPS) * sw_ref[...] * (1.0 - lam_init)
            o_ref[:, hs] = (o * z_ref[:, hs].astype(F32)).astype(BF16)


def _attn_sample(qb, k_new, v_new, cache_k, cache_v, zs, lam_vecs, subln, *, layer, heads, lam_init, tkv):
    b, tq, n = qb.shape
    past = cache_k.shape[2]
    dh = n // heads
    assert past % tkv == 0 and cache_k.shape[3:] == (heads, dh) and k_new.shape[2:] == (tq, heads, dh)
    kern = functools.partial(_attn_sample_kernel, heads=heads, lam_init=lam_init)
    rows = heads * 2 * tq
    tile = pl.BlockSpec((None, tq, n), lambda bi, si: (bi, 0, 0))
    new = pl.BlockSpec((None, None, tq, heads, dh), lambda bi, si: (layer, bi, 0, 0, 0))
    old = pl.BlockSpec((None, None, tkv, heads, dh), lambda bi, si: (layer, bi, si, 0, 0))
    return pl.pallas_call(
        kern,
        grid=(b, past // tkv),
        in_specs=[tile, new, new, old, old, tile,
                  pl.BlockSpec(lam_vecs.shape, lambda bi, si: (0, 0)),
                  pl.BlockSpec((1, dh), lambda bi, si: (0, 0))],
        out_specs=tile,
        out_shape=jax.ShapeDtypeStruct((b, tq, n), BF16),
        scratch_shapes=[pltpu.VMEM((rows, dh), BF16), pltpu.VMEM((rows, tkv * heads), F32),
                        pltpu.VMEM((rows, 1), F32), pltpu.VMEM((rows, 1), F32), pltpu.VMEM((rows, dh), F32)],
        compiler_params=_cparams("parallel", "arbitrary"),
    )(qb, k_new, v_new, cache_k, cache_v, zs, lam_vecs, subln)


def _out_kernel(x_ref, o_ref, w_ref, mod_ref, nw_ref, y_ref):
    y_ref[...] = _residual_update(x_ref[...], o_ref, w_ref, mod_ref, nw_ref)


def _out_proj(x, prev, *, tm):
    b, t, d = x.shape
    pre_specs, pre_args, x_spec, x_shape = _fused_specs(prev, b, t, d, tm)
    return pl.pallas_call(
        _out_kernel,
        grid=(b, t // tm),
        in_specs=[pl.BlockSpec((None, tm, d), lambda bi, ti: (bi, ti, 0))] + pre_specs,
        out_specs=x_spec[0],
        out_shape=x_shape[0],
        compiler_params=_cparams("parallel", "parallel"),
    )(x, *pre_args)


def _pick_tile(t, target):
    tile = min(t, target)
    assert t % tile == 0
    return tile


def _rope_tables(pos):
    half = MASK_CHUNK // 2
    inv = 1.0 / (ROPE_THETA ** (jnp.arange(half, dtype=F32) / half))
    ang = pos.astype(F32)[:, None] * inv[None, :]
    cos, sin = jnp.cos(ang), jnp.sin(ang)
    reps = LANES // MASK_CHUNK
    return (jnp.tile(jnp.concatenate([cos, cos], axis=1), (1, reps)),
            jnp.tile(jnp.concatenate([-sin, sin], axis=1), (1, reps)))


def _trunk(x, ada, pos, conv_bufs, gdn_states, past_k, past_v, p):
    b, t, d = x.shape
    depth = ada.shape[0]
    gdn_heads = p["a_log_gdn"].shape[1]
    gdn_dk = p["onorm_gdn"].shape[1]
    dq = p["lam_q1"].shape[1]
    n_diff = p["w_out_diff"].shape[1]
    diff_heads = n_diff // (2 * dq)
    gdn_chunk = LANES
    n_conv = 3 * gdn_heads * gdn_dk
    tm = _pick_tile(t, 512)
    cos_t, sin_t = _rope_tables(pos)
    prompt = past_k is None
    k_all = jnp.zeros((p["w_in_diff"].shape[0], b, t, diff_heads, 2 * dq), F32)
    v_all = jnp.zeros_like(k_all)
    states, convs = [], []
    prev = None
    for i in range(depth):
        j = i // 2
        mod = ada[i]
        nw_pre = p["norm_pre"][i][None, :]
        nw_post = p["norm_post"][i][None, :]
        if i % 2 == 0:
            a_log = jnp.pad(p["a_log_gdn"][j][None, :], ((0, 0), (0, LANES - gdn_heads)))
            dt_b = jnp.pad(p["dt_bias_gdn"][j][None, :], ((0, 0), (0, LANES - gdn_heads)))
            if conv_bufs is None:
                conv_init = jnp.zeros((b, SUBLANES, n_conv), F32)
                s0 = jnp.zeros((b, gdn_heads, gdn_dk, gdn_dk), F32)
            else:
                conv_init = jnp.pad(conv_bufs[j], ((0, 0), (SUBLANES - (CONV_WIDTH - 1), 0), (0, 0)))
                s0 = gdn_states[j]
            res = _gdn_in(x, mod, nw_pre, prev, p["w_main_gdn"], p["w_ab_gdn"][j], p["conv_gdn"][j],
                          conv_init, a_log, dt_b, layer=j, heads=gdn_heads, dk=gdn_dk, tm=tm,
                          chunk=min(gdn_chunk, tm))
            if prev is not None:
                x, res = res[0], res[1:]
            qkvz, gcol, cout = res
            if t < gdn_chunk:
                front = gdn_chunk - t
                qkvz = jnp.pad(qkvz, ((0, 0), (front, 0), (0, 0)))
                gcol = jnp.pad(gcol, ((0, 0), (front, 0), (0, 0)))
            o, st = _gdn(qkvz, gcol, s0, p["onorm_gdn"][j][None, :], heads=gdn_heads, dk=gdn_dk,
                         tb=_pick_tile(qkvz.shape[1], 256), chunk=gdn_chunk)
            o = o[:, -t:]
            convs.append(cout[:, SUBLANES - (CONV_WIDTH - 1):, :])
            states.append(st)
            w_out = p["w_out_gdn"]
        else:
            lam_init = 0.8 - 0.6 * math.exp(-0.3 * i)
            lam_vecs = jnp.stack([p["lam_q1"][j], p["lam_k1"][j], p["lam_q2"][j], p["lam_k2"][j]])
            res = _diff_in(x, mod, nw_pre, prev, p["w_in_diff"], cos_t, sin_t, k_all, v_all,
                           layer=j, tm=tm, v_transposed=prompt, q_scale=dq ** -0.5 * LOG2E)
            if prev is not None:
                x, res = res[0], res[1:]
            qb, kb, k_all, v_all, vb, zs, kn2 = res
            subln = p["subln_diff"][j][None, :]
            if prompt:
                o = _attn_prompt(qb, kb, vb, zs, kn2, lam_vecs, subln, heads=diff_heads, lam_init=lam_init,
                                 tq=_pick_tile(t, 1024), tk=_pick_tile(t, 256))
            else:
                o = _attn_sample(qb, k_all, v_all, past_k, past_v, zs, lam_vecs, subln, layer=j, heads=diff_heads,
                                 lam_init=lam_init, tkv=_pick_tile(past_k.shape[2], 512))
            w_out = p["w_out_diff"]
        prev = (o, w_out, j, mod, nw_post)
    x = _out_proj(x, prev, tm=tm)
    return x, jnp.stack(states), jnp.stack(convs), k_all, v_all


def kernel(x_prompt, x_sample, c_prompt, c_sample, state_gdn, cache_conv, cache_k, cache_v, norm_pre, norm_post, w_ada, b_ada, w_in_gdn, conv_gdn, a_log_gdn, dt_bias_gdn, onorm_gdn, w_out_gdn, w_in_diff, lam_q1, lam_k1, lam_q2, lam_k2, subln_diff, w_out_diff):
    gdn_heads = a_log_gdn.shape[1]
    n_main = w_in_gdn.shape[2] - 2 * gdn_heads
    w_ab = jnp.pad(w_in_gdn[:, :, n_main:], ((0, 0), (0, 0), (0, LANES - 2 * gdn_heads)))
    w_ab_hi = w_ab.astype(BF16)
    w_ab_lo = (w_ab - w_ab_hi.astype(F32)).astype(BF16)
    p = {"norm_pre": norm_pre, "norm_post": norm_post,
         "w_main_gdn": w_in_gdn[:, :, :n_main].astype(BF16),
         "w_ab_gdn": jnp.concatenate([w_ab_hi, w_ab_lo], axis=-1),

         "conv_gdn": conv_gdn, "a_log_gdn": a_log_gdn,
         "dt_bias_gdn": dt_bias_gdn, "onorm_gdn": onorm_gdn, "w_out_gdn": w_out_gdn.astype(BF16),
         "w_in_diff": w_in_diff.astype(BF16), "lam_q1": lam_q1, "lam_k1": lam_k1, "lam_q2": lam_q2,
         "lam_k2": lam_k2, "subln_diff": subln_diff, "w_out_diff": w_out_diff.astype(BF16)}
    bp, tp, d = x_prompt.shape
    bs, ts, _ = x_sample.shape
    past = cache_k.shape[2]
    depth = w_ada.shape[0]
    ada = _ada(jnp.concatenate([c_prompt, c_sample], axis=0), w_ada, b_ada)
    ada = ada.reshape(depth, bp + bs, 3, d)
    y_p, st_p, conv_p, k_p, v_p = _trunk(x_prompt, ada[:, :bp], jnp.arange(tp), None, None, None, None, p)
    y_s, st_s, conv_s, k_s, v_s = _trunk(x_sample, ada[:, bp:], past + jnp.arange(ts), cache_conv, state_gdn,
                                         cache_k, cache_v, p)
    return (y_p, y_s, st_p, conv_p, k_p, v_p, st_s, conv_s, k_s, v_s)
```

```python
import functools
import math

import jax
import jax.numpy as jnp
from jax import lax
from jax.experimental import pallas as pl
from jax.experimental.pallas import tpu as pltpu

F32 = jnp.float32
BF16 = jnp.bfloat16
HIGHEST = lax.Precision.HIGHEST

RMS_EPS = 1e-6
L2_EPS = 1e-6
ROPE_THETA = 10000.0
CONV_WIDTH = 4
MASK_CHUNK = 64
LANES = 128
SUBLANES = 8
VMEM_LIMIT = 56 * 1024 * 1024
LOG2E = 1.4426950408889634
REF_SLACK = 1.01
MIN_NORMALISER = 1e-30

NT_DIMS = (((1,), (1,)), ((), ()))


def _cparams(*sem):
    return pltpu.CompilerParams(dimension_semantics=sem, vmem_limit_bytes=VMEM_LIMIT)


def _silu(x):
    return x * jax.nn.sigmoid(x)


def _dot(a, b):
    return jnp.dot(a, b, preferred_element_type=F32)


def _dot_nt(a, b):
    return lax.dot_general(a, b, NT_DIMS, preferred_element_type=F32)


def _ada_kernel(c_ref, w_ref, b_ref, o_ref):
    act = _silu(c_ref[...])
    o_ref[...] = jnp.dot(act, w_ref[...], precision=HIGHEST, preferred_element_type=F32) + b_ref[...]


def _ada(c_all, w_ada, b_ada):
    depth, d, n = w_ada.shape
    bc = c_all.shape[0]
    tn = 512
    return pl.pallas_call(
        _ada_kernel,
        grid=(depth, n // tn),
        in_specs=[pl.BlockSpec((bc, d), lambda i, j: (0, 0)),
                  pl.BlockSpec((None, d, tn), lambda i, j: (i, 0, j)),
                  pl.BlockSpec((None, 1, tn), lambda i, j: (i, 0, j))],
        out_specs=pl.BlockSpec((None, bc, tn), lambda i, j: (i, 0, j)),
        out_shape=jax.ShapeDtypeStruct((depth, bc, n), F32),
        compiler_params=_cparams("parallel", "parallel"),
    )(c_all, w_ada, b_ada.reshape(depth, 1, n))


def _residual_update(x, o_ref, w_ref, mod_ref, nw_ref):
    y = _dot(o_ref[...], w_ref[...])
    yn = y * lax.rsqrt(jnp.mean(y * y, axis=-1, keepdims=True) + RMS_EPS) * nw_ref[...]
    return x + mod_ref[2:3, :] * yn


def _prenorm(x, mod_ref, nw_ref):
    y = x * lax.rsqrt(jnp.mean(x * x, axis=-1, keepdims=True) + RMS_EPS) * nw_ref[...]
    return y * (1.0 + mod_ref[1:2, :]) + mod_ref[0:1, :]


def _layer_input(refs, fused, n_in):
    x_ref, mod_ref, nw_ref = refs[:3]
    pos = 7 if fused else 3
    own_in = refs[pos:pos + n_in]
    pos += n_in
    x = x_ref[...]
    if fused:
        x = _residual_update(x, *refs[3:7])
        refs[pos][...] = x
        pos += 1
    return x, mod_ref, nw_ref, own_in, refs[pos:]


def _fused_specs(prev, b, t, d, tm):
    o, w, layer, mod, nw = prev
    n = o.shape[2]
    in_specs = [pl.BlockSpec((None, tm, n), lambda bi, ti: (bi, ti, 0)),
                pl.BlockSpec((None, n, d), lambda bi, ti: (layer, 0, 0)),
                pl.BlockSpec((None, 3, d), lambda bi, ti: (bi, 0, 0)),
                pl.BlockSpec((1, d), lambda bi, ti: (0, 0))]
    return (in_specs, [o, w, mod, nw], [pl.BlockSpec((None, tm, d), lambda bi, ti: (bi, ti, 0))],
            [jax.ShapeDtypeStruct((b, t, d), F32)])


def _conv_silu(u, tail, cw, stage_ref):
    tm = u.shape[0]
    stage_ref[0:SUBLANES, :] = tail
    stage_ref[SUBLANES:SUBLANES + tm, :] = u
    y = u * cw[CONV_WIDTH - 1:CONV_WIDTH, :]
    for k in range(1, CONV_WIDTH):
        y = y + stage_ref[SUBLANES - k:SUBLANES - k + tm, :] * cw[CONV_WIDTH - 1 - k:CONV_WIDTH - k, :]
    return _silu(y)


def _gdn_in_kernel(*refs, fused, heads, dk, chunk, tn):
    x, mod_ref, nw_ref, own_in, (qkvz_ref, gcol_ref, cout_ref, tail_sc, stage_sc) = _layer_input(refs, fused, 6)
    w_ref, wab_ref, cw_ref, cinit_ref, alog_ref, dtb_ref = own_in
    t = pl.program_id(1)
    tm = x.shape[0]
    n = w_ref.shape[1]
    nb = n // 4

    @pl.when(t == 0)
    def _():
        tail_sc[...] = cinit_ref[...]

    h = _prenorm(x, mod_ref, nw_ref)
    hb = h.astype(BF16)
    h_lo = (h - hb.astype(F32)).astype(BF16)
    hi_lo = _dot(hb, wab_ref[...])
    ab = hi_lo[:, :LANES] + hi_lo[:, LANES:] + _dot(h_lo, wab_ref[:, :LANES])
    g = -jnp.exp(alog_ref[...]) * jax.nn.softplus(ab + dtb_ref[...])
    row = lax.broadcasted_iota(jnp.int32, g.shape, 0) % chunk
    s = 1
    while s < chunk:
        g = g + jnp.where(row >= s, pltpu.roll(g, s, axis=0), 0.0)
        s *= 2
    lane = lax.broadcasted_iota(jnp.int32, g.shape, 1)
    gcol_ref[...] = jnp.where(lane < heads, g, jax.nn.sigmoid(ab))

    for grp in range(n // tn):
        cs = slice(grp * tn, (grp + 1) * tn)
        kind = (grp * tn) // nb
        u = _dot(hb, w_ref[:, cs])
        if kind == 3:
            qkvz_ref[:, cs] = _silu(u).astype(BF16)
            continue
        y = _conv_silu(u, tail_sc[:, cs], cw_ref[:, cs], stage_sc.at[grp % 2])
        tail_sc[:, cs] = u[tm - SUBLANES:tm]
        cout_ref[:, cs] = u[tm - SUBLANES:tm]
        if kind == 2:
            qkvz_ref[:, cs] = y.astype(BF16)
            continue
        scale = dk ** -0.5 if kind == 0 else 1.0
        for hh in range(tn // dk):
            ys = y[:, hh * dk:(hh + 1) * dk]
            inv = lax.rsqrt(jnp.sum(ys * ys, axis=-1, keepdims=True) + L2_EPS) * scale
            qkvz_ref[:, grp * tn + hh * dk:grp * tn + (hh + 1) * dk] = (ys * inv).astype(BF16)


def _gdn_in(x, mod, nw, prev, w_main, w_ab, conv_w, conv_init, a_log, dt_bias, *, layer, heads, dk, tm, chunk):
    b, t, d = x.shape
    n = w_main.shape[2]
    nb = n // 4
    tn = 4 * dk
    assert nb == heads * dk and nb % tn == 0 and t % tm == 0 and tm % SUBLANES == 0 and tm % chunk == 0
    fused = prev is not None
    pre_specs, pre_args, x_spec, x_shape = _fused_specs(prev, b, t, d, tm) if fused else ([], [], [], [])
    kern = functools.partial(_gdn_in_kernel, fused=fused, heads=heads, dk=dk, chunk=chunk, tn=tn)
    return pl.pallas_call(
        kern,
        grid=(b, t // tm),
        in_specs=[pl.BlockSpec((None, tm, d), lambda bi, ti: (bi, ti, 0)),
                  pl.BlockSpec((None, 3, d), lambda bi, ti: (bi, 0, 0)),
                  pl.BlockSpec((1, d), lambda bi, ti: (0, 0))] + pre_specs + [
                  pl.BlockSpec((None, d, n), lambda bi, ti: (layer, 0, 0)),
                  pl.BlockSpec((d, 2 * LANES), lambda bi, ti: (0, 0)),
                  pl.BlockSpec((CONV_WIDTH, 3 * nb), lambda bi, ti: (0, 0)),
                  pl.BlockSpec((None, SUBLANES, 3 * nb), lambda bi, ti: (bi, 0, 0)),
                  pl.BlockSpec((1, LANES), lambda bi, ti: (0, 0)),
                  pl.BlockSpec((1, LANES), lambda bi, ti: (0, 0))],
        out_specs=x_spec + [pl.BlockSpec((None, tm, n), lambda bi, ti: (bi, ti, 0)),
                            pl.BlockSpec((None, tm, LANES), lambda bi, ti: (bi, ti, 0)),
                            pl.BlockSpec((None, SUBLANES, 3 * nb), lambda bi, ti: (bi, 0, 0))],
        out_shape=x_shape + [jax.ShapeDtypeStruct((b, t, n), BF16),
                             jax.ShapeDtypeStruct((b, t, LANES), F32),
                             jax.ShapeDtypeStruct((b, SUBLANES, 3 * nb), F32)],
        scratch_shapes=[pltpu.VMEM((SUBLANES, 3 * nb), F32), pltpu.VMEM((2, SUBLANES + tm, tn), F32)],
        compiler_params=_cparams("parallel", "arbitrary"),
    )(x, mod, nw, *pre_args, w_main, w_ab, conv_w, conv_init, a_log, dt_bias)


def _gdn_kernel(q_ref, k_ref, v_ref, z_ref, gcol_ref, s0_ref, ow_ref, o_ref, s_ref, *, heads, dk, chunk):
    t = pl.program_id(1)
    tb = q_ref.shape[0]
    c = chunk
    group = 2 if (tb // c) % 2 == 0 else 1

    @pl.when(t == 0)
    def _():
        s_ref[...] = s0_ref[...]

    ii = lax.broadcasted_iota(jnp.int32, (c, c), 0)
    jj = lax.broadcasted_iota(jnp.int32, (c, c), 1)
    incl = ii >= jj
    strict = ii > jj
    eye = (ii == jj).astype(F32)
    off_masks = []
    s = 1
    while s < c:
        off_masks.append((ii // (2 * s) == jj // (2 * s)) & (ii // s != jj // s) & strict)
        s *= 2

    def chunk_group(ci, carry):
        rows = [pl.ds(pl.multiple_of((ci * group + k) * c, c), c) for k in range(group)]
        hsl = [slice(hh * dk, (hh + 1) * dk) for hh in range(heads)]
        units = [(k, hh) for k in range(group) for hh in range(heads)]
        gall = [gcol_ref[r, :] for r in rows]
        g = [gall[k][:, hh:hh + 1] for k, hh in units]
        beta = [gall[k][:, heads + hh:heads + hh + 1] for k, hh in units]
        kbf = [k_ref[rows[k], hsl[hh]] for k, hh in units]
        qbf = [q_ref[rows[k], hsl[hh]] for k, hh in units]
        kf = [x.astype(F32) for x in kbf]
        kb = [x * y for x, y in zip(kf, beta)]
        gb = [jnp.broadcast_to(x, (c, c)) for x in g]
        decay = [jnp.exp(jnp.where(incl, x - x.T, -jnp.inf)) for x in gb]
        a = [jnp.where(strict, _dot_nt(x.astype(BF16), y) * d, 0.0) for x, y, d in zip(kb, kbf, decay)]
        qk = [(_dot_nt(x, y) * d).astype(BF16) for x, y, d in zip(qbf, kbf, decay)]
        p = [eye - jnp.where(off_masks[0], x, 0.0) for x in a]
        for off in off_masks[1:]:
            pb = [x.astype(BF16) for x in p]
            x = [_dot(jnp.where(off, ai, 0.0).astype(BF16), pi).astype(BF16) for ai, pi in zip(a, pb)]
            p = [pi - _dot(pbi, xi) for pi, pbi, xi in zip(p, pb, x)]
        eg = [jnp.exp(x) for x in g]
        rhs = [jnp.concatenate([v_ref[rows[k], hsl[hh]].astype(F32) * beta[u], kb[u] * eg[u]], axis=1).astype(BF16)
               for u, (k, hh) in enumerate(units)]
        sol = [_dot(x.astype(BF16), y) for x, y in zip(p, rhs)]
        wq = [jnp.concatenate([sol[u][:, dk:].astype(BF16), (qbf[u].astype(F32) * eg[u]).astype(BF16)], axis=0)
              for u in range(len(units))]
        g_last = [x[c - 1:c, :] for x in g]
        kd_t = [(kf[u] * jnp.exp(g_last[u] - g[u])).T.astype(BF16) for u in range(len(units))]
        e_last = [jnp.exp(x) for x in g_last]
        for k in range(group):
            us = [k * heads + hh for hh in range(heads)]
            s_old = [s_ref[hh] for hh in range(heads)]
            r1 = [_dot(wq[u], s_old[hh].astype(BF16)) for hh, u in enumerate(us)]
            ub = [(sol[u][:, :dk] - r1[hh][:c]).astype(BF16) for hh, u in enumerate(us)]
            for hh, u in enumerate(us):
                s_ref[hh] = s_old[hh] * e_last[u] + _dot(kd_t[u], ub[hh])
            o = [r1[hh][c:] + _dot(qk[u], ub[hh]) for hh, u in enumerate(us)]
            for hh in range(heads):
                on = o[hh] * lax.rsqrt(jnp.mean(o[hh] * o[hh], axis=-1, keepdims=True) + RMS_EPS) * ow_ref[...]
                o_ref[rows[k], hsl[hh]] = (on * z_ref[rows[k], hsl[hh]].astype(F32)).astype(BF16)
        return carry

    lax.fori_loop(0, tb // (c * group), chunk_group, 0)


def _gdn(qkvz, gcol, s0, onorm, *, heads, dk, tb, chunk):
    b, t, n = qkvz.shape
    nb = n // 4
    assert t % tb == 0 and tb % chunk == 0 and nb == heads * dk
    kern = functools.partial(_gdn_kernel, heads=heads, dk=dk, chunk=chunk)
    col = lambda jcol: pl.BlockSpec((None, tb, nb), lambda bi, ti: (bi, ti, jcol))
    return pl.pallas_call(
        kern,
        grid=(b, t // tb),
        in_specs=[col(0), col(1), col(2), col(3),
                  pl.BlockSpec((None, tb, LANES), lambda bi, ti: (bi, ti, 0)),
                  pl.BlockSpec((None, heads, dk, dk), lambda bi, ti: (bi, 0, 0, 0)),
                  pl.BlockSpec((1, dk), lambda bi, ti: (0, 0))],
        out_specs=[pl.BlockSpec((None, tb, nb), lambda bi, ti: (bi, ti, 0)),
                   pl.BlockSpec((None, heads, dk, dk), lambda bi, ti: (bi, 0, 0, 0))],
        out_shape=[jax.ShapeDtypeStruct((b, t, nb), BF16),
                   jax.ShapeDtypeStruct((b, heads, dk, dk), F32)],
        compiler_params=_cparams("parallel", "arbitrary"),
    )(qkvz, qkvz, qkvz, qkvz, gcol, s0, onorm)


def _rope(x, cos, sin_signed, lane_lo):
    half = MASK_CHUNK // 2
    swapped = jnp.where(lane_lo, pltpu.roll(x, LANES - half, axis=1), pltpu.roll(x, half, axis=1))
    return x * cos + swapped * sin_signed


def _diff_in_kernel(*refs, fused, v_transposed, q_scale, tn):
    x, mod_ref, nw_ref, own_in, outs = _layer_input(refs, fused, 5)
    w_ref, cos_ref, sin_ref, _, _ = own_in
    qb_ref, kb_ref, kf_ref, vf_ref, vb_ref, zs_ref, kn2_ref = outs
    n = w_ref.shape[1]
    nb = n // 4
    hb = _prenorm(x, mod_ref, nw_ref).astype(BF16)
    cos = cos_ref[...]
    sin = sin_ref[...]
    lane_lo = (lax.broadcasted_iota(jnp.int32, cos.shape, 1) % MASK_CHUNK) < (MASK_CHUNK // 2)
    kn2 = jnp.zeros((1, LANES), F32)
    for grp in range(n // tn):
        kind = (grp * tn) // nb
        lo = grp * tn - kind * nb
        u = _dot(hb, w_ref[:, grp * tn:(grp + 1) * tn])
        squares = []
        for s in range(tn // LANES):
            us = u[:, s * LANES:(s + 1) * LANES]
            cols = slice(lo + s * LANES, lo + (s + 1) * LANES)
            head = (lo + s * LANES) // LANES
            if kind == 0:
                qb_ref[:, cols] = (_rope(us, cos, sin, lane_lo) * q_scale).astype(BF16)
            elif kind == 1:
                r = _rope(us, cos, sin, lane_lo)
                kf_ref[:, head, :] = r
                kb_ref[:, cols] = r.astype(BF16)
                squares.append(r * r)
            elif kind == 2:
                vf_ref[:, head, :] = us
            else:
                zs_ref[:, cols] = _silu(us).astype(BF16)
        if kind == 1:
            group_sum = ((lax.broadcasted_iota(jnp.int32, (tn, LANES), 0) + lo) // MASK_CHUNK
                         == lax.broadcasted_iota(jnp.int32, (tn, LANES), 1)).astype(BF16)
            n2 = _dot(jnp.concatenate(squares, axis=1).astype(BF16), group_sum)
            kn2 = jnp.maximum(kn2, jnp.max(n2, axis=0, keepdims=True))
        if kind == 2:
            if v_transposed:
                vb_ref[lo:lo + tn, :] = u.T.astype(BF16)
            else:
                vb_ref[:, lo:lo + tn] = u.astype(BF16)
    kn2_ref[...] = kn2


def _diff_in(x, mod, nw, prev, w, cos_t, sin_t, k_all, v_all, *, layer, tm, v_transposed, q_scale):
    b, t, d = x.shape
    nb = w.shape[2] // 4
    heads = k_all.shape[3]
    assert t % tm == 0 and k_all.shape[3:] == (nb // LANES, LANES)
    tn = 4 * LANES
    assert nb % tn == 0
    fused = prev is not None
    pre_specs, pre_args, x_spec, x_shape = _fused_specs(prev, b, t, d, tm) if fused else ([], [], [], [])
    n_pre = len(pre_args)
    kern = functools.partial(_diff_in_kernel, fused=fused, v_transposed=v_transposed, q_scale=q_scale, tn=tn)
    tile = pl.BlockSpec((None, tm, nb), lambda bi, ti: (bi, ti, 0))
    cache_tile = pl.BlockSpec((None, None, tm, heads, LANES), lambda bi, ti: (layer, bi, ti, 0, 0))
    untouched = pl.BlockSpec(memory_space=pl.ANY)
    if v_transposed:
        vb_spec = pl.BlockSpec((None, nb, tm), lambda bi, ti: (bi, 0, ti))
        vb_shape = jax.ShapeDtypeStruct((b, nb, t), BF16)
    else:
        vb_spec, vb_shape = tile, jax.ShapeDtypeStruct((b, t, nb), BF16)
    return pl.pallas_call(
        kern,
        grid=(b, t // tm),
        in_specs=[pl.BlockSpec((None, tm, d), lambda bi, ti: (bi, ti, 0)),
                  pl.BlockSpec((None, 3, d), lambda bi, ti: (bi, 0, 0)),
                  pl.BlockSpec((1, d), lambda bi, ti: (0, 0))] + pre_specs + [
                  pl.BlockSpec((None, d, 4 * nb), lambda bi, ti: (layer, 0, 0)),
                  pl.BlockSpec((tm, LANES), lambda bi, ti: (ti, 0)),
                  pl.BlockSpec((tm, LANES), lambda bi, ti: (ti, 0)),
                  untouched, untouched],
        out_specs=x_spec + [tile, tile, cache_tile, cache_tile, vb_spec, tile,
                            pl.BlockSpec((None, None, 1, LANES), lambda bi, ti: (bi, ti, 0, 0))],
        out_shape=x_shape + [jax.ShapeDtypeStruct((b, t, nb), BF16),
                             jax.ShapeDtypeStruct((b, t, nb), BF16),
                             jax.ShapeDtypeStruct(k_all.shape, F32),
                             jax.ShapeDtypeStruct(v_all.shape, F32),
                             vb_shape,
                             jax.ShapeDtypeStruct((b, t, nb), BF16),
                             jax.ShapeDtypeStruct((b, t // tm, 1, LANES), F32)],
        input_output_aliases={6 + n_pre: 2 + len(x_shape), 7 + n_pre: 3 + len(x_shape)},
        compiler_params=_cparams("parallel", "parallel"),
    )(x, mod, nw, *pre_args, w, cos_t, sin_t, k_all, v_all)


def _lambda(lam_ref, lam_init):
    l1 = jnp.sum(lam_ref[0:1, :] * lam_ref[1:2, :], axis=-1, keepdims=True)
    l2 = jnp.sum(lam_ref[2:3, :] * lam_ref[3:4, :], axis=-1, keepdims=True)
    return jnp.exp(l1) - jnp.exp(l2) + lam_init


def _attn_prompt_kernel(q_ref, k_ref, vt_ref, z_ref, kn2_ref, lam_ref, sw_ref, o_ref,
                        m_sc, l_sc, acc_sc, *, lam_init, tk):
    qi = pl.program_id(2)
    tq, dh = q_ref.shape
    dq = dh // 2
    nsub = tq // tk
    chains = [(sub, comp) for sub in range(nsub) for comp in range(2)]
    everyone = list(range(len(chains)))
    n_full = (qi * tq) // tk
    qz = []
    for sub, comp in chains:
        q = q_ref[sub * tk:(sub + 1) * tk, :]
        lane = lax.broadcasted_iota(jnp.int32, q.shape, 1)
        qz.append(jnp.where((lane < dq) if comp == 0 else (lane >= dq), q, jnp.zeros_like(q)))

    kn2_max = jnp.max(kn2_ref[...], axis=0)
    kn2_lane = lax.broadcasted_iota(jnp.int32, kn2_max.shape, 1)
    kmax2 = [jnp.max(jnp.where(kn2_lane == 2 * pl.program_id(1) + comp, kn2_max, 0.0), axis=1, keepdims=True)
             for comp in range(2)]

    ones = jnp.ones((SUBLANES, dh), BF16)
    ref_rows = []
    for c, (sub, comp) in enumerate(chains):
        qf = qz[c].astype(F32)
        qn2 = _dot_nt(ones, (qf * qf).astype(BF16))[0:1, :]
        ref_rows.append(REF_SLACK * jnp.sqrt(qn2 * kmax2[comp]))

    visible = (lax.broadcasted_iota(jnp.int32, (tk, tk), 0) // MASK_CHUNK
               <= lax.broadcasted_iota(jnp.int32, (tk, tk), 1) // MASK_CHUNK)

    def blocks(kb):
        ks = k_ref[pl.ds(pl.multiple_of(kb * tk, tk), tk), :]
        vt = vt_ref[:, pl.ds(pl.multiple_of(kb * tk, tk), tk)]
        return ks, vt

    def fast_step(kb, active, masked_sub):
        ks, vt = blocks(kb)
        s = [_dot_nt(ks, qz[c]) for c in active]
        p = [jnp.exp2(si - ref_rows[c]) for c, si in zip(active, s)]
        p = [jnp.where(visible, pi, 0.0) if chains[c][0] == masked_sub else pi for c, pi in zip(active, p)]
        for c, pi in zip(active, p):
            l_sc[c] += jnp.sum(pi.reshape(tk // SUBLANES, SUBLANES, tk), axis=0)
        pv = [_dot(vt, pi.astype(BF16)) for pi in p]
        for c, pvi in zip(active, pv):
            acc_sc[c] += pvi

    def online_step(kb, active, masked_sub):
        ks, vt = blocks(kb)
        for c in active:
            s = _dot_nt(ks, qz[c])
            if chains[c][0] == masked_sub:
                s = jnp.where(visible, s, -jnp.inf)
            m_old = m_sc[c]
            m_new = jnp.maximum(m_old, jnp.max(s, axis=0, keepdims=True))
            alpha = jnp.exp2(m_old - m_new)
            p = jnp.exp2(s - m_new)
            l_sc[c, 0:1, :] = alpha * l_sc[c, 0:1, :] + jnp.sum(p, axis=0, keepdims=True)
            acc_sc[c] = alpha * acc_sc[c] + _dot(vt, p.astype(BF16))
            m_sc[c] = m_new

    def sweep(step, unroll):
        l_sc[...] = jnp.zeros(l_sc.shape, F32)
        acc_sc[...] = jnp.zeros(acc_sc.shape, F32)
        def body(i, cry):
            for u in range(unroll):
                step(i * unroll + u, everyone, None)
            return cry
        lax.fori_loop(0, n_full // unroll, body, 0)
        for d in range(nsub):
            step(n_full + d, [c for c in everyone if chains[c][0] >= d], d)

    def normalisers():
        return [jnp.sum(l_sc[c], axis=0, keepdims=True) for c in everyone]

    sweep(fast_step, min(nsub, 2))
    l_min = functools.reduce(jnp.minimum, normalisers())

    @pl.when(jnp.min(l_min) < MIN_NORMALISER)
    def _():
        m_sc[...] = jnp.full(m_sc.shape, -jnp.inf, F32)
        sweep(online_step, 1)

    lam = _lambda(lam_ref, lam_init)
    l = normalisers()
    for sub in range(nsub):
        c1, c2 = 2 * sub, 2 * sub + 1
        o_t = acc_sc[c1] / l[c1] - lam * (acc_sc[c2] / l[c2])
        o = o_t.T
        o = o * lax.rsqrt(jnp.mean(o * o, axis=-1, keepdims=True) + RMS_EPS) * sw_ref[...] * (1.0 - lam_init)
        rows = slice(sub * tk, (sub + 1) * tk)
        o_ref[rows, :] = (o * z_ref[rows, :].astype(F32)).astype(BF16)


def _attn_prompt(qb, kb, vt, zs, kn2, lam_vecs, subln, *, heads, lam_init, tq, tk):
    b, t, n = qb.shape
    dh = n // heads
    assert t % tq == 0 and tq % tk == 0 and tk % MASK_CHUNK == 0
    nchains = 2 * (tq // tk)
    kern = functools.partial(_attn_prompt_kernel, lam_init=lam_init, tk=tk)
    return pl.pallas_call(
        kern,
        grid=(b, heads, t // tq),
        in_specs=[pl.BlockSpec((None, tq, dh), lambda bi, hi, qi: (bi, qi, hi)),
                  pl.BlockSpec((None, t, dh), lambda bi, hi, qi: (bi, 0, hi)),
                  pl.BlockSpec((None, dh, t), lambda bi, hi, qi: (bi, hi, 0)),
                  pl.BlockSpec((None, tq, dh), lambda bi, hi, qi: (bi, qi, hi)),
                  pl.BlockSpec((None,) + kn2.shape[1:], lambda bi, hi, qi: (bi, 0, 0, 0)),
                  pl.BlockSpec(lam_vecs.shape, lambda bi, hi, qi: (0, 0)),
                  pl.BlockSpec((1, dh), lambda bi, hi, qi: (0, 0))],
        out_specs=pl.BlockSpec((None, tq, dh), lambda bi, hi, qi: (bi, qi, hi)),
        out_shape=jax.ShapeDtypeStruct((b, t, n), BF16),
        scratch_shapes=[pltpu.VMEM((nchains, 1, tk), F32), pltpu.VMEM((nchains, SUBLANES, tk), F32),
                        pltpu.VMEM((nchains, dh, tk), F32)],
        compiler_params=_cparams("parallel", "parallel", "arbitrary"),
    )(qb, kb, vt, zs, kn2, lam_vecs, subln)


def _attn_sample_kernel(q_ref, kn_ref, vn_ref, ck_ref, cv_ref, z_ref, lam_ref, sw_ref, o_ref,
                        q2_sc, bias_sc, m_sc, l_sc, acc_sc, *, heads, lam_init):
    si = pl.program_id(1)
    ns = pl.num_programs(1)
    tq, n = q_ref.shape
    dh = n // heads
    dq = dh // 2
    rows = heads * 2 * tq

    def head_bias(cols):
        qh = lax.broadcasted_iota(jnp.int32, (rows, cols), 0) // (2 * tq)
        kh = lax.broadcasted_iota(jnp.int32, (rows, cols), 1) % heads
        return jnp.where(qh == kh, 0.0, -jnp.inf)

    @pl.when(si == 0)
    def _():
        m_sc[...] = jnp.full(m_sc.shape, -jnp.inf, F32)
        l_sc[...] = jnp.zeros(l_sc.shape, F32)
        acc_sc[...] = jnp.zeros(acc_sc.shape, F32)
        bias_sc[...] = head_bias(bias_sc.shape[1])
        for hh in range(heads):
            q = q_ref[:, hh * dh:(hh + 1) * dh]
            lane = lax.broadcasted_iota(jnp.int32, q.shape, 1)
            q2_sc[hh * 2 * tq:hh * 2 * tq + tq, :] = jnp.where(lane < dq, q, jnp.zeros_like(q))
            q2_sc[hh * 2 * tq + tq:(hh + 1) * 2 * tq, :] = jnp.where(lane >= dq, q, jnp.zeros_like(q))

    def update(k_rows, v_rows, bias):
        s = _dot_nt(q2_sc[...], k_rows) + bias
        m_old = m_sc[...]
        m_new = jnp.maximum(m_old, jnp.max(s, axis=-1, keepdims=True))
        alpha = jnp.exp2(m_old - m_new)
        p = jnp.exp2(s - m_new)
        l_sc[...] = alpha * l_sc[...] + jnp.sum(p, axis=-1, keepdims=True)
        acc_sc[...] = alpha * acc_sc[...] + _dot(p.astype(BF16), v_rows)
        m_sc[...] = m_new

    tkv = ck_ref.shape[0]
    update(ck_ref[...].reshape(tkv * heads, dh).astype(BF16), cv_ref[...].reshape(tkv * heads, dh).astype(BF16),
           bias_sc[...])

    @pl.when(si == ns - 1)
    def _():
        update(kn_ref[...].reshape(tq * heads, dh).astype(BF16), vn_ref[...].reshape(tq * heads, dh).astype(BF16),
               head_bias(tq * heads))
        lam = _lambda(lam_ref, lam_init)
        on = acc_sc[...] / l_sc[...]
        for hh in range(heads):
            hs = slice(hh * dh, (hh + 1) * dh)
            o = on[hh * 2 * tq:hh * 2 * tq + tq] - lam * on[hh * 2 * tq + tq:(hh + 1) * 2 * tq]
            o = o * lax.rsqrt(jnp.mean(o * o, axis=-1, keepdims=True) + RMS_EPS) * sw_ref[...] * (1.0 - lam_init)
            o_ref[:, hs] = (o * z_ref[:, hs].astype(F32)).astype(BF16)


def _attn_sample(qb, k_new, v_new, cache_k, cache_v, zs, lam_vecs, subln, *, layer, heads, lam_init, tkv):
    b, tq, n = qb.shape
    past = cache_k.shape[2]
    dh = n // heads
    assert past % tkv == 0 and cache_k.shape[3:] == (heads, dh) and k_new.shape[2:] == (tq, heads, dh)
    kern = functools.partial(_attn_sample_kernel, heads=heads, lam_init=lam_init)
    rows = heads * 2 * tq
    tile = pl.BlockSpec((None, tq, n), lambda bi, si: (bi, 0, 0))
    new = pl.BlockSpec((None, None, tq, heads, dh), lambda bi, si: (layer, bi, 0, 0, 0))
    old = pl.BlockSpec((None, None, tkv, heads, dh), lambda bi, si: (layer, bi, si, 0, 0))
    return pl.pallas_call(
        kern,
        grid=(b, past // tkv),
        in_specs=[tile, new, new, old, old, tile,
                  pl.BlockSpec(lam_vecs.shape, lambda bi, si: (0, 0)),
                  pl.BlockSpec((1, dh), lambda bi, si: (0, 0))],
        out_specs=tile,
        out_shape=jax.ShapeDtypeStruct((b, tq, n), BF16),
        scratch_shapes=[pltpu.VMEM((rows, dh), BF16), pltpu.VMEM((rows, tkv * heads), F32),
                        pltpu.VMEM((rows, 1), F32), pltpu.VMEM((rows, 1), F32), pltpu.VMEM((rows, dh), F32)],
        compiler_params=_cparams("parallel", "arbitrary"),
    )(qb, k_new, v_new, cache_k, cache_v, zs, lam_vecs, subln)


def _out_kernel(x_ref, o_ref, w_ref, mod_ref, nw_ref, y_ref):
    y_ref[...] = _residual_update(x_ref[...], o_ref, w_ref, mod_ref, nw_ref)


def _out_proj(x, prev, *, tm):
    b, t, d = x.shape
    pre_specs, pre_args, x_spec, x_shape = _fused_specs(prev, b, t, d, tm)
    return pl.pallas_call(
        _out_kernel,
        grid=(b, t // tm),
        in_specs=[pl.BlockSpec((None, tm, d), lambda bi, ti: (bi, ti, 0))] + pre_specs,
        out_specs=x_spec[0],
        out_shape=x_shape[0],
        compiler_params=_cparams("parallel", "parallel"),
    )(x, *pre_args)


def _pick_tile(t, target):
    tile = min(t, target)
    assert t % tile == 0
    return tile


def _rope_tables(pos):
    half = MASK_CHUNK // 2
    inv = 1.0 / (ROPE_THETA ** (jnp.arange(half, dtype=F32) / half))
    ang = pos.astype(F32)[:, None] * inv[None, :]
    cos, sin = jnp.cos(ang), jnp.sin(ang)
    reps = LANES // MASK_CHUNK
    return (jnp.tile(jnp.concatenate([cos, cos], axis=1), (1, reps)),
            jnp.tile(jnp.concatenate([-sin, sin], axis=1), (1, reps)))


def _trunk(x, ada, pos, conv_bufs, gdn_states, past_k, past_v, p):
    b, t, d = x.shape
    depth = ada.shape[0]
    gdn_heads = p["a_log_gdn"].shape[1]
    gdn_dk = p["onorm_gdn"].shape[1]
    dq = p["lam_q1"].shape[1]
    n_diff = p["w_out_diff"].shape[1]
    diff_heads = n_diff // (2 * dq)
    gdn_chunk = LANES
    n_conv = 3 * gdn_heads * gdn_dk
    tm = _pick_tile(t, 512)
    cos_t, sin_t = _rope_tables(pos)
    prompt = past_k is None
    k_all = jnp.zeros((p["w_in_diff"].shape[0], b, t, diff_heads, 2 * dq), F32)
    v_all = jnp.zeros_like(k_all)
    states, convs = [], []
    prev = None
    for i in range(depth):
        j = i // 2
        mod = ada[i]
        nw_pre = p["norm_pre"][i][None, :]
        nw_post = p["norm_post"][i][None, :]
        if i % 2 == 0:
            a_log = jnp.pad(p["a_log_gdn"][j][None, :], ((0, 0), (0, LANES - gdn_heads)))
            dt_b = jnp.pad(p["dt_bias_gdn"][j][None, :], ((0, 0), (0, LANES - gdn_heads)))
            if conv_bufs is None:
                conv_init = jnp.zeros((b, SUBLANES, n_conv), F32)
                s0 = jnp.zeros((b, gdn_heads, gdn_dk, gdn_dk), F32)
            else:
                conv_init = jnp.pad(conv_bufs[j], ((0, 0), (SUBLANES - (CONV_WIDTH - 1), 0), (0, 0)))
                s0 = gdn_states[j]
            res = _gdn_in(x, mod, nw_pre, prev, p["w_main_gdn"], p["w_ab_gdn"][j], p["conv_gdn"][j],
                          conv_init, a_log, dt_b, layer=j, heads=gdn_heads, dk=gdn_dk, tm=tm,
                          chunk=min(gdn_chunk, tm))
            if prev is not None:
                x, res = res[0], res[1:]
            qkvz, gcol, cout = res
            if t < gdn_chunk:
                front = gdn_chunk - t
                qkvz = jnp.pad(qkvz, ((0, 0), (front, 0), (0, 0)))
                gcol = jnp.pad(gcol, ((0, 0), (front, 0), (0, 0)))
            o, st = _gdn(qkvz, gcol, s0, p["onorm_gdn"][j][None, :], heads=gdn_heads, dk=gdn_dk,
                         tb=_pick_tile(qkvz.shape[1], 256), chunk=gdn_chunk)
            o = o[:, -t:]
            convs.append(cout[:, SUBLANES - (CONV_WIDTH - 1):, :])
            states.append(st)
            w_out = p["w_out_gdn"]
        else:
            lam_init = 0.8 - 0.6 * math.exp(-0.3 * i)
            lam_vecs = jnp.stack([p["lam_q1"][j], p["lam_k1"][j], p["lam_q2"][j], p["lam_k2"][j]])
            res = _diff_in(x, mod, nw_pre, prev, p["w_in_diff"], cos_t, sin_t, k_all, v_all,
                           layer=j, tm=tm, v_transposed=prompt, q_scale=dq ** -0.5 * LOG2E)
            if prev is not None:
                x, res = res[0], res[1:]
            qb, kb, k_all, v_all, vb, zs, kn2 = res
            subln = p["subln_diff"][j][None, :]
            if prompt:
                o = _attn_prompt(qb, kb, vb, zs, kn2, lam_vecs, subln, heads=diff_heads, lam_init=lam_init,
                                 tq=_pick_tile(t, 2048), tk=_pick_tile(t, 256))
            else:
                o = _attn_sample(qb, k_all, v_all, past_k, past_v, zs, lam_vecs, subln, layer=j, heads=diff_heads,
                                 lam_init=lam_init, tkv=_pick_tile(past_k.shape[2], 512))
            w_out = p["w_out_diff"]
        prev = (o, w_out, j, mod, nw_post)
    x = _out_proj(x, prev, tm=tm)
    return x, jnp.stack(states), jnp.stack(convs), k_all, v_all


def kernel(x_prompt, x_sample, c_prompt, c_sample, state_gdn, cache_conv, cache_k, cache_v, norm_pre, norm_post, w_ada, b_ada, w_in_gdn, conv_gdn, a_log_gdn, dt_bias_gdn, onorm_gdn, w_out_gdn, w_in_diff, lam_q1, lam_k1, lam_q2, lam_k2, subln_diff, w_out_diff):
    gdn_heads = a_log_gdn.shape[1]
    n_main = w_in_gdn.shape[2] - 2 * gdn_heads
    w_ab = jnp.pad(w_in_gdn[:, :, n_main:], ((0, 0), (0, 0), (0, LANES - 2 * gdn_heads)))
    w_ab_hi = w_ab.astype(BF16)
    w_ab_lo = (w_ab - w_ab_hi.astype(F32)).astype(BF16)
    p = {"norm_pre": norm_pre, "norm_post": norm_post,
         "w_main_gdn": w_in_gdn[:, :, :n_main].astype(BF16),
         "w_ab_gdn": jnp.concatenate([w_ab_hi, w_ab_lo], axis=-1),

         "conv_gdn": conv_gdn, "a_log_gdn": a_log_gdn,
         "dt_bias_gdn": dt_bias_gdn, "onorm_gdn": onorm_gdn, "w_out_gdn": w_out_gdn.astype(BF16),
         "w_in_diff": w_in_diff.astype(BF16), "lam_q1": lam_q1, "lam_k1": lam_k1, "lam_q2": lam_q2,
         "lam_k2": lam_k2, "subln_diff": subln_diff, "w_out_diff": w_out_diff.astype(BF16)}
    bp, tp, d = x_prompt.shape
    bs, ts, _ = x_sample.shape
    past = cache_k.shape[2]
    depth = w_ada.shape[0]
    ada = _ada(jnp.concatenate([c_prompt, c_sample], axis=0), w_ada, b_ada)
    ada = ada.reshape(depth, bp + bs, 3, d)
    y_p, st_p, conv_p, k_p, v_p = _trunk(x_prompt, ada[:, :bp], jnp.arange(tp), None, None, None, None, p)
    y_s, st_s, conv_s, k_s, v_s = _trunk(x_sample, ada[:, bp:], past + jnp.arange(ts), cache_conv, state_gdn,
                                         cache_k, cache_v, p)
    return (y_p, y_s, st_p, conv_p, k_p, v_p, st_s, conv_s, k_s, v_s)
```

```python
import functools
import math

import jax
import jax.numpy as jnp
from jax import lax
from jax.experimental import pallas as pl
from jax.experimental.pallas import tpu as pltpu

F32 = jnp.float32
BF16 = jnp.bfloat16
HIGHEST = lax.Precision.HIGHEST

RMS_EPS = 1e-6
L2_EPS = 1e-6
ROPE_THETA = 10000.0
CONV_WIDTH = 4
MASK_CHUNK = 64
LANES = 128
SUBLANES = 8
VMEM_LIMIT = 56 * 1024 * 1024
LOG2E = 1.4426950408889634
REF_SLACK = 1.01
MIN_NORMALISER = 1e-30
STAGE_GROUP = 16

NT_DIMS = (((1,), (1,)), ((), ()))


def _cparams(*sem):
    return pltpu.CompilerParams(dimension_semantics=sem, vmem_limit_bytes=VMEM_LIMIT)


def _silu(x):
    return x * jax.nn.sigmoid(x)


def _dot(a, b):
    return jnp.dot(a, b, preferred_element_type=F32)


def _dot_nt(a, b):
    return lax.dot_general(a, b, NT_DIMS, preferred_element_type=F32)


def _ada_kernel(c_ref, w_ref, b_ref, o_ref):
    act = _silu(c_ref[...])
    o_ref[...] = jnp.dot(act, w_ref[...], precision=HIGHEST, preferred_element_type=F32) + b_ref[...]


def _ada(c_all, w_ada, b_ada):
    depth, d, n = w_ada.shape
    bc = c_all.shape[0]
    tn = 512
    return pl.pallas_call(
        _ada_kernel,
        grid=(depth, n // tn),
        in_specs=[pl.BlockSpec((bc, d), lambda i, j: (0, 0)),
                  pl.BlockSpec((None, d, tn), lambda i, j: (i, 0, j)),
                  pl.BlockSpec((None, 1, tn), lambda i, j: (i, 0, j))],
        out_specs=pl.BlockSpec((None, bc, tn), lambda i, j: (i, 0, j)),
        out_shape=jax.ShapeDtypeStruct((depth, bc, n), F32),
        compiler_params=_cparams("parallel", "parallel"),
    )(c_all, w_ada, b_ada.reshape(depth, 1, n))


def _residual_update(x, o_ref, w_ref, mod_ref, nw_ref):
    y = _dot(o_ref[...], w_ref[...])
    yn = y * lax.rsqrt(jnp.mean(y * y, axis=-1, keepdims=True) + RMS_EPS) * nw_ref[...]
    return x + mod_ref[2:3, :] * yn


def _prenorm(x, mod_ref, nw_ref):
    y = x * lax.rsqrt(jnp.mean(x * x, axis=-1, keepdims=True) + RMS_EPS) * nw_ref[...]
    return y * (1.0 + mod_ref[1:2, :]) + mod_ref[0:1, :]


def _layer_input(refs, fused, n_in):
    x_ref, mod_ref, nw_ref = refs[:3]
    pos = 7 if fused else 3
    own_in = refs[pos:pos + n_in]
    pos += n_in
    x = x_ref[...]
    if fused:
        x = _residual_update(x, *refs[3:7])
        refs[pos][...] = x
        pos += 1
    return x, mod_ref, nw_ref, own_in, refs[pos:]


def _fused_specs(prev, b, t, d, tm):
    o, w, layer, mod, nw = prev
    n = o.shape[2]
    in_specs = [pl.BlockSpec((None, tm, n), lambda bi, ti: (bi, ti, 0)),
                pl.BlockSpec((None, n, d), lambda bi, ti: (layer, 0, 0)),
                pl.BlockSpec((None, 3, d), lambda bi, ti: (bi, 0, 0)),
                pl.BlockSpec((1, d), lambda bi, ti: (0, 0))]
    return (in_specs, [o, w, mod, nw], [pl.BlockSpec((None, tm, d), lambda bi, ti: (bi, ti, 0))],
            [jax.ShapeDtypeStruct((b, t, d), F32)])


def _conv_silu(u, tail, cw, stage_ref):
    tm = u.shape[0]
    stage_ref[0:SUBLANES, :] = tail
    stage_ref[SUBLANES:SUBLANES + tm, :] = u
    y = u * cw[CONV_WIDTH - 1:CONV_WIDTH, :]
    for k in range(1, CONV_WIDTH):
        y = y + stage_ref[SUBLANES - k:SUBLANES - k + tm, :] * cw[CONV_WIDTH - 1 - k:CONV_WIDTH - k, :]
    return _silu(y)


def _gdn_in_kernel(*refs, fused, heads, dk, chunk, tn):
    x, mod_ref, nw_ref, own_in, (qkvz_ref, gcol_ref, cout_ref, tail_sc, stage_sc) = _layer_input(refs, fused, 6)
    w_ref, wab_ref, cw_ref, cinit_ref, alog_ref, dtb_ref = own_in
    t = pl.program_id(1)
    tm = x.shape[0]
    n = w_ref.shape[1]
    nb = n // 4

    @pl.when(t == 0)
    def _():
        tail_sc[...] = cinit_ref[...]

    h = _prenorm(x, mod_ref, nw_ref)
    hb = h.astype(BF16)
    h_lo = (h - hb.astype(F32)).astype(BF16)
    hi_lo = _dot(hb, wab_ref[...])
    ab = hi_lo[:, :LANES] + hi_lo[:, LANES:] + _dot(h_lo, wab_ref[:, :LANES])
    g = -jnp.exp(alog_ref[...]) * jax.nn.softplus(ab + dtb_ref[...])
    row = lax.broadcasted_iota(jnp.int32, g.shape, 0) % chunk
    s = 1
    while s < chunk:
        g = g + jnp.where(row >= s, pltpu.roll(g, s, axis=0), 0.0)
        s *= 2
    lane = lax.broadcasted_iota(jnp.int32, g.shape, 1)
    gcol_ref[...] = jnp.where(lane < heads, g, jax.nn.sigmoid(ab))

    for grp in range(n // tn):
        cs = slice(grp * tn, (grp + 1) * tn)
        kind = (grp * tn) // nb
        u = _dot(hb, w_ref[:, cs])
        if kind == 3:
            qkvz_ref[:, cs] = _silu(u).astype(BF16)
            continue
        y = _conv_silu(u, tail_sc[:, cs], cw_ref[:, cs], stage_sc.at[grp % 2])
        tail_sc[:, cs] = u[tm - SUBLANES:tm]
        cout_ref[:, cs] = u[tm - SUBLANES:tm]
        if kind == 2:
            qkvz_ref[:, cs] = y.astype(BF16)
            continue
        scale = dk ** -0.5 if kind == 0 else 1.0
        for hh in range(tn // dk):
            ys = y[:, hh * dk:(hh + 1) * dk]
            inv = lax.rsqrt(jnp.sum(ys * ys, axis=-1, keepdims=True) + L2_EPS) * scale
            qkvz_ref[:, grp * tn + hh * dk:grp * tn + (hh + 1) * dk] = (ys * inv).astype(BF16)


def _gdn_in(x, mod, nw, prev, w_main, w_ab, conv_w, conv_init, a_log, dt_bias, *, layer, heads, dk, tm, chunk):
    b, t, d = x.shape
    n = w_main.shape[2]
    nb = n // 4
    tn = 4 * dk
    assert nb == heads * dk and nb % tn == 0 and t % tm == 0 and tm % SUBLANES == 0 and tm % chunk == 0
    fused = prev is not None
    pre_specs, pre_args, x_spec, x_shape = _fused_specs(prev, b, t, d, tm) if fused else ([], [], [], [])
    kern = functools.partial(_gdn_in_kernel, fused=fused, heads=heads, dk=dk, chunk=chunk, tn=tn)
    return pl.pallas_call(
        kern,
        grid=(b, t // tm),
        in_specs=[pl.BlockSpec((None, tm, d), lambda bi, ti: (bi, ti, 0)),
                  pl.BlockSpec((None, 3, d), lambda bi, ti: (bi, 0, 0)),
                  pl.BlockSpec((1, d), lambda bi, ti: (0, 0))] + pre_specs + [
                  pl.BlockSpec((None, d, n), lambda bi, ti: (layer, 0, 0)),
                  pl.BlockSpec((d, 2 * LANES), lambda bi, ti: (0, 0)),
                  pl.BlockSpec((CONV_WIDTH, 3 * nb), lambda bi, ti: (0, 0)),
                  pl.BlockSpec((None, SUBLANES, 3 * nb), lambda bi, ti: (bi, 0, 0)),
                  pl.BlockSpec((1, LANES), lambda bi, ti: (0, 0)),
                  pl.BlockSpec((1, LANES), lambda bi, ti: (0, 0))],
        out_specs=x_spec + [pl.BlockSpec((None, tm, n), lambda bi, ti: (bi, ti, 0)),
                            pl.BlockSpec((None, tm, LANES), lambda bi, ti: (bi, ti, 0)),
                            pl.BlockSpec((None, SUBLANES, 3 * nb), lambda bi, ti: (bi, 0, 0))],
        out_shape=x_shape + [jax.ShapeDtypeStruct((b, t, n), BF16),
                             jax.ShapeDtypeStruct((b, t, LANES), F32),
                             jax.ShapeDtypeStruct((b, SUBLANES, 3 * nb), F32)],
        scratch_shapes=[pltpu.VMEM((SUBLANES, 3 * nb), F32), pltpu.VMEM((2, SUBLANES + tm, tn), F32)],
        compiler_params=_cparams("parallel", "arbitrary"),
    )(x, mod, nw, *pre_args, w_main, w_ab, conv_w, conv_init, a_log, dt_bias)


def _gdn_kernel(q_ref, k_ref, v_ref, z_ref, gcol_ref, s0_ref, ow_ref, o_ref, s_ref, *, heads, dk, chunk):
    t = pl.program_id(1)
    tb = q_ref.shape[0]
    c = chunk
    group = 2 if (tb // c) % 2 == 0 else 1

    @pl.when(t == 0)
    def _():
        s_ref[...] = s0_ref[...]

    ii = lax.broadcasted_iota(jnp.int32, (c, c), 0)
    jj = lax.broadcasted_iota(jnp.int32, (c, c), 1)
    incl = ii >= jj
    strict = ii > jj
    eye = (ii == jj).astype(F32)
    off_masks = []
    s = 1
    while s < c:
        off_masks.append((ii // (2 * s) == jj // (2 * s)) & (ii // s != jj // s) & strict)
        s *= 2

    def chunk_group(ci, carry):
        rows = [pl.ds(pl.multiple_of((ci * group + k) * c, c), c) for k in range(group)]
        hsl = [slice(hh * dk, (hh + 1) * dk) for hh in range(heads)]
        units = [(k, hh) for k in range(group) for hh in range(heads)]
        gall = [gcol_ref[r, :] for r in rows]
        g = [gall[k][:, hh:hh + 1] for k, hh in units]
        beta = [gall[k][:, heads + hh:heads + hh + 1] for k, hh in units]
        kbf = [k_ref[rows[k], hsl[hh]] for k, hh in units]
        qbf = [q_ref[rows[k], hsl[hh]] for k, hh in units]
        kf = [x.astype(F32) for x in kbf]
        kb = [x * y for x, y in zip(kf, beta)]
        gb = [jnp.broadcast_to(x, (c, c)) for x in g]
        decay = [jnp.exp(jnp.where(incl, x - x.T, -jnp.inf)) for x in gb]
        a = [jnp.where(strict, _dot_nt(x.astype(BF16), y) * d, 0.0) for x, y, d in zip(kb, kbf, decay)]
        qk = [(_dot_nt(x, y) * d).astype(BF16) for x, y, d in zip(qbf, kbf, decay)]
        p = [eye - jnp.where(off_masks[0], x, 0.0) for x in a]
        for off in off_masks[1:]:
            pb = [x.astype(BF16) for x in p]
            x = [_dot(jnp.where(off, ai, 0.0).astype(BF16), pi).astype(BF16) for ai, pi in zip(a, pb)]
            p = [pi - _dot(pbi, xi) for pi, pbi, xi in zip(p, pb, x)]
        eg = [jnp.exp(x) for x in g]
        rhs = [jnp.concatenate([v_ref[rows[k], hsl[hh]].astype(F32) * beta[u], kb[u] * eg[u]], axis=1).astype(BF16)
               for u, (k, hh) in enumerate(units)]
        sol = [_dot(x.astype(BF16), y) for x, y in zip(p, rhs)]
        wq = [jnp.concatenate([sol[u][:, dk:].astype(BF16), (qbf[u].astype(F32) * eg[u]).astype(BF16)], axis=0)
              for u in range(len(units))]
        g_last = [x[c - 1:c, :] for x in g]
        kd_t = [(kf[u] * jnp.exp(g_last[u] - g[u])).T.astype(BF16) for u in range(len(units))]
        e_last = [jnp.exp(x) for x in g_last]
        for k in range(group):
            us = [k * heads + hh for hh in range(heads)]
            s_old = [s_ref[hh] for hh in range(heads)]
            r1 = [_dot(wq[u], s_old[hh].astype(BF16)) for hh, u in enumerate(us)]
            ub = [(sol[u][:, :dk] - r1[hh][:c]).astype(BF16) for hh, u in enumerate(us)]
            for hh, u in enumerate(us):
                s_ref[hh] = s_old[hh] * e_last[u] + _dot(kd_t[u], ub[hh])
            o = [r1[hh][c:] + _dot(qk[u], ub[hh]) for hh, u in enumerate(us)]
            for hh in range(heads):
                on = o[hh] * lax.rsqrt(jnp.mean(o[hh] * o[hh], axis=-1, keepdims=True) + RMS_EPS) * ow_ref[...]
                o_ref[rows[k], hsl[hh]] = (on * z_ref[rows[k], hsl[hh]].astype(F32)).astype(BF16)
        return carry

    lax.fori_loop(0, tb // (c * group), chunk_group, 0)


def _gdn(qkvz, gcol, s0, onorm, *, heads, dk, tb, chunk):
    b, t, n = qkvz.shape
    nb = n // 4
    assert t % tb == 0 and tb % chunk == 0 and nb == heads * dk
    kern = functools.partial(_gdn_kernel, heads=heads, dk=dk, chunk=chunk)
    col = lambda jcol: pl.BlockSpec((None, tb, nb), lambda bi, ti: (bi, ti, jcol))
    return pl.pallas_call(
        kern,
        grid=(b, t // tb),
        in_specs=[col(0), col(1), col(2), col(3),
                  pl.BlockSpec((None, tb, LANES), lambda bi, ti: (bi, ti, 0)),
                  pl.BlockSpec((None, heads, dk, dk), lambda bi, ti: (bi, 0, 0, 0)),
                  pl.BlockSpec((1, dk), lambda bi, ti: (0, 0))],
        out_specs=[pl.BlockSpec((None, tb, nb), lambda bi, ti: (bi, ti, 0)),
                   pl.BlockSpec((None, heads, dk, dk), lambda bi, ti: (bi, 0, 0, 0))],
        out_shape=[jax.ShapeDtypeStruct((b, t, nb), BF16),
                   jax.ShapeDtypeStruct((b, heads, dk, dk), F32)],
        compiler_params=_cparams("parallel", "arbitrary"),
    )(qkvz, qkvz, qkvz, qkvz, gcol, s0, onorm)


def _rope(x, cos, sin_signed, lane_lo):
    half = MASK_CHUNK // 2
    swapped = jnp.where(lane_lo, pltpu.roll(x, LANES - half, axis=1), pltpu.roll(x, half, axis=1))
    return x * cos + swapped * sin_signed


def _diff_in_kernel(*refs, fused, layer, v_transposed, q_scale, tn):
    x, mod_ref, nw_ref, own_in, outs = _layer_input(refs, fused, 5)
    w_ref, cos_ref, sin_ref, _, _ = own_in
    qb_ref, kb_ref, kf_hbm, vf_hbm, vb_ref, zs_ref, kn2_ref, kst_sc, vst_sc, sem = outs
    n = w_ref.shape[1]
    nb = n // 4
    tm = x.shape[0]
    tiles = pl.num_programs(1)
    step = pl.program_id(0) * tiles + pl.program_id(1)
    slot = step % 2

    def cache_copies(slot_, step_):
        bi, ti = step_ // tiles, step_ % tiles
        return [pltpu.make_async_copy(stage.at[slot_, :, pl.ds(hh * LANES, LANES)],
                                      cache.at[layer, bi, pl.ds(ti * tm, tm), hh, :],
                                      sem.at[which, slot_])
                for which, (stage, cache) in enumerate(((kst_sc, kf_hbm), (vst_sc, vf_hbm)))
                for hh in range(nb // LANES)]

    hb = _prenorm(x, mod_ref, nw_ref).astype(BF16)
    cos = cos_ref[...]
    sin = sin_ref[...]
    lane_lo = (lax.broadcasted_iota(jnp.int32, cos.shape, 1) % MASK_CHUNK) < (MASK_CHUNK // 2)
    kn2 = jnp.zeros((1, LANES), F32)
    for grp in range(n // tn):
        kind = (grp * tn) // nb
        lo = grp * tn - kind * nb
        u = _dot(hb, w_ref[:, grp * tn:(grp + 1) * tn])
        squares = []
        for s in range(tn // LANES):
            us = u[:, s * LANES:(s + 1) * LANES]
            cols = slice(lo + s * LANES, lo + (s + 1) * LANES)
            if kind == 0:
                qb_ref[:, cols] = (_rope(us, cos, sin, lane_lo) * q_scale).astype(BF16)
            elif kind == 1:
                r = _rope(us, cos, sin, lane_lo)
                kst_sc[slot, :, cols] = r
                kb_ref[:, cols] = r.astype(BF16)
                squares.append(r * r)
            elif kind == 2:
                vst_sc[slot, :, cols] = us
            else:
                zs_ref[:, cols] = _silu(us).astype(BF16)
        if kind == 1:
            group_sum = ((lax.broadcasted_iota(jnp.int32, (tn, LANES), 0) + lo) // MASK_CHUNK
                         == lax.broadcasted_iota(jnp.int32, (tn, LANES), 1)).astype(BF16)
            n2 = _dot(jnp.concatenate(squares, axis=1).astype(BF16), group_sum)
            kn2 = jnp.maximum(kn2, jnp.max(n2, axis=0, keepdims=True))
        if kind == 2:
            if v_transposed:
                vb_ref[lo:lo + tn, :] = u.T.astype(BF16)
            else:
                vb_ref[:, lo:lo + tn] = u.astype(BF16)
    kn2_ref[...] = kn2

    for copy in cache_copies(slot, step):
        copy.start()

    @pl.when(step > 0)
    def _():
        for copy in cache_copies(1 - slot, step - 1):
            copy.wait()

    @pl.when(step == pl.num_programs(0) * tiles - 1)
    def _():
        for copy in cache_copies(slot, step):
            copy.wait()


def _diff_in(x, mod, nw, prev, w, cos_t, sin_t, k_all, v_all, *, layer, tm, v_transposed, q_scale):
    b, t, d = x.shape
    nb = w.shape[2] // 4
    heads = k_all.shape[3]
    assert t % tm == 0 and k_all.shape[3:] == (nb // LANES, LANES)
    tn = 4 * LANES
    assert nb % tn == 0
    fused = prev is not None
    pre_specs, pre_args, x_spec, x_shape = _fused_specs(prev, b, t, d, tm) if fused else ([], [], [], [])
    n_pre = len(pre_args)
    kern = functools.partial(_diff_in_kernel, fused=fused, layer=layer, v_transposed=v_transposed,
                             q_scale=q_scale, tn=tn)
    tile = pl.BlockSpec((None, tm, nb), lambda bi, ti: (bi, ti, 0))
    untouched = pl.BlockSpec(memory_space=pl.ANY)
    cache_tile = untouched
    if v_transposed:
        vb_spec = pl.BlockSpec((None, nb, tm), lambda bi, ti: (bi, 0, ti))
        vb_shape = jax.ShapeDtypeStruct((b, nb, t), BF16)
    else:
        vb_spec, vb_shape = tile, jax.ShapeDtypeStruct((b, t, nb), BF16)
    return pl.pallas_call(
        kern,
        grid=(b, t // tm),
        in_specs=[pl.BlockSpec((None, tm, d), lambda bi, ti: (bi, ti, 0)),
                  pl.BlockSpec((None, 3, d), lambda bi, ti: (bi, 0, 0)),
                  pl.BlockSpec((1, d), lambda bi, ti: (0, 0))] + pre_specs + [
                  pl.BlockSpec((None, d, 4 * nb), lambda bi, ti: (layer, 0, 0)),
                  pl.BlockSpec((tm, LANES), lambda bi, ti: (ti, 0)),
                  pl.BlockSpec((tm, LANES), lambda bi, ti: (ti, 0)),
                  untouched, untouched],
        out_specs=x_spec + [tile, tile, cache_tile, cache_tile, vb_spec, tile,
                            pl.BlockSpec((None, None, 1, LANES), lambda bi, ti: (bi, ti, 0, 0))],
        out_shape=x_shape + [jax.ShapeDtypeStruct((b, t, nb), BF16),
                             jax.ShapeDtypeStruct((b, t, nb), BF16),
                             jax.ShapeDtypeStruct(k_all.shape, F32),
                             jax.ShapeDtypeStruct(v_all.shape, F32),
                             vb_shape,
                             jax.ShapeDtypeStruct((b, t, nb), BF16),
                             jax.ShapeDtypeStruct((b, t // tm, 1, LANES), F32)],
        input_output_aliases={6 + n_pre: 2 + len(x_shape), 7 + n_pre: 3 + len(x_shape)},
        scratch_shapes=[pltpu.VMEM((2, tm, nb), F32), pltpu.VMEM((2, tm, nb), F32),
                        pltpu.SemaphoreType.DMA((2, 2))],
        compiler_params=_cparams("arbitrary", "arbitrary"),
    )(x, mod, nw, *pre_args, w, cos_t, sin_t, k_all, v_all)


def _lambda(lam_ref, lam_init):
    l1 = jnp.sum(lam_ref[0:1, :] * lam_ref[1:2, :], axis=-1, keepdims=True)
    l2 = jnp.sum(lam_ref[2:3, :] * lam_ref[3:4, :], axis=-1, keepdims=True)
    return jnp.exp(l1) - jnp.exp(l2) + lam_init


def _attn_prompt_kernel(q_ref, k_ref, vt_ref, z_ref, kn2_ref, lam_ref, sw_ref, o_ref,
                        m_sc, l_sc, acc_sc, *, lam_init, tk):
    qi = pl.program_id(2)
    tq, dh = q_ref.shape
    dq = dh // 2
    nsub = tq // tk
    chains = [(sub, comp) for sub in range(nsub) for comp in range(2)]
    everyone = list(range(len(chains)))
    n_full = (qi * tq) // tk
    qz = []
    for sub, comp in chains:
        q = q_ref[sub * tk:(sub + 1) * tk, :]
        lane = lax.broadcasted_iota(jnp.int32, q.shape, 1)
        qz.append(jnp.where((lane < dq) if comp == 0 else (lane >= dq), q, jnp.zeros_like(q)))

    kn2_max = jnp.max(kn2_ref[...], axis=0)
    kn2_lane = lax.broadcasted_iota(jnp.int32, kn2_max.shape, 1)
    kmax2 = [jnp.max(jnp.where(kn2_lane == 2 * pl.program_id(1) + comp, kn2_max, 0.0), axis=1, keepdims=True)
             for comp in range(2)]

    ones = jnp.ones((SUBLANES, dh), BF16)
    ref_rows = []
    for c, (sub, comp) in enumerate(chains):
        qf = qz[c].astype(F32)
        qn2 = _dot_nt(ones, (qf * qf).astype(BF16))[0:1, :]
        ref_rows.append(REF_SLACK * jnp.sqrt(qn2 * kmax2[comp]))

    visible = (lax.broadcasted_iota(jnp.int32, (tk, tk), 0) // MASK_CHUNK
               <= lax.broadcasted_iota(jnp.int32, (tk, tk), 1) // MASK_CHUNK)

    def blocks(kb):
        ks = k_ref[pl.ds(pl.multiple_of(kb * tk, tk), tk), :]
        vt = vt_ref[:, pl.ds(pl.multiple_of(kb * tk, tk), tk)]
        return ks, vt

    def fast_step(kb, active, masked_sub):
        ks, vt = blocks(kb)
        for g0 in range(0, len(active), STAGE_GROUP):
            grp = active[g0:g0 + STAGE_GROUP]
            s = [_dot_nt(ks, qz[c]) for c in grp]
            p = [jnp.exp2(si - ref_rows[c]) for c, si in zip(grp, s)]
            p = [jnp.where(visible, pi, 0.0) if chains[c][0] == masked_sub else pi for c, pi in zip(grp, p)]
            for c, pi in zip(grp, p):
                l_sc[c] += jnp.sum(pi.reshape(tk // SUBLANES, SUBLANES, tk), axis=0)
            pv = [_dot(vt, pi.astype(BF16)) for pi in p]
            for c, pvi in zip(grp, pv):
                acc_sc[c] += pvi

    def online_step(kb, active, masked_sub):
        ks, vt = blocks(kb)
        for c in active:
            s = _dot_nt(ks, qz[c])
            if chains[c][0] == masked_sub:
                s = jnp.where(visible, s, -jnp.inf)
            m_old = m_sc[c]
            m_new = jnp.maximum(m_old, jnp.max(s, axis=0, keepdims=True))
            alpha = jnp.exp2(m_old - m_new)
            p = jnp.exp2(s - m_new)
            l_sc[c, 0:1, :] = alpha * l_sc[c, 0:1, :] + jnp.sum(p, axis=0, keepdims=True)
            acc_sc[c] = alpha * acc_sc[c] + _dot(vt, p.astype(BF16))
            m_sc[c] = m_new

    def sweep(step, unroll):
        l_sc[...] = jnp.zeros(l_sc.shape, F32)
        acc_sc[...] = jnp.zeros(acc_sc.shape, F32)
        def body(i, cry):
            for u in range(unroll):
                step(i * unroll + u, everyone, None)
            return cry
        lax.fori_loop(0, n_full // unroll, body, 0)
        for d in range(nsub):
            step(n_full + d, [c for c in everyone if chains[c][0] >= d], d)

    def normalisers():
        return [jnp.sum(l_sc[c], axis=0, keepdims=True) for c in everyone]

    sweep(fast_step, min(nsub, 4))
    l_min = functools.reduce(jnp.minimum, normalisers())

    @pl.when(jnp.min(l_min) < MIN_NORMALISER)
    def _():
        m_sc[...] = jnp.full(m_sc.shape, -jnp.inf, F32)
        sweep(online_step, 1)

    lam = _lambda(lam_ref, lam_init)
    l = normalisers()
    for sub in range(nsub):
        c1, c2 = 2 * sub, 2 * sub + 1
        o_t = acc_sc[c1] * (1.0 / l[c1]) - acc_sc[c2] * (lam / l[c2])
        o = o_t.T
        gain = sw_ref[...] * (1.0 - lam_init)
        o = o * lax.rsqrt(jnp.mean(o * o, axis=-1, keepdims=True) + RMS_EPS) * gain
        rows = slice(sub * tk, (sub + 1) * tk)
        o_ref[rows, :] = (o * z_ref[rows, :].astype(F32)).astype(BF16)


def _attn_prompt(qb, kb, vt, zs, kn2, lam_vecs, subln, *, heads, lam_init, tq, tk):
    b, t, n = qb.shape
    dh = n // heads
    assert t % tq == 0 and tq % tk == 0 and tk % MASK_CHUNK == 0
    nchains = 2 * (tq // tk)
    kern = functools.partial(_attn_prompt_kernel, lam_init=lam_init, tk=tk)
    return pl.pallas_call(
        kern,
        grid=(b, heads, t // tq),
        in_specs=[pl.BlockSpec((None, tq, dh), lambda bi, hi, qi: (bi, qi, hi)),
                  pl.BlockSpec((None, t, dh), lambda bi, hi, qi: (bi, 0, hi)),
                  pl.BlockSpec((None, dh, t), lambda bi, hi, qi: (bi, hi, 0)),
                  pl.BlockSpec((None, tq, dh), lambda bi, hi, qi: (bi, qi, hi)),
                  pl.BlockSpec((None,) + kn2.shape[1:], lambda bi, hi, qi: (bi, 0, 0, 0)),
                  pl.BlockSpec(lam_vecs.shape, lambda bi, hi, qi: (0, 0)),
                  pl.BlockSpec((1, dh), lambda bi, hi, qi: (0, 0))],
        out_specs=pl.BlockSpec((None, tq, dh), lambda bi, hi, qi: (bi, qi, hi)),
        out_shape=jax.ShapeDtypeStruct((b, t, n), BF16),
        scratch_shapes=[pltpu.VMEM((nchains, 1, tk), F32), pltpu.VMEM((nchains, SUBLANES, tk), F32),
                        pltpu.VMEM((nchains, dh, tk), F32)],
        compiler_params=_cparams("parallel", "parallel", "arbitrary"),
    )(qb, kb, vt, zs, kn2, lam_vecs, subln)


def _attn_sample_kernel(q_ref, kn_ref, vn_ref, ck_ref, cv_ref, z_ref, lam_ref, sw_ref, o_ref,
                        q2_sc, bias_sc, m_sc, l_sc, acc_sc, *, heads, lam_init):
    si = pl.program_id(1)
    ns = pl.num_programs(1)
    tq, n = q_ref.shape
    dh = n // heads
    dq = dh // 2
    rows = heads * 2 * tq

    def head_bias(cols):
        qh = lax.broadcasted_iota(jnp.int32, (rows, cols), 0) // (2 * tq)
        kh = lax.broadcasted_iota(jnp.int32, (rows, cols), 1) % heads
        return jnp.where(qh == kh, 0.0, -jnp.inf)

    @pl.when(si == 0)
    def _():
        m_sc[...] = jnp.full(m_sc.shape, -jnp.inf, F32)
        l_sc[...] = jnp.zeros(l_sc.shape, F32)
        acc_sc[...] = jnp.zeros(acc_sc.shape, F32)
        bias_sc[...] = head_bias(bias_sc.shape[1])
        for hh in range(heads):
            q = q_ref[:, hh * dh:(hh + 1) * dh]
            lane = lax.broadcasted_iota(jnp.int32, q.shape, 1)
            q2_sc[hh * 2 * tq:hh * 2 * tq + tq, :] = jnp.where(lane < dq, q, jnp.zeros_like(q))
            q2_sc[hh * 2 * tq + tq:(hh + 1) * 2 * tq, :] = jnp.where(lane >= dq, q, jnp.zeros_like(q))

    def update(k_rows, v_rows, bias):
        s = _dot_nt(q2_sc[...], k_rows) + bias
        m_old = m_sc[...]
        m_new = jnp.maximum(m_old, jnp.max(s, axis=-1, keepdims=True))
        alpha = jnp.exp2(m_old - m_new)
        p = jnp.exp2(s - m_new)
        l_sc[...] = alpha * l_sc[...] + jnp.sum(p, axis=-1, keepdims=True)
        acc_sc[...] = alpha * acc_sc[...] + _dot(p.astype(BF16), v_rows)
        m_sc[...] = m_new

    tkv = ck_ref.shape[0]
    update(ck_ref[...].reshape(tkv * heads, dh).astype(BF16), cv_ref[...].reshape(tkv * heads, dh).astype(BF16),
           bias_sc[...])

    @pl.when(si == ns - 1)
    def _():
        update(kn_ref[...].reshape(tq * heads, dh).astype(BF16), vn_ref[...].reshape(tq * heads, dh).astype(BF16),
               head_bias(tq * heads))
        lam = _lambda(lam_ref, lam_init)
        on = acc_sc[...] / l_sc[...]
        for hh in range(heads):
            hs = slice(hh * dh, (hh + 1) * dh)
            o = on[hh * 2 * tq:hh * 2 * tq + tq] - lam * on[hh * 2 * tq + tq:(hh + 1) * 2 * tq]
            o = o * lax.rsqrt(jnp.mean(o * o, axis=-1, keepdims=True) + RMS_EPS) * sw_ref[...] * (1.0 - lam_init)
            o_ref[:, hs] = (o * z_ref[:, hs].astype(F32)).astype(BF16)


def _attn_sample(qb, k_new, v_new, cache_k, cache_v, zs, lam_vecs, subln, *, layer, heads, lam_init, tkv):
    b, tq, n = qb.shape
    past = cache_k.shape[2]
    dh = n // heads
    assert past % tkv == 0 and cache_k.shape[3:] == (heads, dh) and k_new.shape[2:] == (tq, heads, dh)
    kern = functools.partial(_attn_sample_kernel, heads=heads, lam_init=lam_init)
    rows = heads * 2 * tq
    tile = pl.BlockSpec((None, tq, n), lambda bi, si: (bi, 0, 0))
    new = pl.BlockSpec((None, None, tq, heads, dh), lambda bi, si: (layer, bi, 0, 0, 0))
    old = pl.BlockSpec((None, None, tkv, heads, dh), lambda bi, si: (layer, bi, si, 0, 0))
    return pl.pallas_call(
        kern,
        grid=(b, past // tkv),
        in_specs=[tile, new, new, old, old, tile,
                  pl.BlockSpec(lam_vecs.shape, lambda bi, si: (0, 0)),
                  pl.BlockSpec((1, dh), lambda bi, si: (0, 0))],
        out_specs=tile,
        out_shape=jax.ShapeDtypeStruct((b, tq, n), BF16),
        scratch_shapes=[pltpu.VMEM((rows, dh), BF16), pltpu.VMEM((rows, tkv * heads), F32),
                        pltpu.VMEM((rows, 1), F32), pltpu.VMEM((rows, 1), F32), pltpu.VMEM((rows, dh), F32)],
        compiler_params=_cparams("parallel", "arbitrary"),
    )(qb, k_new, v_new, cache_k, cache_v, zs, lam_vecs, subln)


def _out_kernel(x_ref, o_ref, w_ref, mod_ref, nw_ref, y_ref):
    y_ref[...] = _residual_update(x_ref[...], o_ref, w_ref, mod_ref, nw_ref)


def _out_proj(x, prev, *, tm):
    b, t, d = x.shape
    pre_specs, pre_args, x_spec, x_shape = _fused_specs(prev, b, t, d, tm)
    return pl.pallas_call(
        _out_kernel,
        grid=(b, t // tm),
        in_specs=[pl.BlockSpec((None, tm, d), lambda bi, ti: (bi, ti, 0))] + pre_specs,
        out_specs=x_spec[0],
        out_shape=x_shape[0],
        compiler_params=_cparams("parallel", "parallel"),
    )(x, *pre_args)


def _pick_tile(t, target):
    tile = min(t, target)
    assert t % tile == 0
    return tile


def _rope_tables(pos):
    half = MASK_CHUNK // 2
    inv = 1.0 / (ROPE_THETA ** (jnp.arange(half, dtype=F32) / half))
    ang = pos.astype(F32)[:, None] * inv[None, :]
    cos, sin = jnp.cos(ang), jnp.sin(ang)
    reps = LANES // MASK_CHUNK
    return (jnp.tile(jnp.concatenate([cos, cos], axis=1), (1, reps)),
            jnp.tile(jnp.concatenate([-sin, sin], axis=1), (1, reps)))


def _trunk(x, ada, pos, conv_bufs, gdn_states, past_k, past_v, p):
    b, t, d = x.shape
    depth = ada.shape[0]
    gdn_heads = p["a_log_gdn"].shape[1]
    gdn_dk = p["onorm_gdn"].shape[1]
    dq = p["lam_q1"].shape[1]
    n_diff = p["w_out_diff"].shape[1]
    diff_heads = n_diff // (2 * dq)
    gdn_chunk = LANES
    n_conv = 3 * gdn_heads * gdn_dk
    tm = _pick_tile(t, 512)
    cos_t, sin_t = _rope_tables(pos)
    prompt = past_k is None
    k_all = jnp.zeros((p["w_in_diff"].shape[0], b, t, diff_heads, 2 * dq), F32)
    v_all = jnp.zeros_like(k_all)
    states, convs = [], []
    prev = None
    for i in range(depth):
        j = i // 2
        mod = ada[i]
        nw_pre = p["norm_pre"][i][None, :]
        nw_post = p["norm_post"][i][None, :]
        if i % 2 == 0:
            a_log = jnp.pad(p["a_log_gdn"][j][None, :], ((0, 0), (0, LANES - gdn_heads)))
            dt_b = jnp.pad(p["dt_bias_gdn"][j][None, :], ((0, 0), (0, LANES - gdn_heads)))
            if conv_bufs is None:
                conv_init = jnp.zeros((b, SUBLANES, n_conv), F32)
                s0 = jnp.zeros((b, gdn_heads, gdn_dk, gdn_dk), F32)
            else:
                conv_init = jnp.pad(conv_bufs[j], ((0, 0), (SUBLANES - (CONV_WIDTH - 1), 0), (0, 0)))
                s0 = gdn_states[j]
            res = _gdn_in(x, mod, nw_pre, prev, p["w_main_gdn"], p["w_ab_gdn"][j], p["conv_gdn"][j],
                          conv_init, a_log, dt_b, layer=j, heads=gdn_heads, dk=gdn_dk, tm=tm,
                          chunk=min(gdn_chunk, tm))
            if prev is not None:
                x, res = res[0], res[1:]
            qkvz, gcol, cout = res
            if t < gdn_chunk:
                front = gdn_chunk - t
                qkvz = jnp.pad(qkvz, ((0, 0), (front, 0), (0, 0)))
                gcol = jnp.pad(gcol, ((0, 0), (front, 0), (0, 0)))
            o, st = _gdn(qkvz, gcol, s0, p["onorm_gdn"][j][None, :], heads=gdn_heads, dk=gdn_dk,
                         tb=_pick_tile(qkvz.shape[1], 256), chunk=gdn_chunk)
            o = o[:, -t:]
            convs.append(cout[:, SUBLANES - (CONV_WIDTH - 1):, :])
            states.append(st)
            w_out = p["w_out_gdn"]
        else:
            lam_init = 0.8 - 0.6 * math.exp(-0.3 * i)
            lam_vecs = jnp.stack([p["lam_q1"][j], p["lam_k1"][j], p["lam_q2"][j], p["lam_k2"][j]])
            res = _diff_in(x, mod, nw_pre, prev, p["w_in_diff"], cos_t, sin_t, k_all, v_all,
                           layer=j, tm=tm, v_transposed=prompt, q_scale=dq ** -0.5 * LOG2E)
            if prev is not None:
                x, res = res[0], res[1:]
            qb, kb, k_all, v_all, vb, zs, kn2 = res
            subln = p["subln_diff"][j][None, :]
            if prompt:
                o = _attn_prompt(qb, kb, vb, zs, kn2, lam_vecs, subln, heads=diff_heads, lam_init=lam_init,
                                 tq=_pick_tile(t, 2048), tk=_pick_tile(t, 256))
            else:
                o = _attn_sample(qb, k_all, v_all, past_k, past_v, zs, lam_vecs, subln, layer=j, heads=diff_heads,
                                 lam_init=lam_init, tkv=_pick_tile(past_k.shape[2], 512))
            w_out = p["w_out_diff"]
        prev = (o, w_out, j, mod, nw_post)
    x = _out_proj(x, prev, tm=tm)
    return x, jnp.stack(states), jnp.stack(convs), k_all, v_all


def kernel(x_prompt, x_sample, c_prompt, c_sample, state_gdn, cache_conv, cache_k, cache_v, norm_pre, norm_post, w_ada, b_ada, w_in_gdn, conv_gdn, a_log_gdn, dt_bias_gdn, onorm_gdn, w_out_gdn, w_in_diff, lam_q1, lam_k1, lam_q2, lam_k2, subln_diff, w_out_diff):
    gdn_heads = a_log_gdn.shape[1]
    n_main = w_in_gdn.shape[2] - 2 * gdn_heads
    w_ab = jnp.pad(w_in_gdn[:, :, n_main:], ((0, 0), (0, 0), (0, LANES - 2 * gdn_heads)))
    w_ab_hi = w_ab.astype(BF16)
    w_ab_lo = (w_ab - w_ab_hi.astype(F32)).astype(BF16)
    p = {"norm_pre": norm_pre, "norm_post": norm_post,
         "w_main_gdn": w_in_gdn[:, :, :n_main].astype(BF16),
         "w_ab_gdn": jnp.concatenate([w_ab_hi, w_ab_lo], axis=-1),

         "conv_gdn": conv_gdn, "a_log_gdn": a_log_gdn,
         "dt_bias_gdn": dt_bias_gdn, "onorm_gdn": onorm_gdn, "w_out_gdn": w_out_gdn.astype(BF16),
         "w_in_diff": w_in_diff.astype(BF16), "lam_q1": lam_q1, "lam_k1": lam_k1, "lam_q2": lam_q2,
         "lam_k2": lam_k2, "subln_diff": subln_diff, "w_out_diff": w_out_diff.astype(BF16)}
    bp, tp, d = x_prompt.shape
    bs, ts, _ = x_sample.shape
    past = cache_k.shape[2]
    depth = w_ada.shape[0]
    ada = _ada(jnp.concatenate([c_prompt, c_sample], axis=0), w_ada, b_ada)
    ada = ada.reshape(depth, bp + bs, 3, d)
    y_p, st_p, conv_p, k_p, v_p = _trunk(x_prompt, ada[:, :bp], jnp.arange(tp), None, None, None, None, p)
    y_s, st_s, conv_s, k_s, v_s = _trunk(x_sample, ada[:, bp:], past + jnp.arange(ts), cache_conv, state_gdn,
                                         cache_k, cache_v, p)
    return (y_p, y_s, st_p, conv_p, k_p, v_p, st_s, conv_s, k_s, v_s)
```

```python
import functools
import math

import jax
import jax.numpy as jnp
from jax import lax
from jax.experimental import pallas as pl
from jax.experimental.pallas import tpu as pltpu

F32 = jnp.float32
BF16 = jnp.bfloat16
HIGHEST = lax.Precision.HIGHEST

RMS_EPS = 1e-6
L2_EPS = 1e-6
ROPE_THETA = 10000.0
CONV_WIDTH = 4
MASK_CHUNK = 64
LANES = 128
SUBLANES = 8
VMEM_LIMIT = 56 * 1024 * 1024
LOG2E = 1.4426950408889634
REF_SLACK = 1.01
MIN_NORMALISER = 1e-30
STAGE_GROUP = 16

NT_DIMS = (((1,), (1,)), ((), ()))


def _cparams(*sem):
    return pltpu.CompilerParams(dimension_semantics=sem, vmem_limit_bytes=VMEM_LIMIT)


def _silu(x):
    return x * jax.nn.sigmoid(x)


def _dot(a, b):
    return jnp.dot(a, b, preferred_element_type=F32)


def _dot_nt(a, b):
    return lax.dot_general(a, b, NT_DIMS, preferred_element_type=F32)


def _ada_kernel(c_ref, w_ref, b_ref, o_ref):
    act = _silu(c_ref[...])
    o_ref[...] = jnp.dot(act, w_ref[...], precision=HIGHEST, preferred_element_type=F32) + b_ref[...]


def _ada(c_all, w_ada, b_ada):
    depth, d, n = w_ada.shape
    bc = c_all.shape[0]
    tn = 512
    return pl.pallas_call(
        _ada_kernel,
        grid=(depth, n // tn),
        in_specs=[pl.BlockSpec((bc, d), lambda i, j: (0, 0)),
                  pl.BlockSpec((None, d, tn), lambda i, j: (i, 0, j)),
                  pl.BlockSpec((None, 1, tn), lambda i, j: (i, 0, j))],
        out_specs=pl.BlockSpec((None, bc, tn), lambda i, j: (i, 0, j)),
        out_shape=jax.ShapeDtypeStruct((depth, bc, n), F32),
        compiler_params=_cparams("parallel", "parallel"),
    )(c_all, w_ada, b_ada.reshape(depth, 1, n))


def _residual_update(x, o_ref, w_ref, mod_ref, nw_ref):
    y = _dot(o_ref[...], w_ref[...])
    yn = y * lax.rsqrt(jnp.mean(y * y, axis=-1, keepdims=True) + RMS_EPS) * nw_ref[...]
    return x + mod_ref[2:3, :] * yn


def _prenorm(x, mod_ref, nw_ref):
    y = x * lax.rsqrt(jnp.mean(x * x, axis=-1, keepdims=True) + RMS_EPS) * nw_ref[...]
    return y * (1.0 + mod_ref[1:2, :]) + mod_ref[0:1, :]


def _layer_input(refs, fused, n_in):
    x_ref, mod_ref, nw_ref = refs[:3]
    pos = 7 if fused else 3
    own_in = refs[pos:pos + n_in]
    pos += n_in
    x = x_ref[...]
    if fused:
        x = _residual_update(x, *refs[3:7])
        refs[pos][...] = x
        pos += 1
    return x, mod_ref, nw_ref, own_in, refs[pos:]


def _fused_specs(prev, b, t, d, tm):
    o, w, layer, mod, nw = prev
    n = o.shape[2]
    in_specs = [pl.BlockSpec((None, tm, n), lambda bi, ti: (bi, ti, 0)),
                pl.BlockSpec((None, n, d), lambda bi, ti: (layer, 0, 0)),
                pl.BlockSpec((None, 3, d), lambda bi, ti: (bi, 0, 0)),
                pl.BlockSpec((1, d), lambda bi, ti: (0, 0))]
    return (in_specs, [o, w, mod, nw], [pl.BlockSpec((None, tm, d), lambda bi, ti: (bi, ti, 0))],
            [jax.ShapeDtypeStruct((b, t, d), F32)])


def _conv_silu(u, tail, cw, stage_ref):
    tm = u.shape[0]
    stage_ref[0:SUBLANES, :] = tail
    stage_ref[SUBLANES:SUBLANES + tm, :] = u
    y = u * cw[CONV_WIDTH - 1:CONV_WIDTH, :]
    for k in range(1, CONV_WIDTH):
        y = y + stage_ref[SUBLANES - k:SUBLANES - k + tm, :] * cw[CONV_WIDTH - 1 - k:CONV_WIDTH - k, :]
    return _silu(y)


def _gdn_in_kernel(*refs, fused, heads, dk, chunk, tn):
    x, mod_ref, nw_ref, own_in, (qkvz_ref, gcol_ref, cout_ref, tail_sc, stage_sc) = _layer_input(refs, fused, 6)
    w_ref, wab_ref, cw_ref, cinit_ref, alog_ref, dtb_ref = own_in
    t = pl.program_id(1)
    tm = x.shape[0]
    n = w_ref.shape[1]
    nb = n // 4

    @pl.when(t == 0)
    def _():
        tail_sc[...] = cinit_ref[...]

    h = _prenorm(x, mod_ref, nw_ref)
    hb = h.astype(BF16)
    h_lo = (h - hb.astype(F32)).astype(BF16)
    hi_lo = _dot(hb, wab_ref[...])
    ab = hi_lo[:, :LANES] + hi_lo[:, LANES:] + _dot(h_lo, wab_ref[:, :LANES])
    g = -jnp.exp(alog_ref[...]) * jax.nn.softplus(ab + dtb_ref[...])
    row = lax.broadcasted_iota(jnp.int32, g.shape, 0) % chunk
    s = 1
    while s < chunk:
        g = g + jnp.where(row >= s, pltpu.roll(g, s, axis=0), 0.0)
        s *= 2
    lane = lax.broadcasted_iota(jnp.int32, g.shape, 1)
    gcol_ref[...] = jnp.where(lane < heads, g, jax.nn.sigmoid(ab))

    for grp in range(n // tn):
        cs = slice(grp * tn, (grp + 1) * tn)
        kind = (grp * tn) // nb
        u = _dot(hb, w_ref[:, cs])
        if kind == 3:
            qkvz_ref[:, cs] = _silu(u).astype(BF16)
            continue
        y = _conv_silu(u, tail_sc[:, cs], cw_ref[:, cs], stage_sc.at[grp % 2])
        tail_sc[:, cs] = u[tm - SUBLANES:tm]
        cout_ref[:, cs] = u[tm - SUBLANES:tm]
        if kind == 2:
            qkvz_ref[:, cs] = y.astype(BF16)
            continue
        scale = dk ** -0.5 if kind == 0 else 1.0
        for hh in range(tn // dk):
            ys = y[:, hh * dk:(hh + 1) * dk]
            inv = lax.rsqrt(jnp.sum(ys * ys, axis=-1, keepdims=True) + L2_EPS) * scale
            qkvz_ref[:, grp * tn + hh * dk:grp * tn + (hh + 1) * dk] = (ys * inv).astype(BF16)


def _gdn_in(x, mod, nw, prev, w_main, w_ab, conv_w, conv_init, a_log, dt_bias, *, layer, heads, dk, tm, chunk):
    b, t, d = x.shape
    n = w_main.shape[2]
    nb = n // 4
    tn = 4 * dk
    assert nb == heads * dk and nb % tn == 0 and t % tm == 0 and tm % SUBLANES == 0 and tm % chunk == 0
    fused = prev is not None
    pre_specs, pre_args, x_spec, x_shape = _fused_specs(prev, b, t, d, tm) if fused else ([], [], [], [])
    kern = functools.partial(_gdn_in_kernel, fused=fused, heads=heads, dk=dk, chunk=chunk, tn=tn)
    return pl.pallas_call(
        kern,
        grid=(b, t // tm),
        in_specs=[pl.BlockSpec((None, tm, d), lambda bi, ti: (bi, ti, 0)),
                  pl.BlockSpec((None, 3, d), lambda bi, ti: (bi, 0, 0)),
                  pl.BlockSpec((1, d), lambda bi, ti: (0, 0))] + pre_specs + [
                  pl.BlockSpec((None, d, n), lambda bi, ti: (layer, 0, 0)),
                  pl.BlockSpec((d, 2 * LANES), lambda bi, ti: (0, 0)),
                  pl.BlockSpec((CONV_WIDTH, 3 * nb), lambda bi, ti: (0, 0)),
                  pl.BlockSpec((None, SUBLANES, 3 * nb), lambda bi, ti: (bi, 0, 0)),
                  pl.BlockSpec((1, LANES), lambda bi, ti: (0, 0)),
                  pl.BlockSpec((1, LANES), lambda bi, ti: (0, 0))],
        out_specs=x_spec + [pl.BlockSpec((None, tm, n), lambda bi, ti: (bi, ti, 0)),
                            pl.BlockSpec((None, tm, LANES), lambda bi, ti: (bi, ti, 0)),
                            pl.BlockSpec((None, SUBLANES, 3 * nb), lambda bi, ti: (bi, 0, 0))],
        out_shape=x_shape + [jax.ShapeDtypeStruct((b, t, n), BF16),
                             jax.ShapeDtypeStruct((b, t, LANES), F32),
                             jax.ShapeDtypeStruct((b, SUBLANES, 3 * nb), F32)],
        scratch_shapes=[pltpu.VMEM((SUBLANES, 3 * nb), F32), pltpu.VMEM((2, SUBLANES + tm, tn), F32)],
        compiler_params=_cparams("parallel", "arbitrary"),
    )(x, mod, nw, *pre_args, w_main, w_ab, conv_w, conv_init, a_log, dt_bias)


def _gdn_kernel(q_ref, k_ref, v_ref, z_ref, gcol_ref, s0_ref, ow_ref, o_ref, s_ref, *, heads, dk, chunk):
    t = pl.program_id(1)
    tb = q_ref.shape[0]
    c = chunk
    group = 2 if (tb // c) % 2 == 0 else 1

    @pl.when(t == 0)
    def _():
        s_ref[...] = s0_ref[...]

    ii = lax.broadcasted_iota(jnp.int32, (c, c), 0)
    jj = lax.broadcasted_iota(jnp.int32, (c, c), 1)
    incl = ii >= jj
    strict = ii > jj
    eye = (ii == jj).astype(F32)
    off_masks = []
    s = 1
    while s < c:
        off_masks.append((ii // (2 * s) == jj // (2 * s)) & (ii // s != jj // s) & strict)
        s *= 2

    def chunk_group(ci, carry):
        rows = [pl.ds(pl.multiple_of((ci * group + k) * c, c), c) for k in range(group)]
        hsl = [slice(hh * dk, (hh + 1) * dk) for hh in range(heads)]
        units = [(k, hh) for k in range(group) for hh in range(heads)]
        gall = [gcol_ref[r, :] for r in rows]
        g = [gall[k][:, hh:hh + 1] for k, hh in units]
        beta = [gall[k][:, heads + hh:heads + hh + 1] for k, hh in units]
        kbf = [k_ref[rows[k], hsl[hh]] for k, hh in units]
        qbf = [q_ref[rows[k], hsl[hh]] for k, hh in units]
        kf = [x.astype(F32) for x in kbf]
        kb = [x * y for x, y in zip(kf, beta)]
        gb = [jnp.broadcast_to(x, (c, c)) for x in g]
        decay = [jnp.exp(jnp.where(incl, x - x.T, -jnp.inf)) for x in gb]
        a = [jnp.where(strict, _dot_nt(x.astype(BF16), y) * d, 0.0) for x, y, d in zip(kb, kbf, decay)]
        qk = [(_dot_nt(x, y) * d).astype(BF16) for x, y, d in zip(qbf, kbf, decay)]
        p = [eye - jnp.where(off_masks[0], x, 0.0) for x in a]
        for off in off_masks[1:]:
            pb = [x.astype(BF16) for x in p]
            x = [_dot(jnp.where(off, ai, 0.0).astype(BF16), pi).astype(BF16) for ai, pi in zip(a, pb)]
            p = [pi - _dot(pbi, xi) for pi, pbi, xi in zip(p, pb, x)]
        eg = [jnp.exp(x) for x in g]
        rhs = [jnp.concatenate([v_ref[rows[k], hsl[hh]].astype(F32) * beta[u], kb[u] * eg[u]], axis=1).astype(BF16)
               for u, (k, hh) in enumerate(units)]
        sol = [_dot(x.astype(BF16), y) for x, y in zip(p, rhs)]
        wq = [jnp.concatenate([sol[u][:, dk:].astype(BF16), (qbf[u].astype(F32) * eg[u]).astype(BF16)], axis=0)
              for u in range(len(units))]
        g_last = [x[c - 1:c, :] for x in g]
        kd_t = [(kf[u] * jnp.exp(g_last[u] - g[u])).T.astype(BF16) for u in range(len(units))]
        e_last = [jnp.exp(x) for x in g_last]
        for k in range(group):
            us = [k * heads + hh for hh in range(heads)]
            s_old = [s_ref[hh] for hh in range(heads)]
            r1 = [_dot(wq[u], s_old[hh].astype(BF16)) for hh, u in enumerate(us)]
            ub = [(sol[u][:, :dk] - r1[hh][:c]).astype(BF16) for hh, u in enumerate(us)]
            for hh, u in enumerate(us):
                s_ref[hh] = s_old[hh] * e_last[u] + _dot(kd_t[u], ub[hh])
            o = [r1[hh][c:] + _dot(qk[u], ub[hh]) for hh, u in enumerate(us)]
            for hh in range(heads):
                on = o[hh] * lax.rsqrt(jnp.mean(o[hh] * o[hh], axis=-1, keepdims=True) + RMS_EPS) * ow_ref[...]
                o_ref[rows[k], hsl[hh]] = (on * z_ref[rows[k], hsl[hh]].astype(F32)).astype(BF16)
        return carry

    lax.fori_loop(0, tb // (c * group), chunk_group, 0)


def _gdn(qkvz, gcol, s0, onorm, *, heads, dk, tb, chunk):
    b, t, n = qkvz.shape
    nb = n // 4
    assert t % tb == 0 and tb % chunk == 0 and nb == heads * dk
    kern = functools.partial(_gdn_kernel, heads=heads, dk=dk, chunk=chunk)
    col = lambda jcol: pl.BlockSpec((None, tb, nb), lambda bi, ti: (bi, ti, jcol))
    return pl.pallas_call(
        kern,
        grid=(b, t // tb),
        in_specs=[col(0), col(1), col(2), col(3),
                  pl.BlockSpec((None, tb, LANES), lambda bi, ti: (bi, ti, 0)),
                  pl.BlockSpec((None, heads, dk, dk), lambda bi, ti: (bi, 0, 0, 0)),
                  pl.BlockSpec((1, dk), lambda bi, ti: (0, 0))],
        out_specs=[pl.BlockSpec((None, tb, nb), lambda bi, ti: (bi, ti, 0)),
                   pl.BlockSpec((None, heads, dk, dk), lambda bi, ti: (bi, 0, 0, 0))],
        out_shape=[jax.ShapeDtypeStruct((b, t, nb), BF16),
                   jax.ShapeDtypeStruct((b, heads, dk, dk), F32)],
        compiler_params=_cparams("parallel", "arbitrary"),
    )(qkvz, qkvz, qkvz, qkvz, gcol, s0, onorm)


def _rope(x, cos, sin_signed, lane_lo):
    half = MASK_CHUNK // 2
    swapped = jnp.where(lane_lo, pltpu.roll(x, LANES - half, axis=1), pltpu.roll(x, half, axis=1))
    return x * cos + swapped * sin_signed


def _diff_in_kernel(*refs, fused, aliased, layers, v_transposed, q_scale, tn):
    x, mod_ref, nw_ref, own_in, outs = _layer_input(refs, fused, 5 if aliased else 3)
    w_ref, cos_ref, sin_ref = own_in[:3]
    qb_ref, kb_ref, kf_hbm, vf_hbm, vb_ref, zs_ref, kn2_ref, kst_sc, vst_sc, sem = outs
    n = w_ref.shape[1]
    nb = n // 4
    tm = x.shape[0]
    tiles = pl.num_programs(1)
    step = pl.program_id(0) * tiles + pl.program_id(1)
    slot = step % 2

    def cache_copies(slot_, step_):
        bi, ti = step_ // tiles, step_ % tiles
        return [pltpu.make_async_copy(stage.at[slot_, :, pl.ds(hh * LANES, LANES)],
                                      cache.at[layer, bi, pl.ds(ti * tm, tm), hh, :],
                                      sem.at[which, slot_])
                for which, (stage, cache) in enumerate(((kst_sc, kf_hbm), (vst_sc, vf_hbm)))
                for layer in layers
                for hh in range(nb // LANES)]

    hb = _prenorm(x, mod_ref, nw_ref).astype(BF16)
    cos = cos_ref[...]
    sin = sin_ref[...]
    lane_lo = (lax.broadcasted_iota(jnp.int32, cos.shape, 1) % MASK_CHUNK) < (MASK_CHUNK // 2)
    kn2 = jnp.zeros((1, LANES), F32)
    for grp in range(n // tn):
        kind = (grp * tn) // nb
        lo = grp * tn - kind * nb
        u = _dot(hb, w_ref[:, grp * tn:(grp + 1) * tn])
        squares = []
        for s in range(tn // LANES):
            us = u[:, s * LANES:(s + 1) * LANES]
            cols = slice(lo + s * LANES, lo + (s + 1) * LANES)
            if kind == 0:
                qb_ref[:, cols] = (_rope(us, cos, sin, lane_lo) * q_scale).astype(BF16)
            elif kind == 1:
                r = _rope(us, cos, sin, lane_lo)
                kst_sc[slot, :, cols] = r
                kb_ref[:, cols] = r.astype(BF16)
                squares.append(r * r)
            elif kind == 2:
                vst_sc[slot, :, cols] = us
            else:
                zs_ref[:, cols] = _silu(us).astype(BF16)
        if kind == 1:
            group_sum = ((lax.broadcasted_iota(jnp.int32, (tn, LANES), 0) + lo) // MASK_CHUNK
                         == lax.broadcasted_iota(jnp.int32, (tn, LANES), 1)).astype(BF16)
            n2 = _dot(jnp.concatenate(squares, axis=1).astype(BF16), group_sum)
            kn2 = jnp.maximum(kn2, jnp.max(n2, axis=0, keepdims=True))
        if kind == 2:
            if v_transposed:
                vb_ref[lo:lo + tn, :] = u.T.astype(BF16)
            else:
                vb_ref[:, lo:lo + tn] = u.astype(BF16)
    kn2_ref[...] = kn2

    for copy in cache_copies(slot, step):
        copy.start()

    @pl.when(step > 0)
    def _():
        for copy in cache_copies(1 - slot, step - 1):
            copy.wait()

    @pl.when(step == pl.num_programs(0) * tiles - 1)
    def _():
        for copy in cache_copies(slot, step):
            copy.wait()


def _diff_in(x, mod, nw, prev, w, cos_t, sin_t, caches, *, layer, heads, tm, v_transposed, q_scale):
    b, t, d = x.shape
    n_layers = w.shape[0]
    nb = w.shape[2] // 4
    tn = 4 * LANES
    assert t % tm == 0 and nb == heads * LANES and nb % tn == 0
    fused = prev is not None
    aliased = caches is not None
    pre_specs, pre_args, x_spec, x_shape = _fused_specs(prev, b, t, d, tm) if fused else ([], [], [], [])
    kern = functools.partial(_diff_in_kernel, fused=fused, aliased=aliased,
                             layers=(layer,) if aliased else tuple(range(n_layers)),
                             v_transposed=v_transposed, q_scale=q_scale, tn=tn)
    tile = pl.BlockSpec((None, tm, nb), lambda bi, ti: (bi, ti, 0))
    in_hbm = pl.BlockSpec(memory_space=pl.ANY)
    cache_shape = jax.ShapeDtypeStruct((n_layers, b, t, heads, LANES), F32)
    if v_transposed:
        vb_spec = pl.BlockSpec((None, nb, tm), lambda bi, ti: (bi, 0, ti))
        vb_shape = jax.ShapeDtypeStruct((b, nb, t), BF16)
    else:
        vb_spec, vb_shape = tile, jax.ShapeDtypeStruct((b, t, nb), BF16)
    n_in = 3 + len(pre_args) + 3
    return pl.pallas_call(
        kern,
        grid=(b, t // tm),
        in_specs=[pl.BlockSpec((None, tm, d), lambda bi, ti: (bi, ti, 0)),
                  pl.BlockSpec((None, 3, d), lambda bi, ti: (bi, 0, 0)),
                  pl.BlockSpec((1, d), lambda bi, ti: (0, 0))] + pre_specs + [
                  pl.BlockSpec((None, d, 4 * nb), lambda bi, ti: (layer, 0, 0)),
                  pl.BlockSpec((tm, LANES), lambda bi, ti: (ti, 0)),
                  pl.BlockSpec((tm, LANES), lambda bi, ti: (ti, 0))] + ([in_hbm, in_hbm] if aliased else []),
        out_specs=x_spec + [tile, tile, in_hbm, in_hbm, vb_spec, tile,
                            pl.BlockSpec((None, None, 1, LANES), lambda bi, ti: (bi, ti, 0, 0))],
        out_shape=x_shape + [jax.ShapeDtypeStruct((b, t, nb), BF16),
                             jax.ShapeDtypeStruct((b, t, nb), BF16),
                             cache_shape,
                             cache_shape,
                             vb_shape,
                             jax.ShapeDtypeStruct((b, t, nb), BF16),
                             jax.ShapeDtypeStruct((b, t // tm, 1, LANES), F32)],
        input_output_aliases={n_in: 2 + len(x_shape), n_in + 1: 3 + len(x_shape)} if aliased else {},
        scratch_shapes=[pltpu.VMEM((2, tm, nb), F32), pltpu.VMEM((2, tm, nb), F32),
                        pltpu.SemaphoreType.DMA((2, 2))],
        compiler_params=_cparams("arbitrary", "arbitrary"),
    )(x, mod, nw, *pre_args, w, cos_t, sin_t, *(caches if aliased else ()))


def _lambda(lam_ref, lam_init):
    l1 = jnp.sum(lam_ref[0:1, :] * lam_ref[1:2, :], axis=-1, keepdims=True)
    l2 = jnp.sum(lam_ref[2:3, :] * lam_ref[3:4, :], axis=-1, keepdims=True)
    return jnp.exp(l1) - jnp.exp(l2) + lam_init


def _attn_prompt_kernel(q_ref, k_ref, vt_ref, z_ref, kn2_ref, lam_ref, sw_ref, o_ref,
                        m_sc, l_sc, acc_sc, *, lam_init, tk):
    qi = pl.program_id(2)
    tq, dh = q_ref.shape
    dq = dh // 2
    nsub = tq // tk
    chains = [(sub, comp) for sub in range(nsub) for comp in range(2)]
    everyone = list(range(len(chains)))
    n_full = (qi * tq) // tk
    qz = []
    for sub, comp in chains:
        q = q_ref[sub * tk:(sub + 1) * tk, :]
        lane = lax.broadcasted_iota(jnp.int32, q.shape, 1)
        qz.append(jnp.where((lane < dq) if comp == 0 else (lane >= dq), q, jnp.zeros_like(q)))

    kn2_max = jnp.max(kn2_ref[...], axis=0)
    kn2_lane = lax.broadcasted_iota(jnp.int32, kn2_max.shape, 1)
    kmax2 = [jnp.max(jnp.where(kn2_lane == 2 * pl.program_id(1) + comp, kn2_max, 0.0), axis=1, keepdims=True)
             for comp in range(2)]

    ones = jnp.ones((SUBLANES, dh), BF16)
    ref_rows = []
    for c, (sub, comp) in enumerate(chains):
        qf = qz[c].astype(F32)
        qn2 = _dot_nt(ones, (qf * qf).astype(BF16))[0:1, :]
        ref_rows.append(REF_SLACK * jnp.sqrt(qn2 * kmax2[comp]))

    visible = (lax.broadcasted_iota(jnp.int32, (tk, tk), 0) // MASK_CHUNK
               <= lax.broadcasted_iota(jnp.int32, (tk, tk), 1) // MASK_CHUNK)

    def blocks(kb):
        ks = k_ref[pl.ds(pl.multiple_of(kb * tk, tk), tk), :]
        vt = vt_ref[:, pl.ds(pl.multiple_of(kb * tk, tk), tk)]
        return ks, vt

    def fast_step(kb, active, masked_sub):
        ks, vt = blocks(kb)
        for g0 in range(0, len(active), STAGE_GROUP):
            grp = active[g0:g0 + STAGE_GROUP]
            s = [_dot_nt(ks, qz[c]) for c in grp]
            p = [jnp.exp2(si - ref_rows[c]) for c, si in zip(grp, s)]
            p = [jnp.where(visible, pi, 0.0) if chains[c][0] == masked_sub else pi for c, pi in zip(grp, p)]
            for c, pi in zip(grp, p):
                l_sc[c] += jnp.sum(pi.reshape(tk // SUBLANES, SUBLANES, tk), axis=0)
            pv = [_dot(vt, pi.astype(BF16)) for pi in p]
            for c, pvi in zip(grp, pv):
                acc_sc[c] += pvi

    def online_step(kb, active, masked_sub):
        ks, vt = blocks(kb)
        for c in active:
            s = _dot_nt(ks, qz[c])
            if chains[c][0] == masked_sub:
                s = jnp.where(visible, s, -jnp.inf)
            m_old = m_sc[c]
            m_new = jnp.maximum(m_old, jnp.max(s, axis=0, keepdims=True))
            alpha = jnp.exp2(m_old - m_new)
            p = jnp.exp2(s - m_new)
            l_sc[c, 0:1, :] = alpha * l_sc[c, 0:1, :] + jnp.sum(p, axis=0, keepdims=True)
            acc_sc[c] = alpha * acc_sc[c] + _dot(vt, p.astype(BF16))
            m_sc[c] = m_new

    def sweep(step, unroll):
        l_sc[...] = jnp.zeros(l_sc.shape, F32)
        acc_sc[...] = jnp.zeros(acc_sc.shape, F32)
        def body(i, cry):
            for u in range(unroll):
                step(i * unroll + u, everyone, None)
            return cry
        lax.fori_loop(0, n_full // unroll, body, 0)
        for d in range(nsub):
            step(n_full + d, [c for c in everyone if chains[c][0] >= d], d)

    def normalisers():
        return [jnp.sum(l_sc[c], axis=0, keepdims=True) for c in everyone]

    sweep(fast_step, min(nsub, 4))
    l_min = functools.reduce(jnp.minimum, normalisers())

    @pl.when(jnp.min(l_min) < MIN_NORMALISER)
    def _():
        m_sc[...] = jnp.full(m_sc.shape, -jnp.inf, F32)
        sweep(online_step, 1)

    lam = _lambda(lam_ref, lam_init)
    l = normalisers()
    for sub in range(nsub):
        c1, c2 = 2 * sub, 2 * sub + 1
        o_t = acc_sc[c1] * (1.0 / l[c1]) - acc_sc[c2] * (lam / l[c2])
        o = o_t.T
        gain = sw_ref[...] * (1.0 - lam_init)
        o = o * lax.rsqrt(jnp.mean(o * o, axis=-1, keepdims=True) + RMS_EPS) * gain
        rows = slice(sub * tk, (sub + 1) * tk)
        o_ref[rows, :] = (o * z_ref[rows, :].astype(F32)).astype(BF16)


def _attn_prompt(qb, kb, vt, zs, kn2, lam_vecs, subln, *, heads, lam_init, tq, tk):
    b, t, n = qb.shape
    dh = n // heads
    assert t % tq == 0 and tq % tk == 0 and tk % MASK_CHUNK == 0
    nchains = 2 * (tq // tk)
    kern = functools.partial(_attn_prompt_kernel, lam_init=lam_init, tk=tk)
    return pl.pallas_call(
        kern,
        grid=(b, heads, t // tq),
        in_specs=[pl.BlockSpec((None, tq, dh), lambda bi, hi, qi: (bi, qi, hi)),
                  pl.BlockSpec((None, t, dh), lambda bi, hi, qi: (bi, 0, hi)),
                  pl.BlockSpec((None, dh, t), lambda bi, hi, qi: (bi, hi, 0)),
                  pl.BlockSpec((None, tq, dh), lambda bi, hi, qi: (bi, qi, hi)),
                  pl.BlockSpec((None,) + kn2.shape[1:], lambda bi, hi, qi: (bi, 0, 0, 0)),
                  pl.BlockSpec(lam_vecs.shape, lambda bi, hi, qi: (0, 0)),
                  pl.BlockSpec((1, dh), lambda bi, hi, qi: (0, 0))],
        out_specs=pl.BlockSpec((None, tq, dh), lambda bi, hi, qi: (bi, qi, hi)),
        out_shape=jax.ShapeDtypeStruct((b, t, n), BF16),
        scratch_shapes=[pltpu.VMEM((nchains, 1, tk), F32), pltpu.VMEM((nchains, SUBLANES, tk), F32),
                        pltpu.VMEM((nchains, dh, tk), F32)],
        compiler_params=_cparams("parallel", "parallel", "arbitrary"),
    )(qb, kb, vt, zs, kn2, lam_vecs, subln)


def _attn_sample_kernel(q_ref, kn_ref, vn_ref, ck_ref, cv_ref, z_ref, lam_ref, sw_ref, o_ref,
                        q2_sc, bias_sc, m_sc, l_sc, acc_sc, *, heads, lam_init):
    si = pl.program_id(1)
    ns = pl.num_programs(1)
    tq, n = q_ref.shape
    dh = n // heads
    dq = dh // 2
    rows = heads * 2 * tq

    def head_bias(cols):
        qh = lax.broadcasted_iota(jnp.int32, (rows, cols), 0) // (2 * tq)
        kh = lax.broadcasted_iota(jnp.int32, (rows, cols), 1) % heads
        return jnp.where(qh == kh, 0.0, -jnp.inf)

    @pl.when(si == 0)
    def _():
        m_sc[...] = jnp.full(m_sc.shape, -jnp.inf, F32)
        l_sc[...] = jnp.zeros(l_sc.shape, F32)
        acc_sc[...] = jnp.zeros(acc_sc.shape, F32)
        bias_sc[...] = head_bias(bias_sc.shape[1])
        for hh in range(heads):
            q = q_ref[:, hh * dh:(hh + 1) * dh]
            lane = lax.broadcasted_iota(jnp.int32, q.shape, 1)
            q2_sc[hh * 2 * tq:hh * 2 * tq + tq, :] = jnp.where(lane < dq, q, jnp.zeros_like(q))
            q2_sc[hh * 2 * tq + tq:(hh + 1) * 2 * tq, :] = jnp.where(lane >= dq, q, jnp.zeros_like(q))

    def update(k_rows, v_rows, bias):
        s = _dot_nt(q2_sc[...], k_rows) + bias
        m_old = m_sc[...]
        m_new = jnp.maximum(m_old, jnp.max(s, axis=-1, keepdims=True))
        alpha = jnp.exp2(m_old - m_new)
        p = jnp.exp2(s - m_new)
        l_sc[...] = alpha * l_sc[...] + jnp.sum(p, axis=-1, keepdims=True)
        acc_sc[...] = alpha * acc_sc[...] + _dot(p.astype(BF16), v_rows)
        m_sc[...] = m_new

    tkv = ck_ref.shape[0]
    update(ck_ref[...].reshape(tkv * heads, dh).astype(BF16), cv_ref[...].reshape(tkv * heads, dh).astype(BF16),
           bias_sc[...])

    @pl.when(si == ns - 1)
    def _():
        update(kn_ref[...].reshape(tq * heads, dh).astype(BF16), vn_ref[...].reshape(tq * heads, dh).astype(BF16),
               head_bias(tq * heads))
        lam = _lambda(lam_ref, lam_init)
        on = acc_sc[...] / l_sc[...]
        for hh in range(heads):
            hs = slice(hh * dh, (hh + 1) * dh)
            o = on[hh * 2 * tq:hh * 2 * tq + tq] - lam * on[hh * 2 * tq + tq:(hh + 1) * 2 * tq]
            o = o * lax.rsqrt(jnp.mean(o * o, axis=-1, keepdims=True) + RMS_EPS) * sw_ref[...] * (1.0 - lam_init)
            o_ref[:, hs] = (o * z_ref[:, hs].astype(F32)).astype(BF16)


def _attn_sample(qb, k_new, v_new, cache_k, cache_v, zs, lam_vecs, subln, *, layer, heads, lam_init, tkv):
    b, tq, n = qb.shape
    past = cache_k.shape[2]
    dh = n // heads
    assert past % tkv == 0 and cache_k.shape[3:] == (heads, dh) and k_new.shape[2:] == (tq, heads, dh)
    kern = functools.partial(_attn_sample_kernel, heads=heads, lam_init=lam_init)
    rows = heads * 2 * tq
    tile = pl.BlockSpec((None, tq, n), lambda bi, si: (bi, 0, 0))
    new = pl.BlockSpec((None, None, tq, heads, dh), lambda bi, si: (layer, bi, 0, 0, 0))
    old = pl.BlockSpec((None, None, tkv, heads, dh), lambda bi, si: (layer, bi, si, 0, 0))
    return pl.pallas_call(
        kern,
        grid=(b, past // tkv),
        in_specs=[tile, new, new, old, old, tile,
                  pl.BlockSpec(lam_vecs.shape, lambda bi, si: (0, 0)),
                  pl.BlockSpec((1, dh), lambda bi, si: (0, 0))],
        out_specs=tile,
        out_shape=jax.ShapeDtypeStruct((b, tq, n), BF16),
        scratch_shapes=[pltpu.VMEM((rows, dh), BF16), pltpu.VMEM((rows, tkv * heads), F32),
                        pltpu.VMEM((rows, 1), F32), pltpu.VMEM((rows, 1), F32), pltpu.VMEM((rows, dh), F32)],
        compiler_params=_cparams("parallel", "arbitrary"),
    )(qb, k_new, v_new, cache_k, cache_v, zs, lam_vecs, subln)


def _out_kernel(x_ref, o_ref, w_ref, mod_ref, nw_ref, y_ref):
    y_ref[...] = _residual_update(x_ref[...], o_ref, w_ref, mod_ref, nw_ref)


def _out_proj(x, prev, *, tm):
    b, t, d = x.shape
    pre_specs, pre_args, x_spec, x_shape = _fused_specs(prev, b, t, d, tm)
    return pl.pallas_call(
        _out_kernel,
        grid=(b, t // tm),
        in_specs=[pl.BlockSpec((None, tm, d), lambda bi, ti: (bi, ti, 0))] + pre_specs,
        out_specs=x_spec[0],
        out_shape=x_shape[0],
        compiler_params=_cparams("parallel", "parallel"),
    )(x, *pre_args)


def _pick_tile(t, target):
    tile = min(t, target)
    assert t % tile == 0
    return tile


def _rope_tables(pos):
    half = MASK_CHUNK // 2
    inv = 1.0 / (ROPE_THETA ** (jnp.arange(half, dtype=F32) / half))
    ang = pos.astype(F32)[:, None] * inv[None, :]
    cos, sin = jnp.cos(ang), jnp.sin(ang)
    reps = LANES // MASK_CHUNK
    return (jnp.tile(jnp.concatenate([cos, cos], axis=1), (1, reps)),
            jnp.tile(jnp.concatenate([-sin, sin], axis=1), (1, reps)))


def _trunk(x, ada, pos, conv_bufs, gdn_states, past_k, past_v, p):
    b, t, d = x.shape
    depth = ada.shape[0]
    gdn_heads = p["a_log_gdn"].shape[1]
    gdn_dk = p["onorm_gdn"].shape[1]
    dq = p["lam_q1"].shape[1]
    n_diff = p["w_out_diff"].shape[1]
    diff_heads = n_diff // (2 * dq)
    gdn_chunk = LANES
    n_conv = 3 * gdn_heads * gdn_dk
    tm = _pick_tile(t, 512)
    cos_t, sin_t = _rope_tables(pos)
    prompt = past_k is None
    caches = None
    states, convs = [], []
    prev = None
    for i in range(depth):
        j = i // 2
        mod = ada[i]
        nw_pre = p["norm_pre"][i][None, :]
        nw_post = p["norm_post"][i][None, :]
        if i % 2 == 0:
            a_log = jnp.pad(p["a_log_gdn"][j][None, :], ((0, 0), (0, LANES - gdn_heads)))
            dt_b = jnp.pad(p["dt_bias_gdn"][j][None, :], ((0, 0), (0, LANES - gdn_heads)))
            if conv_bufs is None:
                conv_init = jnp.zeros((b, SUBLANES, n_conv), F32)
                s0 = jnp.zeros((b, gdn_heads, gdn_dk, gdn_dk), F32)
            else:
                conv_init = jnp.pad(conv_bufs[j], ((0, 0), (SUBLANES - (CONV_WIDTH - 1), 0), (0, 0)))
                s0 = gdn_states[j]
            res = _gdn_in(x, mod, nw_pre, prev, p["w_main_gdn"], p["w_ab_gdn"][j], p["conv_gdn"][j],
                          conv_init, a_log, dt_b, layer=j, heads=gdn_heads, dk=gdn_dk, tm=tm,
                          chunk=min(gdn_chunk, tm))
            if prev is not None:
                x, res = res[0], res[1:]
            qkvz, gcol, cout = res
            if t < gdn_chunk:
                front = gdn_chunk - t
                qkvz = jnp.pad(qkvz, ((0, 0), (front, 0), (0, 0)))
                gcol = jnp.pad(gcol, ((0, 0), (front, 0), (0, 0)))
            o, st = _gdn(qkvz, gcol, s0, p["onorm_gdn"][j][None, :], heads=gdn_heads, dk=gdn_dk,
                         tb=_pick_tile(qkvz.shape[1], 256), chunk=gdn_chunk)
            o = o[:, -t:]
            convs.append(cout[:, SUBLANES - (CONV_WIDTH - 1):, :])
            states.append(st)
            w_out = p["w_out_gdn"]
        else:
            lam_init = 0.8 - 0.6 * math.exp(-0.3 * i)
            lam_vecs = jnp.stack([p["lam_q1"][j], p["lam_k1"][j], p["lam_q2"][j], p["lam_k2"][j]])
            res = _diff_in(x, mod, nw_pre, prev, p["w_in_diff"], cos_t, sin_t, caches, layer=j, heads=diff_heads,
                           tm=tm, v_transposed=prompt, q_scale=dq ** -0.5 * LOG2E)
            if prev is not None:
                x, res = res[0], res[1:]
            qb, kb, k_all, v_all, vb, zs, kn2 = res
            caches = (k_all, v_all)
            subln = p["subln_diff"][j][None, :]
            if prompt:
                o = _attn_prompt(qb, kb, vb, zs, kn2, lam_vecs, subln, heads=diff_heads, lam_init=lam_init,
                                 tq=_pick_tile(t, 2048), tk=_pick_tile(t, 256))
            else:
                o = _attn_sample(qb, k_all, v_all, past_k, past_v, zs, lam_vecs, subln, layer=j, heads=diff_heads,
                                 lam_init=lam_init, tkv=_pick_tile(past_k.shape[2], 512))
            w_out = p["w_out_diff"]
        prev = (o, w_out, j, mod, nw_post)
    x = _out_proj(x, prev, tm=tm)
    return x, jnp.stack(states), jnp.stack(convs), k_all, v_all


def kernel(x_prompt, x_sample, c_prompt, c_sample, state_gdn, cache_conv, cache_k, cache_v, norm_pre, norm_post, w_ada, b_ada, w_in_gdn, conv_gdn, a_log_gdn, dt_bias_gdn, onorm_gdn, w_out_gdn, w_in_diff, lam_q1, lam_k1, lam_q2, lam_k2, subln_diff, w_out_diff):
    gdn_heads = a_log_gdn.shape[1]
    n_main = w_in_gdn.shape[2] - 2 * gdn_heads
    w_ab = jnp.pad(w_in_gdn[:, :, n_main:], ((0, 0), (0, 0), (0, LANES - 2 * gdn_heads)))
    w_ab_hi = w_ab.astype(BF16)
    w_ab_lo = (w_ab - w_ab_hi.astype(F32)).astype(BF16)
    p = {"norm_pre": norm_pre, "norm_post": norm_post,
         "w_main_gdn": w_in_gdn[:, :, :n_main].astype(BF16),
         "w_ab_gdn": jnp.concatenate([w_ab_hi, w_ab_lo], axis=-1),

         "conv_gdn": conv_gdn, "a_log_gdn": a_log_gdn,
         "dt_bias_gdn": dt_bias_gdn, "onorm_gdn": onorm_gdn, "w_out_gdn": w_out_gdn.astype(BF16),
         "w_in_diff": w_in_diff.astype(BF16), "lam_q1": lam_q1, "lam_k1": lam_k1, "lam_q2": lam_q2,
         "lam_k2": lam_k2, "subln_diff": subln_diff, "w_out_diff": w_out_diff.astype(BF16)}
    bp, tp, d = x_prompt.shape
    bs, ts, _ = x_sample.shape
    past = cache_k.shape[2]
    depth = w_ada.shape[0]
    ada = _ada(jnp.concatenate([c_prompt, c_sample], axis=0), w_ada, b_ada)
    ada = ada.reshape(depth, bp + bs, 3, d)
    y_p, st_p, conv_p, k_p, v_p = _trunk(x_prompt, ada[:, :bp], jnp.arange(tp), None, None, None, None, p)
    y_s, st_s, conv_s, k_s, v_s = _trunk(x_sample, ada[:, bp:], past + jnp.arange(ts), cache_conv, state_gdn,
                                         cache_k, cache_v, p)
    return (y_p, y_s, st_p, conv_p, k_p, v_p, st_s, conv_s, k_s, v_s)
```

```python
import functools
import math

import jax
import jax.numpy as jnp
from jax import lax
from jax.experimental import pallas as pl
from jax.experimental.pallas import tpu as pltpu

F32 = jnp.float32
BF16 = jnp.bfloat16
HIGHEST = lax.Precision.HIGHEST

RMS_EPS = 1e-6
L2_EPS = 1e-6
ROPE_THETA = 10000.0
CONV_WIDTH = 4
MASK_CHUNK = 64
LANES = 128
SUBLANES = 8
VMEM_LIMIT = 56 * 1024 * 1024
LOG2E = 1.4426950408889634
REF_SLACK = 1.01
MIN_NORMALISER = 1e-30
STAGE_GROUP = 32

NT_DIMS = (((1,), (1,)), ((), ()))


def _cparams(*sem):
    return pltpu.CompilerParams(dimension_semantics=sem, vmem_limit_bytes=VMEM_LIMIT)


def _silu(x):
    return x * jax.nn.sigmoid(x)


def _dot(a, b):
    return jnp.dot(a, b, preferred_element_type=F32)


def _dot_nt(a, b):
    return lax.dot_general(a, b, NT_DIMS, preferred_element_type=F32)


def _ada_kernel(c_ref, w_ref, b_ref, o_ref):
    act = _silu(c_ref[...])
    o_ref[...] = jnp.dot(act, w_ref[...], precision=HIGHEST, preferred_element_type=F32) + b_ref[...]


def _ada(c_all, w_ada, b_ada):
    depth, d, n = w_ada.shape
    bc = c_all.shape[0]
    tn = 512
    return pl.pallas_call(
        _ada_kernel,
        grid=(depth, n // tn),
        in_specs=[pl.BlockSpec((bc, d), lambda i, j: (0, 0)),
                  pl.BlockSpec((None, d, tn), lambda i, j: (i, 0, j)),
                  pl.BlockSpec((None, 1, tn), lambda i, j: (i, 0, j))],
        out_specs=pl.BlockSpec((None, bc, tn), lambda i, j: (i, 0, j)),
        out_shape=jax.ShapeDtypeStruct((depth, bc, n), F32),
        compiler_params=_cparams("parallel", "parallel"),
    )(c_all, w_ada, b_ada.reshape(depth, 1, n))


def _residual_update(x, o_ref, w_ref, mod_ref, nw_ref):
    y = _dot(o_ref[...], w_ref[...])
    yn = y * lax.rsqrt(jnp.mean(y * y, axis=-1, keepdims=True) + RMS_EPS) * nw_ref[...]
    return x + mod_ref[2:3, :] * yn


def _prenorm(x, mod_ref, nw_ref):
    y = x * lax.rsqrt(jnp.mean(x * x, axis=-1, keepdims=True) + RMS_EPS) * nw_ref[...]
    return y * (1.0 + mod_ref[1:2, :]) + mod_ref[0:1, :]


def _layer_input(refs, fused, n_in):
    x_ref, mod_ref, nw_ref = refs[:3]
    pos = 7 if fused else 3
    own_in = refs[pos:pos + n_in]
    pos += n_in
    x = x_ref[...]
    if fused:
        x = _residual_update(x, *refs[3:7])
        refs[pos][...] = x
        pos += 1
    return x, mod_ref, nw_ref, own_in, refs[pos:]


def _fused_specs(prev, b, t, d, tm):
    o, w, layer, mod, nw = prev
    n = o.shape[2]
    in_specs = [pl.BlockSpec((None, tm, n), lambda bi, ti: (bi, ti, 0)),
                pl.BlockSpec((None, n, d), lambda bi, ti: (layer, 0, 0)),
                pl.BlockSpec((None, 3, d), lambda bi, ti: (bi, 0, 0)),
                pl.BlockSpec((1, d), lambda bi, ti: (0, 0))]
    return (in_specs, [o, w, mod, nw], [pl.BlockSpec((None, tm, d), lambda bi, ti: (bi, ti, 0))],
            [jax.ShapeDtypeStruct((b, t, d), F32)])


def _conv_silu(u, tail, cw, stage_ref):
    tm = u.shape[0]
    stage_ref[0:SUBLANES, :] = tail
    stage_ref[SUBLANES:SUBLANES + tm, :] = u
    y = u * cw[CONV_WIDTH - 1:CONV_WIDTH, :]
    for k in range(1, CONV_WIDTH):
        y = y + stage_ref[SUBLANES - k:SUBLANES - k + tm, :] * cw[CONV_WIDTH - 1 - k:CONV_WIDTH - k, :]
    return _silu(y)


def _gdn_in_kernel(*refs, fused, heads, dk, chunk, tn):
    x, mod_ref, nw_ref, own_in, (qkvz_ref, gcol_ref, cout_ref, tail_sc, stage_sc) = _layer_input(refs, fused, 6)
    w_ref, wab_ref, cw_ref, cinit_ref, alog_ref, dtb_ref = own_in
    t = pl.program_id(1)
    tm = x.shape[0]
    n = w_ref.shape[1]
    nb = n // 4

    @pl.when(t == 0)
    def _():
        tail_sc[...] = cinit_ref[...]

    h = _prenorm(x, mod_ref, nw_ref)
    hb = h.astype(BF16)
    h_lo = (h - hb.astype(F32)).astype(BF16)
    hi_lo = _dot(hb, wab_ref[...])
    ab = hi_lo[:, :LANES] + hi_lo[:, LANES:] + _dot(h_lo, wab_ref[:, :LANES])
    g = -jnp.exp(alog_ref[...]) * jax.nn.softplus(ab + dtb_ref[...])
    row = lax.broadcasted_iota(jnp.int32, g.shape, 0) % chunk
    s = 1
    while s < chunk:
        g = g + jnp.where(row >= s, pltpu.roll(g, s, axis=0), 0.0)
        s *= 2
    lane = lax.broadcasted_iota(jnp.int32, g.shape, 1)
    gcol_ref[...] = jnp.where(lane < heads, g, jax.nn.sigmoid(ab))

    for grp in range(n // tn):
        cs = slice(grp * tn, (grp + 1) * tn)
        kind = (grp * tn) // nb
        u = _dot(hb, w_ref[:, cs])
        if kind == 3:
            qkvz_ref[:, cs] = _silu(u).astype(BF16)
            continue
        y = _conv_silu(u, tail_sc[:, cs], cw_ref[:, cs], stage_sc.at[grp % 2])
        tail_sc[:, cs] = u[tm - SUBLANES:tm]
        cout_ref[:, cs] = u[tm - SUBLANES:tm]
        if kind == 2:
            qkvz_ref[:, cs] = y.astype(BF16)
            continue
        scale = dk ** -0.5 if kind == 0 else 1.0
        for hh in range(tn // dk):
            ys = y[:, hh * dk:(hh + 1) * dk]
            inv = lax.rsqrt(jnp.sum(ys * ys, axis=-1, keepdims=True) + L2_EPS) * scale
            qkvz_ref[:, grp * tn + hh * dk:grp * tn + (hh + 1) * dk] = (ys * inv).astype(BF16)


def _gdn_in(x, mod, nw, prev, w_main, w_ab, conv_w, conv_init, a_log, dt_bias, *, layer, heads, dk, tm, chunk):
    b, t, d = x.shape
    n = w_main.shape[2]
    nb = n // 4
    tn = 4 * dk
    assert nb == heads * dk and nb % tn == 0 and t % tm == 0 and tm % SUBLANES == 0 and tm % chunk == 0
    fused = prev is not None
    pre_specs, pre_args, x_spec, x_shape = _fused_specs(prev, b, t, d, tm) if fused else ([], [], [], [])
    kern = functools.partial(_gdn_in_kernel, fused=fused, heads=heads, dk=dk, chunk=chunk, tn=tn)
    return pl.pallas_call(
        kern,
        grid=(b, t // tm),
        in_specs=[pl.BlockSpec((None, tm, d), lambda bi, ti: (bi, ti, 0)),
                  pl.BlockSpec((None, 3, d), lambda bi, ti: (bi, 0, 0)),
                  pl.BlockSpec((1, d), lambda bi, ti: (0, 0))] + pre_specs + [
                  pl.BlockSpec((None, d, n), lambda bi, ti: (layer, 0, 0)),
                  pl.BlockSpec((d, 2 * LANES), lambda bi, ti: (0, 0)),
                  pl.BlockSpec((CONV_WIDTH, 3 * nb), lambda bi, ti: (0, 0)),
                  pl.BlockSpec((None, SUBLANES, 3 * nb), lambda bi, ti: (bi, 0, 0)),
                  pl.BlockSpec((1, LANES), lambda bi, ti: (0, 0)),
                  pl.BlockSpec((1, LANES), lambda bi, ti: (0, 0))],
        out_specs=x_spec + [pl.BlockSpec((None, tm, n), lambda bi, ti: (bi, ti, 0)),
                            pl.BlockSpec((None, tm, LANES), lambda bi, ti: (bi, ti, 0)),
                            pl.BlockSpec((None, SUBLANES, 3 * nb), lambda bi, ti: (bi, 0, 0))],
        out_shape=x_shape + [jax.ShapeDtypeStruct((b, t, n), BF16),
                             jax.ShapeDtypeStruct((b, t, LANES), F32),
                             jax.ShapeDtypeStruct((b, SUBLANES, 3 * nb), F32)],
        scratch_shapes=[pltpu.VMEM((SUBLANES, 3 * nb), F32), pltpu.VMEM((2, SUBLANES + tm, tn), F32)],
        compiler_params=_cparams("parallel", "arbitrary"),
    )(x, mod, nw, *pre_args, w_main, w_ab, conv_w, conv_init, a_log, dt_bias)


def _gdn_kernel(q_ref, k_ref, v_ref, z_ref, gcol_ref, s0_ref, ow_ref, o_ref, s_ref, *, heads, dk, chunk):
    t = pl.program_id(1)
    tb = q_ref.shape[0]
    c = chunk
    group = 2 if (tb // c) % 2 == 0 else 1

    @pl.when(t == 0)
    def _():
        s_ref[...] = s0_ref[...]

    ii = lax.broadcasted_iota(jnp.int32, (c, c), 0)
    jj = lax.broadcasted_iota(jnp.int32, (c, c), 1)
    incl = ii >= jj
    strict = ii > jj
    eye = (ii == jj).astype(F32)
    off_masks = []
    s = 1
    while s < c:
        off_masks.append((ii // (2 * s) == jj // (2 * s)) & (ii // s != jj // s) & strict)
        s *= 2

    def chunk_group(ci, carry):
        rows = [pl.ds(pl.multiple_of((ci * group + k) * c, c), c) for k in range(group)]
        hsl = [slice(hh * dk, (hh + 1) * dk) for hh in range(heads)]
        units = [(k, hh) for k in range(group) for hh in range(heads)]
        gall = [gcol_ref[r, :] for r in rows]
        g = [gall[k][:, hh:hh + 1] for k, hh in units]
        beta = [gall[k][:, heads + hh:heads + hh + 1] for k, hh in units]
        kbf = [k_ref[rows[k], hsl[hh]] for k, hh in units]
        qbf = [q_ref[rows[k], hsl[hh]] for k, hh in units]
        kf = [x.astype(F32) for x in kbf]
        kb = [x * y for x, y in zip(kf, beta)]
        gb = [jnp.broadcast_to(x, (c, c)) for x in g]
        decay = [jnp.exp(jnp.where(incl, x - x.T, -jnp.inf)) for x in gb]
        a = [jnp.where(strict, _dot_nt(x.astype(BF16), y) * d, 0.0) for x, y, d in zip(kb, kbf, decay)]
        qk = [(_dot_nt(x, y) * d).astype(BF16) for x, y, d in zip(qbf, kbf, decay)]
        p = [eye - jnp.where(off_masks[0], x, 0.0) for x in a]
        for off in off_masks[1:]:
            pb = [x.astype(BF16) for x in p]
            x = [_dot(jnp.where(off, ai, 0.0).astype(BF16), pi).astype(BF16) for ai, pi in zip(a, pb)]
            p = [pi - _dot(pbi, xi) for pi, pbi, xi in zip(p, pb, x)]
        eg = [jnp.exp(x) for x in g]
        rhs = [jnp.concatenate([v_ref[rows[k], hsl[hh]].astype(F32) * beta[u], kb[u] * eg[u]], axis=1).astype(BF16)
               for u, (k, hh) in enumerate(units)]
        sol = [_dot(x.astype(BF16), y) for x, y in zip(p, rhs)]
        wq = [jnp.concatenate([sol[u][:, dk:].astype(BF16), (qbf[u].astype(F32) * eg[u]).astype(BF16)], axis=0)
              for u in range(len(units))]
        g_last = [x[c - 1:c, :] for x in g]
        kd_t = [(kf[u] * jnp.exp(g_last[u] - g[u])).T.astype(BF16) for u in range(len(units))]
        e_last = [jnp.exp(x) for x in g_last]
        for k in range(group):
            us = [k * heads + hh for hh in range(heads)]
            s_old = [s_ref[hh] for hh in range(heads)]
            r1 = [_dot(wq[u], s_old[hh].astype(BF16)) for hh, u in enumerate(us)]
            ub = [(sol[u][:, :dk] - r1[hh][:c]).astype(BF16) for hh, u in enumerate(us)]
            for hh, u in enumerate(us):
                s_ref[hh] = s_old[hh] * e_last[u] + _dot(kd_t[u], ub[hh])
            o = [r1[hh][c:] + _dot(qk[u], ub[hh]) for hh, u in enumerate(us)]
            for hh in range(heads):
                on = o[hh] * lax.rsqrt(jnp.mean(o[hh] * o[hh], axis=-1, keepdims=True) + RMS_EPS) * ow_ref[...]
                o_ref[rows[k], hsl[hh]] = (on * z_ref[rows[k], hsl[hh]].astype(F32)).astype(BF16)
        return carry

    lax.fori_loop(0, tb // (c * group), chunk_group, 0)


def _gdn(qkvz, gcol, s0, onorm, *, heads, dk, tb, chunk):
    b, t, n = qkvz.shape
    nb = n // 4
    assert t % tb == 0 and tb % chunk == 0 and nb == heads * dk
    kern = functools.partial(_gdn_kernel, heads=heads, dk=dk, chunk=chunk)
    col = lambda jcol: pl.BlockSpec((None, tb, nb), lambda bi, ti: (bi, ti, jcol))
    return pl.pallas_call(
        kern,
        grid=(b, t // tb),
        in_specs=[col(0), col(1), col(2), col(3),
                  pl.BlockSpec((None, tb, LANES), lambda bi, ti: (bi, ti, 0)),
                  pl.BlockSpec((None, heads, dk, dk), lambda bi, ti: (bi, 0, 0, 0)),
                  pl.BlockSpec((1, dk), lambda bi, ti: (0, 0))],
        out_specs=[pl.BlockSpec((None, tb, nb), lambda bi, ti: (bi, ti, 0)),
                   pl.BlockSpec((None, heads, dk, dk), lambda bi, ti: (bi, 0, 0, 0))],
        out_shape=[jax.ShapeDtypeStruct((b, t, nb), BF16),
                   jax.ShapeDtypeStruct((b, heads, dk, dk), F32)],
        compiler_params=_cparams("parallel", "arbitrary"),
    )(qkvz, qkvz, qkvz, qkvz, gcol, s0, onorm)


def _rope(x, cos, sin_signed, lane_lo):
    half = MASK_CHUNK // 2
    swapped = jnp.where(lane_lo, pltpu.roll(x, LANES - half, axis=1), pltpu.roll(x, half, axis=1))
    return x * cos + swapped * sin_signed


def _diff_in_kernel(*refs, fused, aliased, layers, v_transposed, q_scale, tn):
    x, mod_ref, nw_ref, own_in, outs = _layer_input(refs, fused, 5 if aliased else 3)
    w_ref, cos_ref, sin_ref = own_in[:3]
    qb_ref, kb_ref, kf_hbm, vf_hbm, vb_ref, zs_ref, kn2_ref, kst_sc, vst_sc, sem = outs
    n = w_ref.shape[1]
    nb = n // 4
    tm = x.shape[0]
    tiles = pl.num_programs(1)
    step = pl.program_id(0) * tiles + pl.program_id(1)
    slot = step % 2

    def cache_copies(slot_, step_):
        bi, ti = step_ // tiles, step_ % tiles
        return [pltpu.make_async_copy(stage.at[slot_, :, pl.ds(hh * LANES, LANES)],
                                      cache.at[layer, bi, pl.ds(ti * tm, tm), hh, :],
                                      sem.at[which, slot_])
                for which, (stage, cache) in enumerate(((kst_sc, kf_hbm), (vst_sc, vf_hbm)))
                for layer in layers
                for hh in range(nb // LANES)]

    hb = _prenorm(x, mod_ref, nw_ref).astype(BF16)
    cos = cos_ref[...]
    sin = sin_ref[...]
    lane_lo = (lax.broadcasted_iota(jnp.int32, cos.shape, 1) % MASK_CHUNK) < (MASK_CHUNK // 2)
    kn2 = jnp.zeros((1, LANES), F32)
    for grp in range(n // tn):
        kind = (grp * tn) // nb
        lo = grp * tn - kind * nb
        u = _dot(hb, w_ref[:, grp * tn:(grp + 1) * tn])
        squares = []
        for s in range(tn // LANES):
            us = u[:, s * LANES:(s + 1) * LANES]
            cols = slice(lo + s * LANES, lo + (s + 1) * LANES)
            if kind == 0:
                qb_ref[:, cols] = (_rope(us, cos, sin, lane_lo) * q_scale).astype(BF16)
            elif kind == 1:
                r = _rope(us, cos, sin, lane_lo)
                kst_sc[slot, :, cols] = r
                kb_ref[:, cols] = r.astype(BF16)
                squares.append(r * r)
            elif kind == 2:
                vst_sc[slot, :, cols] = us
            else:
                zs_ref[:, cols] = _silu(us).astype(BF16)
        if kind == 1:
            group_sum = ((lax.broadcasted_iota(jnp.int32, (tn, LANES), 0) + lo) // MASK_CHUNK
                         == lax.broadcasted_iota(jnp.int32, (tn, LANES), 1)).astype(BF16)
            n2 = _dot(jnp.concatenate(squares, axis=1).astype(BF16), group_sum)
            kn2 = jnp.maximum(kn2, jnp.max(n2, axis=0, keepdims=True))
        if kind == 2:
            if v_transposed:
                vb_ref[lo:lo + tn, :] = u.T.astype(BF16)
            else:
                vb_ref[:, lo:lo + tn] = u.astype(BF16)
    kn2_ref[...] = kn2

    for copy in cache_copies(slot, step):
        copy.start()

    @pl.when(step > 0)
    def _():
        for copy in cache_copies(1 - slot, step - 1):
            copy.wait()

    @pl.when(step == pl.num_programs(0) * tiles - 1)
    def _():
        for copy in cache_copies(slot, step):
            copy.wait()


def _diff_in(x, mod, nw, prev, w, cos_t, sin_t, caches, *, layer, heads, tm, v_transposed, q_scale):
    b, t, d = x.shape
    n_layers = w.shape[0]
    nb = w.shape[2] // 4
    tn = 4 * LANES
    assert t % tm == 0 and nb == heads * LANES and nb % tn == 0
    fused = prev is not None
    aliased = caches is not None
    pre_specs, pre_args, x_spec, x_shape = _fused_specs(prev, b, t, d, tm) if fused else ([], [], [], [])
    kern = functools.partial(_diff_in_kernel, fused=fused, aliased=aliased,
                             layers=(layer,) if aliased else tuple(range(n_layers)),
                             v_transposed=v_transposed, q_scale=q_scale, tn=tn)
    tile = pl.BlockSpec((None, tm, nb), lambda bi, ti: (bi, ti, 0))
    in_hbm = pl.BlockSpec(memory_space=pl.ANY)
    cache_shape = jax.ShapeDtypeStruct((n_layers, b, t, heads, LANES), F32)
    if v_transposed:
        vb_spec = pl.BlockSpec((None, nb, tm), lambda bi, ti: (bi, 0, ti))
        vb_shape = jax.ShapeDtypeStruct((b, nb, t), BF16)
    else:
        vb_spec, vb_shape = tile, jax.ShapeDtypeStruct((b, t, nb), BF16)
    n_in = 3 + len(pre_args) + 3
    return pl.pallas_call(
        kern,
        grid=(b, t // tm),
        in_specs=[pl.BlockSpec((None, tm, d), lambda bi, ti: (bi, ti, 0)),
                  pl.BlockSpec((None, 3, d), lambda bi, ti: (bi, 0, 0)),
                  pl.BlockSpec((1, d), lambda bi, ti: (0, 0))] + pre_specs + [
                  pl.BlockSpec((None, d, 4 * nb), lambda bi, ti: (layer, 0, 0)),
                  pl.BlockSpec((tm, LANES), lambda bi, ti: (ti, 0)),
                  pl.BlockSpec((tm, LANES), lambda bi, ti: (ti, 0))] + ([in_hbm, in_hbm] if aliased else []),
        out_specs=x_spec + [tile, tile, in_hbm, in_hbm, vb_spec, tile,
                            pl.BlockSpec((None, None, 1, LANES), lambda bi, ti: (bi, ti, 0, 0))],
        out_shape=x_shape + [jax.ShapeDtypeStruct((b, t, nb), BF16),
                             jax.ShapeDtypeStruct((b, t, nb), BF16),
                             cache_shape,
                             cache_shape,
                             vb_shape,
                             jax.ShapeDtypeStruct((b, t, nb), BF16),
                             jax.ShapeDtypeStruct((b, t // tm, 1, LANES), F32)],
        input_output_aliases={n_in: 2 + len(x_shape), n_in + 1: 3 + len(x_shape)} if aliased else {},
        scratch_shapes=[pltpu.VMEM((2, tm, nb), F32), pltpu.VMEM((2, tm, nb), F32),
                        pltpu.SemaphoreType.DMA((2, 2))],
        compiler_params=_cparams("arbitrary", "arbitrary"),
    )(x, mod, nw, *pre_args, w, cos_t, sin_t, *(caches if aliased else ()))


def _lambda(lam_ref, lam_init):
    l1 = jnp.sum(lam_ref[0:1, :] * lam_ref[1:2, :], axis=-1, keepdims=True)
    l2 = jnp.sum(lam_ref[2:3, :] * lam_ref[3:4, :], axis=-1, keepdims=True)
    return jnp.exp(l1) - jnp.exp(l2) + lam_init


def _attn_prompt_kernel(q_ref, k_ref, vt_ref, z_ref, kn2_ref, lam_ref, sw_ref, o_ref,
                        m_sc, l_sc, acc_sc, qz_sc, *, lam_init, tk):
    qi = pl.program_id(2)
    tq, dh = q_ref.shape
    dq = dh // 2
    nsub = tq // tk
    chains = [(sub, comp) for sub in range(nsub) for comp in range(2)]
    everyone = list(range(len(chains)))
    n_full = (qi * tq) // tk
    qz = []
    for sub, comp in chains:
        q = q_ref[sub * tk:(sub + 1) * tk, :]
        lane = lax.broadcasted_iota(jnp.int32, q.shape, 1)
        qz.append(jnp.where((lane < dq) if comp == 0 else (lane >= dq), q, jnp.zeros_like(q)))

    kn2_max = jnp.max(kn2_ref[...], axis=0)
    kn2_lane = lax.broadcasted_iota(jnp.int32, kn2_max.shape, 1)
    kmax2 = [jnp.max(jnp.where(kn2_lane == 2 * pl.program_id(1) + comp, kn2_max, 0.0), axis=1, keepdims=True)
             for comp in range(2)]

    ones = jnp.ones((SUBLANES, dh), BF16)
    ref_rows = []
    for c, (sub, comp) in enumerate(chains):
        qf = qz[c].astype(F32)
        qn2 = _dot_nt(ones, (qf * qf).astype(BF16))[0:1, :]
        ref_rows.append(REF_SLACK * jnp.sqrt(qn2 * kmax2[comp]))

    visible = (lax.broadcasted_iota(jnp.int32, (tk, tk), 0) // MASK_CHUNK
               <= lax.broadcasted_iota(jnp.int32, (tk, tk), 1) // MASK_CHUNK)

    def blocks(kb):
        ks = k_ref[pl.ds(pl.multiple_of(kb * tk, tk), tk), :]
        vt = vt_ref[:, pl.ds(pl.multiple_of(kb * tk, tk), tk)]
        return ks, vt

    def fast_step(kb, active, masked_sub):
        ks, vt = blocks(kb)
        for g0 in range(0, len(active), STAGE_GROUP):
            grp = active[g0:g0 + STAGE_GROUP]
            s = [_dot_nt(ks, qz[c]) for c in grp]
            p = [jnp.exp2(si - ref_rows[c]) for c, si in zip(grp, s)]
            p = [jnp.where(visible, pi, 0.0) if chains[c][0] == masked_sub else pi for c, pi in zip(grp, p)]
            for c, pi in zip(grp, p):
                l_sc[c] += jnp.sum(pi.reshape(tk // SUBLANES, SUBLANES, tk), axis=0)
            pv = [_dot(vt, pi.astype(BF16)) for pi in p]
            for c, pvi in zip(grp, pv):
                acc_sc[c] += pvi

    def online_chain(c, cry):
        n_blocks = n_full + c // 2 + 1
        q_c = qz_sc[c]

        def block(kb, cry2):
            ks, vt = blocks(kb)
            s = _dot_nt(ks, q_c)
            s = jnp.where(jnp.logical_or(kb < n_blocks - 1, visible), s, -jnp.inf)
            m_old = m_sc[c]
            m_new = jnp.maximum(m_old, jnp.max(s, axis=0, keepdims=True))
            alpha = jnp.exp2(m_old - m_new)
            p = jnp.exp2(s - m_new)
            l_sc[c, 0:1, :] = alpha * l_sc[c, 0:1, :] + jnp.sum(p, axis=0, keepdims=True)
            acc_sc[c] = alpha * acc_sc[c] + _dot(vt, p.astype(BF16))
            m_sc[c] = m_new
            return cry2
        return lax.fori_loop(0, n_blocks, block, cry)

    def clear():
        l_sc[...] = jnp.zeros(l_sc.shape, F32)
        acc_sc[...] = jnp.zeros(acc_sc.shape, F32)

    def normalisers():
        return [jnp.sum(l_sc[c], axis=0, keepdims=True) for c in everyone]

    clear()
    unroll = min(nsub, 4)

    def body(i, cry):
        for u in range(unroll):
            fast_step(i * unroll + u, everyone, None)
        return cry
    lax.fori_loop(0, n_full // unroll, body, 0)
    for d in range(nsub):
        fast_step(n_full + d, [c for c in everyone if chains[c][0] >= d], d)
    l_min = functools.reduce(jnp.minimum, normalisers())

    @pl.when(jnp.min(l_min) < MIN_NORMALISER)
    def _():
        clear()
        m_sc[...] = jnp.full(m_sc.shape, -jnp.inf, F32)
        for c in everyone:
            qz_sc[c] = qz[c]
        lax.fori_loop(0, len(chains), online_chain, 0)

    lam = _lambda(lam_ref, lam_init)
    l = normalisers()
    for sub in range(nsub):
        c1, c2 = 2 * sub, 2 * sub + 1
        o_t = acc_sc[c1] * (1.0 / l[c1]) - acc_sc[c2] * (lam / l[c2])
        o = o_t.T
        gain = sw_ref[...] * (1.0 - lam_init)
        o = o * lax.rsqrt(jnp.mean(o * o, axis=-1, keepdims=True) + RMS_EPS) * gain
        rows = slice(sub * tk, (sub + 1) * tk)
        o_ref[rows, :] = (o * z_ref[rows, :].astype(F32)).astype(BF16)


def _attn_prompt(qb, kb, vt, zs, kn2, lam_vecs, subln, *, heads, lam_init, tq, tk):
    b, t, n = qb.shape
    dh = n // heads
    assert t % tq == 0 and tq % tk == 0 and tk % MASK_CHUNK == 0
    nchains = 2 * (tq // tk)
    kern = functools.partial(_attn_prompt_kernel, lam_init=lam_init, tk=tk)
    return pl.pallas_call(
        kern,
        grid=(b, heads, t // tq),
        in_specs=[pl.BlockSpec((None, tq, dh), lambda bi, hi, qi: (bi, qi, hi)),
                  pl.BlockSpec((None, t, dh), lambda bi, hi, qi: (bi, 0, hi)),
                  pl.BlockSpec((None, dh, t), lambda bi, hi, qi: (bi, hi, 0)),
                  pl.BlockSpec((None, tq, dh), lambda bi, hi, qi: (bi, qi, hi)),
                  pl.BlockSpec((None,) + kn2.shape[1:], lambda bi, hi, qi: (bi, 0, 0, 0)),
                  pl.BlockSpec(lam_vecs.shape, lambda bi, hi, qi: (0, 0)),
                  pl.BlockSpec((1, dh), lambda bi, hi, qi: (0, 0))],
        out_specs=pl.BlockSpec((None, tq, dh), lambda bi, hi, qi: (bi, qi, hi)),
        out_shape=jax.ShapeDtypeStruct((b, t, n), BF16),
        scratch_shapes=[pltpu.VMEM((nchains, 1, tk), F32), pltpu.VMEM((nchains, SUBLANES, tk), F32),
                        pltpu.VMEM((nchains, dh, tk), F32), pltpu.VMEM((nchains, tk, dh), BF16)],
        compiler_params=_cparams("parallel", "parallel", "arbitrary"),
    )(qb, kb, vt, zs, kn2, lam_vecs, subln)


def _attn_sample_kernel(q_ref, kn_ref, vn_ref, ck_ref, cv_ref, z_ref, lam_ref, sw_ref, o_ref,
                        q2_sc, bias_sc, m_sc, l_sc, acc_sc, *, heads, lam_init):
    si = pl.program_id(1)
    ns = pl.num_programs(1)
    tq, n = q_ref.shape
    dh = n // heads
    dq = dh // 2
    rows = heads * 2 * tq

    def head_bias(cols):
        qh = lax.broadcasted_iota(jnp.int32, (rows, cols), 0) // (2 * tq)
        kh = lax.broadcasted_iota(jnp.int32, (rows, cols), 1) % heads
        return jnp.where(qh == kh, 0.0, -jnp.inf)

    @pl.when(si == 0)
    def _():
        m_sc[...] = jnp.full(m_sc.shape, -jnp.inf, F32)
        l_sc[...] = jnp.zeros(l_sc.shape, F32)
        acc_sc[...] = jnp.zeros(acc_sc.shape, F32)
        bias_sc[...] = head_bias(bias_sc.shape[1])
        for hh in range(heads):
            q = q_ref[:, hh * dh:(hh + 1) * dh]
            lane = lax.broadcasted_iota(jnp.int32, q.shape, 1)
            q2_sc[hh * 2 * tq:hh * 2 * tq + tq, :] = jnp.where(lane < dq, q, jnp.zeros_like(q))
            q2_sc[hh * 2 * tq + tq:(hh + 1) * 2 * tq, :] = jnp.where(lane >= dq, q, jnp.zeros_like(q))

    def update(k_rows, v_rows, bias):
        s = _dot_nt(q2_sc[...], k_rows) + bias
        m_old = m_sc[...]
        m_new = jnp.maximum(m_old, jnp.max(s, axis=-1, keepdims=True))
        alpha = jnp.exp2(m_old - m_new)
        p = jnp.exp2(s - m_new)
        l_sc[...] = alpha * l_sc[...] + jnp.sum(p, axis=-1, keepdims=True)
        acc_sc[...] = alpha * acc_sc[...] + _dot(p.astype(BF16), v_rows)
        m_sc[...] = m_new

    tkv = ck_ref.shape[0]
    update(ck_ref[...].reshape(tkv * heads, dh).astype(BF16), cv_ref[...].reshape(tkv * heads, dh).astype(BF16),
           bias_sc[...])

    @pl.when(si == ns - 1)
    def _():
        update(kn_ref[...].reshape(tq * heads, dh).astype(BF16), vn_ref[...].reshape(tq * heads, dh).astype(BF16),
               head_bias(tq * heads))
        lam = _lambda(lam_ref, lam_init)
        on = acc_sc[...] / l_sc[...]
        for hh in range(heads):
            hs = slice(hh * dh, (hh + 1) * dh)
            o = on[hh * 2 * tq:hh * 2 * tq + tq] - lam * on[hh * 2 * tq + tq:(hh + 1) * 2 * tq]
            o = o * lax.rsqrt(jnp.mean(o * o, axis=-1, keepdims=True) + RMS_EPS) * sw_ref[...] * (1.0 - lam_init)
            o_ref[:, hs] = (o * z_ref[:, hs].astype(F32)).astype(BF16)


def _attn_sample(qb, k_new, v_new, cache_k, cache_v, zs, lam_vecs, subln, *, layer, heads, lam_init, tkv):
    b, tq, n = qb.shape
    past = cache_k.shape[2]
    dh = n // heads
    assert past % tkv == 0 and cache_k.shape[3:] == (heads, dh) and k_new.shape[2:] == (tq, heads, dh)
    kern = functools.partial(_attn_sample_kernel, heads=heads, lam_init=lam_init)
    rows = heads * 2 * tq
    tile = pl.BlockSpec((None, tq, n), lambda bi, si: (bi, 0, 0))
    new = pl.BlockSpec((None, None, tq, heads, dh), lambda bi, si: (layer, bi, 0, 0, 0))
    old = pl.BlockSpec((None, None, tkv, heads, dh), lambda bi, si: (layer, bi, si, 0, 0))
    return pl.pallas_call(
        kern,
        grid=(b, past // tkv),
        in_specs=[tile, new, new, old, old, tile,
                  pl.BlockSpec(lam_vecs.shape, lambda bi, si: (0, 0)),
                  pl.BlockSpec((1, dh), lambda bi, si: (0, 0))],
        out_specs=tile,
        out_shape=jax.ShapeDtypeStruct((b, tq, n), BF16),
        scratch_shapes=[pltpu.VMEM((rows, dh), BF16), pltpu.VMEM((rows, tkv * heads), F32),
                        pltpu.VMEM((rows, 1), F32), pltpu.VMEM((rows, 1), F32), pltpu.VMEM((rows, dh), F32)],
        compiler_params=_cparams("parallel", "arbitrary"),
    )(qb, k_new, v_new, cache_k, cache_v, zs, lam_vecs, subln)


def _out_kernel(x_ref, o_ref, w_ref, mod_ref, nw_ref, y_ref):
    y_ref[...] = _residual_update(x_ref[...], o_ref, w_ref, mod_ref, nw_ref)


def _out_proj(x, prev, *, tm):
    b, t, d = x.shape
    pre_specs, pre_args, x_spec, x_shape = _fused_specs(prev, b, t, d, tm)
    return pl.pallas_call(
        _out_kernel,
        grid=(b, t // tm),
        in_specs=[pl.BlockSpec((None, tm, d), lambda bi, ti: (bi, ti, 0))] + pre_specs,
        out_specs=x_spec[0],
        out_shape=x_shape[0],
        compiler_params=_cparams("parallel", "parallel"),
    )(x, *pre_args)


def _pick_tile(t, target):
    tile = min(t, target)
    assert t % tile == 0
    return tile


def _rope_tables(pos):
    half = MASK_CHUNK // 2
    inv = 1.0 / (ROPE_THETA ** (jnp.arange(half, dtype=F32) / half))
    ang = pos.astype(F32)[:, None] * inv[None, :]
    cos, sin = jnp.cos(ang), jnp.sin(ang)
    reps = LANES // MASK_CHUNK
    return (jnp.tile(jnp.concatenate([cos, cos], axis=1), (1, reps)),
            jnp.tile(jnp.concatenate([-sin, sin], axis=1), (1, reps)))


def _trunk(x, ada, pos, conv_bufs, gdn_states, past_k, past_v, p):
    b, t, d = x.shape
    depth = ada.shape[0]
    gdn_heads = p["a_log_gdn"].shape[1]
    gdn_dk = p["onorm_gdn"].shape[1]
    dq = p["lam_q1"].shape[1]
    n_diff = p["w_out_diff"].shape[1]
    diff_heads = n_diff // (2 * dq)
    gdn_chunk = LANES
    n_conv = 3 * gdn_heads * gdn_dk
    tm = _pick_tile(t, 512)
    cos_t, sin_t = _rope_tables(pos)
    prompt = past_k is None
    caches = None
    states, convs = [], []
    prev = None
    for i in range(depth):
        j = i // 2
        mod = ada[i]
        nw_pre = p["norm_pre"][i][None, :]
        nw_post = p["norm_post"][i][None, :]
        if i % 2 == 0:
            a_log = jnp.pad(p["a_log_gdn"][j][None, :], ((0, 0), (0, LANES - gdn_heads)))
            dt_b = jnp.pad(p["dt_bias_gdn"][j][None, :], ((0, 0), (0, LANES - gdn_heads)))
            if conv_bufs is None:
                conv_init = jnp.zeros((b, SUBLANES, n_conv), F32)
                s0 = jnp.zeros((b, gdn_heads, gdn_dk, gdn_dk), F32)
            else:
                conv_init = jnp.pad(conv_bufs[j], ((0, 0), (SUBLANES - (CONV_WIDTH - 1), 0), (0, 0)))
                s0 = gdn_states[j]
            res = _gdn_in(x, mod, nw_pre, prev, p["w_main_gdn"], p["w_ab_gdn"][j], p["conv_gdn"][j],
                          conv_init, a_log, dt_b, layer=j, heads=gdn_heads, dk=gdn_dk, tm=tm,
                          chunk=min(gdn_chunk, tm))
            if prev is not None:
                x, res = res[0], res[1:]
            qkvz, gcol, cout = res
            if t < gdn_chunk:
                front = gdn_chunk - t
                qkvz = jnp.pad(qkvz, ((0, 0), (front, 0), (0, 0)))
                gcol = jnp.pad(gcol, ((0, 0), (front, 0), (0, 0)))
            o, st = _gdn(qkvz, gcol, s0, p["onorm_gdn"][j][None, :], heads=gdn_heads, dk=gdn_dk,
                         tb=_pick_tile(qkvz.shape[1], 256), chunk=gdn_chunk)
            o = o[:, -t:]
            convs.append(cout[:, SUBLANES - (CONV_WIDTH - 1):, :])
            states.append(st)
            w_out = p["w_out_gdn"]
        else:
            lam_init = 0.8 - 0.6 * math.exp(-0.3 * i)
            lam_vecs = jnp.stack([p["lam_q1"][j], p["lam_k1"][j], p["lam_q2"][j], p["lam_k2"][j]])
            res = _diff_in(x, mod, nw_pre, prev, p["w_in_diff"], cos_t, sin_t, caches, layer=j, heads=diff_heads,
                           tm=tm, v_transposed=prompt, q_scale=dq ** -0.5 * LOG2E)
            if prev is not None:
                x, res = res[0], res[1:]
            qb, kb, k_all, v_all, vb, zs, kn2 = res
            caches = (k_all, v_all)
            subln = p["subln_diff"][j][None, :]
            if prompt:
                o = _attn_prompt(qb, kb, vb, zs, kn2, lam_vecs, subln, heads=diff_heads, lam_init=lam_init,
                                 tq=_pick_tile(t, 4096), tk=_pick_tile(t, 256))
            else:
                o = _attn_sample(qb, k_all, v_all, past_k, past_v, zs, lam_vecs, subln, layer=j, heads=diff_heads,
                                 lam_init=lam_init, tkv=_pick_tile(past_k.shape[2], 512))
            w_out = p["w_out_diff"]
        prev = (o, w_out, j, mod, nw_post)
    x = _out_proj(x, prev, tm=tm)
    return x, jnp.stack(states), jnp.stack(convs), k_all, v_all


def kernel(x_prompt, x_sample, c_prompt, c_sample, state_gdn, cache_conv, cache_k, cache_v, norm_pre, norm_post, w_ada, b_ada, w_in_gdn, conv_gdn, a_log_gdn, dt_bias_gdn, onorm_gdn, w_out_gdn, w_in_diff, lam_q1, lam_k1, lam_q2, lam_k2, subln_diff, w_out_diff):
    gdn_heads = a_log_gdn.shape[1]
    n_main = w_in_gdn.shape[2] - 2 * gdn_heads
    w_ab = jnp.pad(w_in_gdn[:, :, n_main:], ((0, 0), (0, 0), (0, LANES - 2 * gdn_heads)))
    w_ab_hi = w_ab.astype(BF16)
    w_ab_lo = (w_ab - w_ab_hi.astype(F32)).astype(BF16)
    p = {"norm_pre": norm_pre, "norm_post": norm_post,
         "w_main_gdn": w_in_gdn[:, :, :n_main].astype(BF16),
         "w_ab_gdn": jnp.concatenate([w_ab_hi, w_ab_lo], axis=-1),

         "conv_gdn": conv_gdn, "a_log_gdn": a_log_gdn,
         "dt_bias_gdn": dt_bias_gdn, "onorm_gdn": onorm_gdn, "w_out_gdn": w_out_gdn.astype(BF16),
         "w_in_diff": w_in_diff.astype(BF16), "lam_q1": lam_q1, "lam_k1": lam_k1, "lam_q2": lam_q2,
         "lam_k2": lam_k2, "subln_diff": subln_diff, "w_out_diff": w_out_diff.astype(BF16)}
    bp, tp, d = x_prompt.shape
    bs, ts, _ = x_sample.shape
    past = cache_k.shape[2]
    depth = w_ada.shape[0]
    ada = _ada(jnp.concatenate([c_prompt, c_sample], axis=0), w_ada, b_ada)
    ada = ada.reshape(depth, bp + bs, 3, d)
    y_p, st_p, conv_p, k_p, v_p = _trunk(x_prompt, ada[:, :bp], jnp.arange(tp), None, None, None, None, p)
    y_s, st_s, conv_s, k_s, v_s = _trunk(x_sample, ada[:, bp:], past + jnp.arange(ts), cache_conv, state_gdn,
                                         cache_k, cache_v, p)
    return (y_p, y_s, st_p, conv_p, k_p, v_p, st_s, conv_s, k_s, v_s)
```

```python
import functools
import math

import jax
import jax.numpy as jnp
from jax import lax
from jax.experimental import pallas as pl
from jax.experimental.pallas import tpu as pltpu

F32 = jnp.float32
BF16 = jnp.bfloat16
HIGHEST = lax.Precision.HIGHEST

RMS_EPS = 1e-6
L2_EPS = 1e-6
ROPE_THETA = 10000.0
CONV_WIDTH = 4
MASK_CHUNK = 64
LANES = 128
SUBLANES = 8
VMEM_LIMIT = 56 * 1024 * 1024
LOG2E = 1.4426950408889634
REF_SLACK = 1.01
MIN_NORMALISER = 1e-30
STAGE_GROUP = 32

NT_DIMS = (((1,), (1,)), ((), ()))


def _cparams(*sem):
    return pltpu.CompilerParams(dimension_semantics=sem, vmem_limit_bytes=VMEM_LIMIT)


def _silu(x):
    return x * jax.nn.sigmoid(x)


def _dot(a, b):
    return jnp.dot(a, b, preferred_element_type=F32)


def _dot_nt(a, b):
    return lax.dot_general(a, b, NT_DIMS, preferred_element_type=F32)


def _ada_kernel(c_ref, w_ref, b_ref, o_ref):
    act = _silu(c_ref[...])
    o_ref[...] = jnp.dot(act, w_ref[...], precision=HIGHEST, preferred_element_type=F32) + b_ref[...]


def _ada(c_all, w_ada, b_ada):
    depth, d, n = w_ada.shape
    bc = c_all.shape[0]
    tn = 512
    return pl.pallas_call(
        _ada_kernel,
        grid=(depth, n // tn),
        in_specs=[pl.BlockSpec((bc, d), lambda i, j: (0, 0)),
                  pl.BlockSpec((None, d, tn), lambda i, j: (i, 0, j)),
                  pl.BlockSpec((None, 1, tn), lambda i, j: (i, 0, j))],
        out_specs=pl.BlockSpec((None, bc, tn), lambda i, j: (i, 0, j)),
        out_shape=jax.ShapeDtypeStruct((depth, bc, n), F32),
        compiler_params=_cparams("parallel", "parallel"),
    )(c_all, w_ada, b_ada.reshape(depth, 1, n))


def _residual_update(x, o_ref, w_ref, mod_ref, nw_ref):
    y = _dot(o_ref[...], w_ref[...])
    yn = y * lax.rsqrt(jnp.mean(y * y, axis=-1, keepdims=True) + RMS_EPS) * nw_ref[...]
    return x + mod_ref[2:3, :] * yn


def _prenorm(x, mod_ref, nw_ref):
    y = x * lax.rsqrt(jnp.mean(x * x, axis=-1, keepdims=True) + RMS_EPS) * nw_ref[...]
    return y * (1.0 + mod_ref[1:2, :]) + mod_ref[0:1, :]


def _layer_input(refs, fused, n_in):
    x_ref, mod_ref, nw_ref = refs[:3]
    pos = 7 if fused else 3
    own_in = refs[pos:pos + n_in]
    pos += n_in
    x = x_ref[...]
    if fused:
        x = _residual_update(x, *refs[3:7])
        refs[pos][...] = x
        pos += 1
    return x, mod_ref, nw_ref, own_in, refs[pos:]


def _fused_specs(prev, b, t, d, tm):
    o, w, layer, mod, nw = prev
    n = o.shape[2]
    in_specs = [pl.BlockSpec((None, tm, n), lambda bi, ti: (bi, ti, 0)),
                pl.BlockSpec((None, n, d), lambda bi, ti: (layer, 0, 0)),
                pl.BlockSpec((None, 3, d), lambda bi, ti: (bi, 0, 0)),
                pl.BlockSpec((1, d), lambda bi, ti: (0, 0))]
    return (in_specs, [o, w, mod, nw], [pl.BlockSpec((None, tm, d), lambda bi, ti: (bi, ti, 0))],
            [jax.ShapeDtypeStruct((b, t, d), F32)])


def _conv_silu(u, tail, cw, stage_ref):
    tm = u.shape[0]
    stage_ref[0:SUBLANES, :] = tail
    stage_ref[SUBLANES:SUBLANES + tm, :] = u
    y = u * cw[CONV_WIDTH - 1:CONV_WIDTH, :]
    for k in range(1, CONV_WIDTH):
        y = y + stage_ref[SUBLANES - k:SUBLANES - k + tm, :] * cw[CONV_WIDTH - 1 - k:CONV_WIDTH - k, :]
    return _silu(y)


def _gdn_in_kernel(*refs, fused, heads, dk, chunk, tn):
    x, mod_ref, nw_ref, own_in, (qkvz_ref, gcol_ref, cout_ref, tail_sc, stage_sc) = _layer_input(refs, fused, 6)
    w_ref, wab_ref, cw_ref, cinit_ref, alog_ref, dtb_ref = own_in
    t = pl.program_id(1)
    tm = x.shape[0]
    n = w_ref.shape[1]
    nb = n // 4

    @pl.when(t == 0)
    def _():
        tail_sc[...] = cinit_ref[...]

    h = _prenorm(x, mod_ref, nw_ref)
    hb = h.astype(BF16)
    h_lo = (h - hb.astype(F32)).astype(BF16)
    hi_lo = _dot(hb, wab_ref[...])
    ab = hi_lo[:, :LANES] + hi_lo[:, LANES:] + _dot(h_lo, wab_ref[:, :LANES])
    g = -jnp.exp(alog_ref[...]) * jax.nn.softplus(ab + dtb_ref[...])
    row = lax.broadcasted_iota(jnp.int32, g.shape, 0) % chunk
    s = 1
    while s < chunk:
        g = g + jnp.where(row >= s, pltpu.roll(g, s, axis=0), 0.0)
        s *= 2
    lane = lax.broadcasted_iota(jnp.int32, g.shape, 1)
    gcol_ref[...] = jnp.where(lane < heads, g, jax.nn.sigmoid(ab))

    for grp in range(n // tn):
        cs = slice(grp * tn, (grp + 1) * tn)
        kind = (grp * tn) // nb
        u = _dot(hb, w_ref[:, cs])
        if kind == 3:
            qkvz_ref[:, cs] = _silu(u).astype(BF16)
            continue
        y = _conv_silu(u, tail_sc[:, cs], cw_ref[:, cs], stage_sc.at[grp % 2])
        tail_sc[:, cs] = u[tm - SUBLANES:tm]
        cout_ref[:, cs] = u[tm - SUBLANES:tm]
        if kind == 2:
            qkvz_ref[:, cs] = y.astype(BF16)
            continue
        scale = dk ** -0.5 if kind == 0 else 1.0
        for hh in range(tn // dk):
            ys = y[:, hh * dk:(hh + 1) * dk]
            inv = lax.rsqrt(jnp.sum(ys * ys, axis=-1, keepdims=True) + L2_EPS) * scale
            qkvz_ref[:, grp * tn + hh * dk:grp * tn + (hh + 1) * dk] = (ys * inv).astype(BF16)


def _gdn_in(x, mod, nw, prev, w_main, w_ab, conv_w, conv_init, a_log, dt_bias, *, layer, heads, dk, tm, chunk):
    b, t, d = x.shape
    n = w_main.shape[2]
    nb = n // 4
    tn = 4 * dk
    assert nb == heads * dk and nb % tn == 0 and t % tm == 0 and tm % SUBLANES == 0 and tm % chunk == 0
    fused = prev is not None
    pre_specs, pre_args, x_spec, x_shape = _fused_specs(prev, b, t, d, tm) if fused else ([], [], [], [])
    kern = functools.partial(_gdn_in_kernel, fused=fused, heads=heads, dk=dk, chunk=chunk, tn=tn)
    return pl.pallas_call(
        kern,
        grid=(b, t // tm),
        in_specs=[pl.BlockSpec((None, tm, d), lambda bi, ti: (bi, ti, 0)),
                  pl.BlockSpec((None, 3, d), lambda bi, ti: (bi, 0, 0)),
                  pl.BlockSpec((1, d), lambda bi, ti: (0, 0))] + pre_specs + [
                  pl.BlockSpec((None, d, n), lambda bi, ti: (layer, 0, 0)),
                  pl.BlockSpec((d, 2 * LANES), lambda bi, ti: (0, 0)),
                  pl.BlockSpec((CONV_WIDTH, 3 * nb), lambda bi, ti: (0, 0)),
                  pl.BlockSpec((None, SUBLANES, 3 * nb), lambda bi, ti: (bi, 0, 0)),
                  pl.BlockSpec((1, LANES), lambda bi, ti: (0, 0)),
                  pl.BlockSpec((1, LANES), lambda bi, ti: (0, 0))],
        out_specs=x_spec + [pl.BlockSpec((None, tm, n), lambda bi, ti: (bi, ti, 0)),
                            pl.BlockSpec((None, tm, LANES), lambda bi, ti: (bi, ti, 0)),
                            pl.BlockSpec((None, SUBLANES, 3 * nb), lambda bi, ti: (bi, 0, 0))],
        out_shape=x_shape + [jax.ShapeDtypeStruct((b, t, n), BF16),
                             jax.ShapeDtypeStruct((b, t, LANES), F32),
                             jax.ShapeDtypeStruct((b, SUBLANES, 3 * nb), F32)],
        scratch_shapes=[pltpu.VMEM((SUBLANES, 3 * nb), F32), pltpu.VMEM((2, SUBLANES + tm, tn), F32)],
        compiler_params=_cparams("parallel", "arbitrary"),
    )(x, mod, nw, *pre_args, w_main, w_ab, conv_w, conv_init, a_log, dt_bias)


def _gdn_kernel(q_ref, k_ref, v_ref, z_ref, gcol_ref, s0_ref, ow_ref, o_ref, s_ref, *, heads, dk, chunk):
    t = pl.program_id(1)
    tb = q_ref.shape[0]
    c = chunk
    group = 2 if (tb // c) % 2 == 0 else 1

    @pl.when(t == 0)
    def _():
        s_ref[...] = s0_ref[...]

    ii = lax.broadcasted_iota(jnp.int32, (c, c), 0)
    jj = lax.broadcasted_iota(jnp.int32, (c, c), 1)
    incl = ii >= jj
    strict = ii > jj
    eye = (ii == jj).astype(F32)
    off_masks = []
    s = 1
    while s < c:
        off_masks.append((ii // (2 * s) == jj // (2 * s)) & (ii // s != jj // s) & strict)
        s *= 2

    def chunk_group(ci, carry):
        rows = [pl.ds(pl.multiple_of((ci * group + k) * c, c), c) for k in range(group)]
        hsl = [slice(hh * dk, (hh + 1) * dk) for hh in range(heads)]
        units = [(k, hh) for k in range(group) for hh in range(heads)]
        gall = [gcol_ref[r, :] for r in rows]
        g = [gall[k][:, hh:hh + 1] for k, hh in units]
        beta = [gall[k][:, heads + hh:heads + hh + 1] for k, hh in units]
        kbf = [k_ref[rows[k], hsl[hh]] for k, hh in units]
        qbf = [q_ref[rows[k], hsl[hh]] for k, hh in units]
        kf = [x.astype(F32) for x in kbf]
        kb = [x * y for x, y in zip(kf, beta)]
        gb = [jnp.broadcast_to(x, (c, c)) for x in g]
        decay = [jnp.exp(jnp.where(incl, x - x.T, -jnp.inf)) for x in gb]
        a = [jnp.where(strict, _dot_nt(x.astype(BF16), y) * d, 0.0) for x, y, d in zip(kb, kbf, decay)]
        qk = [(_dot_nt(x, y) * d).astype(BF16) for x, y, d in zip(qbf, kbf, decay)]
        p = [eye - jnp.where(off_masks[0], x, 0.0) for x in a]
        for off in off_masks[1:]:
            pb = [x.astype(BF16) for x in p]
            x = [_dot(jnp.where(off, ai, 0.0).astype(BF16), pi).astype(BF16) for ai, pi in zip(a, pb)]
            p = [pi - _dot(pbi, xi) for pi, pbi, xi in zip(p, pb, x)]
        eg = [jnp.exp(x) for x in g]
        rhs = [jnp.concatenate([v_ref[rows[k], hsl[hh]].astype(F32) * beta[u], kb[u] * eg[u]], axis=1).astype(BF16)
               for u, (k, hh) in enumerate(units)]
        sol = [_dot(x.astype(BF16), y) for x, y in zip(p, rhs)]
        wq = [jnp.concatenate([sol[u][:, dk:].astype(BF16), (qbf[u].astype(F32) * eg[u]).astype(BF16)], axis=0)
              for u in range(len(units))]
        g_last = [x[c - 1:c, :] for x in g]
        kd_t = [(kf[u] * jnp.exp(g_last[u] - g[u])).T.astype(BF16) for u in range(len(units))]
        e_last = [jnp.exp(x) for x in g_last]
        for k in range(group):
            us = [k * heads + hh for hh in range(heads)]
            s_old = [s_ref[hh] for hh in range(heads)]
            r1 = [_dot(wq[u], s_old[hh].astype(BF16)) for hh, u in enumerate(us)]
            ub = [(sol[u][:, :dk] - r1[hh][:c]).astype(BF16) for hh, u in enumerate(us)]
            for hh, u in enumerate(us):
                s_ref[hh] = s_old[hh] * e_last[u] + _dot(kd_t[u], ub[hh])
            o = [r1[hh][c:] + _dot(qk[u], ub[hh]) for hh, u in enumerate(us)]
            for hh in range(heads):
                on = o[hh] * lax.rsqrt(jnp.mean(o[hh] * o[hh], axis=-1, keepdims=True) + RMS_EPS) * ow_ref[...]
                o_ref[rows[k], hsl[hh]] = (on * z_ref[rows[k], hsl[hh]].astype(F32)).astype(BF16)
        return carry

    lax.fori_loop(0, tb // (c * group), chunk_group, 0)


def _gdn(qkvz, gcol, s0, onorm, *, heads, dk, tb, chunk):
    b, t, n = qkvz.shape
    nb = n // 4
    assert t % tb == 0 and tb % chunk == 0 and nb == heads * dk
    kern = functools.partial(_gdn_kernel, heads=heads, dk=dk, chunk=chunk)
    col = lambda jcol: pl.BlockSpec((None, tb, nb), lambda bi, ti: (bi, ti, jcol))
    return pl.pallas_call(
        kern,
        grid=(b, t // tb),
        in_specs=[col(0), col(1), col(2), col(3),
                  pl.BlockSpec((None, tb, LANES), lambda bi, ti: (bi, ti, 0)),
                  pl.BlockSpec((None, heads, dk, dk), lambda bi, ti: (bi, 0, 0, 0)),
                  pl.BlockSpec((1, dk), lambda bi, ti: (0, 0))],
        out_specs=[pl.BlockSpec((None, tb, nb), lambda bi, ti: (bi, ti, 0)),
                   pl.BlockSpec((None, heads, dk, dk), lambda bi, ti: (bi, 0, 0, 0))],
        out_shape=[jax.ShapeDtypeStruct((b, t, nb), BF16),
                   jax.ShapeDtypeStruct((b, heads, dk, dk), F32)],
        compiler_params=_cparams("parallel", "arbitrary"),
    )(qkvz, qkvz, qkvz, qkvz, gcol, s0, onorm)


def _rope(x, cos, sin_signed, lane_lo):
    half = MASK_CHUNK // 2
    swapped = jnp.where(lane_lo, pltpu.roll(x, LANES - half, axis=1), pltpu.roll(x, half, axis=1))
    return x * cos + swapped * sin_signed


def _diff_in_kernel(*refs, fused, aliased, layers, v_transposed, q_scale, tn):
    x, mod_ref, nw_ref, own_in, outs = _layer_input(refs, fused, 5 if aliased else 3)
    w_ref, cos_ref, sin_ref = own_in[:3]
    qb_ref, kb_ref, kf_hbm, vf_hbm, vb_ref, zs_ref, kn2_ref, kst_sc, vst_sc, sem = outs
    n = w_ref.shape[1]
    nb = n // 4
    tm = x.shape[0]
    tiles = pl.num_programs(1)
    step = pl.program_id(0) * tiles + pl.program_id(1)
    slot = step % 2

    def cache_copies(slot_, step_):
        bi, ti = step_ // tiles, step_ % tiles
        return [pltpu.make_async_copy(stage.at[slot_, :, pl.ds(hh * LANES, LANES)],
                                      cache.at[layer, bi, pl.ds(ti * tm, tm), hh, :],
                                      sem.at[which, slot_])
                for which, (stage, cache) in enumerate(((kst_sc, kf_hbm), (vst_sc, vf_hbm)))
                for layer in layers
                for hh in range(nb // LANES)]

    hb = _prenorm(x, mod_ref, nw_ref).astype(BF16)
    cos = cos_ref[...]
    sin = sin_ref[...]
    lane_lo = (lax.broadcasted_iota(jnp.int32, cos.shape, 1) % MASK_CHUNK) < (MASK_CHUNK // 2)
    kn2 = jnp.zeros((1, LANES), F32)
    for grp in range(n // tn):
        kind = (grp * tn) // nb
        lo = grp * tn - kind * nb
        u = _dot(hb, w_ref[:, grp * tn:(grp + 1) * tn])
        squares = []
        for s in range(tn // LANES):
            us = u[:, s * LANES:(s + 1) * LANES]
            cols = slice(lo + s * LANES, lo + (s + 1) * LANES)
            if kind == 0:
                qb_ref[:, cols] = (_rope(us, cos, sin, lane_lo) * q_scale).astype(BF16)
            elif kind == 1:
                r = _rope(us, cos, sin, lane_lo)
                kst_sc[slot, :, cols] = r
                kb_ref[:, cols] = r.astype(BF16)
                squares.append(r * r)
            elif kind == 2:
                vst_sc[slot, :, cols] = us
            else:
                zs_ref[:, cols] = _silu(us).astype(BF16)
        if kind == 1:
            group_sum = ((lax.broadcasted_iota(jnp.int32, (tn, LANES), 0) + lo) // MASK_CHUNK
                         == lax.broadcasted_iota(jnp.int32, (tn, LANES), 1)).astype(BF16)
            n2 = _dot(jnp.concatenate(squares, axis=1).astype(BF16), group_sum)
            kn2 = jnp.maximum(kn2, jnp.max(n2, axis=0, keepdims=True))
        if kind == 2:
            if v_transposed:
                vb_ref[lo:lo + tn, :] = u.T.astype(BF16)
            else:
                vb_ref[:, lo:lo + tn] = u.astype(BF16)
    kn2_ref[...] = kn2

    for copy in cache_copies(slot, step):
        copy.start()

    @pl.when(step > 0)
    def _():
        for copy in cache_copies(1 - slot, step - 1):
            copy.wait()

    @pl.when(step == pl.num_programs(0) * tiles - 1)
    def _():
        for copy in cache_copies(slot, step):
            copy.wait()


def _diff_in(x, mod, nw, prev, w, cos_t, sin_t, caches, *, layer, heads, tm, v_transposed, q_scale):
    b, t, d = x.shape
    n_layers = w.shape[0]
    nb = w.shape[2] // 4
    tn = 4 * LANES
    assert t % tm == 0 and nb == heads * LANES and nb % tn == 0
    fused = prev is not None
    aliased = caches is not None
    pre_specs, pre_args, x_spec, x_shape = _fused_specs(prev, b, t, d, tm) if fused else ([], [], [], [])
    kern = functools.partial(_diff_in_kernel, fused=fused, aliased=aliased,
                             layers=(layer,) if aliased else tuple(range(n_layers)),
                             v_transposed=v_transposed, q_scale=q_scale, tn=tn)
    tile = pl.BlockSpec((None, tm, nb), lambda bi, ti: (bi, ti, 0))
    in_hbm = pl.BlockSpec(memory_space=pl.ANY)
    cache_shape = jax.ShapeDtypeStruct((n_layers, b, t, heads, LANES), F32)
    if v_transposed:
        vb_spec = pl.BlockSpec((None, nb, tm), lambda bi, ti: (bi, 0, ti))
        vb_shape = jax.ShapeDtypeStruct((b, nb, t), BF16)
    else:
        vb_spec, vb_shape = tile, jax.ShapeDtypeStruct((b, t, nb), BF16)
    n_in = 3 + len(pre_args) + 3
    return pl.pallas_call(
        kern,
        grid=(b, t // tm),
        in_specs=[pl.BlockSpec((None, tm, d), lambda bi, ti: (bi, ti, 0)),
                  pl.BlockSpec((None, 3, d), lambda bi, ti: (bi, 0, 0)),
                  pl.BlockSpec((1, d), lambda bi, ti: (0, 0))] + pre_specs + [
                  pl.BlockSpec((None, d, 4 * nb), lambda bi, ti: (layer, 0, 0)),
                  pl.BlockSpec((tm, LANES), lambda bi, ti: (ti, 0)),
                  pl.BlockSpec((tm, LANES), lambda bi, ti: (ti, 0))] + ([in_hbm, in_hbm] if aliased else []),
        out_specs=x_spec + [tile, tile, in_hbm, in_hbm, vb_spec, tile,
                            pl.BlockSpec((None, None, 1, LANES), lambda bi, ti: (bi, ti, 0, 0))],
        out_shape=x_shape + [jax.ShapeDtypeStruct((b, t, nb), BF16),
                             jax.ShapeDtypeStruct((b, t, nb), BF16),
                             cache_shape,
                             cache_shape,
                             vb_shape,
                             jax.ShapeDtypeStruct((b, t, nb), BF16),
                             jax.ShapeDtypeStruct((b, t // tm, 1, LANES), F32)],
        input_output_aliases={n_in: 2 + len(x_shape), n_in + 1: 3 + len(x_shape)} if aliased else {},
        scratch_shapes=[pltpu.VMEM((2, tm, nb), F32), pltpu.VMEM((2, tm, nb), F32),
                        pltpu.SemaphoreType.DMA((2, 2))],
        compiler_params=_cparams("arbitrary", "arbitrary"),
    )(x, mod, nw, *pre_args, w, cos_t, sin_t, *(caches if aliased else ()))


def _lambda(lam_ref, lam_init):
    l1 = jnp.sum(lam_ref[0:1, :] * lam_ref[1:2, :], axis=-1, keepdims=True)
    l2 = jnp.sum(lam_ref[2:3, :] * lam_ref[3:4, :], axis=-1, keepdims=True)
    return jnp.exp(l1) - jnp.exp(l2) + lam_init


def _attn_prompt_kernel(q_ref, k_ref, vt_ref, z_ref, kn2_ref, lam_ref, sw_ref, o_ref,
                        m_sc, l_sc, acc_sc, qz_sc, *, lam_init, tk):
    qi = pl.program_id(2)
    tq, dh = q_ref.shape
    dq = dh // 2
    nsub = tq // tk
    chains = [(sub, comp) for sub in range(nsub) for comp in range(2)]
    everyone = list(range(len(chains)))
    n_full = (qi * tq) // tk
    qz = []
    for sub, comp in chains:
        q = q_ref[sub * tk:(sub + 1) * tk, :]
        lane = lax.broadcasted_iota(jnp.int32, q.shape, 1)
        qz.append(jnp.where((lane < dq) if comp == 0 else (lane >= dq), q, jnp.zeros_like(q)))

    kn2_max = jnp.max(kn2_ref[...], axis=0)
    kn2_lane = lax.broadcasted_iota(jnp.int32, kn2_max.shape, 1)
    kmax2 = [jnp.max(jnp.where(kn2_lane == 2 * pl.program_id(1) + comp, kn2_max, 0.0), axis=1, keepdims=True)
             for comp in range(2)]

    ones = jnp.ones((SUBLANES, dh), BF16)
    ref_rows = []
    for c, (sub, comp) in enumerate(chains):
        qf = qz[c].astype(F32)
        qn2 = _dot_nt(ones, (qf * qf).astype(BF16))[0:1, :]
        ref_rows.append(REF_SLACK * jnp.sqrt(qn2 * kmax2[comp]))

    visible = (lax.broadcasted_iota(jnp.int32, (tk, tk), 0) // MASK_CHUNK
               <= lax.broadcasted_iota(jnp.int32, (tk, tk), 1) // MASK_CHUNK)

    def blocks(kb):
        ks = k_ref[pl.ds(pl.multiple_of(kb * tk, tk), tk), :]
        vt = vt_ref[:, pl.ds(pl.multiple_of(kb * tk, tk), tk)]
        return ks, vt

    def fast_step(kb, active, masked_sub):
        ks, vt = blocks(kb)
        for g0 in range(0, len(active), STAGE_GROUP):
            grp = active[g0:g0 + STAGE_GROUP]
            s = [_dot_nt(ks, qz[c]) for c in grp]
            p = [jnp.exp2(si - ref_rows[c]) for c, si in zip(grp, s)]
            p = [jnp.where(visible, pi, 0.0) if chains[c][0] == masked_sub else pi for c, pi in zip(grp, p)]
            for c, pi in zip(grp, p):
                l_sc[c] += jnp.sum(pi.reshape(tk // SUBLANES, SUBLANES, tk), axis=0)
            pv = [_dot(vt, pi.astype(BF16)) for pi in p]
            for c, pvi in zip(grp, pv):
                acc_sc[c] += pvi

    def online_chain(c, cry):
        n_blocks = n_full + c // 2 + 1
        q_c = qz_sc[c]

        def block(kb, cry2):
            ks, vt = blocks(kb)
            s = _dot_nt(ks, q_c)
            s = jnp.where(jnp.logical_or(kb < n_blocks - 1, visible), s, -jnp.inf)
            m_old = m_sc[c]
            m_new = jnp.maximum(m_old, jnp.max(s, axis=0, keepdims=True))
            alpha = jnp.exp2(m_old - m_new)
            p = jnp.exp2(s - m_new)
            l_sc[c, 0:1, :] = alpha * l_sc[c, 0:1, :] + jnp.sum(p, axis=0, keepdims=True)
            acc_sc[c] = alpha * acc_sc[c] + _dot(vt, p.astype(BF16))
            m_sc[c] = m_new
            return cry2
        return lax.fori_loop(0, n_blocks, block, cry)

    def clear():
        l_sc[...] = jnp.zeros(l_sc.shape, F32)
        acc_sc[...] = jnp.zeros(acc_sc.shape, F32)

    def normalisers():
        return [jnp.sum(l_sc[c], axis=0, keepdims=True) for c in everyone]

    clear()
    unroll = min(nsub, 4)

    def body(i, cry):
        for u in range(unroll):
            fast_step(i * unroll + u, everyone, None)
        return cry
    lax.fori_loop(0, n_full // unroll, body, 0)
    for d in range(nsub):
        fast_step(n_full + d, [c for c in everyone if chains[c][0] >= d], d)
    l_min = functools.reduce(jnp.minimum, normalisers())

    @pl.when(jnp.min(l_min) < MIN_NORMALISER)
    def _():
        clear()
        m_sc[...] = jnp.full(m_sc.shape, -jnp.inf, F32)
        for c in everyone:
            qz_sc[c] = qz[c]
        lax.fori_loop(0, len(chains), online_chain, 0)

    lam = _lambda(lam_ref, lam_init)
    l = normalisers()
    for sub in range(nsub):
        c1, c2 = 2 * sub, 2 * sub + 1
        o_t = acc_sc[c1] * (1.0 / l[c1]) - acc_sc[c2] * (lam / l[c2])
        o = o_t.T
        gain = sw_ref[...] * (1.0 - lam_init)
        o = o * lax.rsqrt(jnp.mean(o * o, axis=-1, keepdims=True) + RMS_EPS) * gain
        rows = slice(sub * tk, (sub + 1) * tk)
        o_ref[rows, :] = (o * z_ref[rows, :].astype(F32)).astype(BF16)


def _attn_prompt(qb, kb, vt, zs, kn2, lam_vecs, subln, *, heads, lam_init, tq, tk):
    b, t, n = qb.shape
    dh = n // heads
    assert t % tq == 0 and tq % tk == 0 and tk % MASK_CHUNK == 0
    nchains = 2 * (tq // tk)
    kern = functools.partial(_attn_prompt_kernel, lam_init=lam_init, tk=tk)
    return pl.pallas_call(
        kern,
        grid=(b, heads, t // tq),
        in_specs=[pl.BlockSpec((None, tq, dh), lambda bi, hi, qi: (bi, qi, hi)),
                  pl.BlockSpec((None, t, dh), lambda bi, hi, qi: (bi, 0, hi)),
                  pl.BlockSpec((None, dh, t), lambda bi, hi, qi: (bi, hi, 0)),
                  pl.BlockSpec((None, tq, dh), lambda bi, hi, qi: (bi, qi, hi)),
                  pl.BlockSpec((None,) + kn2.shape[1:], lambda bi, hi, qi: (bi, 0, 0, 0)),
                  pl.BlockSpec(lam_vecs.shape, lambda bi, hi, qi: (0, 0)),
                  pl.BlockSpec((1, dh), lambda bi, hi, qi: (0, 0))],
        out_specs=pl.BlockSpec((None, tq, dh), lambda bi, hi, qi: (bi, qi, hi)),
        out_shape=jax.ShapeDtypeStruct((b, t, n), BF16),
        scratch_shapes=[pltpu.VMEM((nchains, 1, tk), F32), pltpu.VMEM((nchains, SUBLANES, tk), F32),
                        pltpu.VMEM((nchains, dh, tk), F32), pltpu.VMEM((nchains, tk, dh), BF16)],
        compiler_params=_cparams("parallel", "parallel", "arbitrary"),
    )(qb, kb, vt, zs, kn2, lam_vecs, subln)


def _attn_sample_kernel(q_ref, kn_ref, vn_ref, ck_hbm, cv_hbm, z_ref, lam_ref, sw_ref, o_ref,
                        q2_sc, kst_sc, vst_sc, m_sc, l_sc, acc_sc, sem, *, layer, heads, lam_init, tkv):
    si = pl.program_id(1)
    ns = pl.num_programs(1)
    step = pl.program_id(0) * ns + si
    slot = step % 2
    tq, n = q_ref.shape
    dh = n // heads
    dq = dh // 2

    def fetch(step_, slot_):
        bi, blk = step_ // ns, step_ % ns
        return [pltpu.make_async_copy(cache.at[layer, bi, pl.ds(blk * tkv, tkv), hh, :], stage.at[slot_, hh],
                                      sem.at[which, slot_])
                for which, (cache, stage) in enumerate(((ck_hbm, kst_sc), (cv_hbm, vst_sc)))
                for hh in range(heads)]

    @pl.when(step == 0)
    def _():
        for copy in fetch(step, slot):
            copy.start()

    @pl.when(step + 1 < pl.num_programs(0) * ns)
    def _():
        for copy in fetch(step + 1, 1 - slot):
            copy.start()

    @pl.when(si == 0)
    def _():
        m_sc[...] = jnp.full(m_sc.shape, -jnp.inf, F32)
        l_sc[...] = jnp.zeros(l_sc.shape, F32)
        acc_sc[...] = jnp.zeros(acc_sc.shape, F32)
        for hh in range(heads):
            q = q_ref[:, hh * dh:(hh + 1) * dh]
            lane = lax.broadcasted_iota(jnp.int32, q.shape, 1)
            q2_sc[hh, 0:tq, :] = jnp.where(lane < dq, q, jnp.zeros_like(q))
            q2_sc[hh, tq:2 * tq, :] = jnp.where(lane >= dq, q, jnp.zeros_like(q))

    def update(kblks, vblks):
        hr = range(heads)
        s = [_dot_nt(q2_sc[hh], kblks[hh]) for hh in hr]
        m_old = [m_sc[hh] for hh in hr]
        m_new = [jnp.maximum(m_old[hh], jnp.max(s[hh], axis=-1, keepdims=True)) for hh in hr]
        alpha = [jnp.exp2(m_old[hh] - m_new[hh]) for hh in hr]
        p = [jnp.exp2(s[hh] - m_new[hh]) for hh in hr]
        pv = [_dot(p[hh].astype(BF16), vblks[hh]) for hh in hr]
        for hh in hr:
            l_sc[hh] = alpha[hh] * l_sc[hh] + jnp.sum(p[hh], axis=-1, keepdims=True)
            acc_sc[hh] = alpha[hh] * acc_sc[hh] + pv[hh]
            m_sc[hh] = m_new[hh]

    for copy in fetch(step, slot):
        copy.wait()
    update([kst_sc[slot, hh].astype(BF16) for hh in range(heads)],
           [vst_sc[slot, hh].astype(BF16) for hh in range(heads)])

    @pl.when(si == ns - 1)
    def _():
        lam = _lambda(lam_ref, lam_init)
        update([kn_ref[:, hh, :].astype(BF16) for hh in range(heads)],
               [vn_ref[:, hh, :].astype(BF16) for hh in range(heads)])
        for hh in range(heads):
            hs = slice(hh * dh, (hh + 1) * dh)
            on = acc_sc[hh] / l_sc[hh]
            o = on[:tq] - lam * on[tq:]
            o = o * lax.rsqrt(jnp.mean(o * o, axis=-1, keepdims=True) + RMS_EPS) * sw_ref[...] * (1.0 - lam_init)
            o_ref[:, hs] = (o * z_ref[:, hs].astype(F32)).astype(BF16)


def _attn_sample(qb, k_new, v_new, cache_k, cache_v, zs, lam_vecs, subln, *, layer, heads, lam_init, tkv):
    b, tq, n = qb.shape
    past = cache_k.shape[2]
    dh = n // heads
    assert past % tkv == 0 and cache_k.shape[3:] == (heads, dh) and k_new.shape[2:] == (tq, heads, dh)
    kern = functools.partial(_attn_sample_kernel, layer=layer, heads=heads, lam_init=lam_init, tkv=tkv)
    tile = pl.BlockSpec((None, tq, n), lambda bi, si: (bi, 0, 0))
    new = pl.BlockSpec((None, None, tq, heads, dh), lambda bi, si: (layer, bi, 0, 0, 0))
    in_hbm = pl.BlockSpec(memory_space=pl.ANY)
    return pl.pallas_call(
        kern,
        grid=(b, past // tkv),
        in_specs=[tile, new, new, in_hbm, in_hbm, tile,
                  pl.BlockSpec(lam_vecs.shape, lambda bi, si: (0, 0)),
                  pl.BlockSpec((1, dh), lambda bi, si: (0, 0))],
        out_specs=tile,
        out_shape=jax.ShapeDtypeStruct((b, tq, n), BF16),
        scratch_shapes=[pltpu.VMEM((heads, 2 * tq, dh), BF16),
                        pltpu.VMEM((2, heads, tkv, dh), F32), pltpu.VMEM((2, heads, tkv, dh), F32),
                        pltpu.VMEM((heads, 2 * tq, 1), F32), pltpu.VMEM((heads, 2 * tq, 1), F32),
                        pltpu.VMEM((heads, 2 * tq, dh), F32), pltpu.SemaphoreType.DMA((2, 2))],
        compiler_params=_cparams("arbitrary", "arbitrary"),
    )(qb, k_new, v_new, cache_k, cache_v, zs, lam_vecs, subln)


def _out_kernel(x_ref, o_ref, w_ref, mod_ref, nw_ref, y_ref):
    y_ref[...] = _residual_update(x_ref[...], o_ref, w_ref, mod_ref, nw_ref)


def _out_proj(x, prev, *, tm):
    b, t, d = x.shape
    pre_specs, pre_args, x_spec, x_shape = _fused_specs(prev, b, t, d, tm)
    return pl.pallas_call(
        _out_kernel,
        grid=(b, t // tm),
        in_specs=[pl.BlockSpec((None, tm, d), lambda bi, ti: (bi, ti, 0))] + pre_specs,
        out_specs=x_spec[0],
        out_shape=x_shape[0],
        compiler_params=_cparams("parallel", "parallel"),
    )(x, *pre_args)


def _pick_tile(t, target):
    tile = min(t, target)
    assert t % tile == 0
    return tile


def _rope_tables(pos):
    half = MASK_CHUNK // 2
    inv = 1.0 / (ROPE_THETA ** (jnp.arange(half, dtype=F32) / half))
    ang = pos.astype(F32)[:, None] * inv[None, :]
    cos, sin = jnp.cos(ang), jnp.sin(ang)
    reps = LANES // MASK_CHUNK
    return (jnp.tile(jnp.concatenate([cos, cos], axis=1), (1, reps)),
            jnp.tile(jnp.concatenate([-sin, sin], axis=1), (1, reps)))


def _trunk(x, ada, pos, conv_bufs, gdn_states, past_k, past_v, p):
    b, t, d = x.shape
    depth = ada.shape[0]
    gdn_heads = p["a_log_gdn"].shape[1]
    gdn_dk = p["onorm_gdn"].shape[1]
    dq = p["lam_q1"].shape[1]
    n_diff = p["w_out_diff"].shape[1]
    diff_heads = n_diff // (2 * dq)
    gdn_chunk = LANES
    n_conv = 3 * gdn_heads * gdn_dk
    tm = _pick_tile(t, 512)
    cos_t, sin_t = _rope_tables(pos)
    prompt = past_k is None
    caches = None
    states, convs = [], []
    prev = None
    for i in range(depth):
        j = i // 2
        mod = ada[i]
        nw_pre = p["norm_pre"][i][None, :]
        nw_post = p["norm_post"][i][None, :]
        if i % 2 == 0:
            a_log = jnp.pad(p["a_log_gdn"][j][None, :], ((0, 0), (0, LANES - gdn_heads)))
            dt_b = jnp.pad(p["dt_bias_gdn"][j][None, :], ((0, 0), (0, LANES - gdn_heads)))
            if conv_bufs is None:
                conv_init = jnp.zeros((b, SUBLANES, n_conv), F32)
                s0 = jnp.zeros((b, gdn_heads, gdn_dk, gdn_dk), F32)
            else:
                conv_init = jnp.pad(conv_bufs[j], ((0, 0), (SUBLANES - (CONV_WIDTH - 1), 0), (0, 0)))
                s0 = gdn_states[j]
            res = _gdn_in(x, mod, nw_pre, prev, p["w_main_gdn"], p["w_ab_gdn"][j], p["conv_gdn"][j],
                          conv_init, a_log, dt_b, layer=j, heads=gdn_heads, dk=gdn_dk, tm=tm,
                          chunk=min(gdn_chunk, tm))
            if prev is not None:
                x, res = res[0], res[1:]
            qkvz, gcol, cout = res
            if t < gdn_chunk:
                front = gdn_chunk - t
                qkvz = jnp.pad(qkvz, ((0, 0), (front, 0), (0, 0)))
                gcol = jnp.pad(gcol, ((0, 0), (front, 0), (0, 0)))
            o, st = _gdn(qkvz, gcol, s0, p["onorm_gdn"][j][None, :], heads=gdn_heads, dk=gdn_dk,
                         tb=_pick_tile(qkvz.shape[1], 256), chunk=gdn_chunk)
            o = o[:, -t:]
            convs.append(cout[:, SUBLANES - (CONV_WIDTH - 1):, :])
            states.append(st)
            w_out = p["w_out_gdn"]
        else:
            lam_init = 0.8 - 0.6 * math.exp(-0.3 * i)
            lam_vecs = jnp.stack([p["lam_q1"][j], p["lam_k1"][j], p["lam_q2"][j], p["lam_k2"][j]])
            res = _diff_in(x, mod, nw_pre, prev, p["w_in_diff"], cos_t, sin_t, caches, layer=j, heads=diff_heads,
                           tm=tm, v_transposed=prompt, q_scale=dq ** -0.5 * LOG2E)
            if prev is not None:
                x, res = res[0], res[1:]
            qb, kb, k_all, v_all, vb, zs, kn2 = res
            caches = (k_all, v_all)
            subln = p["subln_diff"][j][None, :]
            if prompt:
                o = _attn_prompt(qb, kb, vb, zs, kn2, lam_vecs, subln, heads=diff_heads, lam_init=lam_init,
                                 tq=_pick_tile(t, 4096), tk=_pick_tile(t, 256))
            else:
                o = _attn_sample(qb, k_all, v_all, past_k, past_v, zs, lam_vecs, subln, layer=j, heads=diff_heads,
                                 lam_init=lam_init, tkv=_pick_tile(past_k.shape[2], 512))
            w_out = p["w_out_diff"]
        prev = (o, w_out, j, mod, nw_post)
    x = _out_proj(x, prev, tm=tm)
    return x, jnp.stack(states), jnp.stack(convs), k_all, v_all


def kernel(x_prompt, x_sample, c_prompt, c_sample, state_gdn, cache_conv, cache_k, cache_v, norm_pre, norm_post, w_ada, b_ada, w_in_gdn, conv_gdn, a_log_gdn, dt_bias_gdn, onorm_gdn, w_out_gdn, w_in_diff, lam_q1, lam_k1, lam_q2, lam_k2, subln_diff, w_out_diff):
    gdn_heads = a_log_gdn.shape[1]
    n_main = w_in_gdn.shape[2] - 2 * gdn_heads
    w_ab = jnp.pad(w_in_gdn[:, :, n_main:], ((0, 0), (0, 0), (0, LANES - 2 * gdn_heads)))
    w_ab_hi = w_ab.astype(BF16)
    w_ab_lo = (w_ab - w_ab_hi.astype(F32)).astype(BF16)
    p = {"norm_pre": norm_pre, "norm_post": norm_post,
         "w_main_gdn": w_in_gdn[:, :, :n_main].astype(BF16),
         "w_ab_gdn": jnp.concatenate([w_ab_hi, w_ab_lo], axis=-1),

         "conv_gdn": conv_gdn, "a_log_gdn": a_log_gdn,
         "dt_bias_gdn": dt_bias_gdn, "onorm_gdn": onorm_gdn, "w_out_gdn": w_out_gdn.astype(BF16),
         "w_in_diff": w_in_diff.astype(BF16), "lam_q1": lam_q1, "lam_k1": lam_k1, "lam_q2": lam_q2,
         "lam_k2": lam_k2, "subln_diff": subln_diff, "w_out_diff": w_out_diff.astype(BF16)}
    bp, tp, d = x_prompt.shape
    bs, ts, _ = x_sample.shape
    past = cache_k.shape[2]
    depth = w_ada.shape[0]
    ada = _ada(jnp.concatenate([c_prompt, c_sample], axis=0), w_ada, b_ada)
    ada = ada.reshape(depth, bp + bs, 3, d)
    y_p, st_p, conv_p, k_p, v_p = _trunk(x_prompt, ada[:, :bp], jnp.arange(tp), None, None, None, None, p)
    y_s, st_s, conv_s, k_s, v_s = _trunk(x_sample, ada[:, bp:], past + jnp.arange(ts), cache_conv, state_gdn,
                                         cache_k, cache_v, p)
    return (y_p, y_s, st_p, conv_p, k_p, v_p, st_s, conv_s, k_s, v_s)
```

```python
import functools
import math

import jax
import jax.numpy as jnp
from jax import lax
from jax.experimental import pallas as pl
from jax.experimental.pallas import tpu as pltpu

F32 = jnp.float32
BF16 = jnp.bfloat16
HIGHEST = lax.Precision.HIGHEST

RMS_EPS = 1e-6
L2_EPS = 1e-6
ROPE_THETA = 10000.0
CONV_WIDTH = 4
MASK_CHUNK = 64
LANES = 128
SUBLANES = 8
VMEM_LIMIT = 56 * 1024 * 1024
LOG2E = 1.4426950408889634
REF_SLACK = 1.01
MIN_NORMALISER = 1e-30
STAGE_GROUP = 32

NT_DIMS = (((1,), (1,)), ((), ()))


def _cparams(*sem):
    return pltpu.CompilerParams(dimension_semantics=sem, vmem_limit_bytes=VMEM_LIMIT)


def _silu(x):
    return x * jax.nn.sigmoid(x)


def _dot(a, b):
    return jnp.dot(a, b, preferred_element_type=F32)


def _dot_nt(a, b):
    return lax.dot_general(a, b, NT_DIMS, preferred_element_type=F32)


def _ada_kernel(c_ref, w_ref, b_ref, o_ref):
    act = _silu(c_ref[...])
    o_ref[...] = jnp.dot(act, w_ref[...], precision=HIGHEST, preferred_element_type=F32) + b_ref[...]


def _ada(c_all, w_ada, b_ada):
    depth, d, n = w_ada.shape
    bc = c_all.shape[0]
    tn = 512
    return pl.pallas_call(
        _ada_kernel,
        grid=(depth, n // tn),
        in_specs=[pl.BlockSpec((bc, d), lambda i, j: (0, 0)),
                  pl.BlockSpec((None, d, tn), lambda i, j: (i, 0, j)),
                  pl.BlockSpec((None, 1, tn), lambda i, j: (i, 0, j))],
        out_specs=pl.BlockSpec((None, bc, tn), lambda i, j: (i, 0, j)),
        out_shape=jax.ShapeDtypeStruct((depth, bc, n), F32),
        compiler_params=_cparams("parallel", "parallel"),
    )(c_all, w_ada, b_ada.reshape(depth, 1, n))


def _residual_update(x, o_ref, w_ref, mod_ref, nw_ref):
    y = _dot(o_ref[...], w_ref[...])
    yn = y * lax.rsqrt(jnp.mean(y * y, axis=-1, keepdims=True) + RMS_EPS) * nw_ref[...]
    return x + mod_ref[2:3, :] * yn


def _prenorm(x, mod_ref, nw_ref):
    y = x * lax.rsqrt(jnp.mean(x * x, axis=-1, keepdims=True) + RMS_EPS) * nw_ref[...]
    return y * (1.0 + mod_ref[1:2, :]) + mod_ref[0:1, :]


def _layer_input(refs, fused, n_in):
    x_ref, mod_ref, nw_ref = refs[:3]
    pos = 7 if fused else 3
    own_in = refs[pos:pos + n_in]
    pos += n_in
    x = x_ref[...]
    if fused:
        x = _residual_update(x, *refs[3:7])
        refs[pos][...] = x
        pos += 1
    return x, mod_ref, nw_ref, own_in, refs[pos:]


def _fused_specs(prev, b, t, d, tm):
    o, w, layer, mod, nw = prev
    n = o.shape[2]
    in_specs = [pl.BlockSpec((None, tm, n), lambda bi, ti: (bi, ti, 0)),
                pl.BlockSpec((None, n, d), lambda bi, ti: (layer, 0, 0)),
                pl.BlockSpec((None, 3, d), lambda bi, ti: (bi, 0, 0)),
                pl.BlockSpec((1, d), lambda bi, ti: (0, 0))]
    return (in_specs, [o, w, mod, nw], [pl.BlockSpec((None, tm, d), lambda bi, ti: (bi, ti, 0))],
            [jax.ShapeDtypeStruct((b, t, d), F32)])


def _conv_silu(u, tail, cw, stage_ref):
    tm = u.shape[0]
    stage_ref[0:SUBLANES, :] = tail
    stage_ref[SUBLANES:SUBLANES + tm, :] = u
    y = u * cw[CONV_WIDTH - 1:CONV_WIDTH, :]
    for k in range(1, CONV_WIDTH):
        y = y + stage_ref[SUBLANES - k:SUBLANES - k + tm, :] * cw[CONV_WIDTH - 1 - k:CONV_WIDTH - k, :]
    return _silu(y)


def _gdn_in_kernel(*refs, fused, heads, dk, chunk, tn):
    x, mod_ref, nw_ref, own_in, (qkvz_ref, gcol_ref, cout_ref, tail_sc, stage_sc) = _layer_input(refs, fused, 6)
    w_ref, wab_ref, cw_ref, cinit_ref, alog_ref, dtb_ref = own_in
    t = pl.program_id(1)
    tm = x.shape[0]
    n = w_ref.shape[1]
    nb = n // 4

    @pl.when(t == 0)
    def _():
        tail_sc[...] = cinit_ref[...]

    h = _prenorm(x, mod_ref, nw_ref)
    hb = h.astype(BF16)
    h_lo = (h - hb.astype(F32)).astype(BF16)
    hi_lo = _dot(hb, wab_ref[...])
    ab = hi_lo[:, :LANES] + hi_lo[:, LANES:] + _dot(h_lo, wab_ref[:, :LANES])
    g = -jnp.exp(alog_ref[...]) * jax.nn.softplus(ab + dtb_ref[...])
    row = lax.broadcasted_iota(jnp.int32, g.shape, 0) % chunk
    s = 1
    while s < chunk:
        g = g + jnp.where(row >= s, pltpu.roll(g, s, axis=0), 0.0)
        s *= 2
    lane = lax.broadcasted_iota(jnp.int32, g.shape, 1)
    gcol_ref[...] = jnp.where(lane < heads, g, jax.nn.sigmoid(ab))

    for grp in range(n // tn):
        cs = slice(grp * tn, (grp + 1) * tn)
        kind = (grp * tn) // nb
        u = _dot(hb, w_ref[:, cs])
        if kind == 3:
            qkvz_ref[:, cs] = _silu(u).astype(BF16)
            continue
        y = _conv_silu(u, tail_sc[:, cs], cw_ref[:, cs], stage_sc.at[grp % 2])
        tail_sc[:, cs] = u[tm - SUBLANES:tm]
        cout_ref[:, cs] = u[tm - SUBLANES:tm]
        if kind == 2:
            qkvz_ref[:, cs] = y.astype(BF16)
            continue
        scale = dk ** -0.5 if kind == 0 else 1.0
        for hh in range(tn // dk):
            ys = y[:, hh * dk:(hh + 1) * dk]
            inv = lax.rsqrt(jnp.sum(ys * ys, axis=-1, keepdims=True) + L2_EPS) * scale
            qkvz_ref[:, grp * tn + hh * dk:grp * tn + (hh + 1) * dk] = (ys * inv).astype(BF16)


def _gdn_in(x, mod, nw, prev, w_main, w_ab, conv_w, conv_init, a_log, dt_bias, *, layer, heads, dk, tm, chunk):
    b, t, d = x.shape
    n = w_main.shape[2]
    nb = n // 4
    tn = 4 * dk
    assert nb == heads * dk and nb % tn == 0 and t % tm == 0 and tm % SUBLANES == 0 and tm % chunk == 0
    fused = prev is not None
    pre_specs, pre_args, x_spec, x_shape = _fused_specs(prev, b, t, d, tm) if fused else ([], [], [], [])
    kern = functools.partial(_gdn_in_kernel, fused=fused, heads=heads, dk=dk, chunk=chunk, tn=tn)
    return pl.pallas_call(
        kern,
        grid=(b, t // tm),
        in_specs=[pl.BlockSpec((None, tm, d), lambda bi, ti: (bi, ti, 0)),
                  pl.BlockSpec((None, 3, d), lambda bi, ti: (bi, 0, 0)),
                  pl.BlockSpec((1, d), lambda bi, ti: (0, 0))] + pre_specs + [
                  pl.BlockSpec((None, d, n), lambda bi, ti: (layer, 0, 0)),
                  pl.BlockSpec((d, 2 * LANES), lambda bi, ti: (0, 0)),
                  pl.BlockSpec((CONV_WIDTH, 3 * nb), lambda bi, ti: (0, 0)),
                  pl.BlockSpec((None, SUBLANES, 3 * nb), lambda bi, ti: (bi, 0, 0)),
                  pl.BlockSpec((1, LANES), lambda bi, ti: (0, 0)),
                  pl.BlockSpec((1, LANES), lambda bi, ti: (0, 0))],
        out_specs=x_spec + [pl.BlockSpec((None, tm, n), lambda bi, ti: (bi, ti, 0)),
                            pl.BlockSpec((None, tm, LANES), lambda bi, ti: (bi, ti, 0)),
                            pl.BlockSpec((None, SUBLANES, 3 * nb), lambda bi, ti: (bi, 0, 0))],
        out_shape=x_shape + [jax.ShapeDtypeStruct((b, t, n), BF16),
                             jax.ShapeDtypeStruct((b, t, LANES), F32),
                             jax.ShapeDtypeStruct((b, SUBLANES, 3 * nb), F32)],
        scratch_shapes=[pltpu.VMEM((SUBLANES, 3 * nb), F32), pltpu.VMEM((2, SUBLANES + tm, tn), F32)],
        compiler_params=_cparams("parallel", "arbitrary"),
    )(x, mod, nw, *pre_args, w_main, w_ab, conv_w, conv_init, a_log, dt_bias)


def _gdn_kernel(q_ref, k_ref, v_ref, z_ref, gcol_ref, s0_ref, ow_ref, o_ref, s_ref, *, heads, dk, chunk, group):
    t = pl.program_id(1)
    nbatch, tb = q_ref.shape[0], q_ref.shape[1]
    c = chunk

    @pl.when(t == 0)
    def _():
        s_ref[...] = s0_ref[...]

    ii = lax.broadcasted_iota(jnp.int32, (c, c), 0)
    jj = lax.broadcasted_iota(jnp.int32, (c, c), 1)
    incl = ii >= jj
    strict = ii > jj
    eye = (ii == jj).astype(F32)
    off_masks = []
    s = 1
    while s < c:
        off_masks.append((ii // (2 * s) == jj // (2 * s)) & (ii // s != jj // s) & strict)
        s *= 2

    def chunk_group(ci, carry):
        rows = [pl.ds(pl.multiple_of((ci * group + k) * c, c), c) for k in range(group)]
        hsl = [slice(hh * dk, (hh + 1) * dk) for hh in range(heads)]
        units = [(bi, k, hh) for bi in range(nbatch) for k in range(group) for hh in range(heads)]
        nu = range(len(units))
        gall = {(bi, k): gcol_ref[bi, rows[k], :] for bi in range(nbatch) for k in range(group)}
        g = [gall[bi, k][:, hh:hh + 1] for bi, k, hh in units]
        beta = [gall[bi, k][:, heads + hh:heads + hh + 1] for bi, k, hh in units]
        kbf = [k_ref[bi, rows[k], hsl[hh]] for bi, k, hh in units]
        qbf = [q_ref[bi, rows[k], hsl[hh]] for bi, k, hh in units]
        kf = [x.astype(F32) for x in kbf]
        kb = [x * y for x, y in zip(kf, beta)]
        gb = [jnp.broadcast_to(x, (c, c)) for x in g]
        decay = [jnp.exp(jnp.where(incl, x - x.T, -jnp.inf)) for x in gb]
        a = [jnp.where(strict, _dot_nt(x.astype(BF16), y) * d, 0.0) for x, y, d in zip(kb, kbf, decay)]
        qk = [(_dot_nt(x, y) * d).astype(BF16) for x, y, d in zip(qbf, kbf, decay)]
        p = [eye - jnp.where(off_masks[0], x, 0.0) for x in a]
        for off in off_masks[1:]:
            pb = [x.astype(BF16) for x in p]
            x = [_dot(jnp.where(off, ai, 0.0).astype(BF16), pi).astype(BF16) for ai, pi in zip(a, pb)]
            p = [pi - _dot(pbi, xi) for pi, pbi, xi in zip(p, pb, x)]
        eg = [jnp.exp(x) for x in g]
        rhs = [jnp.concatenate([v_ref[bi, rows[k], hsl[hh]].astype(F32) * beta[u], kb[u] * eg[u]],
                               axis=1).astype(BF16) for u, (bi, k, hh) in enumerate(units)]
        sol = [_dot(x.astype(BF16), y) for x, y in zip(p, rhs)]
        wq = [jnp.concatenate([sol[u][:, dk:].astype(BF16), (qbf[u].astype(F32) * eg[u]).astype(BF16)], axis=0)
              for u in nu]
        g_last = [x[c - 1:c, :] for x in g]
        kd_t = [(kf[u] * jnp.exp(g_last[u] - g[u])).T.astype(BF16) for u in nu]
        e_last = [jnp.exp(x) for x in g_last]
        for k in range(group):
            us = [u for u in nu if units[u][1] == k]
            s_old = [s_ref[units[u][0], units[u][2]] for u in us]
            r1 = [_dot(wq[u], so.astype(BF16)) for u, so in zip(us, s_old)]
            ub = [(sol[u][:, :dk] - r[:c]).astype(BF16) for u, r in zip(us, r1)]
            for u, so, ubi in zip(us, s_old, ub):
                s_ref[units[u][0], units[u][2]] = so * e_last[u] + _dot(kd_t[u], ubi)
            o = [r[c:] + _dot(qk[u], ubi) for u, r, ubi in zip(us, r1, ub)]
            for u, oi in zip(us, o):
                bi, _, hh = units[u]
                on = oi * lax.rsqrt(jnp.mean(oi * oi, axis=-1, keepdims=True) + RMS_EPS) * ow_ref[...]
                o_ref[bi, rows[k], hsl[hh]] = (on * z_ref[bi, rows[k], hsl[hh]].astype(F32)).astype(BF16)
        return carry

    lax.fori_loop(0, tb // (c * group), chunk_group, 0)


def _gdn(qkvz, gcol, s0, onorm, *, heads, dk, tb, chunk, nbatch, group):
    b, t, n = qkvz.shape
    nb = n // 4
    assert t % tb == 0 and tb % (chunk * group) == 0 and nb == heads * dk and b % nbatch == 0
    kern = functools.partial(_gdn_kernel, heads=heads, dk=dk, chunk=chunk, group=group)
    col = lambda jcol: pl.BlockSpec((nbatch, tb, nb), lambda bi, ti: (bi, ti, jcol))
    return pl.pallas_call(
        kern,
        grid=(b // nbatch, t // tb),
        in_specs=[col(0), col(1), col(2), col(3),
                  pl.BlockSpec((nbatch, tb, LANES), lambda bi, ti: (bi, ti, 0)),
                  pl.BlockSpec((nbatch, heads, dk, dk), lambda bi, ti: (bi, 0, 0, 0)),
                  pl.BlockSpec((1, dk), lambda bi, ti: (0, 0))],
        out_specs=[pl.BlockSpec((nbatch, tb, nb), lambda bi, ti: (bi, ti, 0)),
                   pl.BlockSpec((nbatch, heads, dk, dk), lambda bi, ti: (bi, 0, 0, 0))],
        out_shape=[jax.ShapeDtypeStruct((b, t, nb), BF16),
                   jax.ShapeDtypeStruct((b, heads, dk, dk), F32)],
        compiler_params=_cparams("parallel", "arbitrary"),
    )(qkvz, qkvz, qkvz, qkvz, gcol, s0, onorm)


def _rope(x, cos, sin_signed, lane_lo):
    half = MASK_CHUNK // 2
    swapped = jnp.where(lane_lo, pltpu.roll(x, LANES - half, axis=1), pltpu.roll(x, half, axis=1))
    return x * cos + swapped * sin_signed


def _diff_in_kernel(*refs, fused, aliased, layers, v_transposed, q_scale, tn):
    x, mod_ref, nw_ref, own_in, outs = _layer_input(refs, fused, 5 if aliased else 3)
    w_ref, cos_ref, sin_ref = own_in[:3]
    qb_ref, kb_ref, kf_hbm, vf_hbm, vb_ref, zs_ref, kn2_ref, kst_sc, vst_sc, sem = outs
    n = w_ref.shape[1]
    nb = n // 4
    tm = x.shape[0]
    tiles = pl.num_programs(1)
    step = pl.program_id(0) * tiles + pl.program_id(1)
    slot = step % 2

    def cache_copies(slot_, step_):
        bi, ti = step_ // tiles, step_ % tiles
        return [pltpu.make_async_copy(stage.at[slot_, :, pl.ds(hh * LANES, LANES)],
                                      cache.at[layer, bi, pl.ds(ti * tm, tm), hh, :],
                                      sem.at[which, slot_])
                for which, (stage, cache) in enumerate(((kst_sc, kf_hbm), (vst_sc, vf_hbm)))
                for layer in layers
                for hh in range(nb // LANES)]

    hb = _prenorm(x, mod_ref, nw_ref).astype(BF16)
    cos = cos_ref[...]
    sin = sin_ref[...]
    lane_lo = (lax.broadcasted_iota(jnp.int32, cos.shape, 1) % MASK_CHUNK) < (MASK_CHUNK // 2)
    kn2 = jnp.zeros((1, LANES), F32)
    for grp in range(n // tn):
        kind = (grp * tn) // nb
        lo = grp * tn - kind * nb
        u = _dot(hb, w_ref[:, grp * tn:(grp + 1) * tn])
        squares = []
        for s in range(tn // LANES):
            us = u[:, s * LANES:(s + 1) * LANES]
            cols = slice(lo + s * LANES, lo + (s + 1) * LANES)
            if kind == 0:
                qb_ref[:, cols] = (_rope(us, cos, sin, lane_lo) * q_scale).astype(BF16)
            elif kind == 1:
                r = _rope(us, cos, sin, lane_lo)
                kst_sc[slot, :, cols] = r
                kb_ref[:, cols] = r.astype(BF16)
                squares.append(r * r)
            elif kind == 2:
                vst_sc[slot, :, cols] = us
            else:
                zs_ref[:, cols] = _silu(us).astype(BF16)
        if kind == 1:
            group_sum = ((lax.broadcasted_iota(jnp.int32, (tn, LANES), 0) + lo) // MASK_CHUNK
                         == lax.broadcasted_iota(jnp.int32, (tn, LANES), 1)).astype(BF16)
            n2 = _dot(jnp.concatenate(squares, axis=1).astype(BF16), group_sum)
            kn2 = jnp.maximum(kn2, jnp.max(n2, axis=0, keepdims=True))
        if kind == 2:
            if v_transposed:
                vb_ref[lo:lo + tn, :] = u.T.astype(BF16)
            else:
                vb_ref[:, lo:lo + tn] = u.astype(BF16)
    kn2_ref[...] = kn2

    for copy in cache_copies(slot, step):
        copy.start()

    @pl.when(step > 0)
    def _():
        for copy in cache_copies(1 - slot, step - 1):
            copy.wait()

    @pl.when(step == pl.num_programs(0) * tiles - 1)
    def _():
        for copy in cache_copies(slot, step):
            copy.wait()


def _diff_in(x, mod, nw, prev, w, cos_t, sin_t, caches, *, layer, heads, tm, v_transposed, q_scale):
    b, t, d = x.shape
    n_layers = w.shape[0]
    nb = w.shape[2] // 4
    tn = 4 * LANES
    assert t % tm == 0 and nb == heads * LANES and nb % tn == 0
    fused = prev is not None
    aliased = caches is not None
    pre_specs, pre_args, x_spec, x_shape = _fused_specs(prev, b, t, d, tm) if fused else ([], [], [], [])
    kern = functools.partial(_diff_in_kernel, fused=fused, aliased=aliased,
                             layers=(layer,) if aliased else tuple(range(n_layers)),
                             v_transposed=v_transposed, q_scale=q_scale, tn=tn)
    tile = pl.BlockSpec((None, tm, nb), lambda bi, ti: (bi, ti, 0))
    in_hbm = pl.BlockSpec(memory_space=pl.ANY)
    cache_shape = jax.ShapeDtypeStruct((n_layers, b, t, heads, LANES), F32)
    if v_transposed:
        vb_spec = pl.BlockSpec((None, nb, tm), lambda bi, ti: (bi, 0, ti))
        vb_shape = jax.ShapeDtypeStruct((b, nb, t), BF16)
    else:
        vb_spec, vb_shape = tile, jax.ShapeDtypeStruct((b, t, nb), BF16)
    n_in = 3 + len(pre_args) + 3
    return pl.pallas_call(
        kern,
        grid=(b, t // tm),
        in_specs=[pl.BlockSpec((None, tm, d), lambda bi, ti: (bi, ti, 0)),
                  pl.BlockSpec((None, 3, d), lambda bi, ti: (bi, 0, 0)),
                  pl.BlockSpec((1, d), lambda bi, ti: (0, 0))] + pre_specs + [
                  pl.BlockSpec((None, d, 4 * nb), lambda bi, ti: (layer, 0, 0)),
                  pl.BlockSpec((tm, LANES), lambda bi, ti: (ti, 0)),
                  pl.BlockSpec((tm, LANES), lambda bi, ti: (ti, 0))] + ([in_hbm, in_hbm] if aliased else []),
        out_specs=x_spec + [tile, tile, in_hbm, in_hbm, vb_spec, tile,
                            pl.BlockSpec((None, None, 1, LANES), lambda bi, ti: (bi, ti, 0, 0))],
        out_shape=x_shape + [jax.ShapeDtypeStruct((b, t, nb), BF16),
                             jax.ShapeDtypeStruct((b, t, nb), BF16),
                             cache_shape,
                             cache_shape,
                             vb_shape,
                             jax.ShapeDtypeStruct((b, t, nb), BF16),
                             jax.ShapeDtypeStruct((b, t // tm, 1, LANES), F32)],
        input_output_aliases={n_in: 2 + len(x_shape), n_in + 1: 3 + len(x_shape)} if aliased else {},
        scratch_shapes=[pltpu.VMEM((2, tm, nb), F32), pltpu.VMEM((2, tm, nb), F32),
                        pltpu.SemaphoreType.DMA((2, 2))],
        compiler_params=_cparams("arbitrary", "arbitrary"),
    )(x, mod, nw, *pre_args, w, cos_t, sin_t, *(caches if aliased else ()))


def _lambda(lam_ref, lam_init):
    l1 = jnp.sum(lam_ref[0:1, :] * lam_ref[1:2, :], axis=-1, keepdims=True)
    l2 = jnp.sum(lam_ref[2:3, :] * lam_ref[3:4, :], axis=-1, keepdims=True)
    return jnp.exp(l1) - jnp.exp(l2) + lam_init


def _attn_prompt_kernel(q_ref, k_ref, vt_ref, z_ref, kn2_ref, lam_ref, sw_ref, o_ref,
                        m_sc, l_sc, acc_sc, qz_sc, *, lam_init, tk):
    qi = pl.program_id(2)
    tq, dh = q_ref.shape
    dq = dh // 2
    nsub = tq // tk
    chains = [(sub, comp) for sub in range(nsub) for comp in range(2)]
    everyone = list(range(len(chains)))
    n_full = (qi * tq) // tk
    qz = []
    for sub, comp in chains:
        q = q_ref[sub * tk:(sub + 1) * tk, :]
        lane = lax.broadcasted_iota(jnp.int32, q.shape, 1)
        qz.append(jnp.where((lane < dq) if comp == 0 else (lane >= dq), q, jnp.zeros_like(q)))

    kn2_max = jnp.max(kn2_ref[...], axis=0)
    kn2_lane = lax.broadcasted_iota(jnp.int32, kn2_max.shape, 1)
    kmax2 = [jnp.max(jnp.where(kn2_lane == 2 * pl.program_id(1) + comp, kn2_max, 0.0), axis=1, keepdims=True)
             for comp in range(2)]

    ones = jnp.ones((SUBLANES, dh), BF16)
    ref_rows = []
    for c, (sub, comp) in enumerate(chains):
        qf = qz[c].astype(F32)
        qn2 = _dot_nt(ones, (qf * qf).astype(BF16))[0:1, :]
        ref_rows.append(REF_SLACK * jnp.sqrt(qn2 * kmax2[comp]))

    visible = (lax.broadcasted_iota(jnp.int32, (tk, tk), 0) // MASK_CHUNK
               <= lax.broadcasted_iota(jnp.int32, (tk, tk), 1) // MASK_CHUNK)

    def blocks(kb):
        ks = k_ref[pl.ds(pl.multiple_of(kb * tk, tk), tk), :]
        vt = vt_ref[:, pl.ds(pl.multiple_of(kb * tk, tk), tk)]
        return ks, vt

    def fast_step(kb, active, masked_sub):
        ks, vt = blocks(kb)
        for g0 in range(0, len(active), STAGE_GROUP):
            grp = active[g0:g0 + STAGE_GROUP]
            s = [_dot_nt(ks, qz[c]) for c in grp]
            p = [jnp.exp2(si - ref_rows[c]) for c, si in zip(grp, s)]
            p = [jnp.where(visible, pi, 0.0) if chains[c][0] == masked_sub else pi for c, pi in zip(grp, p)]
            for c, pi in zip(grp, p):
                l_sc[c] += jnp.sum(pi.reshape(tk // SUBLANES, SUBLANES, tk), axis=0)
            pv = [_dot(vt, pi.astype(BF16)) for pi in p]
            for c, pvi in zip(grp, pv):
                acc_sc[c] += pvi

    def online_chain(c, cry):
        n_blocks = n_full + c // 2 + 1
        q_c = qz_sc[c]

        def block(kb, cry2):
            ks, vt = blocks(kb)
            s = _dot_nt(ks, q_c)
            s = jnp.where(jnp.logical_or(kb < n_blocks - 1, visible), s, -jnp.inf)
            m_old = m_sc[c]
            m_new = jnp.maximum(m_old, jnp.max(s, axis=0, keepdims=True))
            alpha = jnp.exp2(m_old - m_new)
            p = jnp.exp2(s - m_new)
            l_sc[c, 0:1, :] = alpha * l_sc[c, 0:1, :] + jnp.sum(p, axis=0, keepdims=True)
            acc_sc[c] = alpha * acc_sc[c] + _dot(vt, p.astype(BF16))
            m_sc[c] = m_new
            return cry2
        return lax.fori_loop(0, n_blocks, block, cry)

    def clear():
        l_sc[...] = jnp.zeros(l_sc.shape, F32)
        acc_sc[...] = jnp.zeros(acc_sc.shape, F32)

    def normalisers():
        return [jnp.sum(l_sc[c], axis=0, keepdims=True) for c in everyone]

    clear()
    unroll = min(nsub, 4)

    def body(i, cry):
        for u in range(unroll):
            fast_step(i * unroll + u, everyone, None)
        return cry
    lax.fori_loop(0, n_full // unroll, body, 0)
    for d in range(nsub):
        fast_step(n_full + d, [c for c in everyone if chains[c][0] >= d], d)
    l_min = functools.reduce(jnp.minimum, normalisers())

    @pl.when(jnp.min(l_min) < MIN_NORMALISER)
    def _():
        clear()
        m_sc[...] = jnp.full(m_sc.shape, -jnp.inf, F32)
        for c in everyone:
            qz_sc[c] = qz[c]
        lax.fori_loop(0, len(chains), online_chain, 0)

    lam = _lambda(lam_ref, lam_init)
    l = normalisers()
    for sub in range(nsub):
        c1, c2 = 2 * sub, 2 * sub + 1
        o_t = acc_sc[c1] * (1.0 / l[c1]) - acc_sc[c2] * (lam / l[c2])
        o = o_t.T
        gain = sw_ref[...] * (1.0 - lam_init)
        o = o * lax.rsqrt(jnp.mean(o * o, axis=-1, keepdims=True) + RMS_EPS) * gain
        rows = slice(sub * tk, (sub + 1) * tk)
        o_ref[rows, :] = (o * z_ref[rows, :].astype(F32)).astype(BF16)


def _attn_prompt(qb, kb, vt, zs, kn2, lam_vecs, subln, *, heads, lam_init, tq, tk):
    b, t, n = qb.shape
    dh = n // heads
    assert t % tq == 0 and tq % tk == 0 and tk % MASK_CHUNK == 0
    nchains = 2 * (tq // tk)
    kern = functools.partial(_attn_prompt_kernel, lam_init=lam_init, tk=tk)
    return pl.pallas_call(
        kern,
        grid=(b, heads, t // tq),
        in_specs=[pl.BlockSpec((None, tq, dh), lambda bi, hi, qi: (bi, qi, hi)),
                  pl.BlockSpec((None, t, dh), lambda bi, hi, qi: (bi, 0, hi)),
                  pl.BlockSpec((None, dh, t), lambda bi, hi, qi: (bi, hi, 0)),
                  pl.BlockSpec((None, tq, dh), lambda bi, hi, qi: (bi, qi, hi)),
                  pl.BlockSpec((None,) + kn2.shape[1:], lambda bi, hi, qi: (bi, 0, 0, 0)),
                  pl.BlockSpec(lam_vecs.shape, lambda bi, hi, qi: (0, 0)),
                  pl.BlockSpec((1, dh), lambda bi, hi, qi: (0, 0))],
        out_specs=pl.BlockSpec((None, tq, dh), lambda bi, hi, qi: (bi, qi, hi)),
        out_shape=jax.ShapeDtypeStruct((b, t, n), BF16),
        scratch_shapes=[pltpu.VMEM((nchains, 1, tk), F32), pltpu.VMEM((nchains, SUBLANES, tk), F32),
                        pltpu.VMEM((nchains, dh, tk), F32), pltpu.VMEM((nchains, tk, dh), BF16)],
        compiler_params=_cparams("parallel", "parallel", "arbitrary"),
    )(qb, kb, vt, zs, kn2, lam_vecs, subln)


def _attn_sample_kernel(q_ref, kn_ref, vn_ref, ck_hbm, cv_hbm, z_ref, lam_ref, sw_ref, o_ref,
                        q2_sc, kst_sc, vst_sc, m_sc, l_sc, acc_sc, sem, *, layer, heads, lam_init, tkv):
    si = pl.program_id(1)
    ns = pl.num_programs(1)
    step = pl.program_id(0) * ns + si
    slot = step % 2
    tq, n = q_ref.shape
    dh = n // heads
    dq = dh // 2

    def fetch(step_, slot_):
        bi, blk = step_ // ns, step_ % ns
        return [pltpu.make_async_copy(cache.at[layer, bi, pl.ds(blk * tkv, tkv), hh, :], stage.at[slot_, hh],
                                      sem.at[which, slot_])
                for which, (cache, stage) in enumerate(((ck_hbm, kst_sc), (cv_hbm, vst_sc)))
                for hh in range(heads)]

    @pl.when(step == 0)
    def _():
        for copy in fetch(step, slot):
            copy.start()

    @pl.when(step + 1 < pl.num_programs(0) * ns)
    def _():
        for copy in fetch(step + 1, 1 - slot):
            copy.start()

    @pl.when(si == 0)
    def _():
        m_sc[...] = jnp.full(m_sc.shape, -jnp.inf, F32)
        l_sc[...] = jnp.zeros(l_sc.shape, F32)
        acc_sc[...] = jnp.zeros(acc_sc.shape, F32)
        for hh in range(heads):
            q = q_ref[:, hh * dh:(hh + 1) * dh]
            lane = lax.broadcasted_iota(jnp.int32, q.shape, 1)
            q2_sc[hh, 0:tq, :] = jnp.where(lane < dq, q, jnp.zeros_like(q))
            q2_sc[hh, tq:2 * tq, :] = jnp.where(lane >= dq, q, jnp.zeros_like(q))

    def update(kblks, vblks):
        hr = range(heads)
        s = [_dot_nt(q2_sc[hh], kblks[hh]) for hh in hr]
        m_old = [m_sc[hh] for hh in hr]
        m_new = [jnp.maximum(m_old[hh], jnp.max(s[hh], axis=-1, keepdims=True)) for hh in hr]
        alpha = [jnp.exp2(m_old[hh] - m_new[hh]) for hh in hr]
        p = [jnp.exp2(s[hh] - m_new[hh]) for hh in hr]
        pv = [_dot(p[hh].astype(BF16), vblks[hh]) for hh in hr]
        for hh in hr:
            l_sc[hh] = alpha[hh] * l_sc[hh] + jnp.sum(p[hh], axis=-1, keepdims=True)
            acc_sc[hh] = alpha[hh] * acc_sc[hh] + pv[hh]
            m_sc[hh] = m_new[hh]

    for copy in fetch(step, slot):
        copy.wait()
    update([kst_sc[slot, hh].astype(BF16) for hh in range(heads)],
           [vst_sc[slot, hh].astype(BF16) for hh in range(heads)])

    @pl.when(si == ns - 1)
    def _():
        lam = _lambda(lam_ref, lam_init)
        update([kn_ref[:, hh, :].astype(BF16) for hh in range(heads)],
               [vn_ref[:, hh, :].astype(BF16) for hh in range(heads)])
        for hh in range(heads):
            hs = slice(hh * dh, (hh + 1) * dh)
            on = acc_sc[hh] / l_sc[hh]
            o = on[:tq] - lam * on[tq:]
            o = o * lax.rsqrt(jnp.mean(o * o, axis=-1, keepdims=True) + RMS_EPS) * sw_ref[...] * (1.0 - lam_init)
            o_ref[:, hs] = (o * z_ref[:, hs].astype(F32)).astype(BF16)


def _attn_sample(qb, k_new, v_new, cache_k, cache_v, zs, lam_vecs, subln, *, layer, heads, lam_init, tkv):
    b, tq, n = qb.shape
    past = cache_k.shape[2]
    dh = n // heads
    assert past % tkv == 0 and cache_k.shape[3:] == (heads, dh) and k_new.shape[2:] == (tq, heads, dh)
    kern = functools.partial(_attn_sample_kernel, layer=layer, heads=heads, lam_init=lam_init, tkv=tkv)
    tile = pl.BlockSpec((None, tq, n), lambda bi, si: (bi, 0, 0))
    new = pl.BlockSpec((None, None, tq, heads, dh), lambda bi, si: (layer, bi, 0, 0, 0))
    in_hbm = pl.BlockSpec(memory_space=pl.ANY)
    return pl.pallas_call(
        kern,
        grid=(b, past // tkv),
        in_specs=[tile, new, new, in_hbm, in_hbm, tile,
                  pl.BlockSpec(lam_vecs.shape, lambda bi, si: (0, 0)),
                  pl.BlockSpec((1, dh), lambda bi, si: (0, 0))],
        out_specs=tile,
        out_shape=jax.ShapeDtypeStruct((b, tq, n), BF16),
        scratch_shapes=[pltpu.VMEM((heads, 2 * tq, dh), BF16),
                        pltpu.VMEM((2, heads, tkv, dh), F32), pltpu.VMEM((2, heads, tkv, dh), F32),
                        pltpu.VMEM((heads, 2 * tq, 1), F32), pltpu.VMEM((heads, 2 * tq, 1), F32),
                        pltpu.VMEM((heads, 2 * tq, dh), F32), pltpu.SemaphoreType.DMA((2, 2))],
        compiler_params=_cparams("arbitrary", "arbitrary"),
    )(qb, k_new, v_new, cache_k, cache_v, zs, lam_vecs, subln)


def _out_kernel(x_ref, o_ref, w_ref, mod_ref, nw_ref, y_ref):
    y_ref[...] = _residual_update(x_ref[...], o_ref, w_ref, mod_ref, nw_ref)


def _out_proj(x, prev, *, tm):
    b, t, d = x.shape
    pre_specs, pre_args, x_spec, x_shape = _fused_specs(prev, b, t, d, tm)
    return pl.pallas_call(
        _out_kernel,
        grid=(b, t // tm),
        in_specs=[pl.BlockSpec((None, tm, d), lambda bi, ti: (bi, ti, 0))] + pre_specs,
        out_specs=x_spec[0],
        out_shape=x_shape[0],
        compiler_params=_cparams("parallel", "parallel"),
    )(x, *pre_args)


def _pick_tile(t, target):
    tile = min(t, target)
    assert t % tile == 0
    return tile


def _rope_tables(pos):
    half = MASK_CHUNK // 2
    inv = 1.0 / (ROPE_THETA ** (jnp.arange(half, dtype=F32) / half))
    ang = pos.astype(F32)[:, None] * inv[None, :]
    cos, sin = jnp.cos(ang), jnp.sin(ang)
    reps = LANES // MASK_CHUNK
    return (jnp.tile(jnp.concatenate([cos, cos], axis=1), (1, reps)),
            jnp.tile(jnp.concatenate([-sin, sin], axis=1), (1, reps)))


def _trunk(x, ada, pos, conv_bufs, gdn_states, past_k, past_v, p):
    b, t, d = x.shape
    depth = ada.shape[0]
    gdn_heads = p["a_log_gdn"].shape[1]
    gdn_dk = p["onorm_gdn"].shape[1]
    dq = p["lam_q1"].shape[1]
    n_diff = p["w_out_diff"].shape[1]
    diff_heads = n_diff // (2 * dq)
    gdn_chunk = LANES
    n_conv = 3 * gdn_heads * gdn_dk
    tm = _pick_tile(t, 512)
    cos_t, sin_t = _rope_tables(pos)
    prompt = past_k is None
    caches = None
    states, convs = [], []
    prev = None
    for i in range(depth):
        j = i // 2
        mod = ada[i]
        nw_pre = p["norm_pre"][i][None, :]
        nw_post = p["norm_post"][i][None, :]
        if i % 2 == 0:
            a_log = jnp.pad(p["a_log_gdn"][j][None, :], ((0, 0), (0, LANES - gdn_heads)))
            dt_b = jnp.pad(p["dt_bias_gdn"][j][None, :], ((0, 0), (0, LANES - gdn_heads)))
            if conv_bufs is None:
                conv_init = jnp.zeros((b, SUBLANES, n_conv), F32)
                s0 = jnp.zeros((b, gdn_heads, gdn_dk, gdn_dk), F32)
            else:
                conv_init = jnp.pad(conv_bufs[j], ((0, 0), (SUBLANES - (CONV_WIDTH - 1), 0), (0, 0)))
                s0 = gdn_states[j]
            res = _gdn_in(x, mod, nw_pre, prev, p["w_main_gdn"], p["w_ab_gdn"][j], p["conv_gdn"][j],
                          conv_init, a_log, dt_b, layer=j, heads=gdn_heads, dk=gdn_dk, tm=tm,
                          chunk=min(gdn_chunk, tm))
            if prev is not None:
                x, res = res[0], res[1:]
            qkvz, gcol, cout = res
            if t < gdn_chunk:
                front = gdn_chunk - t
                qkvz = jnp.pad(qkvz, ((0, 0), (front, 0), (0, 0)))
                gcol = jnp.pad(gcol, ((0, 0), (front, 0), (0, 0)))
            o, st = _gdn(qkvz, gcol, s0, p["onorm_gdn"][j][None, :], heads=gdn_heads, dk=gdn_dk,
                         tb=_pick_tile(qkvz.shape[1], 256), chunk=gdn_chunk,
                         nbatch=2 if b % 2 == 0 else 1, group=2 if qkvz.shape[1] >= 2 * gdn_chunk else 1)
            o = o[:, -t:]
            convs.append(cout[:, SUBLANES - (CONV_WIDTH - 1):, :])
            states.append(st)
            w_out = p["w_out_gdn"]
        else:
            lam_init = 0.8 - 0.6 * math.exp(-0.3 * i)
            lam_vecs = jnp.stack([p["lam_q1"][j], p["lam_k1"][j], p["lam_q2"][j], p["lam_k2"][j]])
            res = _diff_in(x, mod, nw_pre, prev, p["w_in_diff"], cos_t, sin_t, caches, layer=j, heads=diff_heads,
                           tm=tm, v_transposed=prompt, q_scale=dq ** -0.5 * LOG2E)
            if prev is not None:
                x, res = res[0], res[1:]
            qb, kb, k_all, v_all, vb, zs, kn2 = res
            caches = (k_all, v_all)
            subln = p["subln_diff"][j][None, :]
            if prompt:
                o = _attn_prompt(qb, kb, vb, zs, kn2, lam_vecs, subln, heads=diff_heads, lam_init=lam_init,
                                 tq=_pick_tile(t, 4096), tk=_pick_tile(t, 256))
            else:
                o = _attn_sample(qb, k_all, v_all, past_k, past_v, zs, lam_vecs, subln, layer=j, heads=diff_heads,
                                 lam_init=lam_init, tkv=_pick_tile(past_k.shape[2], 512))
            w_out = p["w_out_diff"]
        prev = (o, w_out, j, mod, nw_post)
    x = _out_proj(x, prev, tm=tm)
    return x, jnp.stack(states), jnp.stack(convs), k_all, v_all


def kernel(x_prompt, x_sample, c_prompt, c_sample, state_gdn, cache_conv, cache_k, cache_v, norm_pre, norm_post, w_ada, b_ada, w_in_gdn, conv_gdn, a_log_gdn, dt_bias_gdn, onorm_gdn, w_out_gdn, w_in_diff, lam_q1, lam_k1, lam_q2, lam_k2, subln_diff, w_out_diff):
    gdn_heads = a_log_gdn.shape[1]
    n_main = w_in_gdn.shape[2] - 2 * gdn_heads
    w_ab = jnp.pad(w_in_gdn[:, :, n_main:], ((0, 0), (0, 0), (0, LANES - 2 * gdn_heads)))
    w_ab_hi = w_ab.astype(BF16)
    w_ab_lo = (w_ab - w_ab_hi.astype(F32)).astype(BF16)
    p = {"norm_pre": norm_pre, "norm_post": norm_post,
         "w_main_gdn": w_in_gdn[:, :, :n_main].astype(BF16),
         "w_ab_gdn": jnp.concatenate([w_ab_hi, w_ab_lo], axis=-1),

         "conv_gdn": conv_gdn, "a_log_gdn": a_log_gdn,
         "dt_bias_gdn": dt_bias_gdn, "onorm_gdn": onorm_gdn, "w_out_gdn": w_out_gdn.astype(BF16),
         "w_in_diff": w_in_diff.astype(BF16), "lam_q1": lam_q1, "lam_k1": lam_k1, "lam_q2": lam_q2,
         "lam_k2": lam_k2, "subln_diff": subln_diff, "w_out_diff": w_out_diff.astype(BF16)}
    bp, tp, d = x_prompt.shape
    bs, ts, _ = x_sample.shape
    past = cache_k.shape[2]
    depth = w_ada.shape[0]
    ada = _ada(jnp.concatenate([c_prompt, c_sample], axis=0), w_ada, b_ada)
    ada = ada.reshape(depth, bp + bs, 3, d)
    y_p, st_p, conv_p, k_p, v_p = _trunk(x_prompt, ada[:, :bp], jnp.arange(tp), None, None, None, None, p)
    y_s, st_s, conv_s, k_s, v_s = _trunk(x_sample, ada[:, bp:], past + jnp.arange(ts), cache_conv, state_gdn,
                                         cache_k, cache_v, p)
    return (y_p, y_s, st_p, conv_p, k_p, v_p, st_s, conv_s, k_s, v_s)
```

```python
import functools
import math

import jax
import jax.numpy as jnp
from jax import lax
from jax.experimental import pallas as pl
from jax.experimental.pallas import tpu as pltpu

F32 = jnp.float32
BF16 = jnp.bfloat16
HIGHEST = lax.Precision.HIGHEST

RMS_EPS = 1e-6
L2_EPS = 1e-6
ROPE_THETA = 10000.0
CONV_WIDTH = 4
MASK_CHUNK = 64
LANES = 128
SUBLANES = 8
VMEM_LIMIT = 56 * 1024 * 1024
LOG2E = 1.4426950408889634
REF_SLACK = 1.01
MIN_NORMALISER = 1e-30
STAGE_GROUP = 32

NT_DIMS = (((1,), (1,)), ((), ()))


def _cparams(*sem):
    return pltpu.CompilerParams(dimension_semantics=sem, vmem_limit_bytes=VMEM_LIMIT)


def _silu(x):
    return x * jax.nn.sigmoid(x)


def _dot(a, b):
    return jnp.dot(a, b, preferred_element_type=F32)


def _dot_nt(a, b):
    return lax.dot_general(a, b, NT_DIMS, preferred_element_type=F32)


def _ada_kernel(c_ref, w_ref, b_ref, o_ref):
    act = _silu(c_ref[...])
    o_ref[...] = jnp.dot(act, w_ref[...], precision=HIGHEST, preferred_element_type=F32) + b_ref[...]


def _ada(c_all, w_ada, b_ada):
    depth, d, n = w_ada.shape
    bc = c_all.shape[0]
    tn = 512
    return pl.pallas_call(
        _ada_kernel,
        grid=(depth, n // tn),
        in_specs=[pl.BlockSpec((bc, d), lambda i, j: (0, 0)),
                  pl.BlockSpec((None, d, tn), lambda i, j: (i, 0, j)),
                  pl.BlockSpec((None, 1, tn), lambda i, j: (i, 0, j))],
        out_specs=pl.BlockSpec((None, bc, tn), lambda i, j: (i, 0, j)),
        out_shape=jax.ShapeDtypeStruct((depth, bc, n), F32),
        compiler_params=_cparams("parallel", "parallel"),
    )(c_all, w_ada, b_ada.reshape(depth, 1, n))


def _residual_update(x, o_ref, w_ref, mod_ref, nw_ref):
    y = _dot(o_ref[...], w_ref[...])
    gain = mod_ref[2:3, :] * nw_ref[...]
    return x + y * lax.rsqrt(jnp.mean(y * y, axis=-1, keepdims=True) + RMS_EPS) * gain


def _prenorm(x, mod_ref, nw_ref):
    gain = nw_ref[...] * (1.0 + mod_ref[1:2, :])
    return x * lax.rsqrt(jnp.mean(x * x, axis=-1, keepdims=True) + RMS_EPS) * gain + mod_ref[0:1, :]


def _layer_input(refs, fused, n_in):
    x_ref, mod_ref, nw_ref = refs[:3]
    pos = 7 if fused else 3
    own_in = refs[pos:pos + n_in]
    pos += n_in
    x = x_ref[...]
    if fused:
        x = _residual_update(x, *refs[3:7])
        refs[pos][...] = x
        pos += 1
    return x, mod_ref, nw_ref, own_in, refs[pos:]


def _fused_specs(prev, b, t, d, tm):
    o, w, layer, mod, nw = prev
    n = o.shape[2]
    in_specs = [pl.BlockSpec((None, tm, n), lambda bi, ti: (bi, ti, 0)),
                pl.BlockSpec((None, n, d), lambda bi, ti: (layer, 0, 0)),
                pl.BlockSpec((None, 3, d), lambda bi, ti: (bi, 0, 0)),
                pl.BlockSpec((1, d), lambda bi, ti: (0, 0))]
    return (in_specs, [o, w, mod, nw], [pl.BlockSpec((None, tm, d), lambda bi, ti: (bi, ti, 0))],
            [jax.ShapeDtypeStruct((b, t, d), F32)])


def _conv_silu(u, tail, cw):
    assert cw.shape[0] == 4
    e = jnp.concatenate([tail, u], axis=0)
    e1 = pltpu.roll(e, 1, axis=0)
    far = pltpu.roll(e * cw[1:2, :] + e1 * cw[0:1, :], 2, axis=0)
    y = e * cw[3:4, :] + e1 * cw[2:3, :] + far
    return _silu(y[SUBLANES:])


def _gdn_in_kernel(*refs, fused, heads, dk, chunk, tn):
    x, mod_ref, nw_ref, own_in, (qkvz_ref, gcol_ref, cout_ref, tail_sc) = _layer_input(refs, fused, 6)
    w_ref, wab_ref, cw_ref, cinit_ref, alog_ref, dtb_ref = own_in
    t = pl.program_id(1)
    tm = x.shape[0]
    n = w_ref.shape[1]
    nb = n // 4

    @pl.when(t == 0)
    def _():
        tail_sc[...] = cinit_ref[...]

    h = _prenorm(x, mod_ref, nw_ref)
    hb = h.astype(BF16)
    h_lo = (h - hb.astype(F32)).astype(BF16)
    hi_lo = _dot(hb, wab_ref[...])
    ab = hi_lo[:, :LANES] + hi_lo[:, LANES:] + _dot(h_lo, wab_ref[:, :LANES])
    g = -jnp.exp(alog_ref[...]) * jax.nn.softplus(ab + dtb_ref[...])
    row = lax.broadcasted_iota(jnp.int32, g.shape, 0) % chunk
    s = 1
    while s < chunk:
        g = g + jnp.where(row >= s, pltpu.roll(g, s, axis=0), 0.0)
        s *= 2
    lane = lax.broadcasted_iota(jnp.int32, g.shape, 1)
    gcol_ref[...] = jnp.where(lane < heads, g, jax.nn.sigmoid(ab))

    for grp in range(n // tn):
        cs = slice(grp * tn, (grp + 1) * tn)
        kind = (grp * tn) // nb
        u = _dot(hb, w_ref[:, cs])
        if kind == 3:
            qkvz_ref[:, cs] = _silu(u).astype(BF16)
            continue
        y = _conv_silu(u, tail_sc[:, cs], cw_ref[:, cs])
        tail_sc[:, cs] = u[tm - SUBLANES:tm]
        cout_ref[:, cs] = u[tm - SUBLANES:tm]
        if kind == 2:
            qkvz_ref[:, cs] = y.astype(BF16)
            continue
        scale = dk ** -0.5 if kind == 0 else 1.0
        for hh in range(tn // dk):
            ys = y[:, hh * dk:(hh + 1) * dk]
            inv = lax.rsqrt(jnp.sum(ys * ys, axis=-1, keepdims=True) + L2_EPS) * scale
            qkvz_ref[:, grp * tn + hh * dk:grp * tn + (hh + 1) * dk] = (ys * inv).astype(BF16)


def _gdn_in(x, mod, nw, prev, w_main, w_ab, conv_w, conv_init, a_log, dt_bias, *, layer, heads, dk, tm, chunk):
    b, t, d = x.shape
    n = w_main.shape[2]
    nb = n // 4
    tn = 4 * dk
    assert nb == heads * dk and nb % tn == 0 and t % tm == 0 and tm % SUBLANES == 0 and tm % chunk == 0
    fused = prev is not None
    pre_specs, pre_args, x_spec, x_shape = _fused_specs(prev, b, t, d, tm) if fused else ([], [], [], [])
    kern = functools.partial(_gdn_in_kernel, fused=fused, heads=heads, dk=dk, chunk=chunk, tn=tn)
    return pl.pallas_call(
        kern,
        grid=(b, t // tm),
        in_specs=[pl.BlockSpec((None, tm, d), lambda bi, ti: (bi, ti, 0)),
                  pl.BlockSpec((None, 3, d), lambda bi, ti: (bi, 0, 0)),
                  pl.BlockSpec((1, d), lambda bi, ti: (0, 0))] + pre_specs + [
                  pl.BlockSpec((None, d, n), lambda bi, ti: (layer, 0, 0)),
                  pl.BlockSpec((d, 2 * LANES), lambda bi, ti: (0, 0)),
                  pl.BlockSpec((CONV_WIDTH, 3 * nb), lambda bi, ti: (0, 0)),
                  pl.BlockSpec((None, SUBLANES, 3 * nb), lambda bi, ti: (bi, 0, 0)),
                  pl.BlockSpec((1, LANES), lambda bi, ti: (0, 0)),
                  pl.BlockSpec((1, LANES), lambda bi, ti: (0, 0))],
        out_specs=x_spec + [pl.BlockSpec((None, tm, n), lambda bi, ti: (bi, ti, 0)),
                            pl.BlockSpec((None, tm, LANES), lambda bi, ti: (bi, ti, 0)),
                            pl.BlockSpec((None, SUBLANES, 3 * nb), lambda bi, ti: (bi, 0, 0))],
        out_shape=x_shape + [jax.ShapeDtypeStruct((b, t, n), BF16),
                             jax.ShapeDtypeStruct((b, t, LANES), F32),
                             jax.ShapeDtypeStruct((b, SUBLANES, 3 * nb), F32)],
        scratch_shapes=[pltpu.VMEM((SUBLANES, 3 * nb), F32)],
        compiler_params=_cparams("parallel", "arbitrary"),
    )(x, mod, nw, *pre_args, w_main, w_ab, conv_w, conv_init, a_log, dt_bias)


def _gdn_kernel(q_ref, k_ref, v_ref, z_ref, gcol_ref, s0_ref, ow_ref, o_ref, s_ref, *, heads, dk, chunk, group):
    t = pl.program_id(1)
    nbatch, tb = q_ref.shape[0], q_ref.shape[1]
    c = chunk

    @pl.when(t == 0)
    def _():
        s_ref[...] = s0_ref[...]

    ii = lax.broadcasted_iota(jnp.int32, (c, c), 0)
    jj = lax.broadcasted_iota(jnp.int32, (c, c), 1)
    incl = ii >= jj
    strict = ii > jj
    eye = (ii == jj).astype(F32)
    off_masks = []
    s = 1
    while s < c:
        off_masks.append((ii // (2 * s) == jj // (2 * s)) & (ii // s != jj // s) & strict)
        s *= 2

    def chunk_group(ci, carry):
        rows = [pl.ds(pl.multiple_of((ci * group + k) * c, c), c) for k in range(group)]
        hsl = [slice(hh * dk, (hh + 1) * dk) for hh in range(heads)]
        units = [(bi, k, hh) for bi in range(nbatch) for k in range(group) for hh in range(heads)]
        nu = range(len(units))
        gall = {(bi, k): gcol_ref[bi, rows[k], :] for bi in range(nbatch) for k in range(group)}
        g = [gall[bi, k][:, hh:hh + 1] for bi, k, hh in units]
        beta = [gall[bi, k][:, heads + hh:heads + hh + 1] for bi, k, hh in units]
        kbf = [k_ref[bi, rows[k], hsl[hh]] for bi, k, hh in units]
        qbf = [q_ref[bi, rows[k], hsl[hh]] for bi, k, hh in units]
        kf = [x.astype(F32) for x in kbf]
        kb = [x * y for x, y in zip(kf, beta)]
        gb = [jnp.broadcast_to(x, (c, c)) for x in g]
        decay = [jnp.exp(jnp.where(incl, x - x.T, -jnp.inf)) for x in gb]
        a = [jnp.where(strict, _dot_nt(x.astype(BF16), y) * d, 0.0) for x, y, d in zip(kb, kbf, decay)]
        qk = [(_dot_nt(x, y) * d).astype(BF16) for x, y, d in zip(qbf, kbf, decay)]
        p = [eye - jnp.where(off_masks[0], x, 0.0) for x in a]
        for off in off_masks[1:]:
            pb = [x.astype(BF16) for x in p]
            x = [_dot(jnp.where(off, ai, 0.0).astype(BF16), pi).astype(BF16) for ai, pi in zip(a, pb)]
            p = [pi - _dot(pbi, xi) for pi, pbi, xi in zip(p, pb, x)]
        eg = [jnp.exp(x) for x in g]
        rhs = [jnp.concatenate([v_ref[bi, rows[k], hsl[hh]].astype(F32) * beta[u], kb[u] * eg[u]],
                               axis=1).astype(BF16) for u, (bi, k, hh) in enumerate(units)]
        sol = [_dot(x.astype(BF16), y) for x, y in zip(p, rhs)]
        wq = [jnp.concatenate([sol[u][:, dk:].astype(BF16), (qbf[u].astype(F32) * eg[u]).astype(BF16)], axis=0)
              for u in nu]
        g_last = [x[c - 1:c, :] for x in g]
        kd_t = [(kf[u] * jnp.exp(g_last[u] - g[u])).T.astype(BF16) for u in nu]
        e_last = [jnp.exp(x) for x in g_last]
        for k in range(group):
            us = [u for u in nu if units[u][1] == k]
            s_old = [s_ref[units[u][0], units[u][2]] for u in us]
            r1 = [_dot(wq[u], so.astype(BF16)) for u, so in zip(us, s_old)]
            ub = [(sol[u][:, :dk] - r[:c]).astype(BF16) for u, r in zip(us, r1)]
            for u, so, ubi in zip(us, s_old, ub):
                s_ref[units[u][0], units[u][2]] = so * e_last[u] + _dot(kd_t[u], ubi)
            o = [r[c:] + _dot(qk[u], ubi) for u, r, ubi in zip(us, r1, ub)]
            for u, oi in zip(us, o):
                bi, _, hh = units[u]
                on = oi * lax.rsqrt(jnp.mean(oi * oi, axis=-1, keepdims=True) + RMS_EPS) * ow_ref[...]
                o_ref[bi, rows[k], hsl[hh]] = (on * z_ref[bi, rows[k], hsl[hh]].astype(F32)).astype(BF16)
        return carry

    lax.fori_loop(0, tb // (c * group), chunk_group, 0)


def _gdn(qkvz, gcol, s0, onorm, *, heads, dk, tb, chunk, nbatch, group):
    b, t, n = qkvz.shape
    nb = n // 4
    assert t % tb == 0 and tb % (chunk * group) == 0 and nb == heads * dk and b % nbatch == 0
    kern = functools.partial(_gdn_kernel, heads=heads, dk=dk, chunk=chunk, group=group)
    col = lambda jcol: pl.BlockSpec((nbatch, tb, nb), lambda bi, ti: (bi, ti, jcol))
    return pl.pallas_call(
        kern,
        grid=(b // nbatch, t // tb),
        in_specs=[col(0), col(1), col(2), col(3),
                  pl.BlockSpec((nbatch, tb, LANES), lambda bi, ti: (bi, ti, 0)),
                  pl.BlockSpec((nbatch, heads, dk, dk), lambda bi, ti: (bi, 0, 0, 0)),
                  pl.BlockSpec((1, dk), lambda bi, ti: (0, 0))],
        out_specs=[pl.BlockSpec((nbatch, tb, nb), lambda bi, ti: (bi, ti, 0)),
                   pl.BlockSpec((nbatch, heads, dk, dk), lambda bi, ti: (bi, 0, 0, 0))],
        out_shape=[jax.ShapeDtypeStruct((b, t, nb), BF16),
                   jax.ShapeDtypeStruct((b, heads, dk, dk), F32)],
        compiler_params=_cparams("parallel", "arbitrary"),
    )(qkvz, qkvz, qkvz, qkvz, gcol, s0, onorm)


def _rope(x, cos, sin_signed, lane_lo):
    half = MASK_CHUNK // 2
    swapped = jnp.where(lane_lo, pltpu.roll(x, LANES - half, axis=1), pltpu.roll(x, half, axis=1))
    return x * cos + swapped * sin_signed


def _diff_in_kernel(*refs, fused, aliased, layers, v_transposed, q_scale, tn):
    x, mod_ref, nw_ref, own_in, outs = _layer_input(refs, fused, 5 if aliased else 3)
    w_ref, cos_ref, sin_ref = own_in[:3]
    qb_ref, kb_ref, kf_hbm, vf_hbm, vb_ref, zs_ref, kn2_ref, kst_sc, vst_sc, sem = outs
    n = w_ref.shape[1]
    nb = n // 4
    tm = x.shape[0]
    tiles = pl.num_programs(1)
    step = pl.program_id(0) * tiles + pl.program_id(1)
    slot = step % 2

    def cache_copies(slot_, step_):
        bi, ti = step_ // tiles, step_ % tiles
        return [pltpu.make_async_copy(stage.at[slot_, :, pl.ds(hh * LANES, LANES)],
                                      cache.at[layer, bi, pl.ds(ti * tm, tm), hh, :],
                                      sem.at[which, slot_])
                for which, (stage, cache) in enumerate(((kst_sc, kf_hbm), (vst_sc, vf_hbm)))
                for layer in layers
                for hh in range(nb // LANES)]

    hb = _prenorm(x, mod_ref, nw_ref).astype(BF16)
    cos = cos_ref[...]
    sin = sin_ref[...]
    lane_lo = (lax.broadcasted_iota(jnp.int32, cos.shape, 1) % MASK_CHUNK) < (MASK_CHUNK // 2)
    kn2 = jnp.zeros((1, LANES), F32)
    for grp in range(n // tn):
        kind = (grp * tn) // nb
        lo = grp * tn - kind * nb
        u = _dot(hb, w_ref[:, grp * tn:(grp + 1) * tn])
        squares = []
        for s in range(tn // LANES):
            us = u[:, s * LANES:(s + 1) * LANES]
            cols = slice(lo + s * LANES, lo + (s + 1) * LANES)
            if kind == 0:
                qb_ref[:, cols] = (_rope(us, cos, sin, lane_lo) * q_scale).astype(BF16)
            elif kind == 1:
                r = _rope(us, cos, sin, lane_lo)
                kst_sc[slot, :, cols] = r
                kb_ref[:, cols] = r.astype(BF16)
                squares.append(r * r)
            elif kind == 2:
                vst_sc[slot, :, cols] = us
            else:
                zs_ref[:, cols] = _silu(us).astype(BF16)
        if kind == 1:
            group_sum = ((lax.broadcasted_iota(jnp.int32, (tn, LANES), 0) + lo) // MASK_CHUNK
                         == lax.broadcasted_iota(jnp.int32, (tn, LANES), 1)).astype(BF16)
            n2 = _dot(jnp.concatenate(squares, axis=1).astype(BF16), group_sum)
            kn2 = jnp.maximum(kn2, jnp.max(n2, axis=0, keepdims=True))
        if kind == 2:
            if v_transposed:
                vb_ref[lo:lo + tn, :] = u.T.astype(BF16)
            else:
                vb_ref[:, lo:lo + tn] = u.astype(BF16)
    kn2_ref[...] = kn2

    for copy in cache_copies(slot, step):
        copy.start()

    @pl.when(step > 0)
    def _():
        for copy in cache_copies(1 - slot, step - 1):
            copy.wait()

    @pl.when(step == pl.num_programs(0) * tiles - 1)
    def _():
        for copy in cache_copies(slot, step):
            copy.wait()


def _diff_in(x, mod, nw, prev, w, cos_t, sin_t, caches, *, layer, heads, tm, v_transposed, q_scale):
    b, t, d = x.shape
    n_layers = w.shape[0]
    nb = w.shape[2] // 4
    tn = 4 * LANES
    assert t % tm == 0 and nb == heads * LANES and nb % tn == 0
    fused = prev is not None
    aliased = caches is not None
    pre_specs, pre_args, x_spec, x_shape = _fused_specs(prev, b, t, d, tm) if fused else ([], [], [], [])
    kern = functools.partial(_diff_in_kernel, fused=fused, aliased=aliased,
                             layers=(layer,) if aliased else tuple(range(n_layers)),
                             v_transposed=v_transposed, q_scale=q_scale, tn=tn)
    tile = pl.BlockSpec((None, tm, nb), lambda bi, ti: (bi, ti, 0))
    in_hbm = pl.BlockSpec(memory_space=pl.ANY)
    cache_shape = jax.ShapeDtypeStruct((n_layers, b, t, heads, LANES), F32)
    if v_transposed:
        vb_spec = pl.BlockSpec((None, nb, tm), lambda bi, ti: (bi, 0, ti))
        vb_shape = jax.ShapeDtypeStruct((b, nb, t), BF16)
    else:
        vb_spec, vb_shape = tile, jax.ShapeDtypeStruct((b, t, nb), BF16)
    n_in = 3 + len(pre_args) + 3
    return pl.pallas_call(
        kern,
        grid=(b, t // tm),
        in_specs=[pl.BlockSpec((None, tm, d), lambda bi, ti: (bi, ti, 0)),
                  pl.BlockSpec((None, 3, d), lambda bi, ti: (bi, 0, 0)),
                  pl.BlockSpec((1, d), lambda bi, ti: (0, 0))] + pre_specs + [
                  pl.BlockSpec((None, d, 4 * nb), lambda bi, ti: (layer, 0, 0)),
                  pl.BlockSpec((tm, LANES), lambda bi, ti: (ti, 0)),
                  pl.BlockSpec((tm, LANES), lambda bi, ti: (ti, 0))] + ([in_hbm, in_hbm] if aliased else []),
        out_specs=x_spec + [tile, tile, in_hbm, in_hbm, vb_spec, tile,
                            pl.BlockSpec((None, None, 1, LANES), lambda bi, ti: (bi, ti, 0, 0))],
        out_shape=x_shape + [jax.ShapeDtypeStruct((b, t, nb), BF16),
                             jax.ShapeDtypeStruct((b, t, nb), BF16),
                             cache_shape,
                             cache_shape,
                             vb_shape,
                             jax.ShapeDtypeStruct((b, t, nb), BF16),
                             jax.ShapeDtypeStruct((b, t // tm, 1, LANES), F32)],
        input_output_aliases={n_in: 2 + len(x_shape), n_in + 1: 3 + len(x_shape)} if aliased else {},
        scratch_shapes=[pltpu.VMEM((2, tm, nb), F32), pltpu.VMEM((2, tm, nb), F32),
                        pltpu.SemaphoreType.DMA((2, 2))],
        compiler_params=_cparams("arbitrary", "arbitrary"),
    )(x, mod, nw, *pre_args, w, cos_t, sin_t, *(caches if aliased else ()))


def _lambda(lam_ref, lam_init):
    l1 = jnp.sum(lam_ref[0:1, :] * lam_ref[1:2, :], axis=-1, keepdims=True)
    l2 = jnp.sum(lam_ref[2:3, :] * lam_ref[3:4, :], axis=-1, keepdims=True)
    return jnp.exp(l1) - jnp.exp(l2) + lam_init


def _attn_prompt_kernel(q_ref, k_ref, vt_ref, z_ref, kn2_ref, lam_ref, sw_ref, o_ref,
                        m_sc, l_sc, acc_sc, qz_sc, *, lam_init, tk):
    qi = pl.program_id(2)
    tq, dh = q_ref.shape
    dq = dh // 2
    nsub = tq // tk
    chains = [(sub, comp) for sub in range(nsub) for comp in range(2)]
    everyone = list(range(len(chains)))
    n_full = (qi * tq) // tk
    qz = []
    for sub, comp in chains:
        q = q_ref[sub * tk:(sub + 1) * tk, :]
        lane = lax.broadcasted_iota(jnp.int32, q.shape, 1)
        qz.append(jnp.where((lane < dq) if comp == 0 else (lane >= dq), q, jnp.zeros_like(q)))

    kn2_max = jnp.max(kn2_ref[...], axis=0)
    kn2_lane = lax.broadcasted_iota(jnp.int32, kn2_max.shape, 1)
    kmax2 = [jnp.max(jnp.where(kn2_lane == 2 * pl.program_id(1) + comp, kn2_max, 0.0), axis=1, keepdims=True)
             for comp in range(2)]

    ones = jnp.ones((SUBLANES, dh), BF16)
    ref_rows = []
    for c, (sub, comp) in enumerate(chains):
        qf = qz[c].astype(F32)
        qn2 = _dot_nt(ones, (qf * qf).astype(BF16))[0:1, :]
        ref_rows.append(REF_SLACK * jnp.sqrt(qn2 * kmax2[comp]))

    visible = (lax.broadcasted_iota(jnp.int32, (tk, tk), 0) // MASK_CHUNK
               <= lax.broadcasted_iota(jnp.int32, (tk, tk), 1) // MASK_CHUNK)

    def blocks(kb):
        ks = k_ref[pl.ds(pl.multiple_of(kb * tk, tk), tk), :]
        vt = vt_ref[:, pl.ds(pl.multiple_of(kb * tk, tk), tk)]
        return ks, vt

    def fast_step(kb, active, masked_sub):
        ks, vt = blocks(kb)
        for g0 in range(0, len(active), STAGE_GROUP):
            grp = active[g0:g0 + STAGE_GROUP]
            s = [_dot_nt(ks, qz[c]) for c in grp]
            p = [jnp.exp2(si - ref_rows[c]) for c, si in zip(grp, s)]
            p = [jnp.where(visible, pi, 0.0) if chains[c][0] == masked_sub else pi for c, pi in zip(grp, p)]
            for c, pi in zip(grp, p):
                l_sc[c] += jnp.sum(pi.reshape(tk // SUBLANES, SUBLANES, tk), axis=0)
            pv = [_dot(vt, pi.astype(BF16)) for pi in p]
            for c, pvi in zip(grp, pv):
                acc_sc[c] += pvi

    def online_chain(c, cry):
        n_blocks = n_full + c // 2 + 1
        q_c = qz_sc[c]

        def block(kb, cry2):
            ks, vt = blocks(kb)
            s = _dot_nt(ks, q_c)
            s = jnp.where(jnp.logical_or(kb < n_blocks - 1, visible), s, -jnp.inf)
            m_old = m_sc[c]
            m_new = jnp.maximum(m_old, jnp.max(s, axis=0, keepdims=True))
            alpha = jnp.exp2(m_old - m_new)
            p = jnp.exp2(s - m_new)
            l_sc[c, 0:1, :] = alpha * l_sc[c, 0:1, :] + jnp.sum(p, axis=0, keepdims=True)
            acc_sc[c] = alpha * acc_sc[c] + _dot(vt, p.astype(BF16))
            m_sc[c] = m_new
            return cry2
        return lax.fori_loop(0, n_blocks, block, cry)

    def clear():
        l_sc[...] = jnp.zeros(l_sc.shape, F32)
        acc_sc[...] = jnp.zeros(acc_sc.shape, F32)

    def normalisers():
        return [jnp.sum(l_sc[c], axis=0, keepdims=True) for c in everyone]

    clear()
    unroll = min(nsub, 4)

    def body(i, cry):
        for u in range(unroll):
            fast_step(i * unroll + u, everyone, None)
        return cry
    lax.fori_loop(0, n_full // unroll, body, 0)
    for d in range(nsub):
        fast_step(n_full + d, [c for c in everyone if chains[c][0] >= d], d)
    l_min = functools.reduce(jnp.minimum, normalisers())

    @pl.when(jnp.min(l_min) < MIN_NORMALISER)
    def _():
        clear()
        m_sc[...] = jnp.full(m_sc.shape, -jnp.inf, F32)
        for c in everyone:
            qz_sc[c] = qz[c]
        lax.fori_loop(0, len(chains), online_chain, 0)

    lam = _lambda(lam_ref, lam_init)
    l = normalisers()
    for sub in range(nsub):
        c1, c2 = 2 * sub, 2 * sub + 1
        o_t = acc_sc[c1] * (1.0 / l[c1]) - acc_sc[c2] * (lam / l[c2])
        o = o_t.T
        gain = sw_ref[...] * (1.0 - lam_init)
        o = o * lax.rsqrt(jnp.mean(o * o, axis=-1, keepdims=True) + RMS_EPS) * gain
        rows = slice(sub * tk, (sub + 1) * tk)
        o_ref[rows, :] = (o * z_ref[rows, :].astype(F32)).astype(BF16)


def _attn_prompt(qb, kb, vt, zs, kn2, lam_vecs, subln, *, heads, lam_init, tq, tk):
    b, t, n = qb.shape
    dh = n // heads
    assert t % tq == 0 and tq % tk == 0 and tk % MASK_CHUNK == 0
    nchains = 2 * (tq // tk)
    kern = functools.partial(_attn_prompt_kernel, lam_init=lam_init, tk=tk)
    return pl.pallas_call(
        kern,
        grid=(b, heads, t // tq),
        in_specs=[pl.BlockSpec((None, tq, dh), lambda bi, hi, qi: (bi, qi, hi)),
                  pl.BlockSpec((None, t, dh), lambda bi, hi, qi: (bi, 0, hi)),
                  pl.BlockSpec((None, dh, t), lambda bi, hi, qi: (bi, hi, 0)),
                  pl.BlockSpec((None, tq, dh), lambda bi, hi, qi: (bi, qi, hi)),
                  pl.BlockSpec((None,) + kn2.shape[1:], lambda bi, hi, qi: (bi, 0, 0, 0)),
                  pl.BlockSpec(lam_vecs.shape, lambda bi, hi, qi: (0, 0)),
                  pl.BlockSpec((1, dh), lambda bi, hi, qi: (0, 0))],
        out_specs=pl.BlockSpec((None, tq, dh), lambda bi, hi, qi: (bi, qi, hi)),
        out_shape=jax.ShapeDtypeStruct((b, t, n), BF16),
        scratch_shapes=[pltpu.VMEM((nchains, 1, tk), F32), pltpu.VMEM((nchains, SUBLANES, tk), F32),
                        pltpu.VMEM((nchains, dh, tk), F32), pltpu.VMEM((nchains, tk, dh), BF16)],
        compiler_params=_cparams("parallel", "parallel", "arbitrary"),
    )(qb, kb, vt, zs, kn2, lam_vecs, subln)


def _attn_sample_kernel(q_ref, kn_ref, vn_ref, ck_hbm, cv_hbm, z_ref, lam_ref, sw_ref, o_ref,
                        q2_sc, kst_sc, vst_sc, m_sc, l_sc, acc_sc, sem, *, layer, heads, lam_init, tkv):
    si = pl.program_id(1)
    ns = pl.num_programs(1)
    step = pl.program_id(0) * ns + si
    slot = step % 2
    tq, n = q_ref.shape
    dh = n // heads
    dq = dh // 2

    def fetch(step_, slot_):
        bi, blk = step_ // ns, step_ % ns
        return [pltpu.make_async_copy(cache.at[layer, bi, pl.ds(blk * tkv, tkv), hh, :], stage.at[slot_, hh],
                                      sem.at[which, slot_])
                for which, (cache, stage) in enumerate(((ck_hbm, kst_sc), (cv_hbm, vst_sc)))
                for hh in range(heads)]

    @pl.when(step == 0)
    def _():
        for copy in fetch(step, slot):
            copy.start()

    @pl.when(step + 1 < pl.num_programs(0) * ns)
    def _():
        for copy in fetch(step + 1, 1 - slot):
            copy.start()

    @pl.when(si == 0)
    def _():
        m_sc[...] = jnp.full(m_sc.shape, -jnp.inf, F32)
        l_sc[...] = jnp.zeros(l_sc.shape, F32)
        acc_sc[...] = jnp.zeros(acc_sc.shape, F32)
        for hh in range(heads):
            q = q_ref[:, hh * dh:(hh + 1) * dh]
            lane = lax.broadcasted_iota(jnp.int32, q.shape, 1)
            q2_sc[hh, 0:tq, :] = jnp.where(lane < dq, q, jnp.zeros_like(q))
            q2_sc[hh, tq:2 * tq, :] = jnp.where(lane >= dq, q, jnp.zeros_like(q))

    def update(kblks, vblks):
        hr = range(heads)
        s = [_dot_nt(q2_sc[hh], kblks[hh]) for hh in hr]
        m_old = [m_sc[hh] for hh in hr]
        m_new = [jnp.maximum(m_old[hh], jnp.max(s[hh], axis=-1, keepdims=True)) for hh in hr]
        alpha = [jnp.exp2(m_old[hh] - m_new[hh]) for hh in hr]
        p = [jnp.exp2(s[hh] - m_new[hh]) for hh in hr]
        pv = [_dot(p[hh].astype(BF16), vblks[hh]) for hh in hr]
        for hh in hr:
            l_sc[hh] = alpha[hh] * l_sc[hh] + jnp.sum(p[hh], axis=-1, keepdims=True)
            acc_sc[hh] = alpha[hh] * acc_sc[hh] + pv[hh]
            m_sc[hh] = m_new[hh]

    for copy in fetch(step, slot):
        copy.wait()
    update([kst_sc[slot, hh].astype(BF16) for hh in range(heads)],
           [vst_sc[slot, hh].astype(BF16) for hh in range(heads)])

    @pl.when(si == ns - 1)
    def _():
        lam = _lambda(lam_ref, lam_init)
        update([kn_ref[:, hh, :].astype(BF16) for hh in range(heads)],
               [vn_ref[:, hh, :].astype(BF16) for hh in range(heads)])
        for hh in range(heads):
            hs = slice(hh * dh, (hh + 1) * dh)
            on = acc_sc[hh] / l_sc[hh]
            o = on[:tq] - lam * on[tq:]
            o = o * lax.rsqrt(jnp.mean(o * o, axis=-1, keepdims=True) + RMS_EPS) * sw_ref[...] * (1.0 - lam_init)
            o_ref[:, hs] = (o * z_ref[:, hs].astype(F32)).astype(BF16)


def _attn_sample(qb, k_new, v_new, cache_k, cache_v, zs, lam_vecs, subln, *, layer, heads, lam_init, tkv):
    b, tq, n = qb.shape
    past = cache_k.shape[2]
    dh = n // heads
    assert past % tkv == 0 and cache_k.shape[3:] == (heads, dh) and k_new.shape[2:] == (tq, heads, dh)
    kern = functools.partial(_attn_sample_kernel, layer=layer, heads=heads, lam_init=lam_init, tkv=tkv)
    tile = pl.BlockSpec((None, tq, n), lambda bi, si: (bi, 0, 0))
    new = pl.BlockSpec((None, None, tq, heads, dh), lambda bi, si: (layer, bi, 0, 0, 0))
    in_hbm = pl.BlockSpec(memory_space=pl.ANY)
    return pl.pallas_call(
        kern,
        grid=(b, past // tkv),
        in_specs=[tile, new, new, in_hbm, in_hbm, tile,
                  pl.BlockSpec(lam_vecs.shape, lambda bi, si: (0, 0)),
                  pl.BlockSpec((1, dh), lambda bi, si: (0, 0))],
        out_specs=tile,
        out_shape=jax.ShapeDtypeStruct((b, tq, n), BF16),
        scratch_shapes=[pltpu.VMEM((heads, 2 * tq, dh), BF16),
                        pltpu.VMEM((2, heads, tkv, dh), F32), pltpu.VMEM((2, heads, tkv, dh), F32),
                        pltpu.VMEM((heads, 2 * tq, 1), F32), pltpu.VMEM((heads, 2 * tq, 1), F32),
                        pltpu.VMEM((heads, 2 * tq, dh), F32), pltpu.SemaphoreType.DMA((2, 2))],
        compiler_params=_cparams("arbitrary", "arbitrary"),
    )(qb, k_new, v_new, cache_k, cache_v, zs, lam_vecs, subln)


def _out_kernel(x_ref, o_ref, w_ref, mod_ref, nw_ref, y_ref):
    y_ref[...] = _residual_update(x_ref[...], o_ref, w_ref, mod_ref, nw_ref)


def _out_proj(x, prev, *, tm):
    b, t, d = x.shape
    pre_specs, pre_args, x_spec, x_shape = _fused_specs(prev, b, t, d, tm)
    return pl.pallas_call(
        _out_kernel,
        grid=(b, t // tm),
        in_specs=[pl.BlockSpec((None, tm, d), lambda bi, ti: (bi, ti, 0))] + pre_specs,
        out_specs=x_spec[0],
        out_shape=x_shape[0],
        compiler_params=_cparams("parallel", "parallel"),
    )(x, *pre_args)


def _pick_tile(t, target):
    tile = min(t, target)
    assert t % tile == 0
    return tile


def _rope_tables(pos):
    half = MASK_CHUNK // 2
    inv = 1.0 / (ROPE_THETA ** (jnp.arange(half, dtype=F32) / half))
    ang = pos.astype(F32)[:, None] * inv[None, :]
    cos, sin = jnp.cos(ang), jnp.sin(ang)
    reps = LANES // MASK_CHUNK
    return (jnp.tile(jnp.concatenate([cos, cos], axis=1), (1, reps)),
            jnp.tile(jnp.concatenate([-sin, sin], axis=1), (1, reps)))


def _trunk(x, ada, pos, conv_bufs, gdn_states, past_k, past_v, p):
    b, t, d = x.shape
    depth = ada.shape[0]
    gdn_heads = p["a_log_gdn"].shape[1]
    gdn_dk = p["onorm_gdn"].shape[1]
    dq = p["lam_q1"].shape[1]
    n_diff = p["w_out_diff"].shape[1]
    diff_heads = n_diff // (2 * dq)
    gdn_chunk = LANES
    n_conv = 3 * gdn_heads * gdn_dk
    tm = _pick_tile(t, 512)
    cos_t, sin_t = _rope_tables(pos)
    prompt = past_k is None
    caches = None
    states, convs = [], []
    prev = None
    for i in range(depth):
        j = i // 2
        mod = ada[i]
        nw_pre = p["norm_pre"][i][None, :]
        nw_post = p["norm_post"][i][None, :]
        if i % 2 == 0:
            a_log = jnp.pad(p["a_log_gdn"][j][None, :], ((0, 0), (0, LANES - gdn_heads)))
            dt_b = jnp.pad(p["dt_bias_gdn"][j][None, :], ((0, 0), (0, LANES - gdn_heads)))
            if conv_bufs is None:
                conv_init = jnp.zeros((b, SUBLANES, n_conv), F32)
                s0 = jnp.zeros((b, gdn_heads, gdn_dk, gdn_dk), F32)
            else:
                conv_init = jnp.pad(conv_bufs[j], ((0, 0), (SUBLANES - (CONV_WIDTH - 1), 0), (0, 0)))
                s0 = gdn_states[j]
            res = _gdn_in(x, mod, nw_pre, prev, p["w_main_gdn"], p["w_ab_gdn"][j], p["conv_gdn"][j],
                          conv_init, a_log, dt_b, layer=j, heads=gdn_heads, dk=gdn_dk, tm=tm,
                          chunk=min(gdn_chunk, tm))
            if prev is not None:
                x, res = res[0], res[1:]
            qkvz, gcol, cout = res
            if t < gdn_chunk:
                front = gdn_chunk - t
                qkvz = jnp.pad(qkvz, ((0, 0), (front, 0), (0, 0)))
                gcol = jnp.pad(gcol, ((0, 0), (front, 0), (0, 0)))
            o, st = _gdn(qkvz, gcol, s0, p["onorm_gdn"][j][None, :], heads=gdn_heads, dk=gdn_dk,
                         tb=_pick_tile(qkvz.shape[1], 256), chunk=gdn_chunk,
                         nbatch=2 if b % 2 == 0 else 1, group=2 if qkvz.shape[1] >= 2 * gdn_chunk else 1)
            o = o[:, -t:]
            convs.append(cout[:, SUBLANES - (CONV_WIDTH - 1):, :])
            states.append(st)
            w_out = p["w_out_gdn"]
        else:
            lam_init = 0.8 - 0.6 * math.exp(-0.3 * i)
            lam_vecs = jnp.stack([p["lam_q1"][j], p["lam_k1"][j], p["lam_q2"][j], p["lam_k2"][j]])
            res = _diff_in(x, mod, nw_pre, prev, p["w_in_diff"], cos_t, sin_t, caches, layer=j, heads=diff_heads,
                           tm=tm, v_transposed=prompt, q_scale=dq ** -0.5 * LOG2E)
            if prev is not None:
                x, res = res[0], res[1:]
            qb, kb, k_all, v_all, vb, zs, kn2 = res
            caches = (k_all, v_all)
            subln = p["subln_diff"][j][None, :]
            if prompt:
                o = _attn_prompt(qb, kb, vb, zs, kn2, lam_vecs, subln, heads=diff_heads, lam_init=lam_init,
                                 tq=_pick_tile(t, 4096), tk=_pick_tile(t, 256))
            else:
                o = _attn_sample(qb, k_all, v_all, past_k, past_v, zs, lam_vecs, subln, layer=j, heads=diff_heads,
                                 lam_init=lam_init, tkv=_pick_tile(past_k.shape[2], 512))
            w_out = p["w_out_diff"]
        prev = (o, w_out, j, mod, nw_post)
    x = _out_proj(x, prev, tm=tm)
    return x, jnp.stack(states), jnp.stack(convs), k_all, v_all


def kernel(x_prompt, x_sample, c_prompt, c_sample, state_gdn, cache_conv, cache_k, cache_v, norm_pre, norm_post, w_ada, b_ada, w_in_gdn, conv_gdn, a_log_gdn, dt_bias_gdn, onorm_gdn, w_out_gdn, w_in_diff, lam_q1, lam_k1, lam_q2, lam_k2, subln_diff, w_out_diff):
    gdn_heads = a_log_gdn.shape[1]
    n_main = w_in_gdn.shape[2] - 2 * gdn_heads
    w_ab = jnp.pad(w_in_gdn[:, :, n_main:], ((0, 0), (0, 0), (0, LANES - 2 * gdn_heads)))
    w_ab_hi = w_ab.astype(BF16)
    w_ab_lo = (w_ab - w_ab_hi.astype(F32)).astype(BF16)
    p = {"norm_pre": norm_pre, "norm_post": norm_post,
         "w_main_gdn": w_in_gdn[:, :, :n_main].astype(BF16),
         "w_ab_gdn": jnp.concatenate([w_ab_hi, w_ab_lo], axis=-1),

         "conv_gdn": conv_gdn, "a_log_gdn": a_log_gdn,
         "dt_bias_gdn": dt_bias_gdn, "onorm_gdn": onorm_gdn, "w_out_gdn": w_out_gdn.astype(BF16),
         "w_in_diff": w_in_diff.astype(BF16), "lam_q1": lam_q1, "lam_k1": lam_k1, "lam_q2": lam_q2,
         "lam_k2": lam_k2, "subln_diff": subln_diff, "w_out_diff": w_out_diff.astype(BF16)}
    bp, tp, d = x_prompt.shape
    bs, ts, _ = x_sample.shape
    past = cache_k.shape[2]
    depth = w_ada.shape[0]
    ada = _ada(jnp.concatenate([c_prompt, c_sample], axis=0), w_ada, b_ada)
    ada = ada.reshape(depth, bp + bs, 3, d)
    y_p, st_p, conv_p, k_p, v_p = _trunk(x_prompt, ada[:, :bp], jnp.arange(tp), None, None, None, None, p)
    y_s, st_s, conv_s, k_s, v_s = _trunk(x_sample, ada[:, bp:], past + jnp.arange(ts), cache_conv, state_gdn,
                                         cache_k, cache_v, p)
    return (y_p, y_s, st_p, conv_p, k_p, v_p, st_s, conv_s, k_s, v_s)
```

```python
import functools
import math

import jax
import jax.numpy as jnp
from jax import lax
from jax.experimental import pallas as pl
from jax.experimental.pallas import tpu as pltpu

F32 = jnp.float32
BF16 = jnp.bfloat16
HIGHEST = lax.Precision.HIGHEST

RMS_EPS = 1e-6
L2_EPS = 1e-6
ROPE_THETA = 10000.0
CONV_WIDTH = 4
MASK_CHUNK = 64
LANES = 128
SUBLANES = 8
VMEM_LIMIT = 56 * 1024 * 1024
LOG2E = 1.4426950408889634
REF_SLACK = 1.01
MIN_NORMALISER = 1e-30
STAGE_GROUP = 32

NT_DIMS = (((1,), (1,)), ((), ()))


def _cparams(*sem):
    return pltpu.CompilerParams(dimension_semantics=sem, vmem_limit_bytes=VMEM_LIMIT)


def _silu(x):
    return x * jax.nn.sigmoid(x)


def _dot(a, b):
    return jnp.dot(a, b, preferred_element_type=F32)


def _dot_nt(a, b):
    return lax.dot_general(a, b, NT_DIMS, preferred_element_type=F32)


def _ada_kernel(c_ref, w_ref, b_ref, o_ref):
    act = _silu(c_ref[...])
    o_ref[...] = jnp.dot(act, w_ref[...], precision=HIGHEST, preferred_element_type=F32) + b_ref[...]


def _ada(c_all, w_ada, b_ada):
    depth, d, n = w_ada.shape
    bc = c_all.shape[0]
    tn = 512
    return pl.pallas_call(
        _ada_kernel,
        grid=(depth, n // tn),
        in_specs=[pl.BlockSpec((bc, d), lambda i, j: (0, 0)),
                  pl.BlockSpec((None, d, tn), lambda i, j: (i, 0, j)),
                  pl.BlockSpec((None, 1, tn), lambda i, j: (i, 0, j))],
        out_specs=pl.BlockSpec((None, bc, tn), lambda i, j: (i, 0, j)),
        out_shape=jax.ShapeDtypeStruct((depth, bc, n), F32),
        compiler_params=_cparams("parallel", "parallel"),
    )(c_all, w_ada, b_ada.reshape(depth, 1, n))


def _residual_update(x, o_ref, w_ref, mod_ref, nw_ref):
    y = _dot(o_ref[...], w_ref[...])
    gain = mod_ref[2:3, :] * nw_ref[...]
    return x + y * lax.rsqrt(jnp.mean(y * y, axis=-1, keepdims=True) + RMS_EPS) * gain


def _prenorm(x, mod_ref, nw_ref):
    gain = nw_ref[...] * (1.0 + mod_ref[1:2, :])
    return x * lax.rsqrt(jnp.mean(x * x, axis=-1, keepdims=True) + RMS_EPS) * gain + mod_ref[0:1, :]


def _layer_input(refs, fused, n_in):
    x_ref, mod_ref, nw_ref = refs[:3]
    pos = 7 if fused else 3
    own_in = refs[pos:pos + n_in]
    pos += n_in
    x = x_ref[...]
    if fused:
        x = _residual_update(x, *refs[3:7])
        refs[pos][...] = x
        pos += 1
    return x, mod_ref, nw_ref, own_in, refs[pos:]


def _fused_specs(prev, b, t, d, tm):
    o, w, layer, mod, nw = prev
    n = o.shape[2]
    in_specs = [pl.BlockSpec((None, tm, n), lambda bi, ti: (bi, ti, 0)),
                pl.BlockSpec((None, n, d), lambda bi, ti: (layer, 0, 0)),
                pl.BlockSpec((None, 3, d), lambda bi, ti: (bi, 0, 0)),
                pl.BlockSpec((1, d), lambda bi, ti: (0, 0))]
    return (in_specs, [o, w, mod, nw], [pl.BlockSpec((None, tm, d), lambda bi, ti: (bi, ti, 0))],
            [jax.ShapeDtypeStruct((b, t, d), F32)])


def _conv_silu(u, tail, cw):
    assert cw.shape[0] == 4
    e = jnp.concatenate([tail, u], axis=0)
    e1 = pltpu.roll(e, 1, axis=0)
    far = pltpu.roll(e * cw[1:2, :] + e1 * cw[0:1, :], 2, axis=0)
    y = e * cw[3:4, :] + e1 * cw[2:3, :] + far
    return _silu(y[SUBLANES:])


def _gdn_in_kernel(*refs, fused, heads, dk, chunk, tn):
    x, mod_ref, nw_ref, own_in, (qkvz_ref, gcol_ref, cout_ref, tail_sc) = _layer_input(refs, fused, 6)
    w_ref, wab_ref, cw_ref, cinit_ref, alog_ref, dtb_ref = own_in
    t = pl.program_id(1)
    tm = x.shape[0]
    nb = heads * dk
    n = 4 * nb

    @pl.when(t == 0)
    def _():
        tail_sc[...] = cinit_ref[...]

    h = _prenorm(x, mod_ref, nw_ref)
    hb = h.astype(BF16)
    h_lo = (h - hb.astype(F32)).astype(BF16)
    hi_lo = _dot(hb, wab_ref[...])
    ab = hi_lo[:, :LANES] + hi_lo[:, LANES:] + _dot(h_lo, wab_ref[:, :LANES])
    g = -jnp.exp(alog_ref[...]) * jax.nn.softplus(ab + dtb_ref[...])
    row = lax.broadcasted_iota(jnp.int32, g.shape, 0) % chunk
    s = 1
    while s < chunk:
        g = g + jnp.where(row >= s, pltpu.roll(g, s, axis=0), 0.0)
        s *= 2
    lane = lax.broadcasted_iota(jnp.int32, g.shape, 1)
    gcol_ref[...] = jnp.where(lane < heads, g, jax.nn.sigmoid(ab))

    for grp in range(n // tn):
        cs = slice(grp * tn, (grp + 1) * tn)
        kind = (grp * tn) // nb
        u = _dot(hb, w_ref[:, cs])
        if kind == 3:
            qkvz_ref[:, cs] = _silu(u).astype(BF16)
            continue
        y = _conv_silu(u, tail_sc[:, cs], cw_ref[:, cs])
        tail_sc[:, cs] = u[tm - SUBLANES:tm]
        cout_ref[:, cs] = u[tm - SUBLANES:tm]
        if kind == 2:
            qkvz_ref[:, cs] = y.astype(BF16)
            continue
        scale = dk ** -0.5 if kind == 0 else 1.0
        for hh in range(tn // dk):
            ys = y[:, hh * dk:(hh + 1) * dk]
            inv = lax.rsqrt(jnp.sum(ys * ys, axis=-1, keepdims=True) + L2_EPS) * scale
            qkvz_ref[:, grp * tn + hh * dk:grp * tn + (hh + 1) * dk] = (ys * inv).astype(BF16)


def _gdn_in(x, mod, nw, prev, w_main, w_ab, conv_w, conv_init, a_log, dt_bias, *, layer, heads, dk, tm, chunk):
    b, t, d = x.shape
    nb = heads * dk
    n = 4 * nb
    tn = 4 * dk
    assert w_main.shape[2] >= n and nb % tn == 0 and t % tm == 0 and tm % SUBLANES == 0 and tm % chunk == 0
    fused = prev is not None
    pre_specs, pre_args, x_spec, x_shape = _fused_specs(prev, b, t, d, tm) if fused else ([], [], [], [])
    kern = functools.partial(_gdn_in_kernel, fused=fused, heads=heads, dk=dk, chunk=chunk, tn=tn)
    return pl.pallas_call(
        kern,
        grid=(b, t // tm),
        in_specs=[pl.BlockSpec((None, tm, d), lambda bi, ti: (bi, ti, 0)),
                  pl.BlockSpec((None, 3, d), lambda bi, ti: (bi, 0, 0)),
                  pl.BlockSpec((1, d), lambda bi, ti: (0, 0))] + pre_specs + [
                  pl.BlockSpec((None, d, w_main.shape[2]), lambda bi, ti: (layer, 0, 0)),
                  pl.BlockSpec((d, 2 * LANES), lambda bi, ti: (0, 0)),
                  pl.BlockSpec((CONV_WIDTH, 3 * nb), lambda bi, ti: (0, 0)),
                  pl.BlockSpec((None, SUBLANES, 3 * nb), lambda bi, ti: (bi, 0, 0)),
                  pl.BlockSpec((1, LANES), lambda bi, ti: (0, 0)),
                  pl.BlockSpec((1, LANES), lambda bi, ti: (0, 0))],
        out_specs=x_spec + [pl.BlockSpec((None, tm, n), lambda bi, ti: (bi, ti, 0)),
                            pl.BlockSpec((None, tm, LANES), lambda bi, ti: (bi, ti, 0)),
                            pl.BlockSpec((None, SUBLANES, 3 * nb), lambda bi, ti: (bi, 0, 0))],
        out_shape=x_shape + [jax.ShapeDtypeStruct((b, t, n), BF16),
                             jax.ShapeDtypeStruct((b, t, LANES), F32),
                             jax.ShapeDtypeStruct((b, SUBLANES, 3 * nb), F32)],
        scratch_shapes=[pltpu.VMEM((SUBLANES, 3 * nb), F32)],
        compiler_params=_cparams("parallel", "arbitrary"),
    )(x, mod, nw, *pre_args, w_main, w_ab, conv_w, conv_init, a_log, dt_bias)


def _gdn_kernel(q_ref, k_ref, v_ref, z_ref, gcol_ref, s0_ref, ow_ref, o_ref, s_ref, *, heads, dk, chunk, group):
    t = pl.program_id(1)
    nbatch, tb = q_ref.shape[0], q_ref.shape[1]
    c = chunk

    @pl.when(t == 0)
    def _():
        s_ref[...] = s0_ref[...]

    ii = lax.broadcasted_iota(jnp.int32, (c, c), 0)
    jj = lax.broadcasted_iota(jnp.int32, (c, c), 1)
    incl = ii >= jj
    strict = ii > jj
    eye = (ii == jj).astype(F32)
    off_masks = []
    s = 1
    while s < c:
        off_masks.append((ii // (2 * s) == jj // (2 * s)) & (ii // s != jj // s) & strict)
        s *= 2

    def chunk_group(ci, carry):
        rows = [pl.ds(pl.multiple_of((ci * group + k) * c, c), c) for k in range(group)]
        hsl = [slice(hh * dk, (hh + 1) * dk) for hh in range(heads)]
        units = [(bi, k, hh) for bi in range(nbatch) for k in range(group) for hh in range(heads)]
        nu = range(len(units))
        gall = {(bi, k): gcol_ref[bi, rows[k], :] for bi in range(nbatch) for k in range(group)}
        g = [gall[bi, k][:, hh:hh + 1] for bi, k, hh in units]
        beta = [gall[bi, k][:, heads + hh:heads + hh + 1] for bi, k, hh in units]
        kbf = [k_ref[bi, rows[k], hsl[hh]] for bi, k, hh in units]
        qbf = [q_ref[bi, rows[k], hsl[hh]] for bi, k, hh in units]
        kf = [x.astype(F32) for x in kbf]
        kb = [x * y for x, y in zip(kf, beta)]
        gb = [jnp.broadcast_to(x, (c, c)) for x in g]
        decay = [jnp.exp(jnp.where(incl, x - x.T, -jnp.inf)) for x in gb]
        a = [jnp.where(strict, _dot_nt(x.astype(BF16), y) * d, 0.0) for x, y, d in zip(kb, kbf, decay)]
        qk = [(_dot_nt(x, y) * d).astype(BF16) for x, y, d in zip(qbf, kbf, decay)]
        p = [eye - jnp.where(off_masks[0], x, 0.0) for x in a]
        for off in off_masks[1:]:
            pb = [x.astype(BF16) for x in p]
            x = [_dot(jnp.where(off, ai, 0.0).astype(BF16), pi).astype(BF16) for ai, pi in zip(a, pb)]
            p = [pi - _dot(pbi, xi) for pi, pbi, xi in zip(p, pb, x)]
        eg = [jnp.exp(x) for x in g]
        rhs = [jnp.concatenate([v_ref[bi, rows[k], hsl[hh]].astype(F32) * beta[u], kb[u] * eg[u]],
                               axis=1).astype(BF16) for u, (bi, k, hh) in enumerate(units)]
        sol = [_dot(x.astype(BF16), y) for x, y in zip(p, rhs)]
        wq = [jnp.concatenate([sol[u][:, dk:].astype(BF16), (qbf[u].astype(F32) * eg[u]).astype(BF16)], axis=0)
              for u in nu]
        g_last = [x[c - 1:c, :] for x in g]
        kd_t = [(kf[u] * jnp.exp(g_last[u] - g[u])).T.astype(BF16) for u in nu]
        e_last = [jnp.exp(x) for x in g_last]
        for k in range(group):
            us = [u for u in nu if units[u][1] == k]
            s_old = [s_ref[units[u][0], units[u][2]] for u in us]
            r1 = [_dot(wq[u], so.astype(BF16)) for u, so in zip(us, s_old)]
            ub = [(sol[u][:, :dk] - r[:c]).astype(BF16) for u, r in zip(us, r1)]
            for u, so, ubi in zip(us, s_old, ub):
                s_ref[units[u][0], units[u][2]] = so * e_last[u] + _dot(kd_t[u], ubi)
            o = [r[c:] + _dot(qk[u], ubi) for u, r, ubi in zip(us, r1, ub)]
            for u, oi in zip(us, o):
                bi, _, hh = units[u]
                on = oi * lax.rsqrt(jnp.mean(oi * oi, axis=-1, keepdims=True) + RMS_EPS) * ow_ref[...]
                o_ref[bi, rows[k], hsl[hh]] = (on * z_ref[bi, rows[k], hsl[hh]].astype(F32)).astype(BF16)
        return carry

    lax.fori_loop(0, tb // (c * group), chunk_group, 0)


def _gdn(qkvz, gcol, s0, onorm, *, heads, dk, tb, chunk, nbatch, group):
    b, t, n = qkvz.shape
    nb = n // 4
    assert t % tb == 0 and tb % (chunk * group) == 0 and nb == heads * dk and b % nbatch == 0
    kern = functools.partial(_gdn_kernel, heads=heads, dk=dk, chunk=chunk, group=group)
    col = lambda jcol: pl.BlockSpec((nbatch, tb, nb), lambda bi, ti: (bi, ti, jcol))
    return pl.pallas_call(
        kern,
        grid=(b // nbatch, t // tb),
        in_specs=[col(0), col(1), col(2), col(3),
                  pl.BlockSpec((nbatch, tb, LANES), lambda bi, ti: (bi, ti, 0)),
                  pl.BlockSpec((nbatch, heads, dk, dk), lambda bi, ti: (bi, 0, 0, 0)),
                  pl.BlockSpec((1, dk), lambda bi, ti: (0, 0))],
        out_specs=[pl.BlockSpec((nbatch, tb, nb), lambda bi, ti: (bi, ti, 0)),
                   pl.BlockSpec((nbatch, heads, dk, dk), lambda bi, ti: (bi, 0, 0, 0))],
        out_shape=[jax.ShapeDtypeStruct((b, t, nb), BF16),
                   jax.ShapeDtypeStruct((b, heads, dk, dk), F32)],
        compiler_params=_cparams("parallel", "arbitrary"),
    )(qkvz, qkvz, qkvz, qkvz, gcol, s0, onorm)


def _rope(x, cos, sin_signed, lane_lo):
    half = MASK_CHUNK // 2
    swapped = jnp.where(lane_lo, pltpu.roll(x, LANES - half, axis=1), pltpu.roll(x, half, axis=1))
    return x * cos + swapped * sin_signed


def _diff_in_kernel(*refs, fused, aliased, layers, v_transposed, q_scale, tn):
    x, mod_ref, nw_ref, own_in, outs = _layer_input(refs, fused, 5 if aliased else 3)
    w_ref, cos_ref, sin_ref = own_in[:3]
    qb_ref, kb_ref, kf_hbm, vf_hbm, vb_ref, zs_ref, kn2_ref, kst_sc, vst_sc, sem = outs
    n = w_ref.shape[1]
    nb = n // 4
    tm = x.shape[0]
    tiles = pl.num_programs(1)
    step = pl.program_id(0) * tiles + pl.program_id(1)
    slot = step % 2

    def cache_copies(slot_, step_):
        bi, ti = step_ // tiles, step_ % tiles
        return [pltpu.make_async_copy(stage.at[slot_, :, pl.ds(hh * LANES, LANES)],
                                      cache.at[layer, bi, pl.ds(ti * tm, tm), hh, :],
                                      sem.at[which, slot_])
                for which, (stage, cache) in enumerate(((kst_sc, kf_hbm), (vst_sc, vf_hbm)))
                for layer in layers
                for hh in range(nb // LANES)]

    hb = _prenorm(x, mod_ref, nw_ref).astype(BF16)
    cos = cos_ref[...]
    sin = sin_ref[...]
    lane_lo = (lax.broadcasted_iota(jnp.int32, cos.shape, 1) % MASK_CHUNK) < (MASK_CHUNK // 2)
    kn2 = jnp.zeros((1, LANES), F32)
    for grp in range(n // tn):
        kind = (grp * tn) // nb
        lo = grp * tn - kind * nb
        u = _dot(hb, w_ref[:, grp * tn:(grp + 1) * tn])
        squares = []
        for s in range(tn // LANES):
            us = u[:, s * LANES:(s + 1) * LANES]
            cols = slice(lo + s * LANES, lo + (s + 1) * LANES)
            if kind == 0:
                qb_ref[:, cols] = (_rope(us, cos, sin, lane_lo) * q_scale).astype(BF16)
            elif kind == 1:
                r = _rope(us, cos, sin, lane_lo)
                kst_sc[slot, :, cols] = r
                kb_ref[:, cols] = r.astype(BF16)
                squares.append(r * r)
            elif kind == 2:
                vst_sc[slot, :, cols] = us
            else:
                zs_ref[:, cols] = _silu(us).astype(BF16)
        if kind == 1:
            group_sum = ((lax.broadcasted_iota(jnp.int32, (tn, LANES), 0) + lo) // MASK_CHUNK
                         == lax.broadcasted_iota(jnp.int32, (tn, LANES), 1)).astype(BF16)
            n2 = _dot(jnp.concatenate(squares, axis=1).astype(BF16), group_sum)
            kn2 = jnp.maximum(kn2, jnp.max(n2, axis=0, keepdims=True))
        if kind == 2:
            if v_transposed:
                vb_ref[lo:lo + tn, :] = u.T.astype(BF16)
            else:
                vb_ref[:, lo:lo + tn] = u.astype(BF16)
    kn2_ref[...] = kn2

    for copy in cache_copies(slot, step):
        copy.start()

    @pl.when(step > 0)
    def _():
        for copy in cache_copies(1 - slot, step - 1):
            copy.wait()

    @pl.when(step == pl.num_programs(0) * tiles - 1)
    def _():
        for copy in cache_copies(slot, step):
            copy.wait()


def _diff_in(x, mod, nw, prev, w, cos_t, sin_t, caches, *, layer, heads, tm, v_transposed, q_scale):
    b, t, d = x.shape
    n_layers = w.shape[0]
    nb = w.shape[2] // 4
    tn = 4 * LANES
    assert t % tm == 0 and nb == heads * LANES and nb % tn == 0
    fused = prev is not None
    aliased = caches is not None
    pre_specs, pre_args, x_spec, x_shape = _fused_specs(prev, b, t, d, tm) if fused else ([], [], [], [])
    kern = functools.partial(_diff_in_kernel, fused=fused, aliased=aliased,
                             layers=(layer,) if aliased else tuple(range(n_layers)),
                             v_transposed=v_transposed, q_scale=q_scale, tn=tn)
    tile = pl.BlockSpec((None, tm, nb), lambda bi, ti: (bi, ti, 0))
    in_hbm = pl.BlockSpec(memory_space=pl.ANY)
    cache_shape = jax.ShapeDtypeStruct((n_layers, b, t, heads, LANES), F32)
    if v_transposed:
        vb_spec = pl.BlockSpec((None, nb, tm), lambda bi, ti: (bi, 0, ti))
        vb_shape = jax.ShapeDtypeStruct((b, nb, t), BF16)
    else:
        vb_spec, vb_shape = tile, jax.ShapeDtypeStruct((b, t, nb), BF16)
    n_in = 3 + len(pre_args) + 3
    return pl.pallas_call(
        kern,
        grid=(b, t // tm),
        in_specs=[pl.BlockSpec((None, tm, d), lambda bi, ti: (bi, ti, 0)),
                  pl.BlockSpec((None, 3, d), lambda bi, ti: (bi, 0, 0)),
                  pl.BlockSpec((1, d), lambda bi, ti: (0, 0))] + pre_specs + [
                  pl.BlockSpec((None, d, 4 * nb), lambda bi, ti: (layer, 0, 0)),
                  pl.BlockSpec((tm, LANES), lambda bi, ti: (ti, 0)),
                  pl.BlockSpec((tm, LANES), lambda bi, ti: (ti, 0))] + ([in_hbm, in_hbm] if aliased else []),
        out_specs=x_spec + [tile, tile, in_hbm, in_hbm, vb_spec, tile,
                            pl.BlockSpec((None, None, 1, LANES), lambda bi, ti: (bi, ti, 0, 0))],
        out_shape=x_shape + [jax.ShapeDtypeStruct((b, t, nb), BF16),
                             jax.ShapeDtypeStruct((b, t, nb), BF16),
                             cache_shape,
                             cache_shape,
                             vb_shape,
                             jax.ShapeDtypeStruct((b, t, nb), BF16),
                             jax.ShapeDtypeStruct((b, t // tm, 1, LANES), F32)],
        input_output_aliases={n_in: 2 + len(x_shape), n_in + 1: 3 + len(x_shape)} if aliased else {},
        scratch_shapes=[pltpu.VMEM((2, tm, nb), F32), pltpu.VMEM((2, tm, nb), F32),
                        pltpu.SemaphoreType.DMA((2, 2))],
        compiler_params=_cparams("arbitrary", "arbitrary"),
    )(x, mod, nw, *pre_args, w, cos_t, sin_t, *(caches if aliased else ()))


def _lambda(lam_ref, lam_init):
    l1 = jnp.sum(lam_ref[0:1, :] * lam_ref[1:2, :], axis=-1, keepdims=True)
    l2 = jnp.sum(lam_ref[2:3, :] * lam_ref[3:4, :], axis=-1, keepdims=True)
    return jnp.exp(l1) - jnp.exp(l2) + lam_init


def _attn_prompt_kernel(q_ref, k_ref, vt_ref, z_ref, kn2_ref, lam_ref, sw_ref, o_ref,
                        m_sc, l_sc, acc_sc, qz_sc, *, lam_init, tk):
    qi = pl.program_id(2)
    tq, dh = q_ref.shape
    dq = dh // 2
    nsub = tq // tk
    chains = [(sub, comp) for sub in range(nsub) for comp in range(2)]
    everyone = list(range(len(chains)))
    n_full = (qi * tq) // tk
    qz = []
    for sub, comp in chains:
        q = q_ref[sub * tk:(sub + 1) * tk, :]
        lane = lax.broadcasted_iota(jnp.int32, q.shape, 1)
        qz.append(jnp.where((lane < dq) if comp == 0 else (lane >= dq), q, jnp.zeros_like(q)))

    kn2_max = jnp.max(kn2_ref[...], axis=0)
    kn2_lane = lax.broadcasted_iota(jnp.int32, kn2_max.shape, 1)
    kmax2 = [jnp.max(jnp.where(kn2_lane == 2 * pl.program_id(1) + comp, kn2_max, 0.0), axis=1, keepdims=True)
             for comp in range(2)]

    ones = jnp.ones((SUBLANES, dh), BF16)
    ref_rows = []
    for c, (sub, comp) in enumerate(chains):
        qf = qz[c].astype(F32)
        qn2 = _dot_nt(ones, (qf * qf).astype(BF16))[0:1, :]
        ref_rows.append(REF_SLACK * jnp.sqrt(qn2 * kmax2[comp]))

    visible = (lax.broadcasted_iota(jnp.int32, (tk, tk), 0) // MASK_CHUNK
               <= lax.broadcasted_iota(jnp.int32, (tk, tk), 1) // MASK_CHUNK)

    def blocks(kb):
        ks = k_ref[pl.ds(pl.multiple_of(kb * tk, tk), tk), :]
        vt = vt_ref[:, pl.ds(pl.multiple_of(kb * tk, tk), tk)]
        return ks, vt

    def fast_step(kb, active, masked_sub):
        ks, vt = blocks(kb)
        for g0 in range(0, len(active), STAGE_GROUP):
            grp = active[g0:g0 + STAGE_GROUP]
            s = [_dot_nt(ks, qz[c]) for c in grp]
            p = [jnp.exp2(si - ref_rows[c]) for c, si in zip(grp, s)]
            p = [jnp.where(visible, pi, 0.0) if chains[c][0] == masked_sub else pi for c, pi in zip(grp, p)]
            for c, pi in zip(grp, p):
                l_sc[c] += jnp.sum(pi.reshape(tk // SUBLANES, SUBLANES, tk), axis=0)
            pv = [_dot(vt, pi.astype(BF16)) for pi in p]
            for c, pvi in zip(grp, pv):
                acc_sc[c] += pvi

    def online_chain(c, cry):
        n_blocks = n_full + c // 2 + 1
        q_c = qz_sc[c]

        def block(kb, cry2):
            ks, vt = blocks(kb)
            s = _dot_nt(ks, q_c)
            s = jnp.where(jnp.logical_or(kb < n_blocks - 1, visible), s, -jnp.inf)
            m_old = m_sc[c]
            m_new = jnp.maximum(m_old, jnp.max(s, axis=0, keepdims=True))
            alpha = jnp.exp2(m_old - m_new)
            p = jnp.exp2(s - m_new)
            l_sc[c, 0:1, :] = alpha * l_sc[c, 0:1, :] + jnp.sum(p, axis=0, keepdims=True)
            acc_sc[c] = alpha * acc_sc[c] + _dot(vt, p.astype(BF16))
            m_sc[c] = m_new
            return cry2
        return lax.fori_loop(0, n_blocks, block, cry)

    def clear():
        l_sc[...] = jnp.zeros(l_sc.shape, F32)
        acc_sc[...] = jnp.zeros(acc_sc.shape, F32)

    def normalisers():
        return [jnp.sum(l_sc[c], axis=0, keepdims=True) for c in everyone]

    clear()
    unroll = min(nsub, 4)

    def body(i, cry):
        for u in range(unroll):
            fast_step(i * unroll + u, everyone, None)
        return cry
    lax.fori_loop(0, n_full // unroll, body, 0)
    for d in range(nsub):
        fast_step(n_full + d, [c for c in everyone if chains[c][0] >= d], d)
    l_min = functools.reduce(jnp.minimum, normalisers())

    @pl.when(jnp.min(l_min) < MIN_NORMALISER)
    def _():
        clear()
        m_sc[...] = jnp.full(m_sc.shape, -jnp.inf, F32)
        for c in everyone:
            qz_sc[c] = qz[c]
        lax.fori_loop(0, len(chains), online_chain, 0)

    lam = _lambda(lam_ref, lam_init)
    l = normalisers()
    for sub in range(nsub):
        c1, c2 = 2 * sub, 2 * sub + 1
        o_t = acc_sc[c1] * (1.0 / l[c1]) - acc_sc[c2] * (lam / l[c2])
        o = o_t.T
        gain = sw_ref[...] * (1.0 - lam_init)
        o = o * lax.rsqrt(jnp.mean(o * o, axis=-1, keepdims=True) + RMS_EPS) * gain
        rows = slice(sub * tk, (sub + 1) * tk)
        o_ref[rows, :] = (o * z_ref[rows, :].astype(F32)).astype(BF16)


def _attn_prompt(qb, kb, vt, zs, kn2, lam_vecs, subln, *, heads, lam_init, tq, tk):
    b, t, n = qb.shape
    dh = n // heads
    assert t % tq == 0 and tq % tk == 0 and tk % MASK_CHUNK == 0
    nchains = 2 * (tq // tk)
    kern = functools.partial(_attn_prompt_kernel, lam_init=lam_init, tk=tk)
    return pl.pallas_call(
        kern,
        grid=(b, heads, t // tq),
        in_specs=[pl.BlockSpec((None, tq, dh), lambda bi, hi, qi: (bi, qi, hi)),
                  pl.BlockSpec((None, t, dh), lambda bi, hi, qi: (bi, 0, hi)),
                  pl.BlockSpec((None, dh, t), lambda bi, hi, qi: (bi, hi, 0)),
                  pl.BlockSpec((None, tq, dh), lambda bi, hi, qi: (bi, qi, hi)),
                  pl.BlockSpec((None,) + kn2.shape[1:], lambda bi, hi, qi: (bi, 0, 0, 0)),
                  pl.BlockSpec(lam_vecs.shape, lambda bi, hi, qi: (0, 0)),
                  pl.BlockSpec((1, dh), lambda bi, hi, qi: (0, 0))],
        out_specs=pl.BlockSpec((None, tq, dh), lambda bi, hi, qi: (bi, qi, hi)),
        out_shape=jax.ShapeDtypeStruct((b, t, n), BF16),
        scratch_shapes=[pltpu.VMEM((nchains, 1, tk), F32), pltpu.VMEM((nchains, SUBLANES, tk), F32),
                        pltpu.VMEM((nchains, dh, tk), F32), pltpu.VMEM((nchains, tk, dh), BF16)],
        compiler_params=_cparams("parallel", "parallel", "arbitrary"),
    )(qb, kb, vt, zs, kn2, lam_vecs, subln)


def _attn_sample_kernel(q_ref, kn_ref, vn_ref, ck_hbm, cv_hbm, z_ref, lam_ref, sw_ref, o_ref,
                        q2_sc, kst_sc, vst_sc, m_sc, l_sc, acc_sc, sem, *, layer, heads, lam_init, tkv):
    si = pl.program_id(1)
    ns = pl.num_programs(1)
    step = pl.program_id(0) * ns + si
    slot = step % 2
    tq, n = q_ref.shape
    dh = n // heads
    dq = dh // 2

    def fetch(step_, slot_):
        bi, blk = step_ // ns, step_ % ns
        return [pltpu.make_async_copy(cache.at[layer, bi, pl.ds(blk * tkv, tkv), hh, :], stage.at[slot_, hh],
                                      sem.at[which, slot_])
                for which, (cache, stage) in enumerate(((ck_hbm, kst_sc), (cv_hbm, vst_sc)))
                for hh in range(heads)]

    @pl.when(step == 0)
    def _():
        for copy in fetch(step, slot):
            copy.start()

    @pl.when(step + 1 < pl.num_programs(0) * ns)
    def _():
        for copy in fetch(step + 1, 1 - slot):
            copy.start()

    @pl.when(si == 0)
    def _():
        m_sc[...] = jnp.full(m_sc.shape, -jnp.inf, F32)
        l_sc[...] = jnp.zeros(l_sc.shape, F32)
        acc_sc[...] = jnp.zeros(acc_sc.shape, F32)
        for hh in range(heads):
            q = q_ref[:, hh * dh:(hh + 1) * dh]
            lane = lax.broadcasted_iota(jnp.int32, q.shape, 1)
            q2_sc[hh, 0:tq, :] = jnp.where(lane < dq, q, jnp.zeros_like(q))
            q2_sc[hh, tq:2 * tq, :] = jnp.where(lane >= dq, q, jnp.zeros_like(q))

    def update(kblks, vblks):
        hr = range(heads)
        s = [_dot_nt(q2_sc[hh], kblks[hh]) for hh in hr]
        m_old = [m_sc[hh] for hh in hr]
        m_new = [jnp.maximum(m_old[hh], jnp.max(s[hh], axis=-1, keepdims=True)) for hh in hr]
        alpha = [jnp.exp2(m_old[hh] - m_new[hh]) for hh in hr]
        p = [jnp.exp2(s[hh] - m_new[hh]) for hh in hr]
        pv = [_dot(p[hh].astype(BF16), vblks[hh]) for hh in hr]
        for hh in hr:
            l_sc[hh] = alpha[hh] * l_sc[hh] + jnp.sum(p[hh], axis=-1, keepdims=True)
            acc_sc[hh] = alpha[hh] * acc_sc[hh] + pv[hh]
            m_sc[hh] = m_new[hh]

    for copy in fetch(step, slot):
        copy.wait()
    update([kst_sc[slot, hh].astype(BF16) for hh in range(heads)],
           [vst_sc[slot, hh].astype(BF16) for hh in range(heads)])

    @pl.when(si == ns - 1)
    def _():
        lam = _lambda(lam_ref, lam_init)
        update([kn_ref[:, hh, :].astype(BF16) for hh in range(heads)],
               [vn_ref[:, hh, :].astype(BF16) for hh in range(heads)])
        for hh in range(heads):
            hs = slice(hh * dh, (hh + 1) * dh)
            on = acc_sc[hh] / l_sc[hh]
            o = on[:tq] - lam * on[tq:]
            o = o * lax.rsqrt(jnp.mean(o * o, axis=-1, keepdims=True) + RMS_EPS) * sw_ref[...] * (1.0 - lam_init)
            o_ref[:, hs] = (o * z_ref[:, hs].astype(F32)).astype(BF16)


def _attn_sample(qb, k_new, v_new, cache_k, cache_v, zs, lam_vecs, subln, *, layer, heads, lam_init, tkv):
    b, tq, n = qb.shape
    past = cache_k.shape[2]
    dh = n // heads
    assert past % tkv == 0 and cache_k.shape[3:] == (heads, dh) and k_new.shape[2:] == (tq, heads, dh)
    kern = functools.partial(_attn_sample_kernel, layer=layer, heads=heads, lam_init=lam_init, tkv=tkv)
    tile = pl.BlockSpec((None, tq, n), lambda bi, si: (bi, 0, 0))
    new = pl.BlockSpec((None, None, tq, heads, dh), lambda bi, si: (layer, bi, 0, 0, 0))
    in_hbm = pl.BlockSpec(memory_space=pl.ANY)
    return pl.pallas_call(
        kern,
        grid=(b, past // tkv),
        in_specs=[tile, new, new, in_hbm, in_hbm, tile,
                  pl.BlockSpec(lam_vecs.shape, lambda bi, si: (0, 0)),
                  pl.BlockSpec((1, dh), lambda bi, si: (0, 0))],
        out_specs=tile,
        out_shape=jax.ShapeDtypeStruct((b, tq, n), BF16),
        scratch_shapes=[pltpu.VMEM((heads, 2 * tq, dh), BF16),
                        pltpu.VMEM((2, heads, tkv, dh), F32), pltpu.VMEM((2, heads, tkv, dh), F32),
                        pltpu.VMEM((heads, 2 * tq, 1), F32), pltpu.VMEM((heads, 2 * tq, 1), F32),
                        pltpu.VMEM((heads, 2 * tq, dh), F32), pltpu.SemaphoreType.DMA((2, 2))],
        compiler_params=_cparams("arbitrary", "arbitrary"),
    )(qb, k_new, v_new, cache_k, cache_v, zs, lam_vecs, subln)


def _out_kernel(x_ref, o_ref, w_ref, mod_ref, nw_ref, y_ref):
    y_ref[...] = _residual_update(x_ref[...], o_ref, w_ref, mod_ref, nw_ref)


def _out_proj(x, prev, *, tm):
    b, t, d = x.shape
    pre_specs, pre_args, x_spec, x_shape = _fused_specs(prev, b, t, d, tm)
    return pl.pallas_call(
        _out_kernel,
        grid=(b, t // tm),
        in_specs=[pl.BlockSpec((None, tm, d), lambda bi, ti: (bi, ti, 0))] + pre_specs,
        out_specs=x_spec[0],
        out_shape=x_shape[0],
        compiler_params=_cparams("parallel", "parallel"),
    )(x, *pre_args)


def _pick_tile(t, target):
    tile = min(t, target)
    assert t % tile == 0
    return tile


def _rope_tables(pos):
    half = MASK_CHUNK // 2
    inv = 1.0 / (ROPE_THETA ** (jnp.arange(half, dtype=F32) / half))
    ang = pos.astype(F32)[:, None] * inv[None, :]
    cos, sin = jnp.cos(ang), jnp.sin(ang)
    reps = LANES // MASK_CHUNK
    return (jnp.tile(jnp.concatenate([cos, cos], axis=1), (1, reps)),
            jnp.tile(jnp.concatenate([-sin, sin], axis=1), (1, reps)))


def _trunk(x, ada, pos, conv_bufs, gdn_states, past_k, past_v, p):
    b, t, d = x.shape
    depth = ada.shape[0]
    gdn_heads = p["a_log_gdn"].shape[1]
    gdn_dk = p["onorm_gdn"].shape[1]
    dq = p["lam_q1"].shape[1]
    n_diff = p["w_out_diff"].shape[1]
    diff_heads = n_diff // (2 * dq)
    gdn_chunk = LANES
    n_conv = 3 * gdn_heads * gdn_dk
    tm = _pick_tile(t, 512)
    cos_t, sin_t = _rope_tables(pos)
    prompt = past_k is None
    caches = None
    states, convs = [], []
    prev = None
    for i in range(depth):
        j = i // 2
        mod = ada[i]
        nw_pre = p["norm_pre"][i][None, :]
        nw_post = p["norm_post"][i][None, :]
        if i % 2 == 0:
            a_log = jnp.pad(p["a_log_gdn"][j][None, :], ((0, 0), (0, LANES - gdn_heads)))
            dt_b = jnp.pad(p["dt_bias_gdn"][j][None, :], ((0, 0), (0, LANES - gdn_heads)))
            if conv_bufs is None:
                conv_init = jnp.zeros((b, SUBLANES, n_conv), F32)
                s0 = jnp.zeros((b, gdn_heads, gdn_dk, gdn_dk), F32)
            else:
                conv_init = jnp.pad(conv_bufs[j], ((0, 0), (SUBLANES - (CONV_WIDTH - 1), 0), (0, 0)))
                s0 = gdn_states[j]
            res = _gdn_in(x, mod, nw_pre, prev, p["w_main_gdn"], p["w_ab_gdn"][j], p["conv_gdn"][j],
                          conv_init, a_log, dt_b, layer=j, heads=gdn_heads, dk=gdn_dk, tm=tm,
                          chunk=min(gdn_chunk, tm))
            if prev is not None:
                x, res = res[0], res[1:]
            qkvz, gcol, cout = res
            if t < gdn_chunk:
                front = gdn_chunk - t
                qkvz = jnp.pad(qkvz, ((0, 0), (front, 0), (0, 0)))
                gcol = jnp.pad(gcol, ((0, 0), (front, 0), (0, 0)))
            o, st = _gdn(qkvz, gcol, s0, p["onorm_gdn"][j][None, :], heads=gdn_heads, dk=gdn_dk,
                         tb=_pick_tile(qkvz.shape[1], 512), chunk=gdn_chunk,
                         nbatch=2 if b % 2 == 0 else 1, group=2 if qkvz.shape[1] >= 2 * gdn_chunk else 1)
            o = o[:, -t:]
            convs.append(cout[:, SUBLANES - (CONV_WIDTH - 1):, :])
            states.append(st)
            w_out = p["w_out_gdn"]
        else:
            lam_init = 0.8 - 0.6 * math.exp(-0.3 * i)
            lam_vecs = jnp.stack([p["lam_q1"][j], p["lam_k1"][j], p["lam_q2"][j], p["lam_k2"][j]])
            res = _diff_in(x, mod, nw_pre, prev, p["w_in_diff"], cos_t, sin_t, caches, layer=j, heads=diff_heads,
                           tm=tm, v_transposed=prompt, q_scale=dq ** -0.5 * LOG2E)
            if prev is not None:
                x, res = res[0], res[1:]
            qb, kb, k_all, v_all, vb, zs, kn2 = res
            caches = (k_all, v_all)
            subln = p["subln_diff"][j][None, :]
            if prompt:
                o = _attn_prompt(qb, kb, vb, zs, kn2, lam_vecs, subln, heads=diff_heads, lam_init=lam_init,
                                 tq=_pick_tile(t, 4096), tk=_pick_tile(t, 256))
            else:
                o = _attn_sample(qb, k_all, v_all, past_k, past_v, zs, lam_vecs, subln, layer=j, heads=diff_heads,
                                 lam_init=lam_init, tkv=_pick_tile(past_k.shape[2], 512))
            w_out = p["w_out_diff"]
        prev = (o, w_out, j, mod, nw_post)
    x = _out_proj(x, prev, tm=_pick_tile(t, 2 * tm))
    return x, jnp.stack(states), jnp.stack(convs), k_all, v_all


def kernel(x_prompt, x_sample, c_prompt, c_sample, state_gdn, cache_conv, cache_k, cache_v, norm_pre, norm_post, w_ada, b_ada, w_in_gdn, conv_gdn, a_log_gdn, dt_bias_gdn, onorm_gdn, w_out_gdn, w_in_diff, lam_q1, lam_k1, lam_q2, lam_k2, subln_diff, w_out_diff):
    gdn_heads = a_log_gdn.shape[1]
    n_main = w_in_gdn.shape[2] - 2 * gdn_heads
    w_ab = jnp.pad(w_in_gdn[:, :, n_main:], ((0, 0), (0, 0), (0, LANES - 2 * gdn_heads)))
    w_ab_hi = w_ab.astype(BF16)
    w_ab_lo = (w_ab - w_ab_hi.astype(F32)).astype(BF16)
    p = {"norm_pre": norm_pre, "norm_post": norm_post,
         "w_main_gdn": w_in_gdn.astype(BF16),
         "w_ab_gdn": jnp.concatenate([w_ab_hi, w_ab_lo], axis=-1),

         "conv_gdn": conv_gdn, "a_log_gdn": a_log_gdn,
         "dt_bias_gdn": dt_bias_gdn, "onorm_gdn": onorm_gdn, "w_out_gdn": w_out_gdn.astype(BF16),
         "w_in_diff": w_in_diff.astype(BF16), "lam_q1": lam_q1, "lam_k1": lam_k1, "lam_q2": lam_q2,
         "lam_k2": lam_k2, "subln_diff": subln_diff, "w_out_diff": w_out_diff.astype(BF16)}
    bp, tp, d = x_prompt.shape
    bs, ts, _ = x_sample.shape
    past = cache_k.shape[2]
    depth = w_ada.shape[0]
    ada = _ada(jnp.concatenate([c_prompt, c_sample], axis=0), w_ada, b_ada)
    ada = ada.reshape(depth, bp + bs, 3, d)
    y_p, st_p, conv_p, k_p, v_p = _trunk(x_prompt, ada[:, :bp], jnp.arange(tp), None, None, None, None, p)
    y_s, st_s, conv_s, k_s, v_s = _trunk(x_sample, ada[:, bp:], past + jnp.arange(ts), cache_conv, state_gdn,
                                         cache_k, cache_v, p)
    return (y_p, y_s, st_p, conv_p, k_p, v_p, st_s, conv_s, k_s, v_s)
```

```python
import functools
import math

import jax
import jax.numpy as jnp
from jax import lax
from jax.experimental import pallas as pl
from jax.experimental.pallas import tpu as pltpu

F32 = jnp.float32
BF16 = jnp.bfloat16
HIGHEST = lax.Precision.HIGHEST

RMS_EPS = 1e-6
L2_EPS = 1e-6
ROPE_THETA = 10000.0
CONV_WIDTH = 4
MASK_CHUNK = 64
LANES = 128
SUBLANES = 8
VMEM_LIMIT = 56 * 1024 * 1024
LOG2E = 1.4426950408889634
REF_SLACK = 1.01
MIN_NORMALISER = 1e-30
STAGE_GROUP = 32

NT_DIMS = (((1,), (1,)), ((), ()))


def _cparams(*sem):
    return pltpu.CompilerParams(dimension_semantics=sem, vmem_limit_bytes=VMEM_LIMIT)


def _silu(x):
    return x * jax.nn.sigmoid(x)


def _dot(a, b):
    return jnp.dot(a, b, preferred_element_type=F32)


def _dot_nt(a, b):
    return lax.dot_general(a, b, NT_DIMS, preferred_element_type=F32)


def _ada_kernel(c_ref, w_ref, b_ref, o_ref):
    act = _silu(c_ref[...])
    o_ref[...] = jnp.dot(act, w_ref[...], precision=HIGHEST, preferred_element_type=F32) + b_ref[...]


def _ada(c_all, w_ada, b_ada):
    depth, d, n = w_ada.shape
    bc = c_all.shape[0]
    tn = 512
    return pl.pallas_call(
        _ada_kernel,
        grid=(depth, n // tn),
        in_specs=[pl.BlockSpec((bc, d), lambda i, j: (0, 0)),
                  pl.BlockSpec((None, d, tn), lambda i, j: (i, 0, j)),
                  pl.BlockSpec((None, 1, tn), lambda i, j: (i, 0, j))],
        out_specs=pl.BlockSpec((None, bc, tn), lambda i, j: (i, 0, j)),
        out_shape=jax.ShapeDtypeStruct((depth, bc, n), F32),
        compiler_params=_cparams("parallel", "parallel"),
    )(c_all, w_ada, b_ada.reshape(depth, 1, n))


def _residual_update(x, o_ref, w_ref, mod_ref, nw_ref):
    y = _dot(o_ref[...], w_ref[...])
    gain = mod_ref[2:3, :] * nw_ref[...]
    return x + y * lax.rsqrt(jnp.mean(y * y, axis=-1, keepdims=True) + RMS_EPS) * gain


def _prenorm(x, mod_ref, nw_ref):
    gain = nw_ref[...] * (1.0 + mod_ref[1:2, :])
    return x * lax.rsqrt(jnp.mean(x * x, axis=-1, keepdims=True) + RMS_EPS) * gain + mod_ref[0:1, :]


def _layer_input(refs, fused, n_in):
    x_ref, mod_ref, nw_ref = refs[:3]
    pos = 7 if fused else 3
    own_in = refs[pos:pos + n_in]
    pos += n_in
    x = x_ref[...]
    if fused:
        x = _residual_update(x, *refs[3:7])
        refs[pos][...] = x
        pos += 1
    return x, mod_ref, nw_ref, own_in, refs[pos:]


def _fused_specs(prev, b, t, d, tm):
    o, w, layer, mod, nw = prev
    n = o.shape[2]
    in_specs = [pl.BlockSpec((None, tm, n), lambda bi, ti: (bi, ti, 0)),
                pl.BlockSpec((None, n, d), lambda bi, ti: (layer, 0, 0)),
                pl.BlockSpec((None, 3, d), lambda bi, ti: (bi, 0, 0)),
                pl.BlockSpec((1, d), lambda bi, ti: (0, 0))]
    return (in_specs, [o, w, mod, nw], [pl.BlockSpec((None, tm, d), lambda bi, ti: (bi, ti, 0))],
            [jax.ShapeDtypeStruct((b, t, d), F32)])


def _conv_silu(u, tail, cw):
    assert cw.shape[0] == 4
    e = jnp.concatenate([tail, u], axis=0)
    e1 = pltpu.roll(e, 1, axis=0)
    far = pltpu.roll(e * cw[1:2, :] + e1 * cw[0:1, :], 2, axis=0)
    y = e * cw[3:4, :] + e1 * cw[2:3, :] + far
    return _silu(y[SUBLANES:])


def _gdn_in_kernel(*refs, fused, heads, dk, chunk, tn):
    x, mod_ref, nw_ref, own_in, (qkvz_ref, gcol_ref, cout_ref, tail_sc) = _layer_input(refs, fused, 6)
    w_ref, wab_ref, cw_ref, cinit_ref, alog_ref, dtb_ref = own_in
    t = pl.program_id(1)
    tm = x.shape[0]
    nb = heads * dk
    n = 4 * nb

    @pl.when(t == 0)
    def _():
        tail_sc[...] = cinit_ref[...]

    h = _prenorm(x, mod_ref, nw_ref)
    hb = h.astype(BF16)
    h_lo = (h - hb.astype(F32)).astype(BF16)
    hi_lo = _dot(hb, wab_ref[...])
    ab = hi_lo[:, :LANES] + hi_lo[:, LANES:] + _dot(h_lo, wab_ref[:, :LANES])
    g = -jnp.exp(alog_ref[...]) * jax.nn.softplus(ab + dtb_ref[...])
    row = lax.broadcasted_iota(jnp.int32, g.shape, 0) % chunk
    s = 1
    while s < chunk:
        g = g + jnp.where(row >= s, pltpu.roll(g, s, axis=0), 0.0)
        s *= 2
    lane = lax.broadcasted_iota(jnp.int32, g.shape, 1)
    gcol_ref[...] = jnp.where(lane < heads, g, jax.nn.sigmoid(ab))

    for grp in range(n // tn):
        cs = slice(grp * tn, (grp + 1) * tn)
        kind = (grp * tn) // nb
        u = _dot(hb, w_ref[:, cs])
        if kind == 3:
            qkvz_ref[:, cs] = _silu(u).astype(BF16)
            continue
        y = _conv_silu(u, tail_sc[:, cs], cw_ref[:, cs])
        tail_sc[:, cs] = u[tm - SUBLANES:tm]
        cout_ref[:, cs] = u[tm - SUBLANES:tm]
        if kind == 2:
            qkvz_ref[:, cs] = y.astype(BF16)
            continue
        scale = dk ** -0.5 if kind == 0 else 1.0
        for hh in range(tn // dk):
            ys = y[:, hh * dk:(hh + 1) * dk]
            inv = lax.rsqrt(jnp.sum(ys * ys, axis=-1, keepdims=True) + L2_EPS) * scale
            qkvz_ref[:, grp * tn + hh * dk:grp * tn + (hh + 1) * dk] = (ys * inv).astype(BF16)


def _gdn_in(x, mod, nw, prev, w_main, w_ab, conv_w, conv_init, a_log, dt_bias, *, layer, heads, dk, tm, chunk):
    b, t, d = x.shape
    nb = heads * dk
    n = 4 * nb
    tn = 4 * dk
    assert w_main.shape[2] >= n and nb % tn == 0 and t % tm == 0 and tm % SUBLANES == 0 and tm % chunk == 0
    fused = prev is not None
    pre_specs, pre_args, x_spec, x_shape = _fused_specs(prev, b, t, d, tm) if fused else ([], [], [], [])
    kern = functools.partial(_gdn_in_kernel, fused=fused, heads=heads, dk=dk, chunk=chunk, tn=tn)
    return pl.pallas_call(
        kern,
        grid=(b, t // tm),
        in_specs=[pl.BlockSpec((None, tm, d), lambda bi, ti: (bi, ti, 0)),
                  pl.BlockSpec((None, 3, d), lambda bi, ti: (bi, 0, 0)),
                  pl.BlockSpec((1, d), lambda bi, ti: (0, 0))] + pre_specs + [
                  pl.BlockSpec((None, d, w_main.shape[2]), lambda bi, ti: (layer, 0, 0)),
                  pl.BlockSpec((d, 2 * LANES), lambda bi, ti: (0, 0)),
                  pl.BlockSpec((CONV_WIDTH, 3 * nb), lambda bi, ti: (0, 0)),
                  pl.BlockSpec((None, SUBLANES, 3 * nb), lambda bi, ti: (bi, 0, 0)),
                  pl.BlockSpec((1, LANES), lambda bi, ti: (0, 0)),
                  pl.BlockSpec((1, LANES), lambda bi, ti: (0, 0))],
        out_specs=x_spec + [pl.BlockSpec((None, tm, n), lambda bi, ti: (bi, ti, 0)),
                            pl.BlockSpec((None, tm, LANES), lambda bi, ti: (bi, ti, 0)),
                            pl.BlockSpec((None, SUBLANES, 3 * nb), lambda bi, ti: (bi, 0, 0))],
        out_shape=x_shape + [jax.ShapeDtypeStruct((b, t, n), BF16),
                             jax.ShapeDtypeStruct((b, t, LANES), F32),
                             jax.ShapeDtypeStruct((b, SUBLANES, 3 * nb), F32)],
        scratch_shapes=[pltpu.VMEM((SUBLANES, 3 * nb), F32)],
        compiler_params=_cparams("parallel", "arbitrary"),
    )(x, mod, nw, *pre_args, w_main, w_ab, conv_w, conv_init, a_log, dt_bias)


def _gdn_kernel(q_ref, k_ref, v_ref, z_ref, gcol_ref, s0_ref, ow_ref, o_ref, s_ref, *, heads, dk, chunk, group):
    t = pl.program_id(1)
    nbatch, tb = q_ref.shape[0], q_ref.shape[1]
    c = chunk

    @pl.when(t == 0)
    def _():
        s_ref[...] = s0_ref[...]

    ii = lax.broadcasted_iota(jnp.int32, (c, c), 0)
    jj = lax.broadcasted_iota(jnp.int32, (c, c), 1)
    incl = ii >= jj
    strict = ii > jj
    eye = (ii == jj).astype(F32)
    off_masks = []
    s = 1
    while s < c:
        off_masks.append((ii // (2 * s) == jj // (2 * s)) & (ii // s != jj // s) & strict)
        s *= 2

    def chunk_group(ci, carry):
        rows = [pl.ds(pl.multiple_of((ci * group + k) * c, c), c) for k in range(group)]
        hsl = [slice(hh * dk, (hh + 1) * dk) for hh in range(heads)]
        units = [(bi, k, hh) for bi in range(nbatch) for k in range(group) for hh in range(heads)]
        nu = range(len(units))
        gall = {(bi, k): gcol_ref[bi, rows[k], :] for bi in range(nbatch) for k in range(group)}
        g = [gall[bi, k][:, hh:hh + 1] for bi, k, hh in units]
        beta = [gall[bi, k][:, heads + hh:heads + hh + 1] for bi, k, hh in units]
        kbf = [k_ref[bi, rows[k], hsl[hh]] for bi, k, hh in units]
        qbf = [q_ref[bi, rows[k], hsl[hh]] for bi, k, hh in units]
        kf = [x.astype(F32) for x in kbf]
        kb = [x * y for x, y in zip(kf, beta)]
        gb = [jnp.broadcast_to(x, (c, c)) for x in g]
        decay = [jnp.exp(jnp.where(incl, x - x.T, -jnp.inf)) for x in gb]
        a = [jnp.where(strict, _dot_nt(x.astype(BF16), y) * d, 0.0) for x, y, d in zip(kb, kbf, decay)]
        qk = [(_dot_nt(x, y) * d).astype(BF16) for x, y, d in zip(qbf, kbf, decay)]
        p = [eye - jnp.where(off_masks[0], x, 0.0) for x in a]
        for off in off_masks[1:]:
            pb = [x.astype(BF16) for x in p]
            x = [_dot(jnp.where(off, ai, 0.0).astype(BF16), pi).astype(BF16) for ai, pi in zip(a, pb)]
            p = [pi - _dot(pbi, xi) for pi, pbi, xi in zip(p, pb, x)]
        eg = [jnp.exp(x) for x in g]
        rhs = [jnp.concatenate([v_ref[bi, rows[k], hsl[hh]].astype(F32) * beta[u], kb[u] * eg[u]],
                               axis=1).astype(BF16) for u, (bi, k, hh) in enumerate(units)]
        sol = [_dot(x.astype(BF16), y) for x, y in zip(p, rhs)]
        wq = [jnp.concatenate([sol[u][:, dk:].astype(BF16), (qbf[u].astype(F32) * eg[u]).astype(BF16)], axis=0)
              for u in nu]
        g_last = [x[c - 1:c, :] for x in g]
        kd_t = [(kf[u] * jnp.exp(g_last[u] - g[u])).T.astype(BF16) for u in nu]
        e_last = [jnp.exp(x) for x in g_last]
        for k in range(group):
            us = [u for u in nu if units[u][1] == k]
            s_old = [s_ref[units[u][0], units[u][2]] for u in us]
            r1 = [_dot(wq[u], so.astype(BF16)) for u, so in zip(us, s_old)]
            ub = [(sol[u][:, :dk] - r[:c]).astype(BF16) for u, r in zip(us, r1)]
            for u, so, ubi in zip(us, s_old, ub):
                s_ref[units[u][0], units[u][2]] = so * e_last[u] + _dot(kd_t[u], ubi)
            o = [r[c:] + _dot(qk[u], ubi) for u, r, ubi in zip(us, r1, ub)]
            for u, oi in zip(us, o):
                bi, _, hh = units[u]
                on = oi * lax.rsqrt(jnp.mean(oi * oi, axis=-1, keepdims=True) + RMS_EPS) * ow_ref[...]
                o_ref[bi, rows[k], hsl[hh]] = (on * z_ref[bi, rows[k], hsl[hh]].astype(F32)).astype(BF16)
        return carry

    lax.fori_loop(0, tb // (c * group), chunk_group, 0)


def _gdn(qkvz, gcol, s0, onorm, *, heads, dk, tb, chunk, nbatch, group):
    b, t, n = qkvz.shape
    nb = n // 4
    assert t % tb == 0 and tb % (chunk * group) == 0 and nb == heads * dk and b % nbatch == 0
    kern = functools.partial(_gdn_kernel, heads=heads, dk=dk, chunk=chunk, group=group)
    col = lambda jcol: pl.BlockSpec((nbatch, tb, nb), lambda bi, ti: (bi, ti, jcol))
    return pl.pallas_call(
        kern,
        grid=(b // nbatch, t // tb),
        in_specs=[col(0), col(1), col(2), col(3),
                  pl.BlockSpec((nbatch, tb, LANES), lambda bi, ti: (bi, ti, 0)),
                  pl.BlockSpec((nbatch, heads, dk, dk), lambda bi, ti: (bi, 0, 0, 0)),
                  pl.BlockSpec((1, dk), lambda bi, ti: (0, 0))],
        out_specs=[pl.BlockSpec((nbatch, tb, nb), lambda bi, ti: (bi, ti, 0)),
                   pl.BlockSpec((nbatch, heads, dk, dk), lambda bi, ti: (bi, 0, 0, 0))],
        out_shape=[jax.ShapeDtypeStruct((b, t, nb), BF16),
                   jax.ShapeDtypeStruct((b, heads, dk, dk), F32)],
        compiler_params=_cparams("parallel", "arbitrary"),
    )(qkvz, qkvz, qkvz, qkvz, gcol, s0, onorm)


def _rope(x, cos, sin_signed, lane_lo):
    half = MASK_CHUNK // 2
    swapped = jnp.where(lane_lo, pltpu.roll(x, LANES - half, axis=1), pltpu.roll(x, half, axis=1))
    return x * cos + swapped * sin_signed


def _diff_in_kernel(*refs, fused, aliased, layers, v_transposed, q_scale, tn):
    x, mod_ref, nw_ref, own_in, outs = _layer_input(refs, fused, 5 if aliased else 3)
    w_ref, cos_ref, sin_ref = own_in[:3]
    qb_ref, kb_ref, kf_hbm, vf_hbm, vb_ref, zs_ref, kn2_ref, kst_sc, vst_sc, sem = outs
    n = w_ref.shape[1]
    nb = n // 4
    tm = x.shape[0]
    tiles = pl.num_programs(1)
    step = pl.program_id(0) * tiles + pl.program_id(1)
    slot = step % 2

    def cache_copies(slot_, step_):
        bi, ti = step_ // tiles, step_ % tiles
        return [pltpu.make_async_copy(stage.at[slot_, :, pl.ds(hh * LANES, LANES)],
                                      cache.at[layer, bi, pl.ds(ti * tm, tm), hh, :],
                                      sem.at[which, slot_])
                for which, (stage, cache) in enumerate(((kst_sc, kf_hbm), (vst_sc, vf_hbm)))
                for layer in layers
                for hh in range(nb // LANES)]

    hb = _prenorm(x, mod_ref, nw_ref).astype(BF16)
    cos = cos_ref[...]
    sin = sin_ref[...]
    lane_lo = (lax.broadcasted_iota(jnp.int32, cos.shape, 1) % MASK_CHUNK) < (MASK_CHUNK // 2)
    kn2 = jnp.zeros((1, LANES), F32)
    for grp in range(n // tn):
        kind = (grp * tn) // nb
        lo = grp * tn - kind * nb
        u = _dot(hb, w_ref[:, grp * tn:(grp + 1) * tn])
        squares = []
        for s in range(tn // LANES):
            us = u[:, s * LANES:(s + 1) * LANES]
            cols = slice(lo + s * LANES, lo + (s + 1) * LANES)
            if kind == 0:
                qb_ref[:, cols] = (_rope(us, cos, sin, lane_lo) * q_scale).astype(BF16)
            elif kind == 1:
                r = _rope(us, cos, sin, lane_lo)
                kst_sc[slot, :, cols] = r
                kb_ref[:, cols] = r.astype(BF16)
                squares.append(r * r)
            elif kind == 2:
                vst_sc[slot, :, cols] = us
            else:
                zs_ref[:, cols] = _silu(us).astype(BF16)
        if kind == 1:
            group_sum = ((lax.broadcasted_iota(jnp.int32, (tn, LANES), 0) + lo) // MASK_CHUNK
                         == lax.broadcasted_iota(jnp.int32, (tn, LANES), 1)).astype(BF16)
            n2 = _dot(jnp.concatenate(squares, axis=1).astype(BF16), group_sum)
            kn2 = jnp.maximum(kn2, jnp.max(n2, axis=0, keepdims=True))
        if kind == 2:
            if v_transposed:
                vb_ref[lo:lo + tn, :] = u.T.astype(BF16)
            else:
                vb_ref[:, lo:lo + tn] = u.astype(BF16)
    kn2_ref[...] = kn2

    for copy in cache_copies(slot, step):
        copy.start()

    @pl.when(step > 0)
    def _():
        for copy in cache_copies(1 - slot, step - 1):
            copy.wait()

    @pl.when(step == pl.num_programs(0) * tiles - 1)
    def _():
        for copy in cache_copies(slot, step):
            copy.wait()


def _diff_in(x, mod, nw, prev, w, cos_t, sin_t, caches, *, layer, heads, tm, v_transposed, q_scale):
    b, t, d = x.shape
    n_layers = w.shape[0]
    nb = w.shape[2] // 4
    tn = 4 * LANES
    assert t % tm == 0 and nb == heads * LANES and nb % tn == 0
    fused = prev is not None
    aliased = caches is not None
    pre_specs, pre_args, x_spec, x_shape = _fused_specs(prev, b, t, d, tm) if fused else ([], [], [], [])
    kern = functools.partial(_diff_in_kernel, fused=fused, aliased=aliased,
                             layers=(layer,) if aliased else tuple(range(n_layers)),
                             v_transposed=v_transposed, q_scale=q_scale, tn=tn)
    tile = pl.BlockSpec((None, tm, nb), lambda bi, ti: (bi, ti, 0))
    in_hbm = pl.BlockSpec(memory_space=pl.ANY)
    cache_shape = jax.ShapeDtypeStruct((n_layers, b, t, heads, LANES), F32)
    if v_transposed:
        vb_spec = pl.BlockSpec((None, nb, tm), lambda bi, ti: (bi, 0, ti))
        vb_shape = jax.ShapeDtypeStruct((b, nb, t), BF16)
    else:
        vb_spec, vb_shape = tile, jax.ShapeDtypeStruct((b, t, nb), BF16)
    n_in = 3 + len(pre_args) + 3
    return pl.pallas_call(
        kern,
        grid=(b, t // tm),
        in_specs=[pl.BlockSpec((None, tm, d), lambda bi, ti: (bi, ti, 0)),
                  pl.BlockSpec((None, 3, d), lambda bi, ti: (bi, 0, 0)),
                  pl.BlockSpec((1, d), lambda bi, ti: (0, 0))] + pre_specs + [
                  pl.BlockSpec((None, d, 4 * nb), lambda bi, ti: (layer, 0, 0)),
                  pl.BlockSpec((tm, LANES), lambda bi, ti: (ti, 0)),
                  pl.BlockSpec((tm, LANES), lambda bi, ti: (ti, 0))] + ([in_hbm, in_hbm] if aliased else []),
        out_specs=x_spec + [tile, tile, in_hbm, in_hbm, vb_spec, tile,
                            pl.BlockSpec((None, None, 1, LANES), lambda bi, ti: (bi, ti, 0, 0))],
        out_shape=x_shape + [jax.ShapeDtypeStruct((b, t, nb), BF16),
                             jax.ShapeDtypeStruct((b, t, nb), BF16),
                             cache_shape,
                             cache_shape,
                             vb_shape,
                             jax.ShapeDtypeStruct((b, t, nb), BF16),
                             jax.ShapeDtypeStruct((b, t // tm, 1, LANES), F32)],
        input_output_aliases={n_in: 2 + len(x_shape), n_in + 1: 3 + len(x_shape)} if aliased else {},
        scratch_shapes=[pltpu.VMEM((2, tm, nb), F32), pltpu.VMEM((2, tm, nb), F32),
                        pltpu.SemaphoreType.DMA((2, 2))],
        compiler_params=_cparams("arbitrary", "arbitrary"),
    )(x, mod, nw, *pre_args, w, cos_t, sin_t, *(caches if aliased else ()))


def _lambda(lam_ref, lam_init):
    l1 = jnp.sum(lam_ref[0:1, :] * lam_ref[1:2, :], axis=-1, keepdims=True)
    l2 = jnp.sum(lam_ref[2:3, :] * lam_ref[3:4, :], axis=-1, keepdims=True)
    return jnp.exp(l1) - jnp.exp(l2) + lam_init


def _attn_prompt_kernel(q_ref, k_ref, vt_ref, z_ref, kn2_ref, lam_ref, sw_ref, o_ref,
                        m_sc, l_sc, acc_sc, qz_sc, *, lam_init, tk):
    qi = pl.program_id(2)
    tq, dh = q_ref.shape
    dq = dh // 2
    nsub = tq // tk
    chains = [(sub, comp) for sub in range(nsub) for comp in range(2)]
    everyone = list(range(len(chains)))
    n_full = (qi * tq) // tk
    qz = []
    for sub, comp in chains:
        q = q_ref[sub * tk:(sub + 1) * tk, :]
        lane = lax.broadcasted_iota(jnp.int32, q.shape, 1)
        qz.append(jnp.where((lane < dq) if comp == 0 else (lane >= dq), q, jnp.zeros_like(q)))

    kn2_max = jnp.max(kn2_ref[...], axis=0)
    kn2_lane = lax.broadcasted_iota(jnp.int32, kn2_max.shape, 1)
    kmax2 = [jnp.max(jnp.where(kn2_lane == 2 * pl.program_id(1) + comp, kn2_max, 0.0), axis=1, keepdims=True)
             for comp in range(2)]

    ones = jnp.ones((SUBLANES, dh), BF16)
    ref_rows = []
    for c, (sub, comp) in enumerate(chains):
        qf = qz[c].astype(F32)
        qn2 = _dot_nt(ones, (qf * qf).astype(BF16))[0:1, :]
        ref_rows.append(REF_SLACK * jnp.sqrt(qn2 * kmax2[comp]))

    visible = (lax.broadcasted_iota(jnp.int32, (tk, tk), 0) // MASK_CHUNK
               <= lax.broadcasted_iota(jnp.int32, (tk, tk), 1) // MASK_CHUNK)

    def blocks(kb):
        ks = k_ref[pl.ds(pl.multiple_of(kb * tk, tk), tk), :]
        vt = vt_ref[:, pl.ds(pl.multiple_of(kb * tk, tk), tk)]
        return ks, vt

    def fast_step(kb, active, masked_sub):
        ks, vt = blocks(kb)
        for g0 in range(0, len(active), STAGE_GROUP):
            grp = active[g0:g0 + STAGE_GROUP]
            s = [_dot_nt(ks, qz[c]) for c in grp]
            p = [jnp.exp2(si - ref_rows[c]) for c, si in zip(grp, s)]
            p = [jnp.where(visible, pi, 0.0) if chains[c][0] == masked_sub else pi for c, pi in zip(grp, p)]
            for c, pi in zip(grp, p):
                l_sc[c] += jnp.sum(pi.reshape(tk // SUBLANES, SUBLANES, tk), axis=0)
            pv = [_dot(vt, pi.astype(BF16)) for pi in p]
            for c, pvi in zip(grp, pv):
                acc_sc[c] += pvi

    def online_chain(c, cry):
        n_blocks = n_full + c // 2 + 1
        q_c = qz_sc[c]

        def block(kb, cry2):
            ks, vt = blocks(kb)
            s = _dot_nt(ks, q_c)
            s = jnp.where(jnp.logical_or(kb < n_blocks - 1, visible), s, -jnp.inf)
            m_old = m_sc[c]
            m_new = jnp.maximum(m_old, jnp.max(s, axis=0, keepdims=True))
            alpha = jnp.exp2(m_old - m_new)
            p = jnp.exp2(s - m_new)
            l_sc[c, 0:1, :] = alpha * l_sc[c, 0:1, :] + jnp.sum(p, axis=0, keepdims=True)
            acc_sc[c] = alpha * acc_sc[c] + _dot(vt, p.astype(BF16))
            m_sc[c] = m_new
            return cry2
        return lax.fori_loop(0, n_blocks, block, cry)

    def clear():
        l_sc[...] = jnp.zeros(l_sc.shape, F32)
        acc_sc[...] = jnp.zeros(acc_sc.shape, F32)

    def normalisers():
        return [jnp.sum(l_sc[c], axis=0, keepdims=True) for c in everyone]

    clear()
    unroll = min(nsub, 4)

    def body(i, cry):
        for u in range(unroll):
            fast_step(i * unroll + u, everyone, None)
        return cry
    lax.fori_loop(0, n_full // unroll, body, 0)
    for d in range(nsub):
        fast_step(n_full + d, [c for c in everyone if chains[c][0] >= d], d)
    l_min = functools.reduce(jnp.minimum, normalisers())

    @pl.when(jnp.min(l_min) < MIN_NORMALISER)
    def _():
        clear()
        m_sc[...] = jnp.full(m_sc.shape, -jnp.inf, F32)
        for c in everyone:
            qz_sc[c] = qz[c]
        lax.fori_loop(0, len(chains), online_chain, 0)

    lam = _lambda(lam_ref, lam_init)
    l = normalisers()
    for sub in range(nsub):
        c1, c2 = 2 * sub, 2 * sub + 1
        o_t = acc_sc[c1] * (1.0 / l[c1]) - acc_sc[c2] * (lam / l[c2])
        o = o_t.T
        gain = sw_ref[...] * (1.0 - lam_init)
        o = o * lax.rsqrt(jnp.mean(o * o, axis=-1, keepdims=True) + RMS_EPS) * gain
        rows = slice(sub * tk, (sub + 1) * tk)
        o_ref[rows, :] = (o * z_ref[rows, :].astype(F32)).astype(BF16)


def _attn_prompt(qb, kb, vt, zs, kn2, lam_vecs, subln, *, heads, lam_init, tq, tk):
    b, t, n = qb.shape
    dh = n // heads
    assert t % tq == 0 and tq % tk == 0 and tk % MASK_CHUNK == 0
    nchains = 2 * (tq // tk)
    kern = functools.partial(_attn_prompt_kernel, lam_init=lam_init, tk=tk)
    return pl.pallas_call(
        kern,
        grid=(b, heads, t // tq),
        in_specs=[pl.BlockSpec((None, tq, dh), lambda bi, hi, qi: (bi, qi, hi)),
                  pl.BlockSpec((None, t, dh), lambda bi, hi, qi: (bi, 0, hi)),
                  pl.BlockSpec((None, dh, t), lambda bi, hi, qi: (bi, hi, 0)),
                  pl.BlockSpec((None, tq, dh), lambda bi, hi, qi: (bi, qi, hi)),
                  pl.BlockSpec((None,) + kn2.shape[1:], lambda bi, hi, qi: (bi, 0, 0, 0)),
                  pl.BlockSpec(lam_vecs.shape, lambda bi, hi, qi: (0, 0)),
                  pl.BlockSpec((1, dh), lambda bi, hi, qi: (0, 0))],
        out_specs=pl.BlockSpec((None, tq, dh), lambda bi, hi, qi: (bi, qi, hi)),
        out_shape=jax.ShapeDtypeStruct((b, t, n), BF16),
        scratch_shapes=[pltpu.VMEM((nchains, 1, tk), F32), pltpu.VMEM((nchains, SUBLANES, tk), F32),
                        pltpu.VMEM((nchains, dh, tk), F32), pltpu.VMEM((nchains, tk, dh), BF16)],
        compiler_params=_cparams("parallel", "parallel", "arbitrary"),
    )(qb, kb, vt, zs, kn2, lam_vecs, subln)


def _attn_sample_kernel(q_ref, kn_ref, vn_ref, ck_hbm, cv_hbm, z_ref, lam_ref, sw_ref, o_ref,
                        q2_sc, kst_sc, vst_sc, m_sc, l_sc, acc_sc, sem, *, layer, heads, lam_init, tkv):
    si = pl.program_id(1)
    ns = pl.num_programs(1)
    step = pl.program_id(0) * ns + si
    slot = step % 2
    tq, n = q_ref.shape
    dh = n // heads
    dq = dh // 2

    def fetch(step_, slot_):
        bi, blk = step_ // ns, step_ % ns
        return [pltpu.make_async_copy(cache.at[layer, bi, pl.ds(blk * tkv, tkv), hh, :], stage.at[slot_, hh],
                                      sem.at[which, slot_])
                for which, (cache, stage) in enumerate(((ck_hbm, kst_sc), (cv_hbm, vst_sc)))
                for hh in range(heads)]

    @pl.when(step == 0)
    def _():
        for copy in fetch(step, slot):
            copy.start()

    @pl.when(step + 1 < pl.num_programs(0) * ns)
    def _():
        for copy in fetch(step + 1, 1 - slot):
            copy.start()

    @pl.when(si == 0)
    def _():
        m_sc[...] = jnp.full(m_sc.shape, -jnp.inf, F32)
        l_sc[...] = jnp.zeros(l_sc.shape, F32)
        acc_sc[...] = jnp.zeros(acc_sc.shape, F32)
        for hh in range(heads):
            q = q_ref[:, hh * dh:(hh + 1) * dh]
            lane = lax.broadcasted_iota(jnp.int32, q.shape, 1)
            q2_sc[hh, 0:tq, :] = jnp.where(lane < dq, q, jnp.zeros_like(q))
            q2_sc[hh, tq:2 * tq, :] = jnp.where(lane >= dq, q, jnp.zeros_like(q))

    def update(kblks, vblks):
        hr = range(heads)
        s = [_dot_nt(q2_sc[hh], kblks[hh]) for hh in hr]
        m_old = [m_sc[hh] for hh in hr]
        m_new = [jnp.maximum(m_old[hh], jnp.max(s[hh], axis=-1, keepdims=True)) for hh in hr]
        alpha = [jnp.exp2(m_old[hh] - m_new[hh]) for hh in hr]
        p = [jnp.exp2(s[hh] - m_new[hh]) for hh in hr]
        pv = [_dot(p[hh].astype(BF16), vblks[hh]) for hh in hr]
        for hh in hr:
            l_sc[hh] = alpha[hh] * l_sc[hh] + jnp.sum(p[hh], axis=-1, keepdims=True)
            acc_sc[hh] = alpha[hh] * acc_sc[hh] + pv[hh]
            m_sc[hh] = m_new[hh]

    for copy in fetch(step, slot):
        copy.wait()
    update([kst_sc[slot, hh].astype(BF16) for hh in range(heads)],
           [vst_sc[slot, hh].astype(BF16) for hh in range(heads)])

    @pl.when(si == ns - 1)
    def _():
        lam = _lambda(lam_ref, lam_init)
        update([kn_ref[:, hh, :].astype(BF16) for hh in range(heads)],
               [vn_ref[:, hh, :].astype(BF16) for hh in range(heads)])
        for hh in range(heads):
            hs = slice(hh * dh, (hh + 1) * dh)
            on = acc_sc[hh] / l_sc[hh]
            o = on[:tq] - lam * on[tq:]
            o = o * lax.rsqrt(jnp.mean(o * o, axis=-1, keepdims=True) + RMS_EPS) * sw_ref[...] * (1.0 - lam_init)
            o_ref[:, hs] = (o * z_ref[:, hs].astype(F32)).astype(BF16)


def _attn_sample(qb, k_new, v_new, cache_k, cache_v, zs, lam_vecs, subln, *, layer, heads, lam_init, tkv):
    b, tq, n = qb.shape
    past = cache_k.shape[2]
    dh = n // heads
    assert past % tkv == 0 and cache_k.shape[3:] == (heads, dh) and k_new.shape[2:] == (tq, heads, dh)
    kern = functools.partial(_attn_sample_kernel, layer=layer, heads=heads, lam_init=lam_init, tkv=tkv)
    tile = pl.BlockSpec((None, tq, n), lambda bi, si: (bi, 0, 0))
    new = pl.BlockSpec((None, None, tq, heads, dh), lambda bi, si: (layer, bi, 0, 0, 0))
    in_hbm = pl.BlockSpec(memory_space=pl.ANY)
    return pl.pallas_call(
        kern,
        grid=(b, past // tkv),
        in_specs=[tile, new, new, in_hbm, in_hbm, tile,
                  pl.BlockSpec(lam_vecs.shape, lambda bi, si: (0, 0)),
                  pl.BlockSpec((1, dh), lambda bi, si: (0, 0))],
        out_specs=tile,
        out_shape=jax.ShapeDtypeStruct((b, tq, n), BF16),
        scratch_shapes=[pltpu.VMEM((heads, 2 * tq, dh), BF16),
                        pltpu.VMEM((2, heads, tkv, dh), F32), pltpu.VMEM((2, heads, tkv, dh), F32),
                        pltpu.VMEM((heads, 2 * tq, 1), F32), pltpu.VMEM((heads, 2 * tq, 1), F32),
                        pltpu.VMEM((heads, 2 * tq, dh), F32), pltpu.SemaphoreType.DMA((2, 2))],
        compiler_params=_cparams("arbitrary", "arbitrary"),
    )(qb, k_new, v_new, cache_k, cache_v, zs, lam_vecs, subln)


def _out_kernel(x_ref, o_ref, w_ref, mod_ref, nw_ref, y_ref):
    y_ref[...] = _residual_update(x_ref[...], o_ref, w_ref, mod_ref, nw_ref)


def _out_proj(x, prev, *, tm):
    b, t, d = x.shape
    pre_specs, pre_args, x_spec, x_shape = _fused_specs(prev, b, t, d, tm)
    return pl.pallas_call(
        _out_kernel,
        grid=(b, t // tm),
        in_specs=[pl.BlockSpec((None, tm, d), lambda bi, ti: (bi, ti, 0))] + pre_specs,
        out_specs=x_spec[0],
        out_shape=x_shape[0],
        compiler_params=_cparams("parallel", "parallel"),
    )(x, *pre_args)


def _pick_tile(t, target):
    tile = min(t, target)
    assert t % tile == 0
    return tile


def _rope_tables(pos):
    half = MASK_CHUNK // 2
    inv = 1.0 / (ROPE_THETA ** (jnp.arange(half, dtype=F32) / half))
    ang = pos.astype(F32)[:, None] * inv[None, :]
    cos, sin = jnp.cos(ang), jnp.sin(ang)
    reps = LANES // MASK_CHUNK
    return (jnp.tile(jnp.concatenate([cos, cos], axis=1), (1, reps)),
            jnp.tile(jnp.concatenate([-sin, sin], axis=1), (1, reps)))


def _trunk(x, ada, pos, conv_bufs, gdn_states, past_k, past_v, p):
    b, t, d = x.shape
    depth = ada.shape[0]
    gdn_heads = p["a_log_gdn"].shape[1]
    gdn_dk = p["onorm_gdn"].shape[1]
    dq = p["lam_q1"].shape[1]
    n_diff = p["w_out_diff"].shape[1]
    diff_heads = n_diff // (2 * dq)
    gdn_chunk = LANES
    n_conv = 3 * gdn_heads * gdn_dk
    tm = _pick_tile(t, 512)
    cos_t, sin_t = _rope_tables(pos)
    prompt = past_k is None
    caches = None
    states, convs = [], []
    prev = None
    for i in range(depth):
        j = i // 2
        mod = ada[i]
        nw_pre = p["norm_pre"][i][None, :]
        nw_post = p["norm_post"][i][None, :]
        if i % 2 == 0:
            a_log = jnp.pad(p["a_log_gdn"][j][None, :], ((0, 0), (0, LANES - gdn_heads)))
            dt_b = jnp.pad(p["dt_bias_gdn"][j][None, :], ((0, 0), (0, LANES - gdn_heads)))
            if conv_bufs is None:
                conv_init = jnp.zeros((b, SUBLANES, n_conv), F32)
                s0 = jnp.zeros((b, gdn_heads, gdn_dk, gdn_dk), F32)
            else:
                conv_init = jnp.pad(conv_bufs[j], ((0, 0), (SUBLANES - (CONV_WIDTH - 1), 0), (0, 0)))
                s0 = gdn_states[j]
            res = _gdn_in(x, mod, nw_pre, prev, p["w_main_gdn"], p["w_ab_gdn"][j], p["conv_gdn"][j],
                          conv_init, a_log, dt_b, layer=j, heads=gdn_heads, dk=gdn_dk, tm=tm,
                          chunk=min(gdn_chunk, tm))
            if prev is not None:
                x, res = res[0], res[1:]
            qkvz, gcol, cout = res
            if t < gdn_chunk:
                front = gdn_chunk - t
                qkvz = jnp.pad(qkvz, ((0, 0), (front, 0), (0, 0)))
                gcol = jnp.pad(gcol, ((0, 0), (front, 0), (0, 0)))
            o, st = _gdn(qkvz, gcol, s0, p["onorm_gdn"][j][None, :], heads=gdn_heads, dk=gdn_dk,
                         tb=_pick_tile(qkvz.shape[1], 512), chunk=gdn_chunk,
                         nbatch=2 if b % 2 == 0 else 1, group=2 if qkvz.shape[1] >= 2 * gdn_chunk else 1)
            o = o[:, -t:]
            convs.append(cout[:, SUBLANES - (CONV_WIDTH - 1):, :])
            states.append(st)
            w_out = p["w_out_gdn"]
        else:
            lam_init = 0.8 - 0.6 * math.exp(-0.3 * i)
            lam_vecs = jnp.stack([p["lam_q1"][j], p["lam_k1"][j], p["lam_q2"][j], p["lam_k2"][j]])
            res = _diff_in(x, mod, nw_pre, prev, p["w_in_diff"], cos_t, sin_t, caches, layer=j, heads=diff_heads,
                           tm=tm, v_transposed=prompt, q_scale=dq ** -0.5 * LOG2E)
            if prev is not None:
                x, res = res[0], res[1:]
            qb, kb, k_all, v_all, vb, zs, kn2 = res
            caches = (k_all, v_all)
            subln = p["subln_diff"][j][None, :]
            if prompt:
                o = _attn_prompt(qb, kb, vb, zs, kn2, lam_vecs, subln, heads=diff_heads, lam_init=lam_init,
                                 tq=_pick_tile(t, 4096), tk=_pick_tile(t, 256))
            else:
                o = _attn_sample(qb, k_all, v_all, past_k, past_v, zs, lam_vecs, subln, layer=j, heads=diff_heads,
                                 lam_init=lam_init, tkv=_pick_tile(past_k.shape[2], 1024))
            w_out = p["w_out_diff"]
        prev = (o, w_out, j, mod, nw_post)
    x = _out_proj(x, prev, tm=_pick_tile(t, 4 * tm))
    return x, jnp.stack(states), jnp.stack(convs), k_all, v_all


def kernel(x_prompt, x_sample, c_prompt, c_sample, state_gdn, cache_conv, cache_k, cache_v, norm_pre, norm_post, w_ada, b_ada, w_in_gdn, conv_gdn, a_log_gdn, dt_bias_gdn, onorm_gdn, w_out_gdn, w_in_diff, lam_q1, lam_k1, lam_q2, lam_k2, subln_diff, w_out_diff):
    gdn_heads = a_log_gdn.shape[1]
    n_main = w_in_gdn.shape[2] - 2 * gdn_heads
    w_ab = jnp.pad(w_in_gdn[:, :, n_main:], ((0, 0), (0, 0), (0, LANES - 2 * gdn_heads)))
    w_ab_hi = w_ab.astype(BF16)
    w_ab_lo = (w_ab - w_ab_hi.astype(F32)).astype(BF16)
    p = {"norm_pre": norm_pre, "norm_post": norm_post,
         "w_main_gdn": w_in_gdn.astype(BF16),
         "w_ab_gdn": jnp.concatenate([w_ab_hi, w_ab_lo], axis=-1),

         "conv_gdn": conv_gdn, "a_log_gdn": a_log_gdn,
         "dt_bias_gdn": dt_bias_gdn, "onorm_gdn": onorm_gdn, "w_out_gdn": w_out_gdn.astype(BF16),
         "w_in_diff": w_in_diff.astype(BF16), "lam_q1": lam_q1, "lam_k1": lam_k1, "lam_q2": lam_q2,
         "lam_k2": lam_k2, "subln_diff": subln_diff, "w_out_diff": w_out_diff.astype(BF16)}
    bp, tp, d = x_prompt.shape
    bs, ts, _ = x_sample.shape
    past = cache_k.shape[2]
    depth = w_ada.shape[0]
    ada = _ada(jnp.concatenate([c_prompt, c_sample], axis=0), w_ada, b_ada)
    ada = ada.reshape(depth, bp + bs, 3, d)
    y_p, st_p, conv_p, k_p, v_p = _trunk(x_prompt, ada[:, :bp], jnp.arange(tp), None, None, None, None, p)
    y_s, st_s, conv_s, k_s, v_s = _trunk(x_sample, ada[:, bp:], past + jnp.arange(ts), cache_conv, state_gdn,
                                         cache_k, cache_v, p)
    return (y_p, y_s, st_p, conv_p, k_p, v_p, st_s, conv_s, k_s, v_s)
```

```python
import functools
import math

import jax
import jax.numpy as jnp
from jax import lax
from jax.experimental import pallas as pl
from jax.experimental.pallas import tpu as pltpu

F32 = jnp.float32
BF16 = jnp.bfloat16
HIGHEST = lax.Precision.HIGHEST

RMS_EPS = 1e-6
L2_EPS = 1e-6
ROPE_THETA = 10000.0
CONV_WIDTH = 4
MASK_CHUNK = 64
LANES = 128
SUBLANES = 8
VMEM_LIMIT = 56 * 1024 * 1024
LOG2E = 1.4426950408889634
REF_SLACK = 1.01
MIN_NORMALISER = 1e-30
STAGE_GROUP = 32
RESIDENT = pl.Buffered(1)

NT_DIMS = (((1,), (1,)), ((), ()))


def _cparams(*sem):
    return pltpu.CompilerParams(dimension_semantics=sem, vmem_limit_bytes=VMEM_LIMIT)


def _silu(x):
    return x * jax.nn.sigmoid(x)


def _dot(a, b):
    return jnp.dot(a, b, preferred_element_type=F32)


def _dot_nt(a, b):
    return lax.dot_general(a, b, NT_DIMS, preferred_element_type=F32)


def _ada_kernel(c_ref, w_ref, b_ref, o_ref):
    act = _silu(c_ref[...])
    o_ref[...] = jnp.dot(act, w_ref[...], precision=HIGHEST, preferred_element_type=F32) + b_ref[...]


def _ada(c_all, w_ada, b_ada):
    depth, d, n = w_ada.shape
    bc = c_all.shape[0]
    tn = 512
    return pl.pallas_call(
        _ada_kernel,
        grid=(depth, n // tn),
        in_specs=[pl.BlockSpec((bc, d), lambda i, j: (0, 0)),
                  pl.BlockSpec((None, d, tn), lambda i, j: (i, 0, j)),
                  pl.BlockSpec((None, 1, tn), lambda i, j: (i, 0, j))],
        out_specs=pl.BlockSpec((None, bc, tn), lambda i, j: (i, 0, j)),
        out_shape=jax.ShapeDtypeStruct((depth, bc, n), F32),
        compiler_params=_cparams("parallel", "parallel"),
    )(c_all, w_ada, b_ada.reshape(depth, 1, n))


def _residual_update(x, o_ref, w_ref, mod_ref, nw_ref):
    y = _dot(o_ref[...], w_ref[...])
    gain = mod_ref[2:3, :] * nw_ref[...]
    return x + y * lax.rsqrt(jnp.mean(y * y, axis=-1, keepdims=True) + RMS_EPS) * gain


def _prenorm(x, mod_ref, nw_ref):
    gain = nw_ref[...] * (1.0 + mod_ref[1:2, :])
    return x * lax.rsqrt(jnp.mean(x * x, axis=-1, keepdims=True) + RMS_EPS) * gain + mod_ref[0:1, :]


def _layer_input(refs, fused, n_in):
    x_ref, mod_ref, nw_ref = refs[:3]
    pos = 7 if fused else 3
    own_in = refs[pos:pos + n_in]
    pos += n_in
    x = x_ref[...]
    if fused:
        x = _residual_update(x, *refs[3:7])
        refs[pos][...] = x
        pos += 1
    return x, mod_ref, nw_ref, own_in, refs[pos:]


def _fused_specs(prev, b, t, d, tm):
    o, w, layer, mod, nw = prev
    n = o.shape[2]
    in_specs = [pl.BlockSpec((None, tm, n), lambda bi, ti: (bi, ti, 0)),
                pl.BlockSpec((None, n, d), lambda bi, ti: (layer, 0, 0), pipeline_mode=RESIDENT),
                pl.BlockSpec((None, 3, d), lambda bi, ti: (bi, 0, 0)),
                pl.BlockSpec((1, d), lambda bi, ti: (0, 0))]
    return (in_specs, [o, w, mod, nw], [pl.BlockSpec((None, tm, d), lambda bi, ti: (bi, ti, 0))],
            [jax.ShapeDtypeStruct((b, t, d), F32)])


def _conv_silu(u, tail, cw):
    assert cw.shape[0] == 4
    e = jnp.concatenate([tail, u], axis=0)
    e1 = pltpu.roll(e, 1, axis=0)
    far = pltpu.roll(e * cw[1:2, :] + e1 * cw[0:1, :], 2, axis=0)
    y = e * cw[3:4, :] + e1 * cw[2:3, :] + far
    return _silu(y[SUBLANES:])


def _gdn_in_kernel(*refs, fused, heads, dk, chunk, tn):
    x, mod_ref, nw_ref, own_in, (qkvz_ref, gcol_ref, cout_ref, tail_sc) = _layer_input(refs, fused, 6)
    w_ref, wab_ref, cw_ref, cinit_ref, alog_ref, dtb_ref = own_in
    t = pl.program_id(1)
    tm = x.shape[0]
    nb = heads * dk
    n = 4 * nb

    @pl.when(t == 0)
    def _():
        tail_sc[...] = cinit_ref[...]

    h = _prenorm(x, mod_ref, nw_ref)
    hb = h.astype(BF16)
    h_lo = (h - hb.astype(F32)).astype(BF16)
    hi_lo = _dot(hb, wab_ref[...])
    ab = hi_lo[:, :LANES] + hi_lo[:, LANES:] + _dot(h_lo, wab_ref[:, :LANES])
    g = -jnp.exp(alog_ref[...]) * jax.nn.softplus(ab + dtb_ref[...])
    row = lax.broadcasted_iota(jnp.int32, g.shape, 0) % chunk
    s = 1
    while s < chunk:
        g = g + jnp.where(row >= s, pltpu.roll(g, s, axis=0), 0.0)
        s *= 2
    lane = lax.broadcasted_iota(jnp.int32, g.shape, 1)
    gcol_ref[...] = jnp.where(lane < heads, g, jax.nn.sigmoid(ab))

    for grp in range(n // tn):
        cs = slice(grp * tn, (grp + 1) * tn)
        kind = (grp * tn) // nb
        u = _dot(hb, w_ref[:, cs])
        if kind == 3:
            qkvz_ref[:, cs] = _silu(u).astype(BF16)
            continue
        y = _conv_silu(u, tail_sc[:, cs], cw_ref[:, cs])
        tail_sc[:, cs] = u[tm - SUBLANES:tm]
        cout_ref[:, cs] = u[tm - SUBLANES:tm]
        if kind == 2:
            qkvz_ref[:, cs] = y.astype(BF16)
            continue
        scale = dk ** -0.5 if kind == 0 else 1.0
        for hh in range(tn // dk):
            ys = y[:, hh * dk:(hh + 1) * dk]
            inv = lax.rsqrt(jnp.sum(ys * ys, axis=-1, keepdims=True) + L2_EPS) * scale
            qkvz_ref[:, grp * tn + hh * dk:grp * tn + (hh + 1) * dk] = (ys * inv).astype(BF16)


def _gdn_in(x, mod, nw, prev, w_main, w_ab, conv_w, conv_init, a_log, dt_bias, *, layer, heads, dk, tm, chunk):
    b, t, d = x.shape
    nb = heads * dk
    n = 4 * nb
    tn = 4 * dk
    assert w_main.shape[2] >= n and nb % tn == 0 and t % tm == 0 and tm % SUBLANES == 0 and tm % chunk == 0
    fused = prev is not None
    pre_specs, pre_args, x_spec, x_shape = _fused_specs(prev, b, t, d, tm) if fused else ([], [], [], [])
    kern = functools.partial(_gdn_in_kernel, fused=fused, heads=heads, dk=dk, chunk=chunk, tn=tn)
    return pl.pallas_call(
        kern,
        grid=(b, t // tm),
        in_specs=[pl.BlockSpec((None, tm, d), lambda bi, ti: (bi, ti, 0)),
                  pl.BlockSpec((None, 3, d), lambda bi, ti: (bi, 0, 0)),
                  pl.BlockSpec((1, d), lambda bi, ti: (0, 0))] + pre_specs + [
                  pl.BlockSpec((None, d, w_main.shape[2]), lambda bi, ti: (layer, 0, 0), pipeline_mode=RESIDENT),
                  pl.BlockSpec((d, 2 * LANES), lambda bi, ti: (0, 0)),
                  pl.BlockSpec((CONV_WIDTH, 3 * nb), lambda bi, ti: (0, 0)),
                  pl.BlockSpec((None, SUBLANES, 3 * nb), lambda bi, ti: (bi, 0, 0)),
                  pl.BlockSpec((1, LANES), lambda bi, ti: (0, 0)),
                  pl.BlockSpec((1, LANES), lambda bi, ti: (0, 0))],
        out_specs=x_spec + [pl.BlockSpec((None, tm, n), lambda bi, ti: (bi, ti, 0)),
                            pl.BlockSpec((None, tm, LANES), lambda bi, ti: (bi, ti, 0)),
                            pl.BlockSpec((None, SUBLANES, 3 * nb), lambda bi, ti: (bi, 0, 0))],
        out_shape=x_shape + [jax.ShapeDtypeStruct((b, t, n), BF16),
                             jax.ShapeDtypeStruct((b, t, LANES), F32),
                             jax.ShapeDtypeStruct((b, SUBLANES, 3 * nb), F32)],
        scratch_shapes=[pltpu.VMEM((SUBLANES, 3 * nb), F32)],
        compiler_params=_cparams("parallel", "arbitrary"),
    )(x, mod, nw, *pre_args, w_main, w_ab, conv_w, conv_init, a_log, dt_bias)


def _gdn_kernel(q_ref, k_ref, v_ref, z_ref, gcol_ref, s0_ref, ow_ref, o_ref, s_ref, *, heads, dk, chunk, group):
    t = pl.program_id(1)
    nbatch, tb = q_ref.shape[0], q_ref.shape[1]
    c = chunk

    @pl.when(t == 0)
    def _():
        s_ref[...] = s0_ref[...]

    ii = lax.broadcasted_iota(jnp.int32, (c, c), 0)
    jj = lax.broadcasted_iota(jnp.int32, (c, c), 1)
    incl = ii >= jj
    strict = ii > jj
    eye = (ii == jj).astype(F32)
    off_masks = []
    s = 1
    while s < c:
        off_masks.append((ii // (2 * s) == jj // (2 * s)) & (ii // s != jj // s) & strict)
        s *= 2

    def chunk_group(ci, carry):
        rows = [pl.ds(pl.multiple_of((ci * group + k) * c, c), c) for k in range(group)]
        hsl = [slice(hh * dk, (hh + 1) * dk) for hh in range(heads)]
        units = [(bi, k, hh) for bi in range(nbatch) for k in range(group) for hh in range(heads)]
        nu = range(len(units))
        gall = {(bi, k): gcol_ref[bi, rows[k], :] for bi in range(nbatch) for k in range(group)}
        g = [gall[bi, k][:, hh:hh + 1] for bi, k, hh in units]
        beta = [gall[bi, k][:, heads + hh:heads + hh + 1] for bi, k, hh in units]
        kbf = [k_ref[bi, rows[k], hsl[hh]] for bi, k, hh in units]
        qbf = [q_ref[bi, rows[k], hsl[hh]] for bi, k, hh in units]
        kf = [x.astype(F32) for x in kbf]
        kb = [x * y for x, y in zip(kf, beta)]
        gb = [jnp.broadcast_to(x, (c, c)) for x in g]
        decay = [jnp.exp(jnp.where(incl, x - x.T, -jnp.inf)) for x in gb]
        a = [jnp.where(strict, _dot_nt(x.astype(BF16), y) * d, 0.0) for x, y, d in zip(kb, kbf, decay)]
        qk = [(_dot_nt(x, y) * d).astype(BF16) for x, y, d in zip(qbf, kbf, decay)]
        p = [eye - jnp.where(off_masks[0], x, 0.0) for x in a]
        for off in off_masks[1:]:
            pb = [x.astype(BF16) for x in p]
            x = [_dot(jnp.where(off, ai, 0.0).astype(BF16), pi).astype(BF16) for ai, pi in zip(a, pb)]
            p = [pi - _dot(pbi, xi) for pi, pbi, xi in zip(p, pb, x)]
        eg = [jnp.exp(x) for x in g]
        rhs = [jnp.concatenate([v_ref[bi, rows[k], hsl[hh]].astype(F32) * beta[u], kb[u] * eg[u]],
                               axis=1).astype(BF16) for u, (bi, k, hh) in enumerate(units)]
        sol = [_dot(x.astype(BF16), y) for x, y in zip(p, rhs)]
        wq = [jnp.concatenate([sol[u][:, dk:].astype(BF16), (qbf[u].astype(F32) * eg[u]).astype(BF16)], axis=0)
              for u in nu]
        g_last = [x[c - 1:c, :] for x in g]
        kd_t = [(kf[u] * jnp.exp(g_last[u] - g[u])).T.astype(BF16) for u in nu]
        e_last = [jnp.exp(x) for x in g_last]
        for k in range(group):
            us = [u for u in nu if units[u][1] == k]
            s_old = [s_ref[units[u][0], units[u][2]] for u in us]
            r1 = [_dot(wq[u], so.astype(BF16)) for u, so in zip(us, s_old)]
            ub = [(sol[u][:, :dk] - r[:c]).astype(BF16) for u, r in zip(us, r1)]
            for u, so, ubi in zip(us, s_old, ub):
                s_ref[units[u][0], units[u][2]] = so * e_last[u] + _dot(kd_t[u], ubi)
            o = [r[c:] + _dot(qk[u], ubi) for u, r, ubi in zip(us, r1, ub)]
            for u, oi in zip(us, o):
                bi, _, hh = units[u]
                on = oi * lax.rsqrt(jnp.mean(oi * oi, axis=-1, keepdims=True) + RMS_EPS) * ow_ref[...]
                o_ref[bi, rows[k], hsl[hh]] = (on * z_ref[bi, rows[k], hsl[hh]].astype(F32)).astype(BF16)
        return carry

    lax.fori_loop(0, tb // (c * group), chunk_group, 0)


def _gdn(qkvz, gcol, s0, onorm, *, heads, dk, tb, chunk, nbatch, group):
    b, t, n = qkvz.shape
    nb = n // 4
    assert t % tb == 0 and tb % (chunk * group) == 0 and nb == heads * dk and b % nbatch == 0
    kern = functools.partial(_gdn_kernel, heads=heads, dk=dk, chunk=chunk, group=group)
    col = lambda jcol: pl.BlockSpec((nbatch, tb, nb), lambda bi, ti: (bi, ti, jcol))
    return pl.pallas_call(
        kern,
        grid=(b // nbatch, t // tb),
        in_specs=[col(0), col(1), col(2), col(3),
                  pl.BlockSpec((nbatch, tb, LANES), lambda bi, ti: (bi, ti, 0)),
                  pl.BlockSpec((nbatch, heads, dk, dk), lambda bi, ti: (bi, 0, 0, 0)),
                  pl.BlockSpec((1, dk), lambda bi, ti: (0, 0))],
        out_specs=[pl.BlockSpec((nbatch, tb, nb), lambda bi, ti: (bi, ti, 0)),
                   pl.BlockSpec((nbatch, heads, dk, dk), lambda bi, ti: (bi, 0, 0, 0))],
        out_shape=[jax.ShapeDtypeStruct((b, t, nb), BF16),
                   jax.ShapeDtypeStruct((b, heads, dk, dk), F32)],
        compiler_params=_cparams("parallel", "arbitrary"),
    )(qkvz, qkvz, qkvz, qkvz, gcol, s0, onorm)


def _rope(x, cos, sin_signed, lane_lo):
    half = MASK_CHUNK // 2
    swapped = jnp.where(lane_lo, pltpu.roll(x, LANES - half, axis=1), pltpu.roll(x, half, axis=1))
    return x * cos + swapped * sin_signed


def _diff_in_kernel(*refs, fused, aliased, layers, v_transposed, q_scale, tn):
    x, mod_ref, nw_ref, own_in, outs = _layer_input(refs, fused, 5 if aliased else 3)
    w_ref, cos_ref, sin_ref = own_in[:3]
    qb_ref, kb_ref, kf_hbm, vf_hbm, vb_ref, zs_ref, kn2_ref, kst_sc, vst_sc, sem = outs
    n = w_ref.shape[1]
    nb = n // 4
    tm = x.shape[0]
    tiles = pl.num_programs(1)
    step = pl.program_id(0) * tiles + pl.program_id(1)
    slot = step % 2

    def cache_copies(slot_, step_):
        bi, ti = step_ // tiles, step_ % tiles
        return [pltpu.make_async_copy(stage.at[slot_, :, pl.ds(hh * LANES, LANES)],
                                      cache.at[layer, bi, pl.ds(ti * tm, tm), hh, :],
                                      sem.at[which, slot_])
                for which, (stage, cache) in enumerate(((kst_sc, kf_hbm), (vst_sc, vf_hbm)))
                for layer in layers
                for hh in range(nb // LANES)]

    hb = _prenorm(x, mod_ref, nw_ref).astype(BF16)
    cos = cos_ref[...]
    sin = sin_ref[...]
    lane_lo = (lax.broadcasted_iota(jnp.int32, cos.shape, 1) % MASK_CHUNK) < (MASK_CHUNK // 2)
    kn2 = jnp.zeros((1, LANES), F32)
    for grp in range(n // tn):
        kind = (grp * tn) // nb
        lo = grp * tn - kind * nb
        u = _dot(hb, w_ref[:, grp * tn:(grp + 1) * tn])
        squares = []
        for s in range(tn // LANES):
            us = u[:, s * LANES:(s + 1) * LANES]
            cols = slice(lo + s * LANES, lo + (s + 1) * LANES)
            if kind == 0:
                qb_ref[:, cols] = (_rope(us, cos, sin, lane_lo) * q_scale).astype(BF16)
            elif kind == 1:
                r = _rope(us, cos, sin, lane_lo)
                kst_sc[slot, :, cols] = r
                kb_ref[:, cols] = r.astype(BF16)
                squares.append(r * r)
            elif kind == 2:
                vst_sc[slot, :, cols] = us
            else:
                zs_ref[:, cols] = _silu(us).astype(BF16)
        if kind == 1:
            group_sum = ((lax.broadcasted_iota(jnp.int32, (tn, LANES), 0) + lo) // MASK_CHUNK
                         == lax.broadcasted_iota(jnp.int32, (tn, LANES), 1)).astype(BF16)
            n2 = _dot(jnp.concatenate(squares, axis=1).astype(BF16), group_sum)
            kn2 = jnp.maximum(kn2, jnp.max(n2, axis=0, keepdims=True))
        if kind == 2:
            if v_transposed:
                vb_ref[lo:lo + tn, :] = u.T.astype(BF16)
            else:
                vb_ref[:, lo:lo + tn] = u.astype(BF16)
    kn2_ref[...] = kn2

    for copy in cache_copies(slot, step):
        copy.start()

    @pl.when(step > 0)
    def _():
        for copy in cache_copies(1 - slot, step - 1):
            copy.wait()

    @pl.when(step == pl.num_programs(0) * tiles - 1)
    def _():
        for copy in cache_copies(slot, step):
            copy.wait()


def _diff_in(x, mod, nw, prev, w, cos_t, sin_t, caches, *, layer, heads, tm, v_transposed, q_scale):
    b, t, d = x.shape
    n_layers = w.shape[0]
    nb = w.shape[2] // 4
    tn = 4 * LANES
    assert t % tm == 0 and nb == heads * LANES and nb % tn == 0
    fused = prev is not None
    aliased = caches is not None
    pre_specs, pre_args, x_spec, x_shape = _fused_specs(prev, b, t, d, tm) if fused else ([], [], [], [])
    kern = functools.partial(_diff_in_kernel, fused=fused, aliased=aliased,
                             layers=(layer,) if aliased else tuple(range(n_layers)),
                             v_transposed=v_transposed, q_scale=q_scale, tn=tn)
    tile = pl.BlockSpec((None, tm, nb), lambda bi, ti: (bi, ti, 0))
    in_hbm = pl.BlockSpec(memory_space=pl.ANY)
    cache_shape = jax.ShapeDtypeStruct((n_layers, b, t, heads, LANES), F32)
    if v_transposed:
        vb_spec = pl.BlockSpec((None, nb, tm), lambda bi, ti: (bi, 0, ti))
        vb_shape = jax.ShapeDtypeStruct((b, nb, t), BF16)
    else:
        vb_spec, vb_shape = tile, jax.ShapeDtypeStruct((b, t, nb), BF16)
    n_in = 3 + len(pre_args) + 3
    return pl.pallas_call(
        kern,
        grid=(b, t // tm),
        in_specs=[pl.BlockSpec((None, tm, d), lambda bi, ti: (bi, ti, 0)),
                  pl.BlockSpec((None, 3, d), lambda bi, ti: (bi, 0, 0)),
                  pl.BlockSpec((1, d), lambda bi, ti: (0, 0))] + pre_specs + [
                  pl.BlockSpec((None, d, 4 * nb), lambda bi, ti: (layer, 0, 0), pipeline_mode=RESIDENT),
                  pl.BlockSpec((tm, LANES), lambda bi, ti: (ti, 0)),
                  pl.BlockSpec((tm, LANES), lambda bi, ti: (ti, 0))] + ([in_hbm, in_hbm] if aliased else []),
        out_specs=x_spec + [tile, tile, in_hbm, in_hbm, vb_spec, tile,
                            pl.BlockSpec((None, None, 1, LANES), lambda bi, ti: (bi, ti, 0, 0))],
        out_shape=x_shape + [jax.ShapeDtypeStruct((b, t, nb), BF16),
                             jax.ShapeDtypeStruct((b, t, nb), BF16),
                             cache_shape,
                             cache_shape,
                             vb_shape,
                             jax.ShapeDtypeStruct((b, t, nb), BF16),
                             jax.ShapeDtypeStruct((b, t // tm, 1, LANES), F32)],
        input_output_aliases={n_in: 2 + len(x_shape), n_in + 1: 3 + len(x_shape)} if aliased else {},
        scratch_shapes=[pltpu.VMEM((2, tm, nb), F32), pltpu.VMEM((2, tm, nb), F32),
                        pltpu.SemaphoreType.DMA((2, 2))],
        compiler_params=_cparams("arbitrary", "arbitrary"),
    )(x, mod, nw, *pre_args, w, cos_t, sin_t, *(caches if aliased else ()))


def _lambda(lam_ref, lam_init):
    l1 = jnp.sum(lam_ref[0:1, :] * lam_ref[1:2, :], axis=-1, keepdims=True)
    l2 = jnp.sum(lam_ref[2:3, :] * lam_ref[3:4, :], axis=-1, keepdims=True)
    return jnp.exp(l1) - jnp.exp(l2) + lam_init


def _attn_prompt_kernel(q_ref, k_ref, vt_ref, z_ref, kn2_ref, lam_ref, sw_ref, o_ref,
                        m_sc, l_sc, acc_sc, qz_sc, *, lam_init, tk):
    qi = pl.program_id(2)
    tq, dh = q_ref.shape
    dq = dh // 2
    nsub = tq // tk
    chains = [(sub, comp) for sub in range(nsub) for comp in range(2)]
    everyone = list(range(len(chains)))
    n_full = (qi * tq) // tk
    qz = []
    for sub, comp in chains:
        q = q_ref[sub * tk:(sub + 1) * tk, :]
        lane = lax.broadcasted_iota(jnp.int32, q.shape, 1)
        qz.append(jnp.where((lane < dq) if comp == 0 else (lane >= dq), q, jnp.zeros_like(q)))

    kn2_max = jnp.max(kn2_ref[...], axis=0)
    kn2_lane = lax.broadcasted_iota(jnp.int32, kn2_max.shape, 1)
    kmax2 = [jnp.max(jnp.where(kn2_lane == 2 * pl.program_id(1) + comp, kn2_max, 0.0), axis=1, keepdims=True)
             for comp in range(2)]

    ones = jnp.ones((SUBLANES, dh), BF16)
    ref_rows = []
    for c, (sub, comp) in enumerate(chains):
        qf = qz[c].astype(F32)
        qn2 = _dot_nt(ones, (qf * qf).astype(BF16))[0:1, :]
        ref_rows.append(REF_SLACK * jnp.sqrt(qn2 * kmax2[comp]))

    visible = (lax.broadcasted_iota(jnp.int32, (tk, tk), 0) // MASK_CHUNK
               <= lax.broadcasted_iota(jnp.int32, (tk, tk), 1) // MASK_CHUNK)

    def blocks(kb):
        ks = k_ref[pl.ds(pl.multiple_of(kb * tk, tk), tk), :]
        vt = vt_ref[:, pl.ds(pl.multiple_of(kb * tk, tk), tk)]
        return ks, vt

    def fast_step(kb, active, masked_sub):
        ks, vt = blocks(kb)
        for g0 in range(0, len(active), STAGE_GROUP):
            grp = active[g0:g0 + STAGE_GROUP]
            s = [_dot_nt(ks, qz[c]) for c in grp]
            p = [jnp.exp2(si - ref_rows[c]) for c, si in zip(grp, s)]
            p = [jnp.where(visible, pi, 0.0) if chains[c][0] == masked_sub else pi for c, pi in zip(grp, p)]
            for c, pi in zip(grp, p):
                l_sc[c] += jnp.sum(pi.reshape(tk // SUBLANES, SUBLANES, tk), axis=0)
            pv = [_dot(vt, pi.astype(BF16)) for pi in p]
            for c, pvi in zip(grp, pv):
                acc_sc[c] += pvi

    def online_chain(c, cry):
        n_blocks = n_full + c // 2 + 1
        q_c = qz_sc[c]

        def block(kb, cry2):
            ks, vt = blocks(kb)
            s = _dot_nt(ks, q_c)
            s = jnp.where(jnp.logical_or(kb < n_blocks - 1, visible), s, -jnp.inf)
            m_old = m_sc[c]
            m_new = jnp.maximum(m_old, jnp.max(s, axis=0, keepdims=True))
            alpha = jnp.exp2(m_old - m_new)
            p = jnp.exp2(s - m_new)
            l_sc[c, 0:1, :] = alpha * l_sc[c, 0:1, :] + jnp.sum(p, axis=0, keepdims=True)
            acc_sc[c] = alpha * acc_sc[c] + _dot(vt, p.astype(BF16))
            m_sc[c] = m_new
            return cry2
        return lax.fori_loop(0, n_blocks, block, cry)

    def clear():
        l_sc[...] = jnp.zeros(l_sc.shape, F32)
        acc_sc[...] = jnp.zeros(acc_sc.shape, F32)

    def normalisers():
        return [jnp.sum(l_sc[c], axis=0, keepdims=True) for c in everyone]

    clear()
    unroll = min(nsub, 4)

    def body(i, cry):
        for u in range(unroll):
            fast_step(i * unroll + u, everyone, None)
        return cry
    lax.fori_loop(0, n_full // unroll, body, 0)
    for d in range(nsub):
        fast_step(n_full + d, [c for c in everyone if chains[c][0] >= d], d)
    l_min = functools.reduce(jnp.minimum, normalisers())

    @pl.when(jnp.min(l_min) < MIN_NORMALISER)
    def _():
        clear()
        m_sc[...] = jnp.full(m_sc.shape, -jnp.inf, F32)
        for c in everyone:
            qz_sc[c] = qz[c]
        lax.fori_loop(0, len(chains), online_chain, 0)

    lam = _lambda(lam_ref, lam_init)
    l = normalisers()
    for sub in range(nsub):
        c1, c2 = 2 * sub, 2 * sub + 1
        o_t = acc_sc[c1] * (1.0 / l[c1]) - acc_sc[c2] * (lam / l[c2])
        o = o_t.T
        gain = sw_ref[...] * (1.0 - lam_init)
        o = o * lax.rsqrt(jnp.mean(o * o, axis=-1, keepdims=True) + RMS_EPS) * gain
        rows = slice(sub * tk, (sub + 1) * tk)
        o_ref[rows, :] = (o * z_ref[rows, :].astype(F32)).astype(BF16)


def _attn_prompt(qb, kb, vt, zs, kn2, lam_vecs, subln, *, heads, lam_init, tq, tk):
    b, t, n = qb.shape
    dh = n // heads
    assert t % tq == 0 and tq % tk == 0 and tk % MASK_CHUNK == 0
    nchains = 2 * (tq // tk)
    kern = functools.partial(_attn_prompt_kernel, lam_init=lam_init, tk=tk)
    return pl.pallas_call(
        kern,
        grid=(b, heads, t // tq),
        in_specs=[pl.BlockSpec((None, tq, dh), lambda bi, hi, qi: (bi, qi, hi)),
                  pl.BlockSpec((None, t, dh), lambda bi, hi, qi: (bi, 0, hi)),
                  pl.BlockSpec((None, dh, t), lambda bi, hi, qi: (bi, hi, 0)),
                  pl.BlockSpec((None, tq, dh), lambda bi, hi, qi: (bi, qi, hi)),
                  pl.BlockSpec((None,) + kn2.shape[1:], lambda bi, hi, qi: (bi, 0, 0, 0)),
                  pl.BlockSpec(lam_vecs.shape, lambda bi, hi, qi: (0, 0)),
                  pl.BlockSpec((1, dh), lambda bi, hi, qi: (0, 0))],
        out_specs=pl.BlockSpec((None, tq, dh), lambda bi, hi, qi: (bi, qi, hi)),
        out_shape=jax.ShapeDtypeStruct((b, t, n), BF16),
        scratch_shapes=[pltpu.VMEM((nchains, 1, tk), F32), pltpu.VMEM((nchains, SUBLANES, tk), F32),
                        pltpu.VMEM((nchains, dh, tk), F32), pltpu.VMEM((nchains, tk, dh), BF16)],
        compiler_params=_cparams("parallel", "parallel", "arbitrary"),
    )(qb, kb, vt, zs, kn2, lam_vecs, subln)


def _attn_sample_kernel(q_ref, kn_ref, vn_ref, ck_hbm, cv_hbm, z_ref, lam_ref, sw_ref, o_ref,
                        q2_sc, kst_sc, vst_sc, m_sc, l_sc, acc_sc, sem, *, layer, heads, lam_init, tkv):
    si = pl.program_id(1)
    ns = pl.num_programs(1)
    step = pl.program_id(0) * ns + si
    slot = step % 2
    tq, n = q_ref.shape
    dh = n // heads
    dq = dh // 2

    def fetch(step_, slot_):
        bi, blk = step_ // ns, step_ % ns
        return [pltpu.make_async_copy(cache.at[layer, bi, pl.ds(blk * tkv, tkv), hh, :], stage.at[slot_, hh],
                                      sem.at[which, slot_])
                for which, (cache, stage) in enumerate(((ck_hbm, kst_sc), (cv_hbm, vst_sc)))
                for hh in range(heads)]

    @pl.when(step == 0)
    def _():
        for copy in fetch(step, slot):
            copy.start()

    @pl.when(step + 1 < pl.num_programs(0) * ns)
    def _():
        for copy in fetch(step + 1, 1 - slot):
            copy.start()

    @pl.when(si == 0)
    def _():
        m_sc[...] = jnp.full(m_sc.shape, -jnp.inf, F32)
        l_sc[...] = jnp.zeros(l_sc.shape, F32)
        acc_sc[...] = jnp.zeros(acc_sc.shape, F32)
        for hh in range(heads):
            q = q_ref[:, hh * dh:(hh + 1) * dh]
            lane = lax.broadcasted_iota(jnp.int32, q.shape, 1)
            q2_sc[hh, 0:tq, :] = jnp.where(lane < dq, q, jnp.zeros_like(q))
            q2_sc[hh, tq:2 * tq, :] = jnp.where(lane >= dq, q, jnp.zeros_like(q))

    def update(kblks, vblks):
        hr = range(heads)
        s = [_dot_nt(q2_sc[hh], kblks[hh]) for hh in hr]
        m_old = [m_sc[hh] for hh in hr]
        m_new = [jnp.maximum(m_old[hh], jnp.max(s[hh], axis=-1, keepdims=True)) for hh in hr]
        alpha = [jnp.exp2(m_old[hh] - m_new[hh]) for hh in hr]
        p = [jnp.exp2(s[hh] - m_new[hh]) for hh in hr]
        pv = [_dot(p[hh].astype(BF16), vblks[hh]) for hh in hr]
        for hh in hr:
            l_sc[hh] = alpha[hh] * l_sc[hh] + jnp.sum(p[hh], axis=-1, keepdims=True)
            acc_sc[hh] = alpha[hh] * acc_sc[hh] + pv[hh]
            m_sc[hh] = m_new[hh]

    for copy in fetch(step, slot):
        copy.wait()
    update([kst_sc[slot, hh].astype(BF16) for hh in range(heads)],
           [vst_sc[slot, hh].astype(BF16) for hh in range(heads)])

    @pl.when(si == ns - 1)
    def _():
        lam = _lambda(lam_ref, lam_init)
        update([kn_ref[:, hh, :].astype(BF16) for hh in range(heads)],
               [vn_ref[:, hh, :].astype(BF16) for hh in range(heads)])
        for hh in range(heads):
            hs = slice(hh * dh, (hh + 1) * dh)
            on = acc_sc[hh] / l_sc[hh]
            o = on[:tq] - lam * on[tq:]
            o = o * lax.rsqrt(jnp.mean(o * o, axis=-1, keepdims=True) + RMS_EPS) * sw_ref[...] * (1.0 - lam_init)
            o_ref[:, hs] = (o * z_ref[:, hs].astype(F32)).astype(BF16)


def _attn_sample(qb, k_new, v_new, cache_k, cache_v, zs, lam_vecs, subln, *, layer, heads, lam_init, tkv):
    b, tq, n = qb.shape
    past = cache_k.shape[2]
    dh = n // heads
    assert past % tkv == 0 and cache_k.shape[3:] == (heads, dh) and k_new.shape[2:] == (tq, heads, dh)
    kern = functools.partial(_attn_sample_kernel, layer=layer, heads=heads, lam_init=lam_init, tkv=tkv)
    tile = pl.BlockSpec((None, tq, n), lambda bi, si: (bi, 0, 0))
    new = pl.BlockSpec((None, None, tq, heads, dh), lambda bi, si: (layer, bi, 0, 0, 0))
    in_hbm = pl.BlockSpec(memory_space=pl.ANY)
    return pl.pallas_call(
        kern,
        grid=(b, past // tkv),
        in_specs=[tile, new, new, in_hbm, in_hbm, tile,
                  pl.BlockSpec(lam_vecs.shape, lambda bi, si: (0, 0)),
                  pl.BlockSpec((1, dh), lambda bi, si: (0, 0))],
        out_specs=tile,
        out_shape=jax.ShapeDtypeStruct((b, tq, n), BF16),
        scratch_shapes=[pltpu.VMEM((heads, 2 * tq, dh), BF16),
                        pltpu.VMEM((2, heads, tkv, dh), F32), pltpu.VMEM((2, heads, tkv, dh), F32),
                        pltpu.VMEM((heads, 2 * tq, 1), F32), pltpu.VMEM((heads, 2 * tq, 1), F32),
                        pltpu.VMEM((heads, 2 * tq, dh), F32), pltpu.SemaphoreType.DMA((2, 2))],
        compiler_params=_cparams("arbitrary", "arbitrary"),
    )(qb, k_new, v_new, cache_k, cache_v, zs, lam_vecs, subln)


def _out_kernel(x_ref, o_ref, w_ref, mod_ref, nw_ref, y_ref):
    y_ref[...] = _residual_update(x_ref[...], o_ref, w_ref, mod_ref, nw_ref)


def _out_proj(x, prev, *, tm):
    b, t, d = x.shape
    pre_specs, pre_args, x_spec, x_shape = _fused_specs(prev, b, t, d, tm)
    return pl.pallas_call(
        _out_kernel,
        grid=(b, t // tm),
        in_specs=[pl.BlockSpec((None, tm, d), lambda bi, ti: (bi, ti, 0))] + pre_specs,
        out_specs=x_spec[0],
        out_shape=x_shape[0],
        compiler_params=_cparams("parallel", "parallel"),
    )(x, *pre_args)


def _pick_tile(t, target):
    tile = min(t, target)
    assert t % tile == 0
    return tile


def _rope_tables(pos):
    half = MASK_CHUNK // 2
    inv = 1.0 / (ROPE_THETA ** (jnp.arange(half, dtype=F32) / half))
    ang = pos.astype(F32)[:, None] * inv[None, :]
    cos, sin = jnp.cos(ang), jnp.sin(ang)
    reps = LANES // MASK_CHUNK
    return (jnp.tile(jnp.concatenate([cos, cos], axis=1), (1, reps)),
            jnp.tile(jnp.concatenate([-sin, sin], axis=1), (1, reps)))


def _trunk(x, ada, pos, conv_bufs, gdn_states, past_k, past_v, p):
    b, t, d = x.shape
    depth = ada.shape[0]
    gdn_heads = p["a_log_gdn"].shape[1]
    gdn_dk = p["onorm_gdn"].shape[1]
    dq = p["lam_q1"].shape[1]
    n_diff = p["w_out_diff"].shape[1]
    diff_heads = n_diff // (2 * dq)
    gdn_chunk = LANES
    n_conv = 3 * gdn_heads * gdn_dk
    tm = _pick_tile(t, 512)
    cos_t, sin_t = _rope_tables(pos)
    prompt = past_k is None
    caches = None
    states, convs = [], []
    prev = None
    for i in range(depth):
        j = i // 2
        mod = ada[i]
        nw_pre = p["norm_pre"][i][None, :]
        nw_post = p["norm_post"][i][None, :]
        if i % 2 == 0:
            a_log = jnp.pad(p["a_log_gdn"][j][None, :], ((0, 0), (0, LANES - gdn_heads)))
            dt_b = jnp.pad(p["dt_bias_gdn"][j][None, :], ((0, 0), (0, LANES - gdn_heads)))
            if conv_bufs is None:
                conv_init = jnp.zeros((b, SUBLANES, n_conv), F32)
                s0 = jnp.zeros((b, gdn_heads, gdn_dk, gdn_dk), F32)
            else:
                conv_init = jnp.pad(conv_bufs[j], ((0, 0), (SUBLANES - (CONV_WIDTH - 1), 0), (0, 0)))
                s0 = gdn_states[j]
            res = _gdn_in(x, mod, nw_pre, prev, p["w_main_gdn"], p["w_ab_gdn"][j], p["conv_gdn"][j],
                          conv_init, a_log, dt_b, layer=j, heads=gdn_heads, dk=gdn_dk, tm=tm,
                          chunk=min(gdn_chunk, tm))
            if prev is not None:
                x, res = res[0], res[1:]
            qkvz, gcol, cout = res
            if t < gdn_chunk:
                front = gdn_chunk - t
                qkvz = jnp.pad(qkvz, ((0, 0), (front, 0), (0, 0)))
                gcol = jnp.pad(gcol, ((0, 0), (front, 0), (0, 0)))
            o, st = _gdn(qkvz, gcol, s0, p["onorm_gdn"][j][None, :], heads=gdn_heads, dk=gdn_dk,
                         tb=_pick_tile(qkvz.shape[1], 512), chunk=gdn_chunk,
                         nbatch=2 if b % 2 == 0 else 1, group=2 if qkvz.shape[1] >= 2 * gdn_chunk else 1)
            o = o[:, -t:]
            convs.append(cout[:, SUBLANES - (CONV_WIDTH - 1):, :])
            states.append(st)
            w_out = p["w_out_gdn"]
        else:
            lam_init = 0.8 - 0.6 * math.exp(-0.3 * i)
            lam_vecs = jnp.stack([p["lam_q1"][j], p["lam_k1"][j], p["lam_q2"][j], p["lam_k2"][j]])
            res = _diff_in(x, mod, nw_pre, prev, p["w_in_diff"], cos_t, sin_t, caches, layer=j, heads=diff_heads,
                           tm=tm, v_transposed=prompt, q_scale=dq ** -0.5 * LOG2E)
            if prev is not None:
                x, res = res[0], res[1:]
            qb, kb, k_all, v_all, vb, zs, kn2 = res
            caches = (k_all, v_all)
            subln = p["subln_diff"][j][None, :]
            if prompt:
                o = _attn_prompt(qb, kb, vb, zs, kn2, lam_vecs, subln, heads=diff_heads, lam_init=lam_init,
                                 tq=_pick_tile(t, 4096), tk=_pick_tile(t, 256))
            else:
                o = _attn_sample(qb, k_all, v_all, past_k, past_v, zs, lam_vecs, subln, layer=j, heads=diff_heads,
                                 lam_init=lam_init, tkv=_pick_tile(past_k.shape[2], 1024))
            w_out = p["w_out_diff"]
        prev = (o, w_out, j, mod, nw_post)
    x = _out_proj(x, prev, tm=_pick_tile(t, 4 * tm))
    return x, jnp.stack(states), jnp.stack(convs), k_all, v_all


def kernel(x_prompt, x_sample, c_prompt, c_sample, state_gdn, cache_conv, cache_k, cache_v, norm_pre, norm_post, w_ada, b_ada, w_in_gdn, conv_gdn, a_log_gdn, dt_bias_gdn, onorm_gdn, w_out_gdn, w_in_diff, lam_q1, lam_k1, lam_q2, lam_k2, subln_diff, w_out_diff):
    gdn_heads = a_log_gdn.shape[1]
    n_main = w_in_gdn.shape[2] - 2 * gdn_heads
    w_ab = jnp.pad(w_in_gdn[:, :, n_main:], ((0, 0), (0, 0), (0, LANES - 2 * gdn_heads)))
    w_ab_hi = w_ab.astype(BF16)
    w_ab_lo = (w_ab - w_ab_hi.astype(F32)).astype(BF16)
    p = {"norm_pre": norm_pre, "norm_post": norm_post,
         "w_main_gdn": w_in_gdn.astype(BF16),
         "w_ab_gdn": jnp.concatenate([w_ab_hi, w_ab_lo], axis=-1),

         "conv_gdn": conv_gdn, "a_log_gdn": a_log_gdn,
         "dt_bias_gdn": dt_bias_gdn, "onorm_gdn": onorm_gdn, "w_out_gdn": w_out_gdn.astype(BF16),
         "w_in_diff": w_in_diff.astype(BF16), "lam_q1": lam_q1, "lam_k1": lam_k1, "lam_q2": lam_q2,
         "lam_k2": lam_k2, "subln_diff": subln_diff, "w_out_diff": w_out_diff.astype(BF16)}
    bp, tp, d = x_prompt.shape
    bs, ts, _ = x_sample.shape
    past = cache_k.shape[2]
    depth = w_ada.shape[0]
    ada = _ada(jnp.concatenate([c_prompt, c_sample], axis=0), w_ada, b_ada)
    ada = ada.reshape(depth, bp + bs, 3, d)
    y_p, st_p, conv_p, k_p, v_p = _trunk(x_prompt, ada[:, :bp], jnp.arange(tp), None, None, None, None, p)
    y_s, st_s, conv_s, k_s, v_s = _trunk(x_sample, ada[:, bp:], past + jnp.arange(ts), cache_conv, state_gdn,
                                         cache_k, cache_v, p)
    return (y_p, y_s, st_p, conv_p, k_p, v_p, st_s, conv_s, k_s, v_s)
```
